```python
import math
import jax, jax.numpy as jnp
from jax import lax
import numpy as np

D_MODEL = 2048
BATCH = 2
SEQ = 8192
DEPTH = 1

CHUNK = 64
LN_EPS = 1e-5
NORM_EPS = 1e-6
SSM_GROUP_DIM = 16
SSM_STATE = 64
SSM_WIDTH = D_MODEL // 2
SSM_GROUPS = SSM_WIDTH // SSM_GROUP_DIM
DN_HEAD_DIM = 128
DN_HEADS = D_MODEL // DN_HEAD_DIM
DN_WIDTH = DN_HEADS * DN_HEAD_DIM
CONV_WIDTH = 4
N_EXPERTS = 32
TOP_K = 4
D_FF = D_MODEL
SWIGLU_LIMIT = 7.0
SWIGLU_ALPHA = 1.702
EXPERT_BLOCK = 256
DEEPNORM_ALPHA = (2 * DEPTH) ** 0.25
DEEPNORM_BETA = (8 * DEPTH) ** -0.25
IN_SPLITS = (SSM_WIDTH, DN_WIDTH, DN_WIDTH, DN_WIDTH, DN_WIDTH, DN_HEADS, DN_HEADS, D_MODEL, D_MODEL)
IN_WIDTH = SSM_WIDTH + 4 * DN_WIDTH + 2 * DN_HEADS + 2 * D_MODEL

kernel_name = 'hybrid_s5_gdn_moe_deepnorm'


def split_points():
    return [int(s) for s in np.cumsum(IN_SPLITS)[:-1]]


def layer_norm(x, g, b):
    xf = x.astype(jnp.float32)
    mu = jnp.mean(xf, axis=-1, keepdims=True)
    var = jnp.mean(jnp.square(xf - mu), axis=-1, keepdims=True)
    return ((xf - mu) * lax.rsqrt(var + LN_EPS) * g + b).astype(x.dtype)


def l2_normalize(t):
    return t * lax.rsqrt(jnp.sum(jnp.square(t), axis=-1, keepdims=True) + NORM_EPS)


def s5_mixer(u, lam_re, lam_im, log_step, b_re, b_im, c_re, c_im, d_skip, w_glu, b_glu):
    f32 = jnp.float32
    bsz, seq, _ = u.shape
    uf = u.astype(f32).reshape(bsz, seq, SSM_GROUPS, SSM_GROUP_DIM)
    lr, li = lam_re.astype(f32), lam_im.astype(f32)
    step = jnp.exp(log_step.astype(f32))[:, None]
    mag = jnp.exp(lr * step)
    a_re, a_im = mag * jnp.cos(li * step), mag * jnp.sin(li * step)
    den = lr * lr + li * li
    nr, ni = a_re - 1.0, a_im
    f_re = (nr * lr + ni * li) / den
    f_im = (ni * lr - nr * li) / den
    br, bi = b_re.astype(f32), b_im.astype(f32)
    bb_re = f_re[..., None] * br - f_im[..., None] * bi
    bb_im = f_re[..., None] * bi + f_im[..., None] * br
    bu_re = jnp.einsum('blgh,gph->blgp', uf, bb_re)
    bu_im = jnp.einsum('blgh,gph->blgp', uf, bb_im)
    ar = jnp.broadcast_to(a_re, bu_re.shape)
    ai = jnp.broadcast_to(a_im, bu_re.shape)

    def combine(e1, e2):
        a1r, a1i, b1r, b1i = e1
        a2r, a2i, b2r, b2i = e2
        return (a2r * a1r - a2i * a1i,
                a2r * a1i + a2i * a1r,
                a2r * b1r - a2i * b1i + b2r,
                a2r * b1i + a2i * b1r + b2i)

    _, _, s_re, s_im = lax.associative_scan(combine, (ar, ai, bu_re, bu_im), axis=1)
    y = (jnp.einsum('blgp,ghp->blgh', s_re, c_re.astype(f32))
         - jnp.einsum('blgp,ghp->blgh', s_im, c_im.astype(f32))
         + d_skip.astype(f32) * uf)
    y = jax.nn.gelu(y.reshape(bsz, seq, SSM_WIDTH), approximate=False)
    return y * jax.nn.sigmoid(y @ w_glu.astype(f32) + b_glu.astype(f32))


def causal_depthwise_conv(x, w):
    return lax.conv_general_dilated(x, w.astype(x.dtype), window_strides=(1,),
                                    padding=[(CONV_WIDTH - 1, 0)],
                                    dimension_numbers=('NWC', 'WIO', 'NWC'),
                                    feature_group_count=x.shape[-1])


def gated_delta_rule_chunked(q, k, v, g, beta):
    bsz, seq, heads, dk = q.shape
    dv = v.shape[-1]
    n = seq // CHUNK

    def to_chunks(t):
        t = t.reshape((bsz, n, CHUNK, heads) + t.shape[3:])
        return jnp.moveaxis(t, (1, 3), (0, 2))

    qc, kc, vc = to_chunks(q), to_chunks(k), to_chunks(v)
    gc = jnp.cumsum(to_chunks(g), axis=-1)
    bc = to_chunks(beta)
    idx = jnp.arange(CHUNK)
    causal = idx[:, None] >= idx[None, :]
    strict = idx[:, None] > idx[None, :]
    decay = jnp.exp(jnp.where(causal, gc[..., :, None] - gc[..., None, :], -jnp.inf))
    kb = kc * bc[..., None]
    m = jnp.where(strict, jnp.einsum('nbhcd,nbhsd->nbhcs', kb, kc) * decay, 0.0)
    rhs = jnp.concatenate([vc * bc[..., None], kb * jnp.exp(gc)[..., None]], axis=-1)
    sol = lax.linalg.triangular_solve(m + jnp.eye(CHUNK, dtype=m.dtype), rhs,
                                      left_side=True, lower=True, unit_diagonal=True)
    u_c, w_c = sol[..., :dv], sol[..., dv:]
    qk = jnp.where(causal, jnp.einsum('nbhcd,nbhsd->nbhcs', qc, kc) * decay, 0.0)

    def step(state, xs):
        q_i, k_i, u_i, w_i, g_i, qk_i = xs
        v_new = u_i - jnp.einsum('bhcd,bhde->bhce', w_i, state)
        o_i = (jnp.einsum('bhcd,bhde->bhce', q_i * jnp.exp(g_i)[..., None], state)
               + jnp.einsum('bhcs,bhse->bhce', qk_i, v_new))
        g_last = g_i[..., -1]
        state = (state * jnp.exp(g_last)[..., None, None]
                 + jnp.einsum('bhcd,bhce->bhde', k_i * jnp.exp(g_last[..., None] - g_i)[..., None], v_new))
        return state, o_i

    s0 = jnp.zeros((bsz, heads, dk, dv), jnp.float32)
    _, o = lax.scan(step, s0, (qc, kc, u_c, w_c, gc, qk))
    return jnp.moveaxis(o, (0, 2), (1, 3)).reshape(bsz, seq, heads, dv)


def gated_deltanet_mixer(q, k, v, z, a, b, conv_w, a_log, dt_bias, norm_w):
    f32 = jnp.float32
    bsz, seq, _ = q.shape
    qkv = jax.nn.silu(causal_depthwise_conv(jnp.concatenate([q, k, v], axis=-1), conv_w)).astype(f32)
    qh, kh, vh = jnp.split(qkv, [DN_WIDTH, 2 * DN_WIDTH], axis=-1)
    shape = (bsz, seq, DN_HEADS, DN_HEAD_DIM)
    qh = l2_normalize(qh.reshape(shape)) * (DN_HEAD_DIM ** -0.5)
    kh = l2_normalize(kh.reshape(shape))
    vh = vh.reshape(shape)
    beta = jax.nn.sigmoid(b.astype(f32))
    g = -jnp.exp(a_log.astype(f32)) * jax.nn.softplus(a.astype(f32) + dt_bias.astype(f32))
    o = gated_delta_rule_chunked(qh, kh, vh, g, beta)
    o = o * lax.rsqrt(jnp.mean(jnp.square(o), axis=-1, keepdims=True) + NORM_EPS) * norm_w.astype(f32)
    o = o * jax.nn.silu(z.astype(f32).reshape(shape))
    return o.reshape(bsz, seq, DN_WIDTH)


def token_mixer(h, w_in, lam_re, lam_im, log_step, ssm_b_re, ssm_b_im, ssm_c_re, ssm_c_im, ssm_d,
                w_glu, b_glu, conv_w, a_log, dt_bias, dn_norm_w, w_proj_ssm, w_proj_dn, w_out):
    proj = h @ w_in
    u, q, k, v, z, a, b, gate_ssm, gate_dn = jnp.split(proj, split_points(), axis=-1)
    y_ssm = s5_mixer(u, lam_re, lam_im, log_step, ssm_b_re, ssm_b_im, ssm_c_re, ssm_c_im, ssm_d, w_glu, b_glu)
    y_dn = gated_deltanet_mixer(q, k, v, z, a, b, conv_w, a_log, dt_bias, dn_norm_w)
    merged = (jax.nn.sigmoid(gate_ssm.astype(jnp.float32)) * (y_ssm @ w_proj_ssm)
              + jax.nn.sigmoid(gate_dn.astype(jnp.float32)) * (y_dn @ w_proj_dn))
    return (merged @ w_out).astype(h.dtype)


def moe_ffn(h, w_router, b_router, w_gate_up, b_gate_up, w_down, b_down):
    bsz, seq, dm = h.shape
    n_tok = bsz * seq
    xf = h.reshape(n_tok, dm)
    logits = (xf @ w_router + b_router).astype(jnp.float32)
    top_val, top_idx = lax.top_k(logits, TOP_K)
    gates = jax.nn.softmax(top_val, axis=-1)
    n_assign = n_tok * TOP_K
    e_flat = top_idx.reshape(n_assign)
    order = jnp.argsort(e_flat, stable=True)
    e_sorted = e_flat[order]
    tok_sorted = (order // TOP_K).astype(jnp.int32)
    gate_sorted = gates.reshape(n_assign)[order]
    counts = jnp.bincount(e_flat, length=N_EXPERTS)
    starts = jnp.cumsum(counts) - counts
    padded = (counts + EXPERT_BLOCK - 1) // EXPERT_BLOCK * EXPERT_BLOCK
    pad_ends = jnp.cumsum(padded)
    pad_starts = pad_ends - padded
    dest = pad_starts[e_sorted] + (jnp.arange(n_assign) - starts[e_sorted])
    n_rows = n_assign + N_EXPERTS * EXPERT_BLOCK
    n_blocks = n_rows // EXPERT_BLOCK
    row_tok = jnp.full((n_rows,), n_tok, jnp.int32).at[dest].set(tok_sorted)
    row_gate = jnp.zeros((n_rows,), jnp.float32).at[dest].set(gate_sorted)
    block_expert = jnp.minimum(
        jnp.searchsorted(pad_ends, jnp.arange(n_blocks) * EXPERT_BLOCK, side='right'), N_EXPERTS - 1)
    x_pad = jnp.concatenate([xf, jnp.zeros((1, dm), xf.dtype)], axis=0)
    xb = x_pad[row_tok].reshape(n_blocks, EXPERT_BLOCK, dm)

    def expert_block(args):
        xblk, e = args
        gu = xblk @ w_gate_up[e] + b_gate_up[e]
        gate, up = gu[:, :D_FF], gu[:, D_FF:]
        gate = jnp.minimum(gate, SWIGLU_LIMIT)
        up = jnp.clip(up, -SWIGLU_LIMIT, SWIGLU_LIMIT)
        act = gate * jax.nn.sigmoid(SWIGLU_ALPHA * gate) * (up + 1.0)
        return act @ w_down[e] + b_down[e]

    yb = lax.map(expert_block, (xb, block_expert))
    y = yb.reshape(n_rows, dm) * row_gate[:, None]
    out = jnp.zeros((n_tok + 1, dm), y.dtype).at[row_tok].add(y)[:n_tok]
    return out.reshape(bsz, seq, dm).astype(h.dtype)


def setup_inputs(seed: int = 0) -> dict:
    key = jax.random.key(seed)
    ks = jax.random.split(key, 40)
    f32 = jnp.float32
    L = DEPTH

    def nrm(k, shape, scale):
        return jax.random.normal(k, shape, f32) * scale

    def log_uniform(k, shape, lo, hi):
        return jax.random.uniform(k, shape, f32, math.log(lo), math.log(hi))

    n_idx = jnp.arange(SSM_STATE, dtype=f32)
    dt = jnp.exp(log_uniform(ks[14], (L, DN_HEADS), 1e-3, 1e-1))
    return {
        'x': nrm(ks[0], (BATCH, SEQ, D_MODEL), 1.0),
        'ln_in_g': 1.0 + nrm(ks[1], (D_MODEL,), 0.02),
        'ln_in_b': nrm(ks[2], (D_MODEL,), 0.02),
        'w_in': nrm(ks[3], (L, D_MODEL, IN_WIDTH), D_MODEL ** -0.5),
        'lam_re': -0.5 + nrm(ks[4], (L, SSM_GROUPS, SSM_STATE), 0.01),
        'lam_im': math.pi * n_idx + nrm(ks[5], (L, SSM_GROUPS, SSM_STATE), 0.01),
        'log_step': log_uniform(ks[6], (L, SSM_GROUPS), 1e-3, 1e-1),
        'ssm_b_re': nrm(ks[7], (L, SSM_GROUPS, SSM_STATE, SSM_GROUP_DIM), (2 * SSM_GROUP_DIM) ** -0.5),
        'ssm_b_im': nrm(ks[8], (L, SSM_GROUPS, SSM_STATE, SSM_GROUP_DIM), (2 * SSM_GROUP_DIM) ** -0.5),
        'ssm_c_re': nrm(ks[9], (L, SSM_GROUPS, SSM_GROUP_DIM, SSM_STATE), (2 * SSM_STATE) ** -0.5),
        'ssm_c_im': nrm(ks[10], (L, SSM_GROUPS, SSM_GROUP_DIM, SSM_STATE), (2 * SSM_STATE) ** -0.5),
        'ssm_d': nrm(ks[11], (L, SSM_GROUPS, SSM_GROUP_DIM), 0.5),
        'w_glu': nrm(ks[12], (L, SSM_WIDTH, SSM_WIDTH), SSM_WIDTH ** -0.5),
        'b_glu': nrm(ks[13], (L, SSM_WIDTH), 0.02),
        'conv_w': nrm(ks[15], (L, CONV_WIDTH, 1, 3 * DN_WIDTH), CONV_WIDTH ** -0.5),
        'a_log': jnp.log(jax.random.uniform(ks[16], (L, DN_HEADS), f32, 1.0, 16.0)),
        'dt_bias': dt + jnp.log(-jnp.expm1(-dt)),
        'dn_norm_w': 1.0 + nrm(ks[17], (L, DN_HEAD_DIM), 0.02),
        'w_proj_ssm': nrm(ks[18], (L, SSM_WIDTH, D_MODEL), DEEPNORM_BETA * SSM_WIDTH ** -0.5),
        'w_proj_dn': nrm(ks[19], (L, DN_WIDTH, D_MODEL), DEEPNORM_BETA * DN_WIDTH ** -0.5),
        'w_out': nrm(ks[20], (L, D_MODEL, D_MODEL), DEEPNORM_BETA * D_MODEL ** -0.5),
        'ln1_g': 1.0 + nrm(ks[21], (L, D_MODEL), 0.02),
        'ln1_b': nrm(ks[22], (L, D_MODEL), 0.02),
        'w_router': nrm(ks[23], (L, D_MODEL, N_EXPERTS), D_MODEL ** -0.5),
        'b_router': nrm(ks[24], (L, N_EXPERTS), 0.01),
        'w_gate_up': nrm(ks[25], (L, N_EXPERTS, D_MODEL, 2 * D_FF), DEEPNORM_BETA * D_MODEL ** -0.5),
        'b_gate_up': nrm(ks[26], (L, N_EXPERTS, 2 * D_FF), 0.01),
        'w_down': nrm(ks[27], (L, N_EXPERTS, D_FF, D_MODEL), DEEPNORM_BETA * D_FF ** -0.5),
        'b_down': nrm(ks[28], (L, N_EXPERTS, D_MODEL), 0.01),
        'ln2_g': 1.0 + nrm(ks[29], (L, D_MODEL), 0.02),
        'ln2_b': nrm(ks[30], (L, D_MODEL), 0.02),
    }


def reference(x, ln_in_g, ln_in_b, w_in, lam_re, lam_im, log_step, ssm_b_re, ssm_b_im, ssm_c_re,
              ssm_c_im, ssm_d, w_glu, b_glu, conv_w, a_log, dt_bias, dn_norm_w, w_proj_ssm,
              w_proj_dn, w_out, ln1_g, ln1_b, w_router, b_router, w_gate_up, b_gate_up, w_down,
              b_down, ln2_g, ln2_b):
    h = layer_norm(x, ln_in_g, ln_in_b)
    for i in range(DEPTH):
        mix = token_mixer(h, w_in[i], lam_re[i], lam_im[i], log_step[i], ssm_b_re[i], ssm_b_im[i],
                          ssm_c_re[i], ssm_c_im[i], ssm_d[i], w_glu[i], b_glu[i], conv_w[i],
                          a_log[i], dt_bias[i], dn_norm_w[i], w_proj_ssm[i], w_proj_dn[i], w_out[i])
        h = layer_norm(DEEPNORM_ALPHA * h + mix, ln1_g[i], ln1_b[i])
        ffn = moe_ffn(h, w_router[i], b_router[i], w_gate_up[i], b_gate_up[i], w_down[i], b_down[i])
        h = layer_norm(DEEPNORM_ALPHA * h + ffn, ln2_g[i], ln2_b[i])
    return h
```

```python
import functools
import math

import jax
import jax.numpy as jnp
from jax import lax
from jax.experimental import pallas as pl
from jax.experimental.pallas import tpu as pltpu

F32 = jnp.float32
BF16 = jnp.bfloat16
HI = lax.Precision.HIGHEST

LANES = 128
CHUNK = 64
LN_EPS = 1e-5
NORM_EPS = 1e-6
SSM_GROUP_DIM = 16
SSM_TC = 16
DN_HEAD_DIM = 128
CONV_WIDTH = 4
TOP_K = 4
SWIGLU_LIMIT = 7.0
SWIGLU_ALPHA = 1.702
DEPTH = 1
DEEPNORM_ALPHA = (2 * DEPTH) ** 0.25
ROW_BLOCK = 512
VMEM_LIMIT = 56 * 1024 * 1024


def _cparams(sem):
    return pltpu.CompilerParams(dimension_semantics=sem, vmem_limit_bytes=VMEM_LIMIT)


def _pick(n, pref):
    t = min(n, pref)
    while n % t:
        t //= 2
    return t


def _ln(x, g, b):
    mu = jnp.mean(x, axis=-1, keepdims=True)
    xc = x - mu
    var = jnp.mean(xc * xc, axis=-1, keepdims=True)
    return xc * lax.rsqrt(var + LN_EPS) * g + b


def _ln_in_kernel(x_ref, g_ref, b_ref, h_ref, hb_ref):
    h = _ln(x_ref[...], g_ref[...], b_ref[...])
    h_ref[...] = h
    hb_ref[...] = h.astype(BF16)


def _ln_in(x, g, b):
    t, d = x.shape
    tm = _pick(t, 512)
    return pl.pallas_call(
        _ln_in_kernel,
        grid=(t // tm,),
        in_specs=[pl.BlockSpec((tm, d), lambda i: (i, 0)),
                  pl.BlockSpec((1, d), lambda i: (0, 0)),
                  pl.BlockSpec((1, d), lambda i: (0, 0))],
        out_specs=[pl.BlockSpec((tm, d), lambda i: (i, 0)),
                   pl.BlockSpec((tm, d), lambda i: (i, 0))],
        out_shape=[jax.ShapeDtypeStruct((t, d), F32), jax.ShapeDtypeStruct((t, d), BF16)],
        compiler_params=_cparams(("arbitrary",)),
        name="ln_in",
    )(x, g.reshape(1, d), b.reshape(1, d))


def _mm_kernel(x_ref, w_ref, o_ref):
    o_ref[...] = jnp.dot(x_ref[...], w_ref[...], preferred_element_type=F32).astype(o_ref.dtype)


def _matmul(x, w, out_dtype, name):
    m, k = x.shape
    n = w.shape[1]
    tm = _pick(m, 1024)
    tn = _pick(n, 1024)
    return pl.pallas_call(
        _mm_kernel,
        grid=(n // tn, m // tm),
        in_specs=[pl.BlockSpec((tm, k), lambda j, i: (i, 0)),
                  pl.BlockSpec((k, tn), lambda j, i: (0, j))],
        out_specs=pl.BlockSpec((tm, tn), lambda j, i: (i, j)),
        out_shape=jax.ShapeDtypeStruct((m, n), out_dtype),
        compiler_params=_cparams(("arbitrary", "arbitrary")),
        name=name,
    )(x, w)


def _s5_operators(lam_re, lam_im, log_step, b_re, b_im, c_re, c_im, d_skip):
    g, p = lam_re.shape
    hd, tc = SSM_GROUP_DIM, SSM_TC
    lr, li = lam_re.astype(F32), lam_im.astype(F32)
    step = jnp.exp(log_step.astype(F32))[:, None]
    tau = jnp.arange(tc + 1, dtype=F32)[:, None, None]
    mag = jnp.exp(lr * step * tau)
    pr, pi = mag * jnp.cos(li * step * tau), mag * jnp.sin(li * step * tau)
    a_re, a_im = pr[1], pi[1]
    den = lr * lr + li * li
    nr, ni = a_re - 1.0, a_im
    f_re = (nr * lr + ni * li) / den
    f_im = (ni * lr - nr * li) / den
    br, bi = b_re.astype(F32), b_im.astype(F32)
    bb_re = f_re[..., None] * br - f_im[..., None] * bi
    bb_im = f_re[..., None] * bi + f_im[..., None] * br
    cr, ci = c_re.astype(F32), c_im.astype(F32)
    ca_re = cr[None] * pr[:, :, None, :] - ci[None] * pi[:, :, None, :]
    ca_im = cr[None] * pi[:, :, None, :] + ci[None] * pr[:, :, None, :]
    kk = (jnp.einsum('tgop,gpi->tgoi', ca_re[:tc], bb_re, precision=HI)
          - jnp.einsum('tgop,gpi->tgoi', ca_im[:tc], bb_im, precision=HI))
    kk = kk.at[0].add(d_skip.astype(F32)[:, :, None] * jnp.eye(hd, dtype=F32)[None])
    dt = jnp.arange(tc)
    diff = dt[None, :] - dt[:, None]
    kt = kk[jnp.clip(diff, 0, tc - 1)]
    kt = jnp.where((diff >= 0)[:, :, None, None, None], kt, 0.0)
    m_op = jnp.transpose(kt, (2, 0, 4, 1, 3)).reshape(g, tc * hd, tc * hd)
    prr, pir = pr[tc - 1 - dt], pi[tc - 1 - dt]
    bc_re = prr[..., None] * bb_re[None] - pir[..., None] * bb_im[None]
    bc_im = prr[..., None] * bb_im[None] + pir[..., None] * bb_re[None]
    bc_re = jnp.transpose(bc_re, (1, 0, 3, 2)).reshape(g, tc * hd, p)
    bc_im = jnp.transpose(bc_im, (1, 0, 3, 2)).reshape(g, tc * hd, p)
    cc_re = jnp.transpose(ca_re[1:], (1, 3, 0, 2)).reshape(g, p, tc * hd)
    cc_im = -jnp.transpose(ca_im[1:], (1, 3, 0, 2)).reshape(g, p, tc * hd)
    gp = g // 2
    w = tc * hd
    z = jnp.zeros((gp, w, p), F32)
    bcp = jnp.concatenate([
        jnp.concatenate([bc_re[0::2], z, bc_im[0::2], z], axis=2),
        jnp.concatenate([z, bc_re[1::2], z, bc_im[1::2]], axis=2)], axis=1)
    zc = jnp.zeros((gp, p, w), F32)
    ccr = jnp.concatenate([jnp.concatenate([cc_re[0::2], zc], axis=2),
                           jnp.concatenate([zc, cc_re[1::2]], axis=2)], axis=1)
    cci = jnp.concatenate([jnp.concatenate([cc_im[0::2], zc], axis=2),
                           jnp.concatenate([zc, cc_im[1::2]], axis=2)], axis=1)
    a16r = pr[tc].reshape(1, g * p)
    a16i = pi[tc].reshape(1, g * p)
    return (m_op.astype(BF16), bcp.astype(BF16), ccr.astype(BF16), cci.astype(BF16), a16r, a16i)


def _s5_kernel(u_ref, m_ref, bc_ref, ccr_ref, cci_ref, ar_ref, ai_ref, y_ref, xr_ref, xi_ref, *, npair, nc, sw):
    w = SSM_TC * SSM_GROUP_DIM
    for q in range(npair):
        up = u_ref[0, q]
        xl = jnp.dot(up, bc_ref[q], preferred_element_type=F32)
        xr_ref[:, q * sw:(q + 1) * sw] = xl[:, :sw]
        xi_ref[:, q * sw:(q + 1) * sw] = xl[:, sw:]
    ar = ar_ref[...]
    ai = ai_ref[...]

    def body(c, carry):
        sr, si = carry
        lr = xr_ref[pl.ds(c, 1), :]
        li = xi_ref[pl.ds(c, 1), :]
        xr_ref[pl.ds(c, 1), :] = sr
        xi_ref[pl.ds(c, 1), :] = si
        return ar * sr - ai * si + lr, ar * si + ai * sr + li

    z = jnp.zeros((1, npair * sw), F32)
    lax.fori_loop(0, nc, body, (z, z))
    for q in range(npair):
        up = u_ref[0, q]
        y0 = jnp.dot(up[:, :w], m_ref[2 * q], preferred_element_type=F32)
        y1 = jnp.dot(up[:, w:], m_ref[2 * q + 1], preferred_element_type=F32)
        xr = xr_ref[:, q * sw:(q + 1) * sw].astype(BF16)
        xi = xi_ref[:, q * sw:(q + 1) * sw].astype(BF16)
        yi = (jnp.dot(xr, ccr_ref[q], preferred_element_type=F32)
              + jnp.dot(xi, cci_ref[q], preferred_element_type=F32))
        y_ref[0, q, :, :w] = y0 + yi[:, :w]
        y_ref[0, q, :, w:] = y1 + yi[:, w:]


def _s5_scan(u_t, ops, n_state):
    m_op, bcp, ccr, cci, a16r, a16i = ops
    bsz, gp, nc, w2 = u_t.shape
    npair = _pick(gp, 4)
    sw = 2 * n_state
    kern = functools.partial(_s5_kernel, npair=npair, nc=nc, sw=sw)
    return pl.pallas_call(
        kern,
        grid=(bsz, gp // npair),
        in_specs=[pl.BlockSpec((1, npair, nc, w2), lambda b, j: (b, j, 0, 0)),
                  pl.BlockSpec((2 * npair, w2 // 2, w2 // 2), lambda b, j: (j, 0, 0)),
                  pl.BlockSpec((npair, w2, 2 * sw), lambda b, j: (j, 0, 0)),
                  pl.BlockSpec((npair, sw, w2), lambda b, j: (j, 0, 0)),
                  pl.BlockSpec((npair, sw, w2), lambda b, j: (j, 0, 0)),
                  pl.BlockSpec((1, npair * sw), lambda b, j: (0, j)),
                  pl.BlockSpec((1, npair * sw), lambda b, j: (0, j))],
        out_specs=pl.BlockSpec((1, npair, nc, w2), lambda b, j: (b, j, 0, 0)),
        out_shape=jax.ShapeDtypeStruct((bsz, gp, nc, w2), F32),
        scratch_shapes=[pltpu.VMEM((nc, npair * sw), F32), pltpu.VMEM((nc, npair * sw), F32)],
        compiler_params=_cparams(("arbitrary", "arbitrary")),
        name="s5_scan",
    )(u_t, m_op, bcp, ccr, cci, a16r, a16i)


def _ssm_post_kernel(y_ref, w_ref, b_ref, o_ref):
    y = y_ref[...]
    yg = 0.5 * y * (1.0 + lax.erf(y * (1.0 / math.sqrt(2.0))))
    s = jnp.dot(yg.astype(BF16), w_ref[...], preferred_element_type=F32) + b_ref[...]
    o_ref[...] = (yg * jax.nn.sigmoid(s)).astype(o_ref.dtype)


def _ssm_post(y, w_glu, b_glu):
    t, n = y.shape
    tm = _pick(t, 512)
    return pl.pallas_call(
        _ssm_post_kernel,
        grid=(t // tm,),
        in_specs=[pl.BlockSpec((tm, n), lambda i: (i, 0)),
                  pl.BlockSpec((n, n), lambda i: (0, 0)),
                  pl.BlockSpec((1, n), lambda i: (0, 0))],
        out_specs=pl.BlockSpec((tm, n), lambda i: (i, 0)),
        out_shape=jax.ShapeDtypeStruct((t, n), BF16),
        compiler_params=_cparams(("arbitrary",)),
        name="ssm_post",
    )(y, w_glu, b_glu.reshape(1, n))


def _dot_nt(a, b):
    return lax.dot_general(a, b, (((1,), (1,)), ((), ())), preferred_element_type=F32)


def _dot_tn(a, b):
    return lax.dot_general(a, b, (((0,), (0,)), ((), ())), preferred_element_type=F32)


def _dot_hi(a, b):
    return jnp.dot(a, b, preferred_element_type=F32, precision=HI)


def _silu(x):
    return x * jax.nn.sigmoid(x)


def _dn_kernel(q_ref, k_ref, v_ref, z_ref, ab_ref, cq_ref, ck_ref, cv_ref, al_ref, dtb_ref, nw_ref,
               o_ref, s_ref, xx_ref, *, tt, heads):
    h = pl.program_id(1)
    t = pl.program_id(2)
    hal = 8

    @pl.when(t == 0)
    def _():
        s_ref[...] = jnp.zeros_like(s_ref)
        xx_ref[:, 0:hal, :] = jnp.zeros((3, hal, DN_HEAD_DIM), F32)

    def conv(idx, x_ref, cw_ref):
        x = x_ref[...]
        xx_ref[idx, hal:, :] = x
        w = cw_ref[...]
        acc = w[CONV_WIDTH - 1:CONV_WIDTH, :] * x
        for j in range(CONV_WIDTH - 1):
            off = hal - (CONV_WIDTH - 1) + j
            acc = acc + w[j:j + 1, :] * xx_ref[idx, off:off + tt, :]
        xx_ref[idx, 0:hal, :] = x[tt - hal:, :]
        return _silu(acc)

    qc = conv(0, q_ref, cq_ref)
    kc = conv(1, k_ref, ck_ref)
    vc = conv(2, v_ref, cv_ref)
    qn = qc * lax.rsqrt(jnp.sum(qc * qc, axis=-1, keepdims=True) + NORM_EPS) * (DN_HEAD_DIM ** -0.5)
    kn = kc * lax.rsqrt(jnp.sum(kc * kc, axis=-1, keepdims=True) + NORM_EPS)

    ab = ab_ref[...]
    lane = lax.broadcasted_iota(jnp.int32, ab.shape, 1)
    xa = ab + dtb_ref[...]
    sp = jnp.maximum(xa, 0.0) + jnp.log1p(jnp.exp(-jnp.abs(xa)))
    g_all = -jnp.exp(al_ref[...]) * sp
    g_col = jnp.sum(jnp.where(lane == h, g_all, 0.0), axis=-1, keepdims=True)
    beta_col = jnp.sum(jnp.where(lane == h + heads, jax.nn.sigmoid(ab), 0.0), axis=-1, keepdims=True)

    ri = lax.broadcasted_iota(jnp.int32, (CHUNK, CHUNK), 0)
    ci = lax.broadcasted_iota(jnp.int32, (CHUNK, CHUNK), 1)
    causal = ri >= ci
    strict = ri > ci
    eye = (ri == ci).astype(F32)
    tri = causal.astype(F32)
    ones = jnp.ones((CHUNK, CHUNK), F32)

    outs = []
    for c in range(tt // CHUNK):
        sl = slice(c * CHUNK, (c + 1) * CHUNK)
        q_c, k_c, v_c = qn[sl], kn[sl], vc[sl]
        b_c = beta_col[sl]
        gb = jnp.broadcast_to(g_col[sl], (CHUNK, DN_HEAD_DIM))
        gc = _dot_hi(tri, gb)
        gc_r = _dot_hi(ones, gc[:, :CHUNK] * eye)
        decay = jnp.where(causal, jnp.exp(jnp.where(causal, gc[:, :CHUNK] - gc_r, 0.0)), 0.0)
        kb = k_c * b_c
        kbb = kb.astype(BF16)
        kcb = k_c.astype(BF16)
        a_m = jnp.where(strict, _dot_nt(kbb, kcb) * decay, 0.0)
        inv = eye - a_m
        pw = a_m
        for _ in range(5):
            pw = _dot_hi(pw, pw)
            inv = inv + _dot_hi(inv, pw)
        egc = jnp.exp(gc)
        rhs = jnp.concatenate([v_c * b_c, kb * egc], axis=1)
        sol = jnp.dot(inv.astype(BF16), rhs.astype(BF16), preferred_element_type=F32)
        u_c, w_c = sol[:, :DN_HEAD_DIM], sol[:, DN_HEAD_DIM:]
        qk = jnp.where(causal, _dot_nt(q_c.astype(BF16), kcb) * decay, 0.0)
        s = s_ref[...]
        sb = s.astype(BF16)
        v_new = u_c - jnp.dot(w_c.astype(BF16), sb, preferred_element_type=F32)
        vnb = v_new.astype(BF16)
        o_c = (jnp.dot((q_c * egc).astype(BF16), sb, preferred_element_type=F32)
               + jnp.dot(qk.astype(BF16), vnb, preferred_element_type=F32))
        g_last = gc[CHUNK - 1:CHUNK, :]
        k_dec = (k_c * jnp.exp(g_last - gc)).astype(BF16)
        s_ref[...] = s * jnp.exp(g_last) + _dot_tn(k_dec, vnb)
        outs.append(o_c)
    o = jnp.concatenate(outs, axis=0)
    o = o * lax.rsqrt(jnp.mean(o * o, axis=-1, keepdims=True) + NORM_EPS) * nw_ref[...]
    o_ref[...] = (o * _silu(z_ref[...])).astype(o_ref.dtype)


def _deltanet(proj, ab, conv_w, a_log, dt_bias, norm_w, bsz, seq, heads, off_q):
    t = bsz * seq
    tt = _pick(seq, 256)
    nt = seq // tt
    hd = DN_HEAD_DIM
    cw = conv_w.reshape(CONV_WIDTH, 3 * heads * hd).astype(F32)
    pad = LANES - heads
    al = jnp.pad(a_log.astype(F32), (0, pad)).reshape(1, LANES)
    dtb = jnp.pad(dt_bias.astype(F32), (0, pad)).reshape(1, LANES)
    nw = norm_w.astype(F32).reshape(1, hd)

    def act(o):
        return pl.BlockSpec((tt, hd), lambda b, h, i: (b * nt + i, off_q + o * heads + h))

    def cws(o):
        return pl.BlockSpec((CONV_WIDTH, hd), lambda b, h, i: (0, o * heads + h))

    row = pl.BlockSpec((1, LANES), lambda b, h, i: (0, 0))
    kern = functools.partial(_dn_kernel, tt=tt, heads=heads)
    return pl.pallas_call(
        kern,
        grid=(bsz, heads, nt),
        in_specs=[act(0), act(1), act(2), act(3),
                  pl.BlockSpec((tt, LANES), lambda b, h, i: (b * nt + i, 0)),
                  cws(0), cws(1), cws(2), row, row, row],
        out_specs=pl.BlockSpec((tt, hd), lambda b, h, i: (b * nt + i, h)),
        out_shape=jax.ShapeDtypeStruct((t, heads * hd), BF16),
        scratch_shapes=[pltpu.VMEM((hd, hd), F32), pltpu.VMEM((3, tt + 8, hd), F32)],
        compiler_params=_cparams(("arbitrary", "arbitrary", "arbitrary")),
        name="deltanet",
    )(proj, proj, proj, proj, ab, cw, cw, cw, al, dtb, nw)


def _merge_kernel(ys_ref, yd_ref, gs_ref, gd_ref, ws_ref, wd_ref, o_ref):
    ps = jnp.dot(ys_ref[...], ws_ref[...], preferred_element_type=F32)
    pd = jnp.dot(yd_ref[...], wd_ref[...], preferred_element_type=F32)
    o_ref[...] = (jax.nn.sigmoid(gs_ref[...]) * ps + jax.nn.sigmoid(gd_ref[...]) * pd).astype(o_ref.dtype)


def _merge(ys, yd, proj, off_gs, w_ps, w_pd):
    t, ns = ys.shape
    d = w_ps.shape[1]
    tm = _pick(t, 512)
    tn = _pick(ns, 1024)
    nb = d // tn
    return pl.pallas_call(
        _merge_kernel,
        grid=(nb, t // tm),
        in_specs=[pl.BlockSpec((tm, ns), lambda j, i: (i, 0)),
                  pl.BlockSpec((tm, d), lambda j, i: (i, 0)),
                  pl.BlockSpec((tm, tn), lambda j, i: (i, off_gs + j)),
                  pl.BlockSpec((tm, tn), lambda j, i: (i, off_gs + nb + j)),
                  pl.BlockSpec((ns, tn), lambda j, i: (0, j)),
                  pl.BlockSpec((d, tn), lambda j, i: (0, j))],
        out_specs=pl.BlockSpec((tm, tn), lambda j, i: (i, j)),
        out_shape=jax.ShapeDtypeStruct((t, d), BF16),
        compiler_params=_cparams(("arbitrary", "arbitrary")),
        name="merge",
    )(ys, yd, proj, proj, w_ps, w_pd)


def _pack_bf16_pair(x):
    half = x.shape[1] // 2
    lo = pltpu.bitcast(x[:, :half].astype(BF16).astype(F32), jnp.uint32)
    hi = pltpu.bitcast(x[:, half:].astype(BF16).astype(F32), jnp.uint32)
    return (lo >> 16) | (hi & jnp.uint32(0xFFFF0000))


def _unpack_bf16_pair(u):
    lo = pltpu.bitcast(u << 16, F32).astype(BF16)
    hi = pltpu.bitcast(u & jnp.uint32(0xFFFF0000), F32).astype(BF16)
    return lo, hi


def _out_ln_kernel(m_ref, w_ref, h_ref, g_ref, b_ref, h1_ref, hp_ref):
    mix = jnp.dot(m_ref[...], w_ref[...], preferred_element_type=F32)
    h1 = _ln(DEEPNORM_ALPHA * h_ref[...] + mix, g_ref[...], b_ref[...])
    h1_ref[...] = h1
    hp_ref[...] = _pack_bf16_pair(h1)


def _out_ln(merged, w_out, h, g, b):
    t, d = h.shape
    tm = _pick(t, 512)
    return pl.pallas_call(
        _out_ln_kernel,
        grid=(t // tm,),
        in_specs=[pl.BlockSpec((tm, d), lambda i: (i, 0)),
                  pl.BlockSpec((d, d), lambda i: (0, 0)),
                  pl.BlockSpec((tm, d), lambda i: (i, 0)),
                  pl.BlockSpec((1, d), lambda i: (0, 0)),
                  pl.BlockSpec((1, d), lambda i: (0, 0))],
        out_specs=[pl.BlockSpec((tm, d), lambda i: (i, 0)),
                   pl.BlockSpec((tm, d // 2), lambda i: (i, 0))],
        out_shape=[jax.ShapeDtypeStruct((t, d), F32), jax.ShapeDtypeStruct((t, d // 2), jnp.uint32)],
        compiler_params=_cparams(("arbitrary",)),
        name="out_ln1",
    )(merged, w_out, h, g.reshape(1, d), b.reshape(1, d))


def _router_kernel(h_ref, w_ref, b_ref, idx_ref, gate_ref, rank_ref, cnt_ref, base_ref, *, n_exp, tm):
    i = pl.program_id(0)

    @pl.when(i == 0)
    def _():
        base_ref[...] = jnp.zeros_like(base_ref)

    logits = _dot_hi(h_ref[...], w_ref[...]) + b_ref[...]
    lane = lax.broadcasted_iota(jnp.int32, logits.shape, 1)
    vals = jnp.where(lane < n_exp, logits, -jnp.inf)
    sels, tops, ams = [], [], []
    for _ in range(TOP_K):
        m = jnp.max(vals, axis=-1, keepdims=True)
        am = jnp.min(jnp.where(vals == m, lane, LANES), axis=-1, keepdims=True)
        sel = lane == am
        vals = jnp.where(sel, -jnp.inf, vals)
        sels.append(sel)
        tops.append(m)
        ams.append(am)
    es = [jnp.exp(v - tops[0]) for v in tops]
    den = es[0] + es[1] + es[2] + es[3]
    onehot = sels[0] | sels[1] | sels[2] | sels[3]
    mt = onehot.astype(BF16)
    ri = lax.broadcasted_iota(jnp.int32, (tm, tm), 0)
    ci = lax.broadcasted_iota(jnp.int32, (tm, tm), 1)
    before = (ri > ci).astype(BF16)
    prior = jnp.dot(before, mt, preferred_element_type=F32) + base_ref[...]
    idx_o = jnp.zeros(logits.shape, jnp.int32)
    gate_o = jnp.zeros(logits.shape, F32)
    rank_o = jnp.zeros(logits.shape, jnp.int32)
    for k in range(TOP_K):
        rk = jnp.sum(jnp.where(sels[k], prior, 0.0), axis=-1, keepdims=True).astype(jnp.int32)
        idx_o = jnp.where(lane == k, ams[k], idx_o)
        gate_o = jnp.where(lane == k, es[k] / den, gate_o)
        rank_o = jnp.where(lane == k, rk, rank_o)
    idx_ref[...] = idx_o
    gate_ref[...] = gate_o
    rank_ref[...] = rank_o
    base_ref[...] += jnp.sum(onehot.astype(F32), axis=0, keepdims=True)
    cnt_ref[...] = base_ref[...]


def _router(h1, w_router, b_router):
    t, d = h1.shape
    n_exp = w_router.shape[1]
    tm = _pick(t, 256)
    w = jnp.pad(w_router.astype(F32), ((0, 0), (0, LANES - n_exp)))
    b = jnp.pad(b_router.astype(F32), (0, LANES - n_exp)).reshape(1, LANES)
    tok = pl.BlockSpec((tm, LANES), lambda i: (i, 0))
    kern = functools.partial(_router_kernel, n_exp=n_exp, tm=tm)
    return pl.pallas_call(
        kern,
        grid=(t // tm,),
        in_specs=[pl.BlockSpec((tm, d), lambda i: (i, 0)),
                  pl.BlockSpec((d, LANES), lambda i: (0, 0)),
                  pl.BlockSpec((1, LANES), lambda i: (0, 0))],
        out_specs=[tok, tok, tok, pl.BlockSpec((1, LANES), lambda i: (0, 0))],
        out_shape=[jax.ShapeDtypeStruct((t, LANES), jnp.int32), jax.ShapeDtypeStruct((t, LANES), F32),
                   jax.ShapeDtypeStruct((t, LANES), jnp.int32), jax.ShapeDtypeStruct((1, LANES), F32)],
        scratch_shapes=[pltpu.VMEM((1, LANES), F32)],
        compiler_params=_cparams(("arbitrary",)),
        name="router",
    )(h1, w, b)


def _dispatch_kernel(dest_ref, x_ref, xs_in_ref, xs_ref, sem, *, tt):
    del xs_in_ref
    i = pl.program_id(0)
    base = i * (tt * TOP_K)

    def issue(r, carry):
        for k in range(TOP_K):
            row = dest_ref[base + r * TOP_K + k]
            pltpu.make_async_copy(x_ref.at[pl.ds(r, 1)], xs_ref.at[pl.ds(row, 1)], sem).start()
        return carry

    lax.fori_loop(0, tt, issue, 0)

    def drain(r, carry):
        for k in range(TOP_K):
            pltpu.make_async_copy(x_ref.at[pl.ds(0, 1)], xs_ref.at[pl.ds(0, 1)], sem).wait()
        return carry

    lax.fori_loop(0, tt, drain, 0)


def _dispatch(xp, dest, n_rows):
    t, dh = xp.shape
    tt = _pick(t, 256)
    xs0 = jnp.zeros((n_rows, dh), jnp.uint32)
    kern = functools.partial(_dispatch_kernel, tt=tt)
    return pl.pallas_call(
        kern,
        grid_spec=pltpu.PrefetchScalarGridSpec(
            num_scalar_prefetch=1,
            grid=(t // tt,),
            in_specs=[pl.BlockSpec((tt, dh), lambda i, dst: (i, 0)),
                      pl.BlockSpec(memory_space=pl.ANY)],
            out_specs=pl.BlockSpec(memory_space=pl.ANY),
            scratch_shapes=[pltpu.SemaphoreType.DMA(())]),
        out_shape=jax.ShapeDtypeStruct((n_rows, dh), jnp.uint32),
        input_output_aliases={2: 0},
        compiler_params=_cparams(("arbitrary",)),
        name="dispatch",
    )(dest, xp, xs0)


def _expert_kernel(be_ref, nb_ref, x_ref, wg_ref, wu_ref, bg_ref, bu_ref, wd_ref, bd_ref, o_ref, *, half):
    i = pl.program_id(0)
    j = pl.program_id(1)

    @pl.when(i < nb_ref[0])
    def _():
        lo, hi = _unpack_bf16_pair(x_ref[...])
        gate = (jnp.dot(lo, wg_ref[0, :half, :], preferred_element_type=F32)
                + jnp.dot(hi, wg_ref[0, half:, :], preferred_element_type=F32) + bg_ref[0])
        up = (jnp.dot(lo, wu_ref[0, :half, :], preferred_element_type=F32)
              + jnp.dot(hi, wu_ref[0, half:, :], preferred_element_type=F32) + bu_ref[0])
        gate = jnp.minimum(gate, SWIGLU_LIMIT)
        up = jnp.clip(up, -SWIGLU_LIMIT, SWIGLU_LIMIT)
        act = gate * jax.nn.sigmoid(SWIGLU_ALPHA * gate) * (up + 1.0)
        part = jnp.dot(act.astype(BF16), wd_ref[0], preferred_element_type=F32)

        @pl.when(j == 0)
        def _():
            o_ref[...] = part + bd_ref[0]

        @pl.when(j > 0)
        def _():
            o_ref[...] += part

    @pl.when((i >= nb_ref[0]) & (j == 0))
    def _():
        o_ref[...] = jnp.zeros_like(o_ref)


def _experts(xs, block_expert, n_used, w_gu, b_gu, w_down, b_down):
    n_rows, dh = xs.shape
    n_exp, d, ff2 = w_gu.shape
    ff = ff2 // 2
    tf = _pick(ff, 1024)
    nj = ff // tf
    nblk = n_rows // ROW_BLOCK
    b_gu3 = b_gu.astype(F32).reshape(n_exp, 1, ff2)
    b_d3 = b_down.astype(F32).reshape(n_exp, 1, d)

    def blk(i, nb):
        return jnp.minimum(i, nb[0] - 1)

    def jj(i, j, nb):
        return jnp.where(i < nb[0], j, nj - 1)

    kern = functools.partial(_expert_kernel, half=d // 2)
    return pl.pallas_call(
        kern,
        grid_spec=pltpu.PrefetchScalarGridSpec(
            num_scalar_prefetch=2,
            grid=(nblk, nj),
            in_specs=[
                pl.BlockSpec((ROW_BLOCK, dh), lambda i, j, be, nb: (blk(i, nb), 0)),
                pl.BlockSpec((1, d, tf), lambda i, j, be, nb: (be[blk(i, nb)], 0, jj(i, j, nb))),
                pl.BlockSpec((1, d, tf), lambda i, j, be, nb: (be[blk(i, nb)], 0, nj + jj(i, j, nb))),
                pl.BlockSpec((1, 1, tf), lambda i, j, be, nb: (be[blk(i, nb)], 0, jj(i, j, nb))),
                pl.BlockSpec((1, 1, tf), lambda i, j, be, nb: (be[blk(i, nb)], 0, nj + jj(i, j, nb))),
                pl.BlockSpec((1, tf, d), lambda i, j, be, nb: (be[blk(i, nb)], jj(i, j, nb), 0)),
                pl.BlockSpec((1, 1, d), lambda i, j, be, nb: (be[blk(i, nb)], 0, 0)),
            ],
            out_specs=pl.BlockSpec((ROW_BLOCK, d), lambda i, j, be, nb: (i, 0))),
        out_shape=jax.ShapeDtypeStruct((n_rows, d), F32),
        compiler_params=_cparams(("arbitrary", "arbitrary")),
        name="experts",
    )(block_expert, n_used, xs, w_gu, w_gu, b_gu3, b_gu3, w_down, b_d3)


def _combine_kernel(dest_ref, y_ref, h_ref, gate_ref, g_ref, b_ref, o_ref, buf_ref, sem, *, tt):
    i = pl.program_id(0)
    base = i * (tt * TOP_K)

    def issue(r, carry):
        for k in range(TOP_K):
            row = dest_ref[base + r * TOP_K + k]
            pltpu.make_async_copy(y_ref.at[pl.ds(row, 1)], buf_ref.at[k, pl.ds(r, 1)], sem).start()
        return carry

    lax.fori_loop(0, tt, issue, 0)

    def drain(r, carry):
        for k in range(TOP_K):
            pltpu.make_async_copy(y_ref.at[pl.ds(0, 1)], buf_ref.at[0, pl.ds(0, 1)], sem).wait()
        return carry

    lax.fori_loop(0, tt, drain, 0)
    gates = gate_ref[...]
    ffn = gates[:, 0:1] * buf_ref[0]
    for k in range(1, TOP_K):
        ffn = ffn + gates[:, k:k + 1] * buf_ref[k]
    o_ref[...] = _ln(DEEPNORM_ALPHA * h_ref[...] + ffn, g_ref[...], b_ref[...])


def _combine(y, dest, h1, gates, g, b):
    t, d = h1.shape
    tt = _pick(t, 128)
    kern = functools.partial(_combine_kernel, tt=tt)
    return pl.pallas_call(
        kern,
        grid_spec=pltpu.PrefetchScalarGridSpec(
            num_scalar_prefetch=1,
            grid=(t // tt,),
            in_specs=[pl.BlockSpec(memory_space=pl.ANY),
                      pl.BlockSpec((tt, d), lambda i, dst: (i, 0)),
                      pl.BlockSpec((tt, LANES), lambda i, dst: (i, 0)),
                      pl.BlockSpec((1, d), lambda i, dst: (0, 0)),
                      pl.BlockSpec((1, d), lambda i, dst: (0, 0))],
            out_specs=pl.BlockSpec((tt, d), lambda i, dst: (i, 0)),
            scratch_shapes=[pltpu.VMEM((TOP_K, tt, d), F32), pltpu.SemaphoreType.DMA(())]),
        out_shape=jax.ShapeDtypeStruct((t, d), F32),
        compiler_params=_cparams(("arbitrary",)),
        name="combine",
    )(dest, y, h1, gates, g.reshape(1, d), b.reshape(1, d))


def kernel(x, ln_in_g, ln_in_b, w_in, lam_re, lam_im, log_step, ssm_b_re, ssm_b_im, ssm_c_re, ssm_c_im, ssm_d, w_glu, b_glu, conv_w, a_log, dt_bias, dn_norm_w, w_proj_ssm, w_proj_dn, w_out, ln1_g, ln1_b, w_router, b_router, w_gate_up, b_gate_up, w_down, b_down, ln2_g, ln2_b):
    bsz, seq, d = x.shape
    t = bsz * seq
    sw = d // 2
    heads = d // DN_HEAD_DIM
    dn = heads * DN_HEAD_DIM
    groups = sw // SSM_GROUP_DIM
    n_state = lam_re.shape[-1]
    n_exp = w_router.shape[-1]
    assert w_in.shape[0] == DEPTH == 1
    assert seq % (CHUNK * 4) == 0 and sw % LANES == 0 and groups % 2 == 0 and 2 * heads <= LANES

    h, hb = _ln_in(x.reshape(t, d), ln_in_g, ln_in_b)

    wi = w_in[0]
    c_ab = sw + 4 * dn
    w_main = jnp.concatenate([wi[:, :c_ab], wi[:, c_ab + 2 * heads:]], axis=1).astype(BF16)
    w_ab = jnp.pad(wi[:, c_ab:c_ab + 2 * heads], ((0, 0), (0, LANES - 2 * heads))).astype(BF16)
    proj = _matmul(hb, w_main, F32, "proj_in")
    ab = _matmul(hb, w_ab, F32, "proj_ab")

    nc = seq // SSM_TC
    u = proj[:, :sw].astype(BF16).reshape(bsz, nc, SSM_TC, groups // 2, 2, SSM_GROUP_DIM)
    u_t = jnp.transpose(u, (0, 3, 1, 4, 2, 5)).reshape(bsz, groups // 2, nc, 2 * SSM_TC * SSM_GROUP_DIM)
    ops = _s5_operators(lam_re[0], lam_im[0], log_step[0], ssm_b_re[0], ssm_b_im[0],
                        ssm_c_re[0], ssm_c_im[0], ssm_d[0])
    y_t = _s5_scan(u_t, ops, n_state)
    y_s = jnp.transpose(y_t.reshape(bsz, groups // 2, nc, 2, SSM_TC, SSM_GROUP_DIM),
                        (0, 2, 4, 1, 3, 5)).reshape(t, sw)
    y_ssm = _ssm_post(y_s, w_glu[0].astype(BF16), b_glu[0].astype(F32))

    y_dn = _deltanet(proj, ab, conv_w[0], a_log[0], dt_bias[0], dn_norm_w[0], bsz, seq, heads,
                     off_q=sw // LANES)

    tn = _pick(sw, 1024)
    merged = _merge(y_ssm, y_dn, proj, (sw + 4 * dn) // tn, w_proj_ssm[0].astype(BF16),
                    w_proj_dn[0].astype(BF16))
    h1, h1p = _out_ln(merged, w_out[0].astype(BF16), h, ln1_g[0], ln1_b[0])

    idx, gates, rank, cnt = _router(h1, w_router[0], b_router[0])
    counts = cnt[0, :n_exp].astype(jnp.int32)
    padded = (counts + ROW_BLOCK - 1) // ROW_BLOCK * ROW_BLOCK
    pad_ends = jnp.cumsum(padded)
    pad_starts = pad_ends - padded
    dest = (pad_starts[idx[:, :TOP_K]] + rank[:, :TOP_K]).astype(jnp.int32).reshape(t * TOP_K)
    n_rows = t * TOP_K + n_exp * ROW_BLOCK
    nblk = n_rows // ROW_BLOCK
    block_expert = jnp.minimum(
        jnp.searchsorted(pad_ends, jnp.arange(nblk, dtype=jnp.int32) * ROW_BLOCK, side='right'),
        n_exp - 1).astype(jnp.int32)
    n_used = (pad_ends[-1:] // ROW_BLOCK).astype(jnp.int32)
    xs = _dispatch(h1p, dest, n_rows)
    y = _experts(xs, block_expert, n_used, w_gate_up[0].astype(BF16), b_gate_up[0],
                 w_down[0].astype(BF16), b_down[0])
    out = _combine(y, dest, h1, gates, ln2_g[0], ln2_b[0])
    return out.reshape(bsz, seq, d)
```

```python
import functools
import itertools
import math

import jax
import jax.numpy as jnp
from jax import lax
from jax.experimental import pallas as pl
from jax.experimental.pallas import tpu as pltpu

F32 = jnp.float32
BF16 = jnp.bfloat16
HI = lax.Precision.HIGHEST

LANES = 128
CHUNK = 64
LN_EPS = 1e-5
NORM_EPS = 1e-6
SSM_GROUP_DIM = 16
SSM_TC = 16
DN_HEAD_DIM = 128
CONV_WIDTH = 4
TOP_K = 4
SWIGLU_LIMIT = 7.0
SWIGLU_ALPHA = 1.702
DEPTH = 1
DEEPNORM_ALPHA = (2 * DEPTH) ** 0.25
ROW_BLOCK = 512
VMEM_LIMIT = 56 * 1024 * 1024


def _cparams(sem):
    return pltpu.CompilerParams(dimension_semantics=sem, vmem_limit_bytes=VMEM_LIMIT)


def _pick(n, pref):
    t = min(n, pref)
    while n % t:
        t //= 2
    return t


def _ln(x, g, b):
    mu = jnp.mean(x, axis=-1, keepdims=True)
    xc = x - mu
    var = jnp.mean(xc * xc, axis=-1, keepdims=True)
    return xc * lax.rsqrt(var + LN_EPS) * g + b


def _ln_in_kernel(x_ref, g_ref, b_ref, h_ref, hb_ref):
    h = _ln(x_ref[...], g_ref[...], b_ref[...])
    h_ref[...] = h
    hb_ref[...] = h.astype(BF16)


def _ln_in(x, g, b):
    t, d = x.shape
    tm = _pick(t, 512)
    return pl.pallas_call(
        _ln_in_kernel,
        grid=(t // tm,),
        in_specs=[pl.BlockSpec((tm, d), lambda i: (i, 0)),
                  pl.BlockSpec((1, d), lambda i: (0, 0)),
                  pl.BlockSpec((1, d), lambda i: (0, 0))],
        out_specs=[pl.BlockSpec((tm, d), lambda i: (i, 0)),
                   pl.BlockSpec((tm, d), lambda i: (i, 0))],
        out_shape=[jax.ShapeDtypeStruct((t, d), F32), jax.ShapeDtypeStruct((t, d), BF16)],
        compiler_params=_cparams(("arbitrary",)),
        name="ln_in",
    )(x, g.reshape(1, d), b.reshape(1, d))


def _mm_kernel(x_ref, w_ref, o_ref):
    o_ref[...] = jnp.dot(x_ref[...], w_ref[...], preferred_element_type=F32).astype(o_ref.dtype)


def _matmul(x, w, out_dtype, name):
    m, k = x.shape
    n = w.shape[1]
    tm = _pick(m, 1024)
    tn = _pick(n, 1024)
    return pl.pallas_call(
        _mm_kernel,
        grid=(n // tn, m // tm),
        in_specs=[pl.BlockSpec((tm, k), lambda j, i: (i, 0)),
                  pl.BlockSpec((k, tn), lambda j, i: (0, j))],
        out_specs=pl.BlockSpec((tm, tn), lambda j, i: (i, j)),
        out_shape=jax.ShapeDtypeStruct((m, n), out_dtype),
        compiler_params=_cparams(("arbitrary", "arbitrary")),
        name=name,
    )(x, w)


def _s5_operators(lam_re, lam_im, log_step, b_re, b_im, c_re, c_im, d_skip):
    g, p = lam_re.shape
    hd, tc = SSM_GROUP_DIM, SSM_TC
    lr, li = lam_re.astype(F32), lam_im.astype(F32)
    step = jnp.exp(log_step.astype(F32))[:, None]
    tau = jnp.arange(tc + 1, dtype=F32)[:, None, None]
    mag = jnp.exp(lr * step * tau)
    pr, pi = mag * jnp.cos(li * step * tau), mag * jnp.sin(li * step * tau)
    a_re, a_im = pr[1], pi[1]
    den = lr * lr + li * li
    nr, ni = a_re - 1.0, a_im
    f_re = (nr * lr + ni * li) / den
    f_im = (ni * lr - nr * li) / den
    br, bi = b_re.astype(F32), b_im.astype(F32)
    bb_re = f_re[..., None] * br - f_im[..., None] * bi
    bb_im = f_re[..., None] * bi + f_im[..., None] * br
    cr, ci = c_re.astype(F32), c_im.astype(F32)
    ca_re = cr[None] * pr[:, :, None, :] - ci[None] * pi[:, :, None, :]
    ca_im = cr[None] * pi[:, :, None, :] + ci[None] * pr[:, :, None, :]
    kk = (jnp.einsum('tgop,gpi->tgoi', ca_re[:tc], bb_re, precision=HI)
          - jnp.einsum('tgop,gpi->tgoi', ca_im[:tc], bb_im, precision=HI))
    kk = kk.at[0].add(d_skip.astype(F32)[:, :, None] * jnp.eye(hd, dtype=F32)[None])
    dt = jnp.arange(tc)
    diff = dt[None, :] - dt[:, None]
    kt = kk[jnp.clip(diff, 0, tc - 1)]
    kt = jnp.where((diff >= 0)[:, :, None, None, None], kt, 0.0)
    m_op = jnp.transpose(kt, (2, 0, 4, 1, 3)).reshape(g, tc * hd, tc * hd)
    prr, pir = pr[tc - 1 - dt], pi[tc - 1 - dt]
    bc_re = prr[..., None] * bb_re[None] - pir[..., None] * bb_im[None]
    bc_im = prr[..., None] * bb_im[None] + pir[..., None] * bb_re[None]
    bc_re = jnp.transpose(bc_re, (1, 0, 3, 2)).reshape(g, tc * hd, p)
    bc_im = jnp.transpose(bc_im, (1, 0, 3, 2)).reshape(g, tc * hd, p)
    cc_re = jnp.transpose(ca_re[1:], (1, 3, 0, 2)).reshape(g, p, tc * hd)
    cc_im = -jnp.transpose(ca_im[1:], (1, 3, 0, 2)).reshape(g, p, tc * hd)
    gs = LANES // hd
    ns = g // gs
    eye = jnp.eye(gs, dtype=F32)
    m6 = m_op.reshape(ns, gs, tc, hd, tc, hd)
    m_s = m6[:, :, :, :, :, None, :] * eye[None, :, None, None, None, :, None]
    m_s = jnp.transpose(m_s, (0, 2, 1, 3, 4, 5, 6)).reshape(ns, tc * LANES, tc * LANES)

    def bc_slab(x):
        x6 = x.reshape(ns, gs, tc, hd, p)
        y = x6[:, :, :, :, None, :] * eye[None, :, None, None, :, None]
        return jnp.transpose(y, (0, 2, 1, 3, 4, 5)).reshape(ns, tc * LANES, gs * p)

    def cc_slab(x):
        x6 = x.reshape(ns, gs, p, tc, hd)
        y = x6[:, :, :, :, None, :] * eye[None, :, None, None, :, None]
        return y.reshape(ns, gs * p, tc * LANES)

    bc_s = jnp.concatenate([bc_slab(bc_re), bc_slab(bc_im)], axis=2)
    cc_s = jnp.concatenate([cc_slab(cc_re), cc_slab(cc_im)], axis=1)
    a16r = pr[tc].reshape(1, g * p)
    a16i = pi[tc].reshape(1, g * p)
    return (m_s.astype(BF16), bc_s.astype(BF16), cc_s.astype(BF16), a16r, a16i)


def _s5_kernel(x_ref, m_ref, bc_ref, cc_ref, ar_ref, ai_ref, y_ref, st_ref, cr_ref, ci_ref, *, ncb, sl):
    @pl.when(pl.program_id(2) == 0)
    def _():
        cr_ref[...] = jnp.zeros_like(cr_ref)
        ci_ref[...] = jnp.zeros_like(ci_ref)

    xs = jnp.concatenate([x_ref[pl.ds(dt, ncb, stride=SSM_TC), :] for dt in range(SSM_TC)],
                         axis=1).astype(BF16)
    st_ref[...] = jnp.dot(xs, bc_ref[0], preferred_element_type=F32)
    ar = ar_ref[...]
    ai = ai_ref[...]

    def body(c, carry):
        sr, si = carry
        lr = st_ref[pl.ds(c, 1), :sl]
        li = st_ref[pl.ds(c, 1), sl:]
        st_ref[pl.ds(c, 1), :sl] = sr
        st_ref[pl.ds(c, 1), sl:] = si
        return ar * sr - ai * si + lr, ar * si + ai * sr + li

    sr, si = lax.fori_loop(0, ncb, body, (cr_ref[...], ci_ref[...]))
    cr_ref[...] = sr
    ci_ref[...] = si
    y = (jnp.dot(xs, m_ref[0], preferred_element_type=F32)
         + jnp.dot(st_ref[...].astype(BF16), cc_ref[0], preferred_element_type=F32))
    for dt in range(SSM_TC):
        y_ref[pl.ds(dt, ncb, stride=SSM_TC), :] = y[:, dt * LANES:(dt + 1) * LANES]


def _s5_scan(proj, ops, bsz, seq, sw, n_state):
    m_s, bc_s, cc_s, a16r, a16i = ops
    ns = sw // LANES
    sl = (LANES // SSM_GROUP_DIM) * n_state
    ncb = _pick(seq // SSM_TC, 128)
    rt = ncb * SSM_TC
    nt = seq // rt
    kw = SSM_TC * LANES
    kern = functools.partial(_s5_kernel, ncb=ncb, sl=sl)
    return pl.pallas_call(
        kern,
        grid=(ns, bsz, nt),
        in_specs=[pl.BlockSpec((rt, LANES), lambda s, b, i: (b * nt + i, s)),
                  pl.BlockSpec((1, kw, kw), lambda s, b, i: (s, 0, 0)),
                  pl.BlockSpec((1, kw, 2 * sl), lambda s, b, i: (s, 0, 0)),
                  pl.BlockSpec((1, 2 * sl, kw), lambda s, b, i: (s, 0, 0)),
                  pl.BlockSpec((1, sl), lambda s, b, i: (0, s)),
                  pl.BlockSpec((1, sl), lambda s, b, i: (0, s))],
        out_specs=pl.BlockSpec((rt, LANES), lambda s, b, i: (b * nt + i, s)),
        out_shape=jax.ShapeDtypeStruct((bsz * seq, sw), F32),
        scratch_shapes=[pltpu.VMEM((ncb, 2 * sl), F32), pltpu.VMEM((1, sl), F32), pltpu.VMEM((1, sl), F32)],
        compiler_params=_cparams(("arbitrary", "arbitrary", "arbitrary")),
        name="s5_scan",
    )(proj, m_s, bc_s, cc_s, a16r, a16i)


def _ssm_post_kernel(y_ref, w_ref, b_ref, o_ref):
    y = y_ref[...]
    yg = 0.5 * y * (1.0 + lax.erf(y * (1.0 / math.sqrt(2.0))))
    s = jnp.dot(yg.astype(BF16), w_ref[...], preferred_element_type=F32) + b_ref[...]
    o_ref[...] = (yg * jax.nn.sigmoid(s)).astype(o_ref.dtype)


def _ssm_post(y, w_glu, b_glu):
    t, n = y.shape
    tm = _pick(t, 512)
    return pl.pallas_call(
        _ssm_post_kernel,
        grid=(t // tm,),
        in_specs=[pl.BlockSpec((tm, n), lambda i: (i, 0)),
                  pl.BlockSpec((n, n), lambda i: (0, 0)),
                  pl.BlockSpec((1, n), lambda i: (0, 0))],
        out_specs=pl.BlockSpec((tm, n), lambda i: (i, 0)),
        out_shape=jax.ShapeDtypeStruct((t, n), BF16),
        compiler_params=_cparams(("arbitrary",)),
        name="ssm_post",
    )(y, w_glu, b_glu.reshape(1, n))


def _dot_nt(a, b):
    return lax.dot_general(a, b, (((1,), (1,)), ((), ())), preferred_element_type=F32)


def _dot_tn(a, b):
    return lax.dot_general(a, b, (((0,), (0,)), ((), ())), preferred_element_type=F32)


def _dot_hi(a, b):
    return jnp.dot(a, b, preferred_element_type=F32, precision=HI)


def _silu(x):
    return x * jax.nn.sigmoid(x)


DN_HALO = 8


def _dn_kernel(q_ref, k_ref, v_ref, z_ref, ab_ref, cq_ref, ck_ref, cv_ref, al_ref, dtb_ref, nw_ref,
               o_ref, *scratch, tt, heads, hp):
    s_refs, xx_refs = scratch[:hp], scratch[hp:]

    @pl.when(pl.program_id(2) == 0)
    def _():
        for s_ref, xx_ref in zip(s_refs, xx_refs):
            s_ref[...] = jnp.zeros_like(s_ref)
            xx_ref[:, 0:DN_HALO, :] = jnp.zeros((3, DN_HALO, DN_HEAD_DIM), F32)

    gens = [_dn_head(hh, pl.program_id(1) * hp + hh, q_ref, k_ref, v_ref, z_ref, ab_ref, cq_ref, ck_ref, cv_ref,
                     al_ref, dtb_ref, nw_ref, o_ref, s_refs[hh], xx_refs[hh], tt=tt, heads=heads)
            for hh in range(hp)]
    for _ in itertools.zip_longest(*gens):
        pass


def _dn_head(hh, h, q_ref, k_ref, v_ref, z_ref, ab_ref, cq_ref, ck_ref, cv_ref, al_ref, dtb_ref, nw_ref,
             o_ref, s_ref, xx_ref, *, tt, heads):
    hal = DN_HALO
    ls = slice(hh * DN_HEAD_DIM, (hh + 1) * DN_HEAD_DIM)

    def conv(idx, x_ref, cw_ref):
        x = x_ref[:, ls]
        xx_ref[idx, hal:, :] = x
        w = cw_ref[:, ls]
        acc = w[CONV_WIDTH - 1:CONV_WIDTH, :] * x
        for j in range(CONV_WIDTH - 1):
            off = hal - (CONV_WIDTH - 1) + j
            acc = acc + w[j:j + 1, :] * xx_ref[idx, off:off + tt, :]
        xx_ref[idx, 0:hal, :] = x[tt - hal:, :]
        return _silu(acc)

    qc = conv(0, q_ref, cq_ref)
    kc = conv(1, k_ref, ck_ref)
    vc = conv(2, v_ref, cv_ref)
    qn = qc * lax.rsqrt(jnp.sum(qc * qc, axis=-1, keepdims=True) + NORM_EPS) * (DN_HEAD_DIM ** -0.5)
    kn = kc * lax.rsqrt(jnp.sum(kc * kc, axis=-1, keepdims=True) + NORM_EPS)

    ab = ab_ref[...]
    lane = lax.broadcasted_iota(jnp.int32, ab.shape, 1)
    xa = ab + dtb_ref[...]
    sp = jnp.maximum(xa, 0.0) + jnp.log1p(jnp.exp(-jnp.abs(xa)))
    g_all = -jnp.exp(al_ref[...]) * sp
    g_col = jnp.sum(jnp.where(lane == h, g_all, 0.0), axis=-1, keepdims=True)
    beta_col = jnp.sum(jnp.where(lane == h + heads, jax.nn.sigmoid(ab), 0.0), axis=-1, keepdims=True)

    nch = tt // CHUNK
    row = lax.broadcasted_iota(jnp.int32, (tt, DN_HEAD_DIM), 0)
    pos = row & (CHUNK - 1)
    gc = jnp.broadcast_to(g_col, (tt, DN_HEAD_DIM))
    sh = 1
    while sh < CHUNK:
        gc = gc + jnp.where(pos >= sh, pltpu.roll(gc, sh, axis=0), 0.0)
        sh *= 2
    gc_row = jnp.transpose(gc)[0:1, :]
    egc = jnp.exp(gc)

    ri = lax.broadcasted_iota(jnp.int32, (tt, tt), 0)
    ci = lax.broadcasted_iota(jnp.int32, (tt, tt), 1)
    same = (ri // CHUNK) == (ci // CHUNK)
    causal = same & (ri >= ci)
    strict = same & (ri > ci)
    gc_wide = jnp.concatenate([gc] * (tt // DN_HEAD_DIM), axis=1)
    decay = jnp.where(causal, jnp.exp(jnp.where(causal, gc_wide - gc_row, 0.0)), 0.0)
    kb = kn * beta_col
    knb = kn.astype(BF16)
    yield
    a_raw = _dot_nt(kb.astype(BF16), knb)
    qk_raw = _dot_nt(qn.astype(BF16), knb)
    yield
    a_bd = jnp.where(strict, a_raw * decay, 0.0)
    qk_bd = jnp.where(causal, qk_raw * decay, 0.0).astype(BF16)

    def fold(m):
        out = m[0:CHUNK]
        for c in range(1, nch):
            out = out + m[c * CHUNK:(c + 1) * CHUNK]
        return out

    def spread(m):
        return jnp.where(same, jnp.concatenate([m] * nch, axis=0), 0.0)

    r64 = lax.broadcasted_iota(jnp.int32, (CHUNK, tt), 0)
    c64 = lax.broadcasted_iota(jnp.int32, (CHUNK, tt), 1)
    eye_cat = (r64 == (c64 & (CHUNK - 1))).astype(F32)
    pw_cat = fold(a_bd)
    inv_cat = eye_cat - pw_cat
    pw_bd = a_bd.astype(BF16)
    for _ in range(5):
        pw_cat = jnp.dot(pw_cat.astype(BF16), pw_bd, preferred_element_type=F32)
        yield
        pw_bd = spread(pw_cat).astype(BF16)
        inv_add = jnp.dot(inv_cat.astype(BF16), pw_bd, preferred_element_type=F32)
        yield
        inv_cat = inv_cat + inv_add
    inv_bd = spread(inv_cat).astype(BF16)
    rhs = jnp.concatenate([vc * beta_col, kb * egc], axis=1).astype(BF16)
    sol = jnp.dot(inv_bd, rhs, preferred_element_type=F32)
    yield
    u_all, w_all = sol[:, :DN_HEAD_DIM], sol[:, DN_HEAD_DIM:]
    qe = qn * egc

    outs = []
    zblk = jnp.zeros((CHUNK, DN_HEAD_DIM), BF16)
    s = s_ref[...]
    for c in range(nch):
        sl = slice(c * CHUNK, (c + 1) * CHUNK)
        wq = jnp.concatenate([w_all[sl], qe[sl]], axis=0).astype(BF16)
        ws = jnp.dot(wq, s.astype(BF16), preferred_element_type=F32)
        yield
        v_new = u_all[sl] - ws[:CHUNK]
        vnb = v_new.astype(BF16)
        v_pad = jnp.concatenate([zblk] * c + [vnb] + [zblk] * (nch - 1 - c), axis=0)
        gc_c = gc[sl]
        g_last = gc_c[CHUNK - 1:CHUNK, :]
        k_dec = (kn[sl] * jnp.exp(g_last - gc_c)).astype(BF16)
        s_add = _dot_tn(k_dec, vnb)
        o_add = jnp.dot(qk_bd[sl], v_pad, preferred_element_type=F32)
        yield
        s = s * jnp.exp(g_last) + s_add
        outs.append(ws[CHUNK:] + o_add)
    s_ref[...] = s
    o = jnp.concatenate(outs, axis=0)
    o = o * lax.rsqrt(jnp.mean(o * o, axis=-1, keepdims=True) + NORM_EPS) * nw_ref[...]
    o_ref[:, ls] = (o * _silu(z_ref[:, ls])).astype(o_ref.dtype)


def _deltanet(proj, ab, conv_w, a_log, dt_bias, norm_w, bsz, seq, heads, off_q):
    t = bsz * seq
    tt = _pick(seq, 256)
    nt = seq // tt
    hd = DN_HEAD_DIM
    cw = conv_w.reshape(CONV_WIDTH, 3 * heads * hd).astype(F32)
    pad = LANES - heads
    al = jnp.pad(a_log.astype(F32), (0, pad)).reshape(1, LANES)
    dtb = jnp.pad(dt_bias.astype(F32), (0, pad)).reshape(1, LANES)
    nw = norm_w.astype(F32).reshape(1, hd)

    hp = math.gcd(math.gcd(heads, off_q), 4)
    hw = hp * hd
    oq, nh = off_q // hp, heads // hp

    def act(o):
        return pl.BlockSpec((tt, hw), lambda b, h, i: (b * nt + i, oq + o * nh + h))

    def cws(o):
        return pl.BlockSpec((CONV_WIDTH, hw), lambda b, h, i: (0, o * nh + h))

    row = pl.BlockSpec((1, LANES), lambda b, h, i: (0, 0))
    kern = functools.partial(_dn_kernel, tt=tt, heads=heads, hp=hp)
    return pl.pallas_call(
        kern,
        grid=(bsz, nh, nt),
        in_specs=[act(0), act(1), act(2), act(3),
                  pl.BlockSpec((tt, LANES), lambda b, h, i: (b * nt + i, 0)),
                  cws(0), cws(1), cws(2), row, row, row],
        out_specs=pl.BlockSpec((tt, hw), lambda b, h, i: (b * nt + i, h)),
        out_shape=jax.ShapeDtypeStruct((t, heads * hd), BF16),
        scratch_shapes=([pltpu.VMEM((hd, hd), F32)] * hp
                        + [pltpu.VMEM((3, tt + DN_HALO, hd), F32)] * hp),
        compiler_params=_cparams(("arbitrary", "arbitrary", "arbitrary")),
        name="deltanet",
    )(proj, proj, proj, proj, ab, cw, cw, cw, al, dtb, nw)


def _merge_kernel(ys_ref, yd_ref, gs_ref, gd_ref, ws_ref, wd_ref, o_ref):
    ps = jnp.dot(ys_ref[...], ws_ref[...], preferred_element_type=F32)
    pd = jnp.dot(yd_ref[...], wd_ref[...], preferred_element_type=F32)
    o_ref[...] = (jax.nn.sigmoid(gs_ref[...]) * ps + jax.nn.sigmoid(gd_ref[...]) * pd).astype(o_ref.dtype)


def _merge(ys, yd, proj, off_gs, w_ps, w_pd):
    t, ns = ys.shape
    d = w_ps.shape[1]
    tm = _pick(t, 512)
    tn = _pick(ns, 1024)
    nb = d // tn
    return pl.pallas_call(
        _merge_kernel,
        grid=(nb, t // tm),
        in_specs=[pl.BlockSpec((tm, ns), lambda j, i: (i, 0)),
                  pl.BlockSpec((tm, d), lambda j, i: (i, 0)),
                  pl.BlockSpec((tm, tn), lambda j, i: (i, off_gs + j)),
                  pl.BlockSpec((tm, tn), lambda j, i: (i, off_gs + nb + j)),
                  pl.BlockSpec((ns, tn), lambda j, i: (0, j)),
                  pl.BlockSpec((d, tn), lambda j, i: (0, j))],
        out_specs=pl.BlockSpec((tm, tn), lambda j, i: (i, j)),
        out_shape=jax.ShapeDtypeStruct((t, d), BF16),
        compiler_params=_cparams(("arbitrary", "arbitrary")),
        name="merge",
    )(ys, yd, proj, proj, w_ps, w_pd)


def _pack_bf16_pair(x):
    half = x.shape[1] // 2
    lo = pltpu.bitcast(x[:, :half].astype(BF16).astype(F32), jnp.uint32)
    hi = pltpu.bitcast(x[:, half:].astype(BF16).astype(F32), jnp.uint32)
    return (lo >> 16) | (hi & jnp.uint32(0xFFFF0000))


def _unpack_bf16_pair(u):
    lo = pltpu.bitcast(u << 16, F32).astype(BF16)
    hi = pltpu.bitcast(u & jnp.uint32(0xFFFF0000), F32).astype(BF16)
    return lo, hi


def _out_ln_kernel(m_ref, w_ref, h_ref, g_ref, b_ref, h1_ref, hp_ref):
    mix = jnp.dot(m_ref[...], w_ref[...], preferred_element_type=F32)
    h1 = _ln(DEEPNORM_ALPHA * h_ref[...] + mix, g_ref[...], b_ref[...])
    h1_ref[...] = h1
    hp_ref[...] = _pack_bf16_pair(h1)


def _out_ln(merged, w_out, h, g, b):
    t, d = h.shape
    tm = _pick(t, 512)
    return pl.pallas_call(
        _out_ln_kernel,
        grid=(t // tm,),
        in_specs=[pl.BlockSpec((tm, d), lambda i: (i, 0)),
                  pl.BlockSpec((d, d), lambda i: (0, 0)),
                  pl.BlockSpec((tm, d), lambda i: (i, 0)),
                  pl.BlockSpec((1, d), lambda i: (0, 0)),
                  pl.BlockSpec((1, d), lambda i: (0, 0))],
        out_specs=[pl.BlockSpec((tm, d), lambda i: (i, 0)),
                   pl.BlockSpec((tm, d // 2), lambda i: (i, 0))],
        out_shape=[jax.ShapeDtypeStruct((t, d), F32), jax.ShapeDtypeStruct((t, d // 2), jnp.uint32)],
        compiler_params=_cparams(("arbitrary",)),
        name="out_ln1",
    )(merged, w_out, h, g.reshape(1, d), b.reshape(1, d))


def _router_kernel(h_ref, w_ref, b_ref, idx_ref, gate_ref, rank_ref, cnt_ref, base_ref, *, n_exp, tm):
    i = pl.program_id(0)

    @pl.when(i == 0)
    def _():
        base_ref[...] = jnp.zeros_like(base_ref)

    logits = _dot_hi(h_ref[...], w_ref[...]) + b_ref[...]
    lane = lax.broadcasted_iota(jnp.int32, logits.shape, 1)
    vals = jnp.where(lane < n_exp, logits, -jnp.inf)
    sels, tops, ams = [], [], []
    for _ in range(TOP_K):
        m = jnp.max(vals, axis=-1, keepdims=True)
        am = jnp.min(jnp.where(vals == m, lane, LANES), axis=-1, keepdims=True)
        sel = lane == am
        vals = jnp.where(sel, -jnp.inf, vals)
        sels.append(sel)
        tops.append(m)
        ams.append(am)
    es = [jnp.exp(v - tops[0]) for v in tops]
    den = es[0] + es[1] + es[2] + es[3]
    onehot = sels[0] | sels[1] | sels[2] | sels[3]
    mt = onehot.astype(BF16)
    ri = lax.broadcasted_iota(jnp.int32, (tm, tm), 0)
    ci = lax.broadcasted_iota(jnp.int32, (tm, tm), 1)
    before = (ri > ci).astype(BF16)
    prior = jnp.dot(before, mt, preferred_element_type=F32) + base_ref[...]
    idx_o = jnp.zeros(logits.shape, jnp.int32)
    gate_o = jnp.zeros(logits.shape, F32)
    rank_o = jnp.zeros(logits.shape, jnp.int32)
    for k in range(TOP_K):
        rk = jnp.sum(jnp.where(sels[k], prior, 0.0), axis=-1, keepdims=True).astype(jnp.int32)
        idx_o = jnp.where(lane == k, ams[k], idx_o)
        gate_o = jnp.where(lane == k, es[k] / den, gate_o)
        rank_o = jnp.where(lane == k, rk, rank_o)
    idx_ref[...] = idx_o
    gate_ref[...] = gate_o
    rank_ref[...] = rank_o
    base_ref[...] += jnp.sum(onehot.astype(F32), axis=0, keepdims=True)
    cnt_ref[...] = base_ref[...]


def _router(h1, w_router, b_router):
    t, d = h1.shape
    n_exp = w_router.shape[1]
    tm = _pick(t, 256)
    w = jnp.pad(w_router.astype(F32), ((0, 0), (0, LANES - n_exp)))
    b = jnp.pad(b_router.astype(F32), (0, LANES - n_exp)).reshape(1, LANES)
    tok = pl.BlockSpec((tm, LANES), lambda i: (i, 0))
    kern = functools.partial(_router_kernel, n_exp=n_exp, tm=tm)
    return pl.pallas_call(
        kern,
        grid=(t // tm,),
        in_specs=[pl.BlockSpec((tm, d), lambda i: (i, 0)),
                  pl.BlockSpec((d, LANES), lambda i: (0, 0)),
                  pl.BlockSpec((1, LANES), lambda i: (0, 0))],
        out_specs=[tok, tok, tok, pl.BlockSpec((1, LANES), lambda i: (0, 0))],
        out_shape=[jax.ShapeDtypeStruct((t, LANES), jnp.int32), jax.ShapeDtypeStruct((t, LANES), F32),
                   jax.ShapeDtypeStruct((t, LANES), jnp.int32), jax.ShapeDtypeStruct((1, LANES), F32)],
        scratch_shapes=[pltpu.VMEM((1, LANES), F32)],
        compiler_params=_cparams(("arbitrary",)),
        name="router",
    )(h1, w, b)


def _dispatch_kernel(dest_ref, x_ref, xs_in_ref, xs_ref, sem, *, tt):
    del xs_in_ref
    i = pl.program_id(0)
    base = i * (tt * TOP_K)

    def issue(r, carry):
        for k in range(TOP_K):
            row = dest_ref[base + r * TOP_K + k]
            pltpu.make_async_copy(x_ref.at[pl.ds(r, 1)], xs_ref.at[pl.ds(row, 1)], sem).start()
        return carry

    lax.fori_loop(0, tt, issue, 0)

    def drain(r, carry):
        for k in range(TOP_K):
            pltpu.make_async_copy(x_ref.at[pl.ds(0, 1)], xs_ref.at[pl.ds(0, 1)], sem).wait()
        return carry

    lax.fori_loop(0, tt, drain, 0)


def _dispatch(xp, dest, n_rows):
    t, dh = xp.shape
    tt = _pick(t, 256)
    xs0 = jnp.zeros((n_rows, dh), jnp.uint32)
    kern = functools.partial(_dispatch_kernel, tt=tt)
    return pl.pallas_call(
        kern,
        grid_spec=pltpu.PrefetchScalarGridSpec(
            num_scalar_prefetch=1,
            grid=(t // tt,),
            in_specs=[pl.BlockSpec((tt, dh), lambda i, dst: (i, 0)),
                      pl.BlockSpec(memory_space=pl.ANY)],
            out_specs=pl.BlockSpec(memory_space=pl.ANY),
            scratch_shapes=[pltpu.SemaphoreType.DMA(())]),
        out_shape=jax.ShapeDtypeStruct((n_rows, dh), jnp.uint32),
        input_output_aliases={2: 0},
        compiler_params=_cparams(("arbitrary",)),
        name="dispatch",
    )(dest, xp, xs0)


def _expert_kernel(be_ref, nb_ref, x_ref, wg_ref, wu_ref, bg_ref, bu_ref, wd_ref, bd_ref, o_ref, *, half):
    i = pl.program_id(0)
    j = pl.program_id(1)

    @pl.when(i < nb_ref[0])
    def _():
        lo, hi = _unpack_bf16_pair(x_ref[...])
        gate = (jnp.dot(lo, wg_ref[0, :half, :], preferred_element_type=F32)
                + jnp.dot(hi, wg_ref[0, half:, :], preferred_element_type=F32) + bg_ref[0])
        up = (jnp.dot(lo, wu_ref[0, :half, :], preferred_element_type=F32)
              + jnp.dot(hi, wu_ref[0, half:, :], preferred_element_type=F32) + bu_ref[0])
        gate = jnp.minimum(gate, SWIGLU_LIMIT)
        up = jnp.clip(up, -SWIGLU_LIMIT, SWIGLU_LIMIT)
        act = gate * jax.nn.sigmoid(SWIGLU_ALPHA * gate) * (up + 1.0)
        part = jnp.dot(act.astype(BF16), wd_ref[0], preferred_element_type=F32)

        @pl.when(j == 0)
        def _():
            o_ref[...] = part + bd_ref[0]

        @pl.when(j > 0)
        def _():
            o_ref[...] += part

    @pl.when((i >= nb_ref[0]) & (j == 0))
    def _():
        o_ref[...] = jnp.zeros_like(o_ref)


def _experts(xs, block_expert, n_used, w_gu, b_gu, w_down, b_down):
    n_rows, dh = xs.shape
    n_exp, d, ff2 = w_gu.shape
    ff = ff2 // 2
    tf = _pick(ff, 1024)
    nj = ff // tf
    nblk = n_rows // ROW_BLOCK
    b_gu3 = b_gu.astype(F32).reshape(n_exp, 1, ff2)
    b_d3 = b_down.astype(F32).reshape(n_exp, 1, d)

    def blk(i, nb):
        return jnp.minimum(i, nb[0] - 1)

    def jj(i, j, nb):
        return jnp.where(i < nb[0], j, nj - 1)

    kern = functools.partial(_expert_kernel, half=d // 2)
    return pl.pallas_call(
        kern,
        grid_spec=pltpu.PrefetchScalarGridSpec(
            num_scalar_prefetch=2,
            grid=(nblk, nj),
            in_specs=[
                pl.BlockSpec((ROW_BLOCK, dh), lambda i, j, be, nb: (blk(i, nb), 0)),
                pl.BlockSpec((1, d, tf), lambda i, j, be, nb: (be[blk(i, nb)], 0, jj(i, j, nb))),
                pl.BlockSpec((1, d, tf), lambda i, j, be, nb: (be[blk(i, nb)], 0, nj + jj(i, j, nb))),
                pl.BlockSpec((1, 1, tf), lambda i, j, be, nb: (be[blk(i, nb)], 0, jj(i, j, nb))),
                pl.BlockSpec((1, 1, tf), lambda i, j, be, nb: (be[blk(i, nb)], 0, nj + jj(i, j, nb))),
                pl.BlockSpec((1, tf, d), lambda i, j, be, nb: (be[blk(i, nb)], jj(i, j, nb), 0)),
                pl.BlockSpec((1, 1, d), lambda i, j, be, nb: (be[blk(i, nb)], 0, 0)),
            ],
            out_specs=pl.BlockSpec((ROW_BLOCK, d), lambda i, j, be, nb: (i, 0))),
        out_shape=jax.ShapeDtypeStruct((n_rows, d), F32),
        compiler_params=_cparams(("arbitrary", "arbitrary")),
        name="experts",
    )(block_expert, n_used, xs, w_gu, w_gu, b_gu3, b_gu3, w_down, b_d3)


def _combine_kernel(dest_ref, y_ref, h_ref, gate_ref, g_ref, b_ref, o_ref, buf_ref, sem, *, tt):
    i = pl.program_id(0)
    base = i * (tt * TOP_K)

    def issue(r, carry):
        for k in range(TOP_K):
            row = dest_ref[base + r * TOP_K + k]
            pltpu.make_async_copy(y_ref.at[pl.ds(row, 1)], buf_ref.at[k, pl.ds(r, 1)], sem).start()
        return carry

    lax.fori_loop(0, tt, issue, 0)

    def drain(r, carry):
        for k in range(TOP_K):
            pltpu.make_async_copy(y_ref.at[pl.ds(0, 1)], buf_ref.at[0, pl.ds(0, 1)], sem).wait()
        return carry

    lax.fori_loop(0, tt, drain, 0)
    gates = gate_ref[...]
    ffn = gates[:, 0:1] * buf_ref[0]
    for k in range(1, TOP_K):
        ffn = ffn + gates[:, k:k + 1] * buf_ref[k]
    o_ref[...] = _ln(DEEPNORM_ALPHA * h_ref[...] + ffn, g_ref[...], b_ref[...])


def _combine(y, dest, h1, gates, g, b):
    t, d = h1.shape
    tt = _pick(t, 128)
    kern = functools.partial(_combine_kernel, tt=tt)
    return pl.pallas_call(
        kern,
        grid_spec=pltpu.PrefetchScalarGridSpec(
            num_scalar_prefetch=1,
            grid=(t // tt,),
            in_specs=[pl.BlockSpec(memory_space=pl.ANY),
                      pl.BlockSpec((tt, d), lambda i, dst: (i, 0)),
                      pl.BlockSpec((tt, LANES), lambda i, dst: (i, 0)),
                      pl.BlockSpec((1, d), lambda i, dst: (0, 0)),
                      pl.BlockSpec((1, d), lambda i, dst: (0, 0))],
            out_specs=pl.BlockSpec((tt, d), lambda i, dst: (i, 0)),
            scratch_shapes=[pltpu.VMEM((TOP_K, tt, d), F32), pltpu.SemaphoreType.DMA(())]),
        out_shape=jax.ShapeDtypeStruct((t, d), F32),
        compiler_params=_cparams(("arbitrary",)),
        name="combine",
    )(dest, y, h1, gates, g.reshape(1, d), b.reshape(1, d))


def kernel(x, ln_in_g, ln_in_b, w_in, lam_re, lam_im, log_step, ssm_b_re, ssm_b_im, ssm_c_re, ssm_c_im, ssm_d, w_glu, b_glu, conv_w, a_log, dt_bias, dn_norm_w, w_proj_ssm, w_proj_dn, w_out, ln1_g, ln1_b, w_router, b_router, w_gate_up, b_gate_up, w_down, b_down, ln2_g, ln2_b):
    bsz, seq, d = x.shape
    t = bsz * seq
    sw = d // 2
    heads = d // DN_HEAD_DIM
    dn = heads * DN_HEAD_DIM
    groups = sw // SSM_GROUP_DIM
    n_state = lam_re.shape[-1]
    n_exp = w_router.shape[-1]
    assert w_in.shape[0] == DEPTH == 1
    assert seq % (CHUNK * 4) == 0 and sw % LANES == 0 and groups % 2 == 0 and 2 * heads <= LANES

    h, hb = _ln_in(x.reshape(t, d), ln_in_g, ln_in_b)

    wi = w_in[0]
    c_ab = sw + 4 * dn
    w_main = jnp.concatenate([wi[:, :c_ab], wi[:, c_ab + 2 * heads:]], axis=1).astype(BF16)
    w_ab = jnp.pad(wi[:, c_ab:c_ab + 2 * heads], ((0, 0), (0, LANES - 2 * heads))).astype(BF16)
    proj = _matmul(hb, w_main, F32, "proj_in")
    ab = _matmul(hb, w_ab, F32, "proj_ab")

    ops = _s5_operators(lam_re[0], lam_im[0], log_step[0], ssm_b_re[0], ssm_b_im[0],
                        ssm_c_re[0], ssm_c_im[0], ssm_d[0])
    y_s = _s5_scan(proj, ops, bsz, seq, sw, n_state)
    y_ssm = _ssm_post(y_s, w_glu[0].astype(BF16), b_glu[0].astype(F32))

    y_dn = _deltanet(proj, ab, conv_w[0], a_log[0], dt_bias[0], dn_norm_w[0], bsz, seq, heads,
                     off_q=sw // LANES)

    tn = _pick(sw, 1024)
    merged = _merge(y_ssm, y_dn, proj, (sw + 4 * dn) // tn, w_proj_ssm[0].astype(BF16),
                    w_proj_dn[0].astype(BF16))
    h1, h1p = _out_ln(merged, w_out[0].astype(BF16), h, ln1_g[0], ln1_b[0])

    idx, gates, rank, cnt = _router(h1, w_router[0], b_router[0])
    counts = cnt[0, :n_exp].astype(jnp.int32)
    padded = (counts + ROW_BLOCK - 1) // ROW_BLOCK * ROW_BLOCK
    pad_ends = jnp.cumsum(padded)
    pad_starts = pad_ends - padded
    dest = (pad_starts[idx[:, :TOP_K]] + rank[:, :TOP_K]).astype(jnp.int32).reshape(t * TOP_K)
    n_rows = t * TOP_K + n_exp * ROW_BLOCK
    nblk = n_rows // ROW_BLOCK
    blk_start = jnp.arange(nblk, dtype=jnp.int32) * ROW_BLOCK
    block_expert = jnp.minimum(jnp.sum((pad_ends[None, :] <= blk_start[:, None]).astype(jnp.int32), axis=1),
                               n_exp - 1).astype(jnp.int32)
    n_used = (pad_ends[-1:] // ROW_BLOCK).astype(jnp.int32)
    xs = _dispatch(h1p, dest, n_rows)
    y = _experts(xs, block_expert, n_used, w_gate_up[0].astype(BF16), b_gate_up[0],
                 w_down[0].astype(BF16), b_down[0])
    out = _combine(y, dest, h1, gates, ln2_g[0], ln2_b[0])
    return out.reshape(bsz, seq, d)
```

```python
import functools
import itertools
import math

import jax
import jax.numpy as jnp
from jax import lax
from jax.experimental import pallas as pl
from jax.experimental.pallas import tpu as pltpu

F32 = jnp.float32
BF16 = jnp.bfloat16
HI = lax.Precision.HIGHEST

LANES = 128
CHUNK = 64
LN_EPS = 1e-5
NORM_EPS = 1e-6
SSM_GROUP_DIM = 16
SSM_TC = 16
DN_HEAD_DIM = 128
CONV_WIDTH = 4
TOP_K = 4
SWIGLU_LIMIT = 7.0
SWIGLU_ALPHA = 1.702
DEPTH = 1
DEEPNORM_ALPHA = (2 * DEPTH) ** 0.25
ROW_BLOCK = 512
VMEM_LIMIT = 56 * 1024 * 1024


def _cparams(sem):
    return pltpu.CompilerParams(dimension_semantics=sem, vmem_limit_bytes=VMEM_LIMIT)


def _pick(n, pref):
    t = min(n, pref)
    while n % t:
        t //= 2
    return t


def _ln(x, g, b):
    mu = jnp.mean(x, axis=-1, keepdims=True)
    xc = x - mu
    var = jnp.mean(xc * xc, axis=-1, keepdims=True)
    return xc * lax.rsqrt(var + LN_EPS) * g + b


def _ln_in_kernel(x_ref, g_ref, b_ref, h_ref, hb_ref):
    h = _ln(x_ref[...], g_ref[...], b_ref[...])
    h_ref[...] = h
    hb_ref[...] = h.astype(BF16)


def _ln_in(x, g, b):
    t, d = x.shape
    tm = _pick(t, 512)
    return pl.pallas_call(
        _ln_in_kernel,
        grid=(t // tm,),
        in_specs=[pl.BlockSpec((tm, d), lambda i: (i, 0)),
                  pl.BlockSpec((1, d), lambda i: (0, 0)),
                  pl.BlockSpec((1, d), lambda i: (0, 0))],
        out_specs=[pl.BlockSpec((tm, d), lambda i: (i, 0)),
                   pl.BlockSpec((tm, d), lambda i: (i, 0))],
        out_shape=[jax.ShapeDtypeStruct((t, d), F32), jax.ShapeDtypeStruct((t, d), BF16)],
        compiler_params=_cparams(("arbitrary",)),
        name="ln_in",
    )(x, g.reshape(1, d), b.reshape(1, d))


def _mm_kernel(x_ref, w_ref, o_ref):
    o_ref[...] = jnp.dot(x_ref[...], w_ref[...], preferred_element_type=F32).astype(o_ref.dtype)


def _matmul(x, w, out_dtype, name):
    m, k = x.shape
    n = w.shape[1]
    tm = _pick(m, 1024)
    tn = _pick(n, 1024)
    return pl.pallas_call(
        _mm_kernel,
        grid=(n // tn, m // tm),
        in_specs=[pl.BlockSpec((tm, k), lambda j, i: (i, 0)),
                  pl.BlockSpec((k, tn), lambda j, i: (0, j))],
        out_specs=pl.BlockSpec((tm, tn), lambda j, i: (i, j)),
        out_shape=jax.ShapeDtypeStruct((m, n), out_dtype),
        compiler_params=_cparams(("arbitrary", "arbitrary")),
        name=name,
    )(x, w)


def _s5_operators(lam_re, lam_im, log_step, b_re, b_im, c_re, c_im, d_skip):
    g, p = lam_re.shape
    hd, tc = SSM_GROUP_DIM, SSM_TC
    lr, li = lam_re.astype(F32), lam_im.astype(F32)
    step = jnp.exp(log_step.astype(F32))[:, None]
    tau = jnp.arange(tc + 1, dtype=F32)[:, None, None]
    mag = jnp.exp(lr * step * tau)
    pr, pi = mag * jnp.cos(li * step * tau), mag * jnp.sin(li * step * tau)
    a_re, a_im = pr[1], pi[1]
    den = lr * lr + li * li
    nr, ni = a_re - 1.0, a_im
    f_re = (nr * lr + ni * li) / den
    f_im = (ni * lr - nr * li) / den
    br, bi = b_re.astype(F32), b_im.astype(F32)
    bb_re = f_re[..., None] * br - f_im[..., None] * bi
    bb_im = f_re[..., None] * bi + f_im[..., None] * br
    cr, ci = c_re.astype(F32), c_im.astype(F32)
    ca_re = cr[None] * pr[:, :, None, :] - ci[None] * pi[:, :, None, :]
    ca_im = cr[None] * pi[:, :, None, :] + ci[None] * pr[:, :, None, :]
    kk = (jnp.einsum('tgop,gpi->tgoi', ca_re[:tc], bb_re, precision=HI)
          - jnp.einsum('tgop,gpi->tgoi', ca_im[:tc], bb_im, precision=HI))
    kk = kk.at[0].add(d_skip.astype(F32)[:, :, None] * jnp.eye(hd, dtype=F32)[None])
    gs = LANES // hd
    ns = g // gs
    dup = LANES // p
    eye = jnp.eye(gs, dtype=F32)
    k5 = jnp.transpose(kk.reshape(tc, ns, gs, hd, hd), (1, 0, 2, 4, 3))
    bm = (k5[:, :, :, :, None, :] * eye[None, None, :, None, :, None]).reshape(ns, tc, LANES, LANES)
    dt = jnp.arange(tc)

    def rows(x_re, x_im):
        x = jnp.stack([x_re, x_im], axis=1).reshape(tc, 2, ns, gs * hd, p)
        x = jnp.transpose(x, (2, 0, 1, 3, 4))
        return jnp.concatenate([x] * dup, axis=-1)

    prr, pir = pr[tc - 1 - dt], pi[tc - 1 - dt]
    bc_re = prr[:, :, None, :] * jnp.transpose(bb_re, (0, 2, 1))[None] - pir[:, :, None, :] * jnp.transpose(bb_im, (0, 2, 1))[None]
    bc_im = prr[:, :, None, :] * jnp.transpose(bb_im, (0, 2, 1))[None] + pir[:, :, None, :] * jnp.transpose(bb_re, (0, 2, 1))[None]
    bcc = rows(bc_re, bc_im)
    cct = rows(ca_re[1:], -ca_im[1:])
    a16r = pr[tc].reshape(1, g * p)
    a16i = pi[tc].reshape(1, g * p)
    return (bm.astype(BF16), bcc.astype(BF16), cct.astype(BF16), a16r, a16i)


def _s5_kernel(x_ref, bm_ref, bcc_ref, cct_ref, ar_ref, ai_ref, y_ref,
               mf_ref, bcf_ref, ccf_ref, st_ref, cr_ref, ci_ref, *, ncb, sl, n_state):
    tc = SSM_TC

    @pl.when((pl.program_id(1) == 0) & (pl.program_id(2) == 0))
    def _():
        zero = jnp.zeros((LANES, LANES), BF16)
        for di in range(tc):
            for do in range(tc):
                mf_ref[di * LANES:(di + 1) * LANES, do * LANES:(do + 1) * LANES] = (
                    bm_ref[0, do - di] if do >= di else zero)
        r = lax.broadcasted_iota(jnp.int32, (LANES, sl), 0)
        c = lax.broadcasted_iota(jnp.int32, (LANES, sl), 1)
        same_group = (r // SSM_GROUP_DIM) == (c // n_state)

        def widen(blk):
            wide = jnp.concatenate([blk.astype(F32)] * (sl // LANES), axis=1)
            return jnp.where(same_group, wide, 0.0).astype(BF16)

        for dt in range(tc):
            for ri in range(2):
                bcf_ref[dt * LANES:(dt + 1) * LANES, ri * sl:(ri + 1) * sl] = widen(bcc_ref[0, dt, ri])
                ccf_ref[dt * LANES:(dt + 1) * LANES, ri * sl:(ri + 1) * sl] = widen(cct_ref[0, dt, ri])

    @pl.when(pl.program_id(2) == 0)
    def _():
        cr_ref[...] = jnp.zeros_like(cr_ref)
        ci_ref[...] = jnp.zeros_like(ci_ref)

    xs = jnp.concatenate([x_ref[pl.ds(dt, ncb, stride=SSM_TC), :] for dt in range(SSM_TC)],
                         axis=1).astype(BF16)
    st_ref[...] = jnp.dot(xs, bcf_ref[...], preferred_element_type=F32)
    ar = ar_ref[...]
    ai = ai_ref[...]

    def body(c, carry):
        sr, si = carry
        lr = st_ref[pl.ds(c, 1), :sl]
        li = st_ref[pl.ds(c, 1), sl:]
        st_ref[pl.ds(c, 1), :sl] = sr
        st_ref[pl.ds(c, 1), sl:] = si
        return ar * sr - ai * si + lr, ar * si + ai * sr + li

    sr, si = lax.fori_loop(0, ncb, body, (cr_ref[...], ci_ref[...]), unroll=8)
    cr_ref[...] = sr
    ci_ref[...] = si
    y = (jnp.dot(xs, mf_ref[...], preferred_element_type=F32)
         + _dot_nt(st_ref[...].astype(BF16), ccf_ref[...]))
    for dt in range(SSM_TC):
        y_ref[pl.ds(dt, ncb, stride=SSM_TC), :] = y[:, dt * LANES:(dt + 1) * LANES]


def _s5_scan(proj, ops, bsz, seq, sw, n_state):
    bm, bcc, cct, a16r, a16i = ops
    ns = sw // LANES
    sl = (LANES // SSM_GROUP_DIM) * n_state
    ncb = _pick(seq // SSM_TC, 128)
    rt = ncb * SSM_TC
    nt = seq // rt
    kw = SSM_TC * LANES
    kern = functools.partial(_s5_kernel, ncb=ncb, sl=sl, n_state=n_state)
    return pl.pallas_call(
        kern,
        grid=(ns, bsz, nt),
        in_specs=[pl.BlockSpec((rt, LANES), lambda s, b, i: (b * nt + i, s)),
                  pl.BlockSpec((1, SSM_TC, LANES, LANES), lambda s, b, i: (s, 0, 0, 0)),
                  pl.BlockSpec((1, SSM_TC, 2, LANES, LANES), lambda s, b, i: (s, 0, 0, 0, 0)),
                  pl.BlockSpec((1, SSM_TC, 2, LANES, LANES), lambda s, b, i: (s, 0, 0, 0, 0)),
                  pl.BlockSpec((1, sl), lambda s, b, i: (0, s)),
                  pl.BlockSpec((1, sl), lambda s, b, i: (0, s))],
        out_specs=pl.BlockSpec((rt, LANES), lambda s, b, i: (b * nt + i, s)),
        out_shape=jax.ShapeDtypeStruct((bsz * seq, sw), F32),
        scratch_shapes=[pltpu.VMEM((kw, kw), BF16), pltpu.VMEM((kw, 2 * sl), BF16), pltpu.VMEM((kw, 2 * sl), BF16),
                        pltpu.VMEM((ncb, 2 * sl), F32), pltpu.VMEM((1, sl), F32), pltpu.VMEM((1, sl), F32)],
        compiler_params=_cparams(("arbitrary", "arbitrary", "arbitrary")),
        name="s5_scan",
    )(proj, bm, bcc, cct, a16r, a16i)


def _ssm_post_kernel(y_ref, w_ref, b_ref, o_ref):
    y = y_ref[...]
    yg = 0.5 * y * (1.0 + lax.erf(y * (1.0 / math.sqrt(2.0))))
    s = jnp.dot(yg.astype(BF16), w_ref[...], preferred_element_type=F32) + b_ref[...]
    o_ref[...] = (yg * jax.nn.sigmoid(s)).astype(o_ref.dtype)


def _ssm_post(y, w_glu, b_glu):
    t, n = y.shape
    tm = _pick(t, 512)
    return pl.pallas_call(
        _ssm_post_kernel,
        grid=(t // tm,),
        in_specs=[pl.BlockSpec((tm, n), lambda i: (i, 0)),
                  pl.BlockSpec((n, n), lambda i: (0, 0)),
                  pl.BlockSpec((1, n), lambda i: (0, 0))],
        out_specs=pl.BlockSpec((tm, n), lambda i: (i, 0)),
        out_shape=jax.ShapeDtypeStruct((t, n), BF16),
        compiler_params=_cparams(("arbitrary",)),
        name="ssm_post",
    )(y, w_glu, b_glu.reshape(1, n))


def _dot_nt(a, b):
    return lax.dot_general(a, b, (((1,), (1,)), ((), ())), preferred_element_type=F32)


def _dot_tn(a, b):
    return lax.dot_general(a, b, (((0,), (0,)), ((), ())), preferred_element_type=F32)


def _dot_hi(a, b):
    return jnp.dot(a, b, preferred_element_type=F32, precision=HI)


def _silu(x):
    return x * jax.nn.sigmoid(x)


DN_HALO = 8


def _dn_kernel(q_ref, k_ref, v_ref, z_ref, ab_ref, cq_ref, ck_ref, cv_ref, al_ref, dtb_ref, nw_ref,
               o_ref, *scratch, tt, heads, hp):
    s_refs, xx_refs = scratch[:hp], scratch[hp:]

    @pl.when(pl.program_id(2) == 0)
    def _():
        for s_ref, xx_ref in zip(s_refs, xx_refs):
            s_ref[...] = jnp.zeros_like(s_ref)
            xx_ref[:, 0:DN_HALO, :] = jnp.zeros((3, DN_HALO, DN_HEAD_DIM), F32)

    gens = [_dn_head(hh, pl.program_id(1) * hp + hh, q_ref, k_ref, v_ref, z_ref, ab_ref, cq_ref, ck_ref, cv_ref,
                     al_ref, dtb_ref, nw_ref, o_ref, s_refs[hh], xx_refs[hh], tt=tt, heads=heads)
            for hh in range(hp)]
    for _ in itertools.zip_longest(*gens):
        pass


def _dn_head(hh, h, q_ref, k_ref, v_ref, z_ref, ab_ref, cq_ref, ck_ref, cv_ref, al_ref, dtb_ref, nw_ref,
             o_ref, s_ref, xx_ref, *, tt, heads):
    hal = DN_HALO
    ls = slice(hh * DN_HEAD_DIM, (hh + 1) * DN_HEAD_DIM)

    def conv(idx, x_ref, cw_ref):
        x = x_ref[:, ls]
        xx_ref[idx, hal:, :] = x
        w = cw_ref[:, ls]
        acc = w[CONV_WIDTH - 1:CONV_WIDTH, :] * x
        for j in range(CONV_WIDTH - 1):
            off = hal - (CONV_WIDTH - 1) + j
            acc = acc + w[j:j + 1, :] * xx_ref[idx, off:off + tt, :]
        xx_ref[idx, 0:hal, :] = x[tt - hal:, :]
        return _silu(acc)

    qc = conv(0, q_ref, cq_ref)
    kc = conv(1, k_ref, ck_ref)
    vc = conv(2, v_ref, cv_ref)
    qn = qc * lax.rsqrt(jnp.sum(qc * qc, axis=-1, keepdims=True) + NORM_EPS) * (DN_HEAD_DIM ** -0.5)
    kn = kc * lax.rsqrt(jnp.sum(kc * kc, axis=-1, keepdims=True) + NORM_EPS)

    ab = ab_ref[...]
    lane = lax.broadcasted_iota(jnp.int32, ab.shape, 1)
    xa = ab + dtb_ref[...]
    sp = jnp.maximum(xa, 0.0) + jnp.log1p(jnp.exp(-jnp.abs(xa)))
    g_all = -jnp.exp(al_ref[...]) * sp
    g_col = jnp.sum(jnp.where(lane == h, g_all, 0.0), axis=-1, keepdims=True)
    beta_col = jnp.sum(jnp.where(lane == h + heads, jax.nn.sigmoid(ab), 0.0), axis=-1, keepdims=True)

    nch = tt // CHUNK
    row = lax.broadcasted_iota(jnp.int32, (tt, DN_HEAD_DIM), 0)
    pos = row & (CHUNK - 1)
    gc = jnp.broadcast_to(g_col, (tt, DN_HEAD_DIM))
    sh = 1
    while sh < CHUNK:
        gc = gc + jnp.where(pos >= sh, pltpu.roll(gc, sh, axis=0), 0.0)
        sh *= 2
    gc_row = jnp.transpose(gc)[0:1, :]
    egc = jnp.exp(gc)

    ri = lax.broadcasted_iota(jnp.int32, (tt, tt), 0)
    ci = lax.broadcasted_iota(jnp.int32, (tt, tt), 1)
    same = (ri // CHUNK) == (ci // CHUNK)
    causal = same & (ri >= ci)
    strict = same & (ri > ci)
    gc_wide = jnp.concatenate([gc] * (tt // DN_HEAD_DIM), axis=1)
    decay = jnp.where(causal, jnp.exp(jnp.where(causal, gc_wide - gc_row, 0.0)), 0.0)
    kb = kn * beta_col
    knb = kn.astype(BF16)
    yield
    a_raw = _dot_nt(kb.astype(BF16), knb)
    qk_raw = _dot_nt(qn.astype(BF16), knb)
    yield
    a_bd = jnp.where(strict, a_raw * decay, 0.0)
    qk_bd = jnp.where(causal, qk_raw * decay, 0.0).astype(BF16)

    def fold(m):
        out = m[0:CHUNK]
        for c in range(1, nch):
            out = out + m[c * CHUNK:(c + 1) * CHUNK]
        return out

    def spread(m):
        return jnp.where(same, jnp.concatenate([m] * nch, axis=0), 0.0)

    r64 = lax.broadcasted_iota(jnp.int32, (CHUNK, tt), 0)
    c64 = lax.broadcasted_iota(jnp.int32, (CHUNK, tt), 1)
    eye_cat = (r64 == (c64 & (CHUNK - 1))).astype(F32)
    pw_cat = fold(a_bd)
    inv_cat = eye_cat - pw_cat
    pw_bd = a_bd.astype(BF16)
    for _ in range(5):
        pw_cat = jnp.dot(pw_cat.astype(BF16), pw_bd, preferred_element_type=F32)
        yield
        pw_bd = spread(pw_cat).astype(BF16)
        inv_add = jnp.dot(inv_cat.astype(BF16), pw_bd, preferred_element_type=F32)
        yield
        inv_cat = inv_cat + inv_add
    inv_bd = spread(inv_cat).astype(BF16)
    rhs = jnp.concatenate([vc * beta_col, kb * egc], axis=1).astype(BF16)
    sol = jnp.dot(inv_bd, rhs, preferred_element_type=F32)
    yield
    u_all, w_all = sol[:, :DN_HEAD_DIM], sol[:, DN_HEAD_DIM:]
    qe = qn * egc

    outs = []
    zblk = jnp.zeros((CHUNK, DN_HEAD_DIM), BF16)
    s = s_ref[...]
    for c in range(nch):
        sl = slice(c * CHUNK, (c + 1) * CHUNK)
        wq = jnp.concatenate([w_all[sl], qe[sl]], axis=0).astype(BF16)
        ws = jnp.dot(wq, s.astype(BF16), preferred_element_type=F32)
        yield
        v_new = u_all[sl] - ws[:CHUNK]
        vnb = v_new.astype(BF16)
        v_pad = jnp.concatenate([zblk] * c + [vnb] + [zblk] * (nch - 1 - c), axis=0)
        gc_c = gc[sl]
        g_last = gc_c[CHUNK - 1:CHUNK, :]
        k_dec = (kn[sl] * jnp.exp(g_last - gc_c)).astype(BF16)
        s_add = _dot_tn(k_dec, vnb)
        o_add = jnp.dot(qk_bd[sl], v_pad, preferred_element_type=F32)
        yield
        s = s * jnp.exp(g_last) + s_add
        outs.append(ws[CHUNK:] + o_add)
    s_ref[...] = s
    o = jnp.concatenate(outs, axis=0)
    o = o * lax.rsqrt(jnp.mean(o * o, axis=-1, keepdims=True) + NORM_EPS) * nw_ref[...]
    o_ref[:, ls] = (o * _silu(z_ref[:, ls])).astype(o_ref.dtype)


def _deltanet(proj, ab, conv_w, a_log, dt_bias, norm_w, bsz, seq, heads, off_q):
    t = bsz * seq
    tt = _pick(seq, 256)
    nt = seq // tt
    hd = DN_HEAD_DIM
    cw = conv_w.reshape(CONV_WIDTH, 3 * heads * hd).astype(F32)
    pad = LANES - heads
    al = jnp.pad(a_log.astype(F32), (0, pad)).reshape(1, LANES)
    dtb = jnp.pad(dt_bias.astype(F32), (0, pad)).reshape(1, LANES)
    nw = norm_w.astype(F32).reshape(1, hd)

    hp = math.gcd(math.gcd(heads, off_q), 4)
    hw = hp * hd
    oq, nh = off_q // hp, heads // hp

    def act(o):
        return pl.BlockSpec((tt, hw), lambda b, h, i: (b * nt + i, oq + o * nh + h))

    def cws(o):
        return pl.BlockSpec((CONV_WIDTH, hw), lambda b, h, i: (0, o * nh + h))

    row = pl.BlockSpec((1, LANES), lambda b, h, i: (0, 0))
    kern = functools.partial(_dn_kernel, tt=tt, heads=heads, hp=hp)
    return pl.pallas_call(
        kern,
        grid=(bsz, nh, nt),
        in_specs=[act(0), act(1), act(2), act(3),
                  pl.BlockSpec((tt, LANES), lambda b, h, i: (b * nt + i, 0)),
                  cws(0), cws(1), cws(2), row, row, row],
        out_specs=pl.BlockSpec((tt, hw), lambda b, h, i: (b * nt + i, h)),
        out_shape=jax.ShapeDtypeStruct((t, heads * hd), BF16),
        scratch_shapes=([pltpu.VMEM((hd, hd), F32)] * hp
                        + [pltpu.VMEM((3, tt + DN_HALO, hd), F32)] * hp),
        compiler_params=_cparams(("arbitrary", "arbitrary", "arbitrary")),
        name="deltanet",
    )(proj, proj, proj, proj, ab, cw, cw, cw, al, dtb, nw)


def _merge_kernel(ys_ref, yd_ref, gs_ref, gd_ref, ws_ref, wd_ref, o_ref):
    ps = jnp.dot(ys_ref[...], ws_ref[...], preferred_element_type=F32)
    pd = jnp.dot(yd_ref[...], wd_ref[...], preferred_element_type=F32)
    o_ref[...] = (jax.nn.sigmoid(gs_ref[...]) * ps + jax.nn.sigmoid(gd_ref[...]) * pd).astype(o_ref.dtype)


def _merge(ys, yd, proj, off_gs, w_ps, w_pd):
    t, ns = ys.shape
    d = w_ps.shape[1]
    tm = _pick(t, 512)
    tn = _pick(ns, 1024)
    nb = d // tn
    return pl.pallas_call(
        _merge_kernel,
        grid=(nb, t // tm),
        in_specs=[pl.BlockSpec((tm, ns), lambda j, i: (i, 0)),
                  pl.BlockSpec((tm, d), lambda j, i: (i, 0)),
                  pl.BlockSpec((tm, tn), lambda j, i: (i, off_gs + j)),
                  pl.BlockSpec((tm, tn), lambda j, i: (i, off_gs + nb + j)),
                  pl.BlockSpec((ns, tn), lambda j, i: (0, j)),
                  pl.BlockSpec((d, tn), lambda j, i: (0, j))],
        out_specs=pl.BlockSpec((tm, tn), lambda j, i: (i, j)),
        out_shape=jax.ShapeDtypeStruct((t, d), BF16),
        compiler_params=_cparams(("arbitrary", "arbitrary")),
        name="merge",
    )(ys, yd, proj, proj, w_ps, w_pd)


def _pack_bf16_pair(x):
    half = x.shape[1] // 2
    lo = pltpu.bitcast(x[:, :half].astype(BF16).astype(F32), jnp.uint32)
    hi = pltpu.bitcast(x[:, half:].astype(BF16).astype(F32), jnp.uint32)
    return (lo >> 16) | (hi & jnp.uint32(0xFFFF0000))


def _unpack_bf16_pair(u):
    lo = pltpu.bitcast(u << 16, F32).astype(BF16)
    hi = pltpu.bitcast(u & jnp.uint32(0xFFFF0000), F32).astype(BF16)
    return lo, hi


def _out_ln_kernel(m_ref, w_ref, h_ref, g_ref, b_ref, h1_ref, hp_ref):
    mix = jnp.dot(m_ref[...], w_ref[...], preferred_element_type=F32)
    h1 = _ln(DEEPNORM_ALPHA * h_ref[...] + mix, g_ref[...], b_ref[...])
    h1_ref[...] = h1
    hp_ref[...] = _pack_bf16_pair(h1)


def _out_ln(merged, w_out, h, g, b):
    t, d = h.shape
    tm = _pick(t, 512)
    return pl.pallas_call(
        _out_ln_kernel,
        grid=(t // tm,),
        in_specs=[pl.BlockSpec((tm, d), lambda i: (i, 0)),
                  pl.BlockSpec((d, d), lambda i: (0, 0)),
                  pl.BlockSpec((tm, d), lambda i: (i, 0)),
                  pl.BlockSpec((1, d), lambda i: (0, 0)),
                  pl.BlockSpec((1, d), lambda i: (0, 0))],
        out_specs=[pl.BlockSpec((tm, d), lambda i: (i, 0)),
                   pl.BlockSpec((tm, d // 2), lambda i: (i, 0))],
        out_shape=[jax.ShapeDtypeStruct((t, d), F32), jax.ShapeDtypeStruct((t, d // 2), jnp.uint32)],
        compiler_params=_cparams(("arbitrary",)),
        name="out_ln1",
    )(merged, w_out, h, g.reshape(1, d), b.reshape(1, d))


def _router_kernel(h_ref, w_ref, b_ref, idx_ref, gate_ref, rank_ref, cnt_ref, base_ref, *, n_exp, tm):
    i = pl.program_id(0)

    @pl.when(i == 0)
    def _():
        base_ref[...] = jnp.zeros_like(base_ref)

    logits = _dot_hi(h_ref[...], w_ref[...]) + b_ref[...]
    lane = lax.broadcasted_iota(jnp.int32, logits.shape, 1)
    vals = jnp.where(lane < n_exp, logits, -jnp.inf)
    sels, tops, ams = [], [], []
    for _ in range(TOP_K):
        m = jnp.max(vals, axis=-1, keepdims=True)
        am = jnp.min(jnp.where(vals == m, lane, LANES), axis=-1, keepdims=True)
        sel = lane == am
        vals = jnp.where(sel, -jnp.inf, vals)
        sels.append(sel)
        tops.append(m)
        ams.append(am)
    es = [jnp.exp(v - tops[0]) for v in tops]
    den = es[0] + es[1] + es[2] + es[3]
    onehot = sels[0] | sels[1] | sels[2] | sels[3]
    mt = onehot.astype(BF16)
    ri = lax.broadcasted_iota(jnp.int32, (tm, tm), 0)
    ci = lax.broadcasted_iota(jnp.int32, (tm, tm), 1)
    before = (ri > ci).astype(BF16)
    prior = jnp.dot(before, mt, preferred_element_type=F32) + base_ref[...]
    idx_o = jnp.zeros(logits.shape, jnp.int32)
    gate_o = jnp.zeros(logits.shape, F32)
    rank_o = jnp.zeros(logits.shape, jnp.int32)
    for k in range(TOP_K):
        rk = jnp.sum(jnp.where(sels[k], prior, 0.0), axis=-1, keepdims=True).astype(jnp.int32)
        idx_o = jnp.where(lane == k, ams[k], idx_o)
        gate_o = jnp.where(lane == k, es[k] / den, gate_o)
        rank_o = jnp.where(lane == k, rk, rank_o)
    idx_ref[...] = idx_o
    gate_ref[...] = gate_o
    rank_ref[...] = rank_o
    base_ref[...] += jnp.sum(onehot.astype(F32), axis=0, keepdims=True)
    cnt_ref[...] = base_ref[...]


def _router(h1, w_router, b_router):
    t, d = h1.shape
    n_exp = w_router.shape[1]
    tm = _pick(t, 256)
    w = jnp.pad(w_router.astype(F32), ((0, 0), (0, LANES - n_exp)))
    b = jnp.pad(b_router.astype(F32), (0, LANES - n_exp)).reshape(1, LANES)
    tok = pl.BlockSpec((tm, LANES), lambda i: (i, 0))
    kern = functools.partial(_router_kernel, n_exp=n_exp, tm=tm)
    return pl.pallas_call(
        kern,
        grid=(t // tm,),
        in_specs=[pl.BlockSpec((tm, d), lambda i: (i, 0)),
                  pl.BlockSpec((d, LANES), lambda i: (0, 0)),
                  pl.BlockSpec((1, LANES), lambda i: (0, 0))],
        out_specs=[tok, tok, tok, pl.BlockSpec((1, LANES), lambda i: (0, 0))],
        out_shape=[jax.ShapeDtypeStruct((t, LANES), jnp.int32), jax.ShapeDtypeStruct((t, LANES), F32),
                   jax.ShapeDtypeStruct((t, LANES), jnp.int32), jax.ShapeDtypeStruct((1, LANES), F32)],
        scratch_shapes=[pltpu.VMEM((1, LANES), F32)],
        compiler_params=_cparams(("arbitrary",)),
        name="router",
    )(h1, w, b)


def _dispatch_kernel(dest_ref, x_ref, xs_in_ref, xs_ref, sem, *, tt):
    del xs_in_ref
    i = pl.program_id(0)
    base = i * (tt * TOP_K)

    def issue(r, carry):
        for k in range(TOP_K):
            row = dest_ref[base + r * TOP_K + k]
            pltpu.make_async_copy(x_ref.at[pl.ds(r, 1)], xs_ref.at[pl.ds(row, 1)], sem).start()
        return carry

    lax.fori_loop(0, tt, issue, 0, unroll=4)

    for k in range(TOP_K):
        pltpu.make_async_copy(x_ref, xs_ref.at[pl.ds(0, tt)], sem).wait()


def _dispatch(xp, dest, n_rows):
    t, dh = xp.shape
    tt = _pick(t, 256)
    xs0 = jnp.zeros((n_rows, dh), jnp.uint32)
    kern = functools.partial(_dispatch_kernel, tt=tt)
    return pl.pallas_call(
        kern,
        grid_spec=pltpu.PrefetchScalarGridSpec(
            num_scalar_prefetch=1,
            grid=(t // tt,),
            in_specs=[pl.BlockSpec((tt, dh), lambda i, dst: (i, 0)),
                      pl.BlockSpec(memory_space=pl.ANY)],
            out_specs=pl.BlockSpec(memory_space=pl.ANY),
            scratch_shapes=[pltpu.SemaphoreType.DMA(())]),
        out_shape=jax.ShapeDtypeStruct((n_rows, dh), jnp.uint32),
        input_output_aliases={2: 0},
        compiler_params=_cparams(("arbitrary",)),
        name="dispatch",
    )(dest, xp, xs0)


def _expert_kernel(be_ref, nb_ref, x_ref, wg_ref, wu_ref, bg_ref, bu_ref, wd_ref, bd_ref, o_ref, *, half):
    i = pl.program_id(0)
    j = pl.program_id(1)

    @pl.when(i < nb_ref[0])
    def _():
        lo, hi = _unpack_bf16_pair(x_ref[...])
        gate = (jnp.dot(lo, wg_ref[0, :half, :], preferred_element_type=F32)
                + jnp.dot(hi, wg_ref[0, half:, :], preferred_element_type=F32) + bg_ref[0])
        up = (jnp.dot(lo, wu_ref[0, :half, :], preferred_element_type=F32)
              + jnp.dot(hi, wu_ref[0, half:, :], preferred_element_type=F32) + bu_ref[0])
        gate = jnp.minimum(gate, SWIGLU_LIMIT)
        up = jnp.clip(up, -SWIGLU_LIMIT, SWIGLU_LIMIT)
        act = gate * jax.nn.sigmoid(SWIGLU_ALPHA * gate) * (up + 1.0)
        part = jnp.dot(act.astype(BF16), wd_ref[0], preferred_element_type=F32)

        @pl.when(j == 0)
        def _():
            o_ref[...] = part + bd_ref[0]

        @pl.when(j > 0)
        def _():
            o_ref[...] += part

    @pl.when((i >= nb_ref[0]) & (j == 0))
    def _():
        o_ref[...] = jnp.zeros_like(o_ref)


def _experts(xs, block_expert, n_used, w_gu, b_gu, w_down, b_down):
    n_rows, dh = xs.shape
    n_exp, d, ff2 = w_gu.shape
    ff = ff2 // 2
    tf = _pick(ff, 1024)
    nj = ff // tf
    nblk = n_rows // ROW_BLOCK
    b_gu3 = b_gu.astype(F32).reshape(n_exp, 1, ff2)
    b_d3 = b_down.astype(F32).reshape(n_exp, 1, d)

    def blk(i, nb):
        return jnp.maximum(jnp.minimum(i, nb[0] - 1), 0)

    def jj(i, j, nb):
        return jnp.where(i < nb[0], j, nj - 1)

    kern = functools.partial(_expert_kernel, half=d // 2)
    return pl.pallas_call(
        kern,
        grid_spec=pltpu.PrefetchScalarGridSpec(
            num_scalar_prefetch=2,
            grid=(nblk, nj),
            in_specs=[
                pl.BlockSpec((ROW_BLOCK, dh), lambda i, j, be, nb: (blk(i, nb), 0)),
                pl.BlockSpec((1, d, tf), lambda i, j, be, nb: (be[blk(i, nb)], 0, jj(i, j, nb))),
                pl.BlockSpec((1, d, tf), lambda i, j, be, nb: (be[blk(i, nb)], 0, nj + jj(i, j, nb))),
                pl.BlockSpec((1, 1, tf), lambda i, j, be, nb: (be[blk(i, nb)], 0, jj(i, j, nb))),
                pl.BlockSpec((1, 1, tf), lambda i, j, be, nb: (be[blk(i, nb)], 0, nj + jj(i, j, nb))),
                pl.BlockSpec((1, tf, d), lambda i, j, be, nb: (be[blk(i, nb)], jj(i, j, nb), 0)),
                pl.BlockSpec((1, 1, d), lambda i, j, be, nb: (be[blk(i, nb)], 0, 0)),
            ],
            out_specs=pl.BlockSpec((ROW_BLOCK, d), lambda i, j, be, nb: (i, 0))),
        out_shape=jax.ShapeDtypeStruct((n_rows, d), F32),
        compiler_params=_cparams(("arbitrary", "arbitrary")),
        name="experts",
    )(block_expert, n_used, xs, w_gu, w_gu, b_gu3, b_gu3, w_down, b_d3)


def _combine_kernel(dest_ref, y_ref, h_ref, gate_ref, g_ref, b_ref, o_ref, buf_ref, sem, *, tt):
    i = pl.program_id(0)
    base = i * (tt * TOP_K)

    def issue(r, carry):
        for k in range(TOP_K):
            row = dest_ref[base + r * TOP_K + k]
            pltpu.make_async_copy(y_ref.at[pl.ds(row, 1)], buf_ref.at[k, pl.ds(r, 1)], sem).start()
        return carry

    lax.fori_loop(0, tt, issue, 0, unroll=4)

    for k in range(TOP_K):
        pltpu.make_async_copy(y_ref.at[pl.ds(0, tt)], buf_ref.at[k], sem).wait()
    gates = gate_ref[...]
    ffn = gates[:, 0:1] * buf_ref[0]
    for k in range(1, TOP_K):
        ffn = ffn + gates[:, k:k + 1] * buf_ref[k]
    o_ref[...] = _ln(DEEPNORM_ALPHA * h_ref[...] + ffn, g_ref[...], b_ref[...])


def _combine(y, dest, h1, gates, g, b):
    t, d = h1.shape
    tt = _pick(t, 128)
    kern = functools.partial(_combine_kernel, tt=tt)
    return pl.pallas_call(
        kern,
        grid_spec=pltpu.PrefetchScalarGridSpec(
            num_scalar_prefetch=1,
            grid=(t // tt,),
            in_specs=[pl.BlockSpec(memory_space=pl.ANY),
                      pl.BlockSpec((tt, d), lambda i, dst: (i, 0)),
                      pl.BlockSpec((tt, LANES), lambda i, dst: (i, 0)),
                      pl.BlockSpec((1, d), lambda i, dst: (0, 0)),
                      pl.BlockSpec((1, d), lambda i, dst: (0, 0))],
            out_specs=pl.BlockSpec((tt, d), lambda i, dst: (i, 0)),
            scratch_shapes=[pltpu.VMEM((TOP_K, tt, d), F32), pltpu.SemaphoreType.DMA(())]),
        out_shape=jax.ShapeDtypeStruct((t, d), F32),
        compiler_params=_cparams(("arbitrary",)),
        name="combine",
    )(dest, y, h1, gates, g.reshape(1, d), b.reshape(1, d))


def kernel(x, ln_in_g, ln_in_b, w_in, lam_re, lam_im, log_step, ssm_b_re, ssm_b_im, ssm_c_re, ssm_c_im, ssm_d, w_glu, b_glu, conv_w, a_log, dt_bias, dn_norm_w, w_proj_ssm, w_proj_dn, w_out, ln1_g, ln1_b, w_router, b_router, w_gate_up, b_gate_up, w_down, b_down, ln2_g, ln2_b):
    bsz, seq, d = x.shape
    t = bsz * seq
    sw = d // 2
    heads = d // DN_HEAD_DIM
    dn = heads * DN_HEAD_DIM
    groups = sw // SSM_GROUP_DIM
    n_state = lam_re.shape[-1]
    n_exp = w_router.shape[-1]
    assert w_in.shape[0] == DEPTH == 1
    assert seq % (CHUNK * 4) == 0 and sw % LANES == 0 and groups % 2 == 0 and 2 * heads <= LANES

    h, hb = _ln_in(x.reshape(t, d), ln_in_g, ln_in_b)

    wi = w_in[0]
    c_ab = sw + 4 * dn
    w_main = jnp.concatenate([wi[:, :c_ab], wi[:, c_ab + 2 * heads:]], axis=1).astype(BF16)
    w_ab = jnp.pad(wi[:, c_ab:c_ab + 2 * heads], ((0, 0), (0, LANES - 2 * heads))).astype(BF16)
    proj = _matmul(hb, w_main, F32, "proj_in")
    ab = _matmul(hb, w_ab, F32, "proj_ab")

    ops = _s5_operators(lam_re[0], lam_im[0], log_step[0], ssm_b_re[0], ssm_b_im[0],
                        ssm_c_re[0], ssm_c_im[0], ssm_d[0])
    y_s = _s5_scan(proj, ops, bsz, seq, sw, n_state)
    y_ssm = _ssm_post(y_s, w_glu[0].astype(BF16), b_glu[0].astype(F32))

    y_dn = _deltanet(proj, ab, conv_w[0], a_log[0], dt_bias[0], dn_norm_w[0], bsz, seq, heads,
                     off_q=sw // LANES)

    tn = _pick(sw, 1024)
    merged = _merge(y_ssm, y_dn, proj, (sw + 4 * dn) // tn, w_proj_ssm[0].astype(BF16),
                    w_proj_dn[0].astype(BF16))
    h1, h1p = _out_ln(merged, w_out[0].astype(BF16), h, ln1_g[0], ln1_b[0])

    idx, gates, rank, cnt = _router(h1, w_router[0], b_router[0])
    counts = cnt[0, :n_exp].astype(jnp.int32)
    padded = (counts + ROW_BLOCK - 1) // ROW_BLOCK * ROW_BLOCK
    pad_ends = jnp.cumsum(padded)
    pad_starts = pad_ends - padded
    dest = (pad_starts[idx[:, :TOP_K]] + rank[:, :TOP_K]).astype(jnp.int32).reshape(t * TOP_K)
    n_rows = t * TOP_K + n_exp * ROW_BLOCK
    nblk = n_rows // ROW_BLOCK
    blk_start = jnp.arange(nblk, dtype=jnp.int32) * ROW_BLOCK
    block_expert = jnp.minimum(jnp.sum((pad_ends[None, :] <= blk_start[:, None]).astype(jnp.int32), axis=1),
                               n_exp - 1).astype(jnp.int32)
    n_used = (pad_ends[-1:] // ROW_BLOCK).astype(jnp.int32)
    xs = _dispatch(h1p, dest, n_rows)
    y = _experts(xs, block_expert, n_used, w_gate_up[0].astype(BF16), b_gate_up[0],
                 w_down[0].astype(BF16), b_down[0])
    out = _combine(y, dest, h1, gates, ln2_g[0], ln2_b[0])
    return out.reshape(bsz, seq, d)
```

```python
import functools
import itertools
import math

import jax
import jax.numpy as jnp
from jax import lax
from jax.experimental import pallas as pl
from jax.experimental.pallas import tpu as pltpu

F32 = jnp.float32
BF16 = jnp.bfloat16
HI = lax.Precision.HIGHEST

LANES = 128
CHUNK = 64
LN_EPS = 1e-5
NORM_EPS = 1e-6
SSM_GROUP_DIM = 16
SSM_TC = 16
DN_HEAD_DIM = 128
CONV_WIDTH = 4
TOP_K = 4
SWIGLU_LIMIT = 7.0
SWIGLU_ALPHA = 1.702
DEPTH = 1
DEEPNORM_ALPHA = (2 * DEPTH) ** 0.25
ROW_BLOCK = 512
VMEM_LIMIT = 56 * 1024 * 1024


def _cparams(sem):
    return pltpu.CompilerParams(dimension_semantics=sem, vmem_limit_bytes=VMEM_LIMIT)


def _pick(n, pref):
    t = min(n, pref)
    while n % t:
        t //= 2
    return t


def _ln(x, g, b):
    mu = jnp.mean(x, axis=-1, keepdims=True)
    xc = x - mu
    var = jnp.mean(xc * xc, axis=-1, keepdims=True)
    return xc * lax.rsqrt(var + LN_EPS) * g + b


def _ln_in_kernel(x_ref, g_ref, b_ref, h_ref, hb_ref):
    h = _ln(x_ref[...], g_ref[...], b_ref[...])
    h_ref[...] = h
    hb_ref[...] = h.astype(BF16)


def _ln_in(x, g, b):
    t, d = x.shape
    tm = _pick(t, 512)
    return pl.pallas_call(
        _ln_in_kernel,
        grid=(t // tm,),
        in_specs=[pl.BlockSpec((tm, d), lambda i: (i, 0)),
                  pl.BlockSpec((1, d), lambda i: (0, 0)),
                  pl.BlockSpec((1, d), lambda i: (0, 0))],
        out_specs=[pl.BlockSpec((tm, d), lambda i: (i, 0)),
                   pl.BlockSpec((tm, d), lambda i: (i, 0))],
        out_shape=[jax.ShapeDtypeStruct((t, d), F32), jax.ShapeDtypeStruct((t, d), BF16)],
        compiler_params=_cparams(("arbitrary",)),
        name="ln_in",
    )(x, g.reshape(1, d), b.reshape(1, d))


def _mm_kernel(x_ref, w_ref, o_ref):
    o_ref[...] = jnp.dot(x_ref[...], w_ref[...], preferred_element_type=F32).astype(o_ref.dtype)


def _matmul(x, w, out_dtype, name):
    m, k = x.shape
    n = w.shape[1]
    tm = _pick(m, 1024)
    tn = _pick(n, 1024)
    return pl.pallas_call(
        _mm_kernel,
        grid=(n // tn, m // tm),
        in_specs=[pl.BlockSpec((tm, k), lambda j, i: (i, 0)),
                  pl.BlockSpec((k, tn), lambda j, i: (0, j))],
        out_specs=pl.BlockSpec((tm, tn), lambda j, i: (i, j)),
        out_shape=jax.ShapeDtypeStruct((m, n), out_dtype),
        compiler_params=_cparams(("arbitrary", "arbitrary")),
        name=name,
    )(x, w)


def _s5_operators(lam_re, lam_im, log_step, b_re, b_im, c_re, c_im, d_skip):
    g, p = lam_re.shape
    hd, tc = SSM_GROUP_DIM, SSM_TC
    lr, li = lam_re.astype(F32), lam_im.astype(F32)
    step = jnp.exp(log_step.astype(F32))[:, None]
    tau = jnp.arange(tc + 1, dtype=F32)[:, None, None]
    mag = jnp.exp(lr * step * tau)
    pr, pi = mag * jnp.cos(li * step * tau), mag * jnp.sin(li * step * tau)
    a_re, a_im = pr[1], pi[1]
    den = lr * lr + li * li
    nr, ni = a_re - 1.0, a_im
    f_re = (nr * lr + ni * li) / den
    f_im = (ni * lr - nr * li) / den
    br, bi = b_re.astype(F32), b_im.astype(F32)
    bb_re = f_re[..., None] * br - f_im[..., None] * bi
    bb_im = f_re[..., None] * bi + f_im[..., None] * br
    cr, ci = c_re.astype(F32), c_im.astype(F32)
    ca_re = cr[None] * pr[:, :, None, :] - ci[None] * pi[:, :, None, :]
    ca_im = cr[None] * pi[:, :, None, :] + ci[None] * pr[:, :, None, :]
    kk = (jnp.einsum('tgop,gpi->tgoi', ca_re[:tc], bb_re, precision=HI)
          - jnp.einsum('tgop,gpi->tgoi', ca_im[:tc], bb_im, precision=HI))
    kk = kk.at[0].add(d_skip.astype(F32)[:, :, None] * jnp.eye(hd, dtype=F32)[None])
    gs = LANES // hd
    ns = g // gs
    dup = LANES // p
    eye = jnp.eye(gs, dtype=F32)
    k5 = jnp.transpose(kk.reshape(tc, ns, gs, hd, hd), (1, 0, 2, 4, 3))
    bm = (k5[:, :, :, :, None, :] * eye[None, None, :, None, :, None]).reshape(ns, tc, LANES, LANES)
    dt = jnp.arange(tc)

    def rows(x_re, x_im):
        x = jnp.stack([x_re, x_im], axis=1).reshape(tc, 2, ns, gs * hd, p)
        x = jnp.transpose(x, (2, 0, 1, 3, 4))
        return jnp.concatenate([x] * dup, axis=-1)

    prr, pir = pr[tc - 1 - dt], pi[tc - 1 - dt]
    bc_re = prr[:, :, None, :] * jnp.transpose(bb_re, (0, 2, 1))[None] - pir[:, :, None, :] * jnp.transpose(bb_im, (0, 2, 1))[None]
    bc_im = prr[:, :, None, :] * jnp.transpose(bb_im, (0, 2, 1))[None] + pir[:, :, None, :] * jnp.transpose(bb_re, (0, 2, 1))[None]
    bcc = rows(bc_re, bc_im)
    cct = rows(ca_re[1:], -ca_im[1:])
    a16r = pr[tc].reshape(1, g * p)
    a16i = pi[tc].reshape(1, g * p)
    return (bm.astype(BF16), bcc.astype(BF16), cct.astype(BF16), a16r, a16i)


def _s5_kernel(x_ref, bm_ref, bcc_ref, cct_ref, ar_ref, ai_ref, y_ref,
               mf_ref, bcf_ref, ccf_ref, st_ref, cr_ref, ci_ref, *, ncb, sl, n_state):
    tc = SSM_TC

    @pl.when((pl.program_id(1) == 0) & (pl.program_id(2) == 0))
    def _():
        zero = jnp.zeros((LANES, LANES), BF16)
        for di in range(tc):
            for do in range(tc):
                mf_ref[di * LANES:(di + 1) * LANES, do * LANES:(do + 1) * LANES] = (
                    bm_ref[0, do - di] if do >= di else zero)
        r = lax.broadcasted_iota(jnp.int32, (LANES, sl), 0)
        c = lax.broadcasted_iota(jnp.int32, (LANES, sl), 1)
        same_group = (r // SSM_GROUP_DIM) == (c // n_state)

        def widen(blk):
            wide = jnp.concatenate([blk.astype(F32)] * (sl // LANES), axis=1)
            return jnp.where(same_group, wide, 0.0).astype(BF16)

        for dt in range(tc):
            for ri in range(2):
                bcf_ref[dt * LANES:(dt + 1) * LANES, ri * sl:(ri + 1) * sl] = widen(bcc_ref[0, dt, ri])
                ccf_ref[dt * LANES:(dt + 1) * LANES, ri * sl:(ri + 1) * sl] = widen(cct_ref[0, dt, ri])

    @pl.when(pl.program_id(2) == 0)
    def _():
        cr_ref[...] = jnp.zeros_like(cr_ref)
        ci_ref[...] = jnp.zeros_like(ci_ref)

    xs = jnp.concatenate([x_ref[pl.ds(dt, ncb, stride=SSM_TC), :] for dt in range(SSM_TC)],
                         axis=1).astype(BF16)
    st_ref[...] = jnp.dot(xs, bcf_ref[...], preferred_element_type=F32)
    ar = ar_ref[...]
    ai = ai_ref[...]

    def body(c, carry):
        sr, si = carry
        lr = st_ref[pl.ds(c, 1), :sl]
        li = st_ref[pl.ds(c, 1), sl:]
        st_ref[pl.ds(c, 1), :sl] = sr
        st_ref[pl.ds(c, 1), sl:] = si
        return ar * sr - ai * si + lr, ar * si + ai * sr + li

    sr, si = lax.fori_loop(0, ncb, body, (cr_ref[...], ci_ref[...]), unroll=8)
    cr_ref[...] = sr
    ci_ref[...] = si
    y = (jnp.dot(xs, mf_ref[...], preferred_element_type=F32)
         + _dot_nt(st_ref[...].astype(BF16), ccf_ref[...]))
    for dt in range(SSM_TC):
        y_ref[pl.ds(dt, ncb, stride=SSM_TC), :] = y[:, dt * LANES:(dt + 1) * LANES]


def _s5_scan(proj, ops, bsz, seq, sw, n_state):
    bm, bcc, cct, a16r, a16i = ops
    ns = sw // LANES
    sl = (LANES // SSM_GROUP_DIM) * n_state
    ncb = _pick(seq // SSM_TC, 128)
    rt = ncb * SSM_TC
    nt = seq // rt
    kw = SSM_TC * LANES
    kern = functools.partial(_s5_kernel, ncb=ncb, sl=sl, n_state=n_state)
    return pl.pallas_call(
        kern,
        grid=(ns, bsz, nt),
        in_specs=[pl.BlockSpec((rt, LANES), lambda s, b, i: (b * nt + i, s)),
                  pl.BlockSpec((1, SSM_TC, LANES, LANES), lambda s, b, i: (s, 0, 0, 0)),
                  pl.BlockSpec((1, SSM_TC, 2, LANES, LANES), lambda s, b, i: (s, 0, 0, 0, 0)),
                  pl.BlockSpec((1, SSM_TC, 2, LANES, LANES), lambda s, b, i: (s, 0, 0, 0, 0)),
                  pl.BlockSpec((1, sl), lambda s, b, i: (0, s)),
                  pl.BlockSpec((1, sl), lambda s, b, i: (0, s))],
        out_specs=pl.BlockSpec((rt, LANES), lambda s, b, i: (b * nt + i, s)),
        out_shape=jax.ShapeDtypeStruct((bsz * seq, sw), F32),
        scratch_shapes=[pltpu.VMEM((kw, kw), BF16), pltpu.VMEM((kw, 2 * sl), BF16), pltpu.VMEM((kw, 2 * sl), BF16),
                        pltpu.VMEM((ncb, 2 * sl), F32), pltpu.VMEM((1, sl), F32), pltpu.VMEM((1, sl), F32)],
        compiler_params=_cparams(("arbitrary", "arbitrary", "arbitrary")),
        name="s5_scan",
    )(proj, bm, bcc, cct, a16r, a16i)


def _ssm_post_kernel(y_ref, w_ref, b_ref, o_ref):
    y = y_ref[...]
    yg = 0.5 * y * (1.0 + lax.erf(y * (1.0 / math.sqrt(2.0))))
    s = jnp.dot(yg.astype(BF16), w_ref[...], preferred_element_type=F32) + b_ref[...]
    o_ref[...] = (yg * jax.nn.sigmoid(s)).astype(o_ref.dtype)


def _ssm_post(y, w_glu, b_glu):
    t, n = y.shape
    tm = _pick(t, 512)
    return pl.pallas_call(
        _ssm_post_kernel,
        grid=(t // tm,),
        in_specs=[pl.BlockSpec((tm, n), lambda i: (i, 0)),
                  pl.BlockSpec((n, n), lambda i: (0, 0)),
                  pl.BlockSpec((1, n), lambda i: (0, 0))],
        out_specs=pl.BlockSpec((tm, n), lambda i: (i, 0)),
        out_shape=jax.ShapeDtypeStruct((t, n), BF16),
        compiler_params=_cparams(("arbitrary",)),
        name="ssm_post",
    )(y, w_glu, b_glu.reshape(1, n))


def _dot_nt(a, b):
    return lax.dot_general(a, b, (((1,), (1,)), ((), ())), preferred_element_type=F32)


def _dot_tn(a, b):
    return lax.dot_general(a, b, (((0,), (0,)), ((), ())), preferred_element_type=F32)


def _dot_hi(a, b):
    return jnp.dot(a, b, preferred_element_type=F32, precision=HI)


def _silu(x):
    return x * jax.nn.sigmoid(x)


DN_HALO = 8


CAST_SLICES = 16


def _cast_slices(src_refs, dst_refs):
    for sl in range(CAST_SLICES):
        for src, dst in zip(src_refs, dst_refs):
            n = src.shape[0] // CAST_SLICES
            dst[sl * n:(sl + 1) * n, :] = src[sl * n:(sl + 1) * n, :].astype(BF16)
        yield


def _dn_kernel(q_ref, k_ref, v_ref, z_ref, ab_ref, cq_ref, ck_ref, cv_ref, al_ref, dtb_ref, nw_ref,
               *rest, tt, heads, hp, ncast):
    cast_src, o_ref, cast_dst, scratch = rest[:ncast], rest[ncast], rest[ncast + 1:2 * ncast + 1], rest[2 * ncast + 1:]
    s_refs, xx_refs = scratch[:hp], scratch[hp:]

    @pl.when(pl.program_id(2) == 0)
    def _():
        for s_ref, xx_ref in zip(s_refs, xx_refs):
            s_ref[...] = jnp.zeros_like(s_ref)
            xx_ref[:, 0:DN_HALO, :] = jnp.zeros((3, DN_HALO, DN_HEAD_DIM), F32)

    gens = [_dn_head(hh, pl.program_id(1) * hp + hh, q_ref, k_ref, v_ref, z_ref, ab_ref, cq_ref, ck_ref, cv_ref,
                     al_ref, dtb_ref, nw_ref, o_ref, s_refs[hh], xx_refs[hh], tt=tt, heads=heads)
            for hh in range(hp)]
    if ncast:
        gens.append(_cast_slices(cast_src, cast_dst))
    for _ in itertools.zip_longest(*gens):
        pass


def _dn_head(hh, h, q_ref, k_ref, v_ref, z_ref, ab_ref, cq_ref, ck_ref, cv_ref, al_ref, dtb_ref, nw_ref,
             o_ref, s_ref, xx_ref, *, tt, heads):
    hal = DN_HALO
    ls = slice(hh * DN_HEAD_DIM, (hh + 1) * DN_HEAD_DIM)

    def conv(idx, x_ref, cw_ref):
        x = x_ref[:, ls]
        xx_ref[idx, hal:, :] = x
        w = cw_ref[:, ls]
        acc = w[CONV_WIDTH - 1:CONV_WIDTH, :] * x
        for j in range(CONV_WIDTH - 1):
            off = hal - (CONV_WIDTH - 1) + j
            acc = acc + w[j:j + 1, :] * xx_ref[idx, off:off + tt, :]
        xx_ref[idx, 0:hal, :] = x[tt - hal:, :]
        return _silu(acc)

    qc = conv(0, q_ref, cq_ref)
    kc = conv(1, k_ref, ck_ref)
    vc = conv(2, v_ref, cv_ref)
    qn = qc * lax.rsqrt(jnp.sum(qc * qc, axis=-1, keepdims=True) + NORM_EPS) * (DN_HEAD_DIM ** -0.5)
    kn = kc * lax.rsqrt(jnp.sum(kc * kc, axis=-1, keepdims=True) + NORM_EPS)

    ab = ab_ref[...]
    lane = lax.broadcasted_iota(jnp.int32, ab.shape, 1)
    xa = ab + dtb_ref[...]
    sp = jnp.maximum(xa, 0.0) + jnp.log1p(jnp.exp(-jnp.abs(xa)))
    g_all = -jnp.exp(al_ref[...]) * sp
    g_col = jnp.sum(jnp.where(lane == h, g_all, 0.0), axis=-1, keepdims=True)
    beta_col = jnp.sum(jnp.where(lane == h + heads, jax.nn.sigmoid(ab), 0.0), axis=-1, keepdims=True)

    nch = tt // CHUNK
    row = lax.broadcasted_iota(jnp.int32, (tt, DN_HEAD_DIM), 0)
    pos = row & (CHUNK - 1)
    gc = jnp.broadcast_to(g_col, (tt, DN_HEAD_DIM))
    sh = 1
    while sh < CHUNK:
        gc = gc + jnp.where(pos >= sh, pltpu.roll(gc, sh, axis=0), 0.0)
        sh *= 2
    gc_row = jnp.transpose(gc)[0:1, :]
    egc = jnp.exp(gc)

    ri = lax.broadcasted_iota(jnp.int32, (tt, tt), 0)
    ci = lax.broadcasted_iota(jnp.int32, (tt, tt), 1)
    same = (ri // CHUNK) == (ci // CHUNK)
    causal = same & (ri >= ci)
    strict = same & (ri > ci)
    gc_wide = jnp.concatenate([gc] * (tt // DN_HEAD_DIM), axis=1)
    decay = jnp.where(causal, jnp.exp(jnp.where(causal, gc_wide - gc_row, 0.0)), 0.0)
    kb = kn * beta_col
    knb = kn.astype(BF16)
    yield
    a_raw = _dot_nt(kb.astype(BF16), knb)
    qk_raw = _dot_nt(qn.astype(BF16), knb)
    yield
    a_bd = jnp.where(strict, a_raw * decay, 0.0)
    qk_bd = jnp.where(causal, qk_raw * decay, 0.0).astype(BF16)

    def fold(m):
        out = m[0:CHUNK]
        for c in range(1, nch):
            out = out + m[c * CHUNK:(c + 1) * CHUNK]
        return out

    def spread(m):
        return jnp.where(same, jnp.concatenate([m] * nch, axis=0), 0.0)

    r64 = lax.broadcasted_iota(jnp.int32, (CHUNK, tt), 0)
    c64 = lax.broadcasted_iota(jnp.int32, (CHUNK, tt), 1)
    eye_cat = (r64 == (c64 & (CHUNK - 1))).astype(F32)
    pw_cat = fold(a_bd)
    inv_cat = eye_cat - pw_cat
    pw_bd = a_bd.astype(BF16)
    for _ in range(5):
        pw_cat = jnp.dot(pw_cat.astype(BF16), pw_bd, preferred_element_type=F32)
        yield
        pw_bd = spread(pw_cat).astype(BF16)
        inv_add = jnp.dot(inv_cat.astype(BF16), pw_bd, preferred_element_type=F32)
        yield
        inv_cat = inv_cat + inv_add
    inv_bd = spread(inv_cat).astype(BF16)
    rhs = jnp.concatenate([vc * beta_col, kb * egc], axis=1).astype(BF16)
    sol = jnp.dot(inv_bd, rhs, preferred_element_type=F32)
    yield
    u_all, w_all = sol[:, :DN_HEAD_DIM], sol[:, DN_HEAD_DIM:]
    qe = qn * egc

    outs = []
    zblk = jnp.zeros((CHUNK, DN_HEAD_DIM), BF16)
    s = s_ref[...]
    for c in range(nch):
        sl = slice(c * CHUNK, (c + 1) * CHUNK)
        wq = jnp.concatenate([w_all[sl], qe[sl]], axis=0).astype(BF16)
        ws = jnp.dot(wq, s.astype(BF16), preferred_element_type=F32)
        yield
        v_new = u_all[sl] - ws[:CHUNK]
        vnb = v_new.astype(BF16)
        v_pad = jnp.concatenate([zblk] * c + [vnb] + [zblk] * (nch - 1 - c), axis=0)
        gc_c = gc[sl]
        g_last = gc_c[CHUNK - 1:CHUNK, :]
        k_dec = (kn[sl] * jnp.exp(g_last - gc_c)).astype(BF16)
        s_add = _dot_tn(k_dec, vnb)
        o_add = jnp.dot(qk_bd[sl], v_pad, preferred_element_type=F32)
        yield
        s = s * jnp.exp(g_last) + s_add
        outs.append(ws[CHUNK:] + o_add)
    s_ref[...] = s
    o = jnp.concatenate(outs, axis=0)
    o = o * lax.rsqrt(jnp.mean(o * o, axis=-1, keepdims=True) + NORM_EPS) * nw_ref[...]
    o_ref[:, ls] = (o * _silu(z_ref[:, ls])).astype(o_ref.dtype)


def _deltanet(proj, ab, conv_w, a_log, dt_bias, norm_w, bsz, seq, heads, off_q, cast_weights=()):
    t = bsz * seq
    tt = _pick(seq, 256)
    nt = seq // tt
    hd = DN_HEAD_DIM
    cw = conv_w.reshape(CONV_WIDTH, 3 * heads * hd).astype(F32)
    pad = LANES - heads
    al = jnp.pad(a_log.astype(F32), (0, pad)).reshape(1, LANES)
    dtb = jnp.pad(dt_bias.astype(F32), (0, pad)).reshape(1, LANES)
    nw = norm_w.astype(F32).reshape(1, hd)

    hp = math.gcd(math.gcd(heads, off_q), 4)
    hw = hp * hd
    oq, nh = off_q // hp, heads // hp

    def act(o):
        return pl.BlockSpec((tt, hw), lambda b, h, i: (b * nt + i, oq + o * nh + h))

    def cws(o):
        return pl.BlockSpec((CONV_WIDTH, hw), lambda b, h, i: (0, o * nh + h))

    row = pl.BlockSpec((1, LANES), lambda b, h, i: (0, 0))
    nsteps = bsz * nh * nt
    ncast = len(cast_weights)
    cast_specs = []
    for w in cast_weights:
        rows, cols = w.shape
        assert rows % (nsteps * CAST_SLICES * 16) == 0, (rows, nsteps)
        cast_specs.append(pl.BlockSpec((rows // nsteps, cols), lambda b, h, i: ((b * nh + h) * nt + i, 0)))
    kern = functools.partial(_dn_kernel, tt=tt, heads=heads, hp=hp, ncast=ncast)
    outs = pl.pallas_call(
        kern,
        grid=(bsz, nh, nt),
        in_specs=[act(0), act(1), act(2), act(3),
                  pl.BlockSpec((tt, LANES), lambda b, h, i: (b * nt + i, 0)),
                  cws(0), cws(1), cws(2), row, row, row] + cast_specs,
        out_specs=[pl.BlockSpec((tt, hw), lambda b, h, i: (b * nt + i, h))] + cast_specs,
        out_shape=([jax.ShapeDtypeStruct((t, heads * hd), BF16)]
                   + [jax.ShapeDtypeStruct(w.shape, BF16) for w in cast_weights]),
        scratch_shapes=([pltpu.VMEM((hd, hd), F32)] * hp
                        + [pltpu.VMEM((3, tt + DN_HALO, hd), F32)] * hp),
        compiler_params=_cparams(("arbitrary", "arbitrary", "arbitrary")),
        name="deltanet",
    )(proj, proj, proj, proj, ab, cw, cw, cw, al, dtb, nw, *cast_weights)
    return outs[0], outs[1:]


def _merge_kernel(ys_ref, yd_ref, gs_ref, gd_ref, ws_ref, wd_ref, o_ref):
    ps = jnp.dot(ys_ref[...], ws_ref[...], preferred_element_type=F32)
    pd = jnp.dot(yd_ref[...], wd_ref[...], preferred_element_type=F32)
    o_ref[...] = (jax.nn.sigmoid(gs_ref[...]) * ps + jax.nn.sigmoid(gd_ref[...]) * pd).astype(o_ref.dtype)


def _merge(ys, yd, proj, off_gs, w_ps, w_pd):
    t, ns = ys.shape
    d = w_ps.shape[1]
    tm = _pick(t, 512)
    tn = _pick(ns, 1024)
    nb = d // tn
    return pl.pallas_call(
        _merge_kernel,
        grid=(nb, t // tm),
        in_specs=[pl.BlockSpec((tm, ns), lambda j, i: (i, 0)),
                  pl.BlockSpec((tm, d), lambda j, i: (i, 0)),
                  pl.BlockSpec((tm, tn), lambda j, i: (i, off_gs + j)),
                  pl.BlockSpec((tm, tn), lambda j, i: (i, off_gs + nb + j)),
                  pl.BlockSpec((ns, tn), lambda j, i: (0, j)),
                  pl.BlockSpec((d, tn), lambda j, i: (0, j))],
        out_specs=pl.BlockSpec((tm, tn), lambda j, i: (i, j)),
        out_shape=jax.ShapeDtypeStruct((t, d), BF16),
        compiler_params=_cparams(("arbitrary", "arbitrary")),
        name="merge",
    )(ys, yd, proj, proj, w_ps, w_pd)


def _pack_bf16_pair(x):
    half = x.shape[1] // 2
    lo = pltpu.bitcast(x[:, :half].astype(BF16).astype(F32), jnp.uint32)
    hi = pltpu.bitcast(x[:, half:].astype(BF16).astype(F32), jnp.uint32)
    return (lo >> 16) | (hi & jnp.uint32(0xFFFF0000))


def _unpack_bf16_pair(u):
    lo = pltpu.bitcast(u << 16, F32).astype(BF16)
    hi = pltpu.bitcast(u & jnp.uint32(0xFFFF0000), F32).astype(BF16)
    return lo, hi


def _out_ln_kernel(m_ref, w_ref, h_ref, g_ref, b_ref, h1_ref, hp_ref):
    mix = jnp.dot(m_ref[...], w_ref[...], preferred_element_type=F32)
    h1 = _ln(DEEPNORM_ALPHA * h_ref[...] + mix, g_ref[...], b_ref[...])
    h1_ref[...] = h1
    hp_ref[...] = _pack_bf16_pair(h1)


def _out_ln(merged, w_out, h, g, b):
    t, d = h.shape
    tm = _pick(t, 512)
    return pl.pallas_call(
        _out_ln_kernel,
        grid=(t // tm,),
        in_specs=[pl.BlockSpec((tm, d), lambda i: (i, 0)),
                  pl.BlockSpec((d, d), lambda i: (0, 0)),
                  pl.BlockSpec((tm, d), lambda i: (i, 0)),
                  pl.BlockSpec((1, d), lambda i: (0, 0)),
                  pl.BlockSpec((1, d), lambda i: (0, 0))],
        out_specs=[pl.BlockSpec((tm, d), lambda i: (i, 0)),
                   pl.BlockSpec((tm, d // 2), lambda i: (i, 0))],
        out_shape=[jax.ShapeDtypeStruct((t, d), F32), jax.ShapeDtypeStruct((t, d // 2), jnp.uint32)],
        compiler_params=_cparams(("arbitrary",)),
        name="out_ln1",
    )(merged, w_out, h, g.reshape(1, d), b.reshape(1, d))


def _router_kernel(h_ref, w_ref, b_ref, idx_ref, gate_ref, rank_ref, cnt_ref, base_ref, *, n_exp, tm):
    i = pl.program_id(0)

    @pl.when(i == 0)
    def _():
        base_ref[...] = jnp.zeros_like(base_ref)

    logits = _dot_hi(h_ref[...], w_ref[...]) + b_ref[...]
    lane = lax.broadcasted_iota(jnp.int32, logits.shape, 1)
    vals = jnp.where(lane < n_exp, logits, -jnp.inf)
    sels, tops, ams = [], [], []
    for _ in range(TOP_K):
        m = jnp.max(vals, axis=-1, keepdims=True)
        am = jnp.min(jnp.where(vals == m, lane, LANES), axis=-1, keepdims=True)
        sel = lane == am
        vals = jnp.where(sel, -jnp.inf, vals)
        sels.append(sel)
        tops.append(m)
        ams.append(am)
    es = [jnp.exp(v - tops[0]) for v in tops]
    den = es[0] + es[1] + es[2] + es[3]
    onehot = sels[0] | sels[1] | sels[2] | sels[3]
    mt = onehot.astype(BF16)
    ri = lax.broadcasted_iota(jnp.int32, (tm, tm), 0)
    ci = lax.broadcasted_iota(jnp.int32, (tm, tm), 1)
    before = (ri > ci).astype(BF16)
    prior = jnp.dot(before, mt, preferred_element_type=F32) + base_ref[...]
    idx_o = jnp.zeros(logits.shape, jnp.int32)
    gate_o = jnp.zeros(logits.shape, F32)
    rank_o = jnp.zeros(logits.shape, jnp.int32)
    for k in range(TOP_K):
        rk = jnp.sum(jnp.where(sels[k], prior, 0.0), axis=-1, keepdims=True).astype(jnp.int32)
        idx_o = jnp.where(lane == k, ams[k], idx_o)
        gate_o = jnp.where(lane == k, es[k] / den, gate_o)
        rank_o = jnp.where(lane == k, rk, rank_o)
    idx_ref[...] = idx_o
    gate_ref[...] = gate_o
    rank_ref[...] = rank_o
    base_ref[...] += jnp.sum(onehot.astype(F32), axis=0, keepdims=True)
    cnt_ref[...] = base_ref[...]


def _router(h1, w_router, b_router):
    t, d = h1.shape
    n_exp = w_router.shape[1]
    tm = _pick(t, 256)
    w = jnp.pad(w_router.astype(F32), ((0, 0), (0, LANES - n_exp)))
    b = jnp.pad(b_router.astype(F32), (0, LANES - n_exp)).reshape(1, LANES)
    tok = pl.BlockSpec((tm, LANES), lambda i: (i, 0))
    kern = functools.partial(_router_kernel, n_exp=n_exp, tm=tm)
    return pl.pallas_call(
        kern,
        grid=(t // tm,),
        in_specs=[pl.BlockSpec((tm, d), lambda i: (i, 0)),
                  pl.BlockSpec((d, LANES), lambda i: (0, 0)),
                  pl.BlockSpec((1, LANES), lambda i: (0, 0))],
        out_specs=[tok, tok, tok, pl.BlockSpec((1, LANES), lambda i: (0, 0))],
        out_shape=[jax.ShapeDtypeStruct((t, LANES), jnp.int32), jax.ShapeDtypeStruct((t, LANES), F32),
                   jax.ShapeDtypeStruct((t, LANES), jnp.int32), jax.ShapeDtypeStruct((1, LANES), F32)],
        scratch_shapes=[pltpu.VMEM((1, LANES), F32)],
        compiler_params=_cparams(("arbitrary",)),
        name="router",
    )(h1, w, b)


def _dispatch_kernel(dest_ref, x_ref, xs_in_ref, xs_ref, sem, *, tt):
    del xs_in_ref
    i = pl.program_id(0)
    base = i * (tt * TOP_K)

    def issue(r, carry):
        for k in range(TOP_K):
            row = dest_ref[base + r * TOP_K + k]
            pltpu.make_async_copy(x_ref.at[pl.ds(r, 1)], xs_ref.at[pl.ds(row, 1)], sem).start()
        return carry

    lax.fori_loop(0, tt, issue, 0, unroll=4)

    for k in range(TOP_K):
        pltpu.make_async_copy(x_ref, xs_ref.at[pl.ds(0, tt)], sem).wait()


def _dispatch(xp, dest, n_rows):
    t, dh = xp.shape
    tt = _pick(t, 256)
    xs0 = jnp.zeros((n_rows, dh), jnp.uint32)
    kern = functools.partial(_dispatch_kernel, tt=tt)
    return pl.pallas_call(
        kern,
        grid_spec=pltpu.PrefetchScalarGridSpec(
            num_scalar_prefetch=1,
            grid=(t // tt,),
            in_specs=[pl.BlockSpec((tt, dh), lambda i, dst: (i, 0)),
                      pl.BlockSpec(memory_space=pl.ANY)],
            out_specs=pl.BlockSpec(memory_space=pl.ANY),
            scratch_shapes=[pltpu.SemaphoreType.DMA(())]),
        out_shape=jax.ShapeDtypeStruct((n_rows, dh), jnp.uint32),
        input_output_aliases={2: 0},
        compiler_params=_cparams(("arbitrary",)),
        name="dispatch",
    )(dest, xp, xs0)


def _expert_kernel(be_ref, nb_ref, x_ref, wg_ref, wu_ref, bg_ref, bu_ref, wd_ref, bd_ref, o_ref, *, half):
    i = pl.program_id(0)
    j = pl.program_id(1)

    @pl.when(i < nb_ref[0])
    def _():
        lo, hi = _unpack_bf16_pair(x_ref[...])
        gate = (jnp.dot(lo, wg_ref[0, :half, :], preferred_element_type=F32)
                + jnp.dot(hi, wg_ref[0, half:, :], preferred_element_type=F32) + bg_ref[0])
        up = (jnp.dot(lo, wu_ref[0, :half, :], preferred_element_type=F32)
              + jnp.dot(hi, wu_ref[0, half:, :], preferred_element_type=F32) + bu_ref[0])
        gate = jnp.minimum(gate, SWIGLU_LIMIT)
        up = jnp.clip(up, -SWIGLU_LIMIT, SWIGLU_LIMIT)
        act = gate * jax.nn.sigmoid(SWIGLU_ALPHA * gate) * (up + 1.0)
        part = jnp.dot(act.astype(BF16), wd_ref[0], preferred_element_type=F32)

        @pl.when(j == 0)
        def _():
            o_ref[...] = part + bd_ref[0]

        @pl.when(j > 0)
        def _():
            o_ref[...] += part

    @pl.when((i >= nb_ref[0]) & (j == 0))
    def _():
        o_ref[...] = jnp.zeros_like(o_ref)


def _experts(xs, block_expert, n_used, w_gu, b_gu, w_down, b_down):
    n_rows, dh = xs.shape
    n_exp, d, ff2 = w_gu.shape
    ff = ff2 // 2
    tf = _pick(ff, 1024)
    nj = ff // tf
    nblk = n_rows // ROW_BLOCK
    b_gu3 = b_gu.astype(F32).reshape(n_exp, 1, ff2)
    b_d3 = b_down.astype(F32).reshape(n_exp, 1, d)

    def blk(i, nb):
        return jnp.maximum(jnp.minimum(i, nb[0] - 1), 0)

    def jj(i, j, nb):
        return jnp.where(i < nb[0], j, nj - 1)

    kern = functools.partial(_expert_kernel, half=d // 2)
    return pl.pallas_call(
        kern,
        grid_spec=pltpu.PrefetchScalarGridSpec(
            num_scalar_prefetch=2,
            grid=(nblk, nj),
            in_specs=[
                pl.BlockSpec((ROW_BLOCK, dh), lambda i, j, be, nb: (blk(i, nb), 0)),
                pl.BlockSpec((1, d, tf), lambda i, j, be, nb: (be[blk(i, nb)], 0, jj(i, j, nb))),
                pl.BlockSpec((1, d, tf), lambda i, j, be, nb: (be[blk(i, nb)], 0, nj + jj(i, j, nb))),
                pl.BlockSpec((1, 1, tf), lambda i, j, be, nb: (be[blk(i, nb)], 0, jj(i, j, nb))),
                pl.BlockSpec((1, 1, tf), lambda i, j, be, nb: (be[blk(i, nb)], 0, nj + jj(i, j, nb))),
                pl.BlockSpec((1, tf, d), lambda i, j, be, nb: (be[blk(i, nb)], jj(i, j, nb), 0)),
                pl.BlockSpec((1, 1, d), lambda i, j, be, nb: (be[blk(i, nb)], 0, 0)),
            ],
            out_specs=pl.BlockSpec((ROW_BLOCK, d), lambda i, j, be, nb: (i, 0))),
        out_shape=jax.ShapeDtypeStruct((n_rows, d), F32),
        compiler_params=_cparams(("arbitrary", "arbitrary")),
        name="experts",
    )(block_expert, n_used, xs, w_gu, w_gu, b_gu3, b_gu3, w_down, b_d3)


def _combine_kernel(dest_ref, y_ref, h_ref, gate_ref, g_ref, b_ref, o_ref, buf_ref, sem, *, tt):
    i = pl.program_id(0)
    base = i * (tt * TOP_K)

    def issue(r, carry):
        for k in range(TOP_K):
            row = dest_ref[base + r * TOP_K + k]
            pltpu.make_async_copy(y_ref.at[pl.ds(row, 1)], buf_ref.at[k, pl.ds(r, 1)], sem).start()
        return carry

    lax.fori_loop(0, tt, issue, 0, unroll=4)

    for k in range(TOP_K):
        pltpu.make_async_copy(y_ref.at[pl.ds(0, tt)], buf_ref.at[k], sem).wait()
    gates = gate_ref[...]
    ffn = gates[:, 0:1] * buf_ref[0]
    for k in range(1, TOP_K):
        ffn = ffn + gates[:, k:k + 1] * buf_ref[k]
    o_ref[...] = _ln(DEEPNORM_ALPHA * h_ref[...] + ffn, g_ref[...], b_ref[...])


def _combine(y, dest, h1, gates, g, b):
    t, d = h1.shape
    tt = _pick(t, 128)
    kern = functools.partial(_combine_kernel, tt=tt)
    return pl.pallas_call(
        kern,
        grid_spec=pltpu.PrefetchScalarGridSpec(
            num_scalar_prefetch=1,
            grid=(t // tt,),
            in_specs=[pl.BlockSpec(memory_space=pl.ANY),
                      pl.BlockSpec((tt, d), lambda i, dst: (i, 0)),
                      pl.BlockSpec((tt, LANES), lambda i, dst: (i, 0)),
                      pl.BlockSpec((1, d), lambda i, dst: (0, 0)),
                      pl.BlockSpec((1, d), lambda i, dst: (0, 0))],
            out_specs=pl.BlockSpec((tt, d), lambda i, dst: (i, 0)),
            scratch_shapes=[pltpu.VMEM((TOP_K, tt, d), F32), pltpu.SemaphoreType.DMA(())]),
        out_shape=jax.ShapeDtypeStruct((t, d), F32),
        compiler_params=_cparams(("arbitrary",)),
        name="combine",
    )(dest, y, h1, gates, g.reshape(1, d), b.reshape(1, d))


def kernel(x, ln_in_g, ln_in_b, w_in, lam_re, lam_im, log_step, ssm_b_re, ssm_b_im, ssm_c_re, ssm_c_im, ssm_d, w_glu, b_glu, conv_w, a_log, dt_bias, dn_norm_w, w_proj_ssm, w_proj_dn, w_out, ln1_g, ln1_b, w_router, b_router, w_gate_up, b_gate_up, w_down, b_down, ln2_g, ln2_b):
    bsz, seq, d = x.shape
    t = bsz * seq
    sw = d // 2
    heads = d // DN_HEAD_DIM
    dn = heads * DN_HEAD_DIM
    groups = sw // SSM_GROUP_DIM
    n_state = lam_re.shape[-1]
    n_exp = w_router.shape[-1]
    assert w_in.shape[0] == DEPTH == 1
    assert seq % (CHUNK * 4) == 0 and sw % LANES == 0 and groups % 2 == 0 and 2 * heads <= LANES

    h, hb = _ln_in(x.reshape(t, d), ln_in_g, ln_in_b)

    wi = w_in[0]
    c_ab = sw + 4 * dn
    w_main = jnp.concatenate([wi[:, :c_ab], wi[:, c_ab + 2 * heads:]], axis=1).astype(BF16)
    w_ab = jnp.pad(wi[:, c_ab:c_ab + 2 * heads], ((0, 0), (0, LANES - 2 * heads))).astype(BF16)
    proj = _matmul(hb, w_main, F32, "proj_in")
    ab = _matmul(hb, w_ab, F32, "proj_ab")

    ops = _s5_operators(lam_re[0], lam_im[0], log_step[0], ssm_b_re[0], ssm_b_im[0],
                        ssm_c_re[0], ssm_c_im[0], ssm_d[0])
    y_s = _s5_scan(proj, ops, bsz, seq, sw, n_state)
    y_ssm = _ssm_post(y_s, w_glu[0].astype(BF16), b_glu[0].astype(F32))

    ff2 = w_gate_up.shape[-1]
    y_dn, (w_gu_bf, w_dn_bf) = _deltanet(
        proj, ab, conv_w[0], a_log[0], dt_bias[0], dn_norm_w[0], bsz, seq, heads, off_q=sw // LANES,
        cast_weights=(w_gate_up[0].reshape(n_exp * d, ff2), w_down[0].reshape(n_exp * (ff2 // 2), d)))
    w_gu_bf = w_gu_bf.reshape(n_exp, d, ff2)
    w_dn_bf = w_dn_bf.reshape(n_exp, ff2 // 2, d)

    tn = _pick(sw, 1024)
    merged = _merge(y_ssm, y_dn, proj, (sw + 4 * dn) // tn, w_proj_ssm[0].astype(BF16),
                    w_proj_dn[0].astype(BF16))
    h1, h1p = _out_ln(merged, w_out[0].astype(BF16), h, ln1_g[0], ln1_b[0])

    idx, gates, rank, cnt = _router(h1, w_router[0], b_router[0])
    counts = cnt[0, :n_exp].astype(jnp.int32)
    padded = (counts + ROW_BLOCK - 1) // ROW_BLOCK * ROW_BLOCK
    pad_ends = jnp.cumsum(padded)
    pad_starts = pad_ends - padded
    dest = (pad_starts[idx[:, :TOP_K]] + rank[:, :TOP_K]).astype(jnp.int32).reshape(t * TOP_K)
    n_rows = t * TOP_K + n_exp * ROW_BLOCK
    nblk = n_rows // ROW_BLOCK
    blk_start = jnp.arange(nblk, dtype=jnp.int32) * ROW_BLOCK
    block_expert = jnp.minimum(jnp.sum((pad_ends[None, :] <= blk_start[:, None]).astype(jnp.int32), axis=1),
                               n_exp - 1).astype(jnp.int32)
    n_used = (pad_ends[-1:] // ROW_BLOCK).astype(jnp.int32)
    xs = _dispatch(h1p, dest, n_rows)
    y = _experts(xs, block_expert, n_used, w_gu_bf, b_gate_up[0], w_dn_bf, b_down[0])
    out = _combine(y, dest, h1, gates, ln2_g[0], ln2_b[0])
    return out.reshape(bsz, seq, d)
```

```python
import functools
import itertools
import math

import jax
import jax.numpy as jnp
from jax import lax
from jax.experimental import pallas as pl
from jax.experimental.pallas import tpu as pltpu

F32 = jnp.float32
BF16 = jnp.bfloat16
HI = lax.Precision.HIGHEST

LANES = 128
CHUNK = 64
LN_EPS = 1e-5
NORM_EPS = 1e-6
SSM_GROUP_DIM = 16
SSM_TC = 16
DN_HEAD_DIM = 128
CONV_WIDTH = 4
TOP_K = 4
SWIGLU_LIMIT = 7.0
SWIGLU_ALPHA = 1.702
DEPTH = 1
DEEPNORM_ALPHA = (2 * DEPTH) ** 0.25
ROW_BLOCK = 512
VMEM_LIMIT = 56 * 1024 * 1024


def _cparams(sem):
    return pltpu.CompilerParams(dimension_semantics=sem, vmem_limit_bytes=VMEM_LIMIT)


def _pick(n, pref):
    t = min(n, pref)
    while n % t:
        t //= 2
    return t


def _ln(x, g, b):
    mu = jnp.mean(x, axis=-1, keepdims=True)
    xc = x - mu
    var = jnp.mean(xc * xc, axis=-1, keepdims=True)
    return xc * lax.rsqrt(var + LN_EPS) * g + b


def _ln_in_kernel(x_ref, g_ref, b_ref, h_ref, hb_ref):
    h = _ln(x_ref[...], g_ref[...], b_ref[...])
    h_ref[...] = h
    hb_ref[...] = h.astype(BF16)


def _ln_in(x, g, b):
    t, d = x.shape
    tm = _pick(t, 512)
    return pl.pallas_call(
        _ln_in_kernel,
        grid=(t // tm,),
        in_specs=[pl.BlockSpec((tm, d), lambda i: (i, 0)),
                  pl.BlockSpec((1, d), lambda i: (0, 0)),
                  pl.BlockSpec((1, d), lambda i: (0, 0))],
        out_specs=[pl.BlockSpec((tm, d), lambda i: (i, 0)),
                   pl.BlockSpec((tm, d), lambda i: (i, 0))],
        out_shape=[jax.ShapeDtypeStruct((t, d), F32), jax.ShapeDtypeStruct((t, d), BF16)],
        compiler_params=_cparams(("arbitrary",)),
        name="ln_in",
    )(x, g.reshape(1, d), b.reshape(1, d))


def _mm_kernel(x_ref, w_ref, o_ref):
    o_ref[...] = jnp.dot(x_ref[...], w_ref[...].astype(BF16), preferred_element_type=F32).astype(o_ref.dtype)


def _matmul(x, w, out_dtype, name, ncols=None):
    m, k = x.shape
    n = w.shape[1] if ncols is None else ncols
    tm = _pick(m, 1024)
    tn = _pick(n, 1024)
    return pl.pallas_call(
        _mm_kernel,
        grid=(n // tn, m // tm),
        in_specs=[pl.BlockSpec((tm, k), lambda j, i: (i, 0)),
                  pl.BlockSpec((k, tn), lambda j, i: (0, j))],
        out_specs=pl.BlockSpec((tm, tn), lambda j, i: (i, j)),
        out_shape=jax.ShapeDtypeStruct((m, n), out_dtype),
        compiler_params=_cparams(("arbitrary", "arbitrary")),
        name=name,
    )(x, w)


def _s5_operators(lam_re, lam_im, log_step, b_re, b_im, c_re, c_im, d_skip):
    g, p = lam_re.shape
    hd, tc = SSM_GROUP_DIM, SSM_TC
    lr, li = lam_re.astype(F32), lam_im.astype(F32)
    step = jnp.exp(log_step.astype(F32))[:, None]
    tau = jnp.arange(tc + 1, dtype=F32)[:, None, None]
    mag = jnp.exp(lr * step * tau)
    pr, pi = mag * jnp.cos(li * step * tau), mag * jnp.sin(li * step * tau)
    a_re, a_im = pr[1], pi[1]
    den = lr * lr + li * li
    nr, ni = a_re - 1.0, a_im
    f_re = (nr * lr + ni * li) / den
    f_im = (ni * lr - nr * li) / den
    br, bi = b_re.astype(F32), b_im.astype(F32)
    bb_re = f_re[..., None] * br - f_im[..., None] * bi
    bb_im = f_re[..., None] * bi + f_im[..., None] * br
    cr, ci = c_re.astype(F32), c_im.astype(F32)
    ca_re = cr[None] * pr[:, :, None, :] - ci[None] * pi[:, :, None, :]
    ca_im = cr[None] * pi[:, :, None, :] + ci[None] * pr[:, :, None, :]
    ca_g = jnp.concatenate([jnp.transpose(ca_re[:tc], (1, 0, 2, 3)).reshape(g, tc * hd, p),
                            -jnp.transpose(ca_im[:tc], (1, 0, 2, 3)).reshape(g, tc * hd, p)], axis=2)
    bb_g = jnp.concatenate([bb_re, bb_im], axis=1)
    kk = jnp.einsum('gxp,gpi->gxi', ca_g, bb_g, precision=HI)
    kk = jnp.transpose(kk.reshape(g, tc, hd, hd), (1, 0, 2, 3))
    kk = kk.at[0].add(d_skip.astype(F32)[:, :, None] * jnp.eye(hd, dtype=F32)[None])
    gs = LANES // hd
    ns = g // gs
    dup = LANES // p
    eye = jnp.eye(gs, dtype=F32)
    k5 = jnp.transpose(kk.reshape(tc, ns, gs, hd, hd), (1, 0, 2, 4, 3))
    bm = (k5[:, :, :, :, None, :] * eye[None, None, :, None, :, None]).reshape(ns, tc, LANES, LANES)
    dt = jnp.arange(tc)

    def rows(x_re, x_im):
        x = jnp.stack([x_re, x_im], axis=1).reshape(tc, 2, ns, gs * hd, p)
        x = jnp.transpose(x, (2, 0, 1, 3, 4))
        return jnp.concatenate([x] * dup, axis=-1)

    prr, pir = pr[tc - 1 - dt], pi[tc - 1 - dt]
    bc_re = prr[:, :, None, :] * jnp.transpose(bb_re, (0, 2, 1))[None] - pir[:, :, None, :] * jnp.transpose(bb_im, (0, 2, 1))[None]
    bc_im = prr[:, :, None, :] * jnp.transpose(bb_im, (0, 2, 1))[None] + pir[:, :, None, :] * jnp.transpose(bb_re, (0, 2, 1))[None]
    bcc = rows(bc_re, bc_im)
    cct = rows(ca_re[1:], -ca_im[1:])
    a16r = pr[tc].reshape(1, g * p)
    a16i = pi[tc].reshape(1, g * p)
    return (bm.astype(BF16), bcc.astype(BF16), cct.astype(BF16), a16r, a16i)


def _s5_kernel(x_ref, bm_ref, bcc_ref, cct_ref, ar_ref, ai_ref, y_ref,
               mf_ref, bcf_ref, ccf_ref, st_ref, cr_ref, ci_ref, *, ncb, sl, n_state):
    tc = SSM_TC

    @pl.when((pl.program_id(1) == 0) & (pl.program_id(2) == 0))
    def _():
        zero = jnp.zeros((LANES, LANES), BF16)
        for di in range(tc):
            for do in range(tc):
                mf_ref[di * LANES:(di + 1) * LANES, do * LANES:(do + 1) * LANES] = (
                    bm_ref[0, do - di] if do >= di else zero)
        r = lax.broadcasted_iota(jnp.int32, (LANES, sl), 0)
        c = lax.broadcasted_iota(jnp.int32, (LANES, sl), 1)
        same_group = (r // SSM_GROUP_DIM) == (c // n_state)

        def widen(blk):
            wide = jnp.concatenate([blk.astype(F32)] * (sl // LANES), axis=1)
            return jnp.where(same_group, wide, 0.0).astype(BF16)

        for dt in range(tc):
            for ri in range(2):
                bcf_ref[dt * LANES:(dt + 1) * LANES, ri * sl:(ri + 1) * sl] = widen(bcc_ref[0, dt, ri])
                ccf_ref[dt * LANES:(dt + 1) * LANES, ri * sl:(ri + 1) * sl] = widen(cct_ref[0, dt, ri])

    @pl.when(pl.program_id(2) == 0)
    def _():
        cr_ref[...] = jnp.zeros_like(cr_ref)
        ci_ref[...] = jnp.zeros_like(ci_ref)

    xs = jnp.concatenate([x_ref[pl.ds(dt, ncb, stride=SSM_TC), :] for dt in range(SSM_TC)],
                         axis=1).astype(BF16)
    st_ref[...] = jnp.dot(xs, bcf_ref[...], preferred_element_type=F32)
    ar = ar_ref[...]
    ai = ai_ref[...]

    def body(c, carry):
        sr, si = carry
        lr = st_ref[pl.ds(c, 1), :sl]
        li = st_ref[pl.ds(c, 1), sl:]
        st_ref[pl.ds(c, 1), :sl] = sr
        st_ref[pl.ds(c, 1), sl:] = si
        return ar * sr - ai * si + lr, ar * si + ai * sr + li

    sr, si = lax.fori_loop(0, ncb, body, (cr_ref[...], ci_ref[...]), unroll=8)
    cr_ref[...] = sr
    ci_ref[...] = si
    y = (jnp.dot(xs, mf_ref[...], preferred_element_type=F32)
         + _dot_nt(st_ref[...].astype(BF16), ccf_ref[...]))
    for dt in range(SSM_TC):
        y_ref[pl.ds(dt, ncb, stride=SSM_TC), :] = y[:, dt * LANES:(dt + 1) * LANES]


def _s5_scan(proj, ops, bsz, seq, sw, n_state):
    bm, bcc, cct, a16r, a16i = ops
    ns = sw // LANES
    sl = (LANES // SSM_GROUP_DIM) * n_state
    ncb = _pick(seq // SSM_TC, 128)
    rt = ncb * SSM_TC
    nt = seq // rt
    kw = SSM_TC * LANES
    kern = functools.partial(_s5_kernel, ncb=ncb, sl=sl, n_state=n_state)
    return pl.pallas_call(
        kern,
        grid=(ns, bsz, nt),
        in_specs=[pl.BlockSpec((rt, LANES), lambda s, b, i: (b * nt + i, s)),
                  pl.BlockSpec((1, SSM_TC, LANES, LANES), lambda s, b, i: (s, 0, 0, 0)),
                  pl.BlockSpec((1, SSM_TC, 2, LANES, LANES), lambda s, b, i: (s, 0, 0, 0, 0)),
                  pl.BlockSpec((1, SSM_TC, 2, LANES, LANES), lambda s, b, i: (s, 0, 0, 0, 0)),
                  pl.BlockSpec((1, sl), lambda s, b, i: (0, s)),
                  pl.BlockSpec((1, sl), lambda s, b, i: (0, s))],
        out_specs=pl.BlockSpec((rt, LANES), lambda s, b, i: (b * nt + i, s)),
        out_shape=jax.ShapeDtypeStruct((bsz * seq, sw), F32),
        scratch_shapes=[pltpu.VMEM((kw, kw), BF16), pltpu.VMEM((kw, 2 * sl), BF16), pltpu.VMEM((kw, 2 * sl), BF16),
                        pltpu.VMEM((ncb, 2 * sl), F32), pltpu.VMEM((1, sl), F32), pltpu.VMEM((1, sl), F32)],
        compiler_params=_cparams(("arbitrary", "arbitrary", "arbitrary")),
        name="s5_scan",
    )(proj, bm, bcc, cct, a16r, a16i)


def _ssm_post_kernel(y_ref, w_ref, b_ref, o_ref):
    y = y_ref[...]
    yg = 0.5 * y * (1.0 + lax.erf(y * (1.0 / math.sqrt(2.0))))
    s = jnp.dot(yg.astype(BF16), w_ref[...], preferred_element_type=F32) + b_ref[...]
    o_ref[...] = (yg * jax.nn.sigmoid(s)).astype(o_ref.dtype)


def _ssm_post(y, w_glu, b_glu):
    t, n = y.shape
    tm = _pick(t, 512)
    return pl.pallas_call(
        _ssm_post_kernel,
        grid=(t // tm,),
        in_specs=[pl.BlockSpec((tm, n), lambda i: (i, 0)),
                  pl.BlockSpec((n, n), lambda i: (0, 0)),
                  pl.BlockSpec((1, n), lambda i: (0, 0))],
        out_specs=pl.BlockSpec((tm, n), lambda i: (i, 0)),
        out_shape=jax.ShapeDtypeStruct((t, n), BF16),
        compiler_params=_cparams(("arbitrary",)),
        name="ssm_post",
    )(y, w_glu, b_glu.reshape(1, n))


def _dot_nt(a, b):
    return lax.dot_general(a, b, (((1,), (1,)), ((), ())), preferred_element_type=F32)


def _dot_tn(a, b):
    return lax.dot_general(a, b, (((0,), (0,)), ((), ())), preferred_element_type=F32)


def _dot_hi(a, b):
    return jnp.dot(a, b, preferred_element_type=F32, precision=HI)


def _sigmoid(x):
    return 0.5 * jnp.tanh(0.5 * x) + 0.5


def _silu(x):
    return x * _sigmoid(x)


DN_HALO = 8


CAST_SLICES = 16


def _cast_slices(src_refs, dst_refs):
    for sl in range(CAST_SLICES):
        for src, dst in zip(src_refs, dst_refs):
            n = src.shape[0] // CAST_SLICES
            dst[sl * n:(sl + 1) * n, :] = src[sl * n:(sl + 1) * n, :].astype(BF16)
        yield


def _dn_kernel(q_ref, k_ref, v_ref, z_ref, ab_ref, cq_ref, ck_ref, cv_ref, al_ref, dtb_ref, nw_ref,
               *rest, tt, heads, hp, ncast):
    cast_src, o_ref, cast_dst, scratch = rest[:ncast], rest[ncast], rest[ncast + 1:2 * ncast + 1], rest[2 * ncast + 1:]
    s_refs, xx_refs = scratch[:hp], scratch[hp:]

    @pl.when(pl.program_id(2) == 0)
    def _():
        for s_ref, xx_ref in zip(s_refs, xx_refs):
            s_ref[...] = jnp.zeros_like(s_ref)
            xx_ref[:, 0:DN_HALO, :] = jnp.zeros((3, DN_HALO, DN_HEAD_DIM), F32)

    ab = ab_ref[...]
    xa = ab + dtb_ref[...]
    sp = jnp.maximum(xa, 0.0) + jnp.log1p(jnp.exp(-jnp.abs(xa)))
    gate_vals = (-jnp.exp(al_ref[...]) * sp, _sigmoid(ab))
    gens = [_dn_head(hh, pl.program_id(1) * hp + hh, q_ref, k_ref, v_ref, z_ref, gate_vals, cq_ref, ck_ref, cv_ref,
                     nw_ref, o_ref, s_refs[hh], xx_refs[hh], tt=tt, heads=heads)
            for hh in range(hp)]
    if ncast:
        gens.append(_cast_slices(cast_src, cast_dst))
    for _ in itertools.zip_longest(*gens):
        pass


def _dn_head(hh, h, q_ref, k_ref, v_ref, z_ref, gate_vals, cq_ref, ck_ref, cv_ref, nw_ref,
             o_ref, s_ref, xx_ref, *, tt, heads):
    hal = DN_HALO
    ls = slice(hh * DN_HEAD_DIM, (hh + 1) * DN_HEAD_DIM)

    def conv(idx, x_ref, cw_ref):
        x = x_ref[:, ls]
        xx_ref[idx, hal:, :] = x
        w = cw_ref[:, ls]
        acc = w[CONV_WIDTH - 1:CONV_WIDTH, :] * x
        for j in range(CONV_WIDTH - 1):
            off = hal - (CONV_WIDTH - 1) + j
            acc = acc + w[j:j + 1, :] * xx_ref[idx, off:off + tt, :]
        xx_ref[idx, 0:hal, :] = x[tt - hal:, :]
        return _silu(acc)

    qc = conv(0, q_ref, cq_ref)
    kc = conv(1, k_ref, ck_ref)
    vc = conv(2, v_ref, cv_ref)
    qn = qc * lax.rsqrt(jnp.sum(qc * qc, axis=-1, keepdims=True) + NORM_EPS) * (DN_HEAD_DIM ** -0.5)
    kn = kc * lax.rsqrt(jnp.sum(kc * kc, axis=-1, keepdims=True) + NORM_EPS)

    g_all, beta_all = gate_vals
    lane = lax.broadcasted_iota(jnp.int32, g_all.shape, 1)
    g_col = jnp.sum(jnp.where(lane == h, g_all, 0.0), axis=-1, keepdims=True)
    beta_col = jnp.sum(jnp.where(lane == h + heads, beta_all, 0.0), axis=-1, keepdims=True)

    nch = tt // CHUNK
    row = lax.broadcasted_iota(jnp.int32, (tt, DN_HEAD_DIM), 0)
    pos = row & (CHUNK - 1)
    gc = jnp.broadcast_to(g_col, (tt, DN_HEAD_DIM))
    sh = 1
    while sh < CHUNK:
        gc = gc + jnp.where(pos >= sh, pltpu.roll(gc, sh, axis=0), 0.0)
        sh *= 2
    gc_row = jnp.transpose(gc)[0:1, :]
    egc = jnp.exp(gc)

    ri = lax.broadcasted_iota(jnp.int32, (tt, tt), 0)
    ci = lax.broadcasted_iota(jnp.int32, (tt, tt), 1)
    same = (ri // CHUNK) == (ci // CHUNK)
    causal = same & (ri >= ci)
    strict = same & (ri > ci)
    gc_wide = jnp.concatenate([gc] * (tt // DN_HEAD_DIM), axis=1)
    decay = jnp.where(causal, jnp.exp(jnp.where(causal, gc_wide - gc_row, 0.0)), 0.0)
    kb = kn * beta_col
    knb = kn.astype(BF16)
    yield
    a_raw = _dot_nt(kb.astype(BF16), knb)
    qk_raw = _dot_nt(qn.astype(BF16), knb)
    yield
    a_bd = jnp.where(strict, a_raw * decay, 0.0)
    qk_bd = jnp.where(causal, qk_raw * decay, 0.0).astype(BF16)

    def fold(m):
        out = m[0:CHUNK]
        for c in range(1, nch):
            out = out + m[c * CHUNK:(c + 1) * CHUNK]
        return out

    def spread(m):
        return jnp.where(same, jnp.concatenate([m] * nch, axis=0), 0.0)

    r64 = lax.broadcasted_iota(jnp.int32, (CHUNK, tt), 0)
    c64 = lax.broadcasted_iota(jnp.int32, (CHUNK, tt), 1)
    eye_cat = (r64 == (c64 & (CHUNK - 1))).astype(F32)
    pw_cat = fold(a_bd)
    inv_cat = eye_cat - pw_cat
    pw_bd = a_bd.astype(BF16)
    for _ in range(5):
        pw_cat = jnp.dot(pw_cat.astype(BF16), pw_bd, preferred_element_type=F32)
        yield
        pw_bd = spread(pw_cat).astype(BF16)
        inv_add = jnp.dot(inv_cat.astype(BF16), pw_bd, preferred_element_type=F32)
        yield
        inv_cat = inv_cat + inv_add
    inv_bd = spread(inv_cat).astype(BF16)
    rhs = jnp.concatenate([vc * beta_col, kb * egc], axis=1).astype(BF16)
    sol = jnp.dot(inv_bd, rhs, preferred_element_type=F32)
    yield
    u_all, w_all = sol[:, :DN_HEAD_DIM], sol[:, DN_HEAD_DIM:]
    qe = qn * egc

    outs = []
    zblk = jnp.zeros((CHUNK, DN_HEAD_DIM), BF16)
    s = s_ref[...]
    for c in range(nch):
        sl = slice(c * CHUNK, (c + 1) * CHUNK)
        wq = jnp.concatenate([w_all[sl], qe[sl]], axis=0).astype(BF16)
        ws = jnp.dot(wq, s.astype(BF16), preferred_element_type=F32)
        yield
        v_new = u_all[sl] - ws[:CHUNK]
        vnb = v_new.astype(BF16)
        v_pad = jnp.concatenate([zblk] * c + [vnb] + [zblk] * (nch - 1 - c), axis=0)
        gc_c = gc[sl]
        g_last = gc_c[CHUNK - 1:CHUNK, :]
        k_dec = (kn[sl] * jnp.exp(g_last - gc_c)).astype(BF16)
        s_add = _dot_tn(k_dec, vnb)
        o_add = jnp.dot(qk_bd[sl], v_pad, preferred_element_type=F32)
        yield
        s = s * jnp.exp(g_last) + s_add
        outs.append(ws[CHUNK:] + o_add)
    s_ref[...] = s
    o = jnp.concatenate(outs, axis=0)
    o = o * lax.rsqrt(jnp.mean(o * o, axis=-1, keepdims=True) + NORM_EPS) * nw_ref[...]
    o_ref[:, ls] = (o * _silu(z_ref[:, ls])).astype(o_ref.dtype)


def _deltanet(proj, ab, conv_w, a_log, dt_bias, norm_w, bsz, seq, heads, off_q, cast_weights=()):
    t = bsz * seq
    tt = _pick(seq, 256)
    nt = seq // tt
    hd = DN_HEAD_DIM
    cw = conv_w.reshape(CONV_WIDTH, 3 * heads * hd).astype(F32)
    pad = LANES - heads
    al = jnp.pad(a_log.astype(F32), (0, pad)).reshape(1, LANES)
    dtb = jnp.pad(dt_bias.astype(F32), (0, pad)).reshape(1, LANES)
    nw = norm_w.astype(F32).reshape(1, hd)

    hp = math.gcd(math.gcd(heads, off_q), 4)
    hw = hp * hd
    oq, nh = off_q // hp, heads // hp

    def act(o):
        return pl.BlockSpec((tt, hw), lambda b, h, i: (b * nt + i, oq + o * nh + h))

    def cws(o):
        return pl.BlockSpec((CONV_WIDTH, hw), lambda b, h, i: (0, o * nh + h))

    row = pl.BlockSpec((1, LANES), lambda b, h, i: (0, 0))
    nsteps = bsz * nh * nt
    ncast = len(cast_weights)
    cast_specs = []
    for w in cast_weights:
        rows, cols = w.shape
        assert rows % (nsteps * CAST_SLICES * 16) == 0, (rows, nsteps)
        cast_specs.append(pl.BlockSpec((rows // nsteps, cols), lambda b, h, i: ((b * nh + h) * nt + i, 0)))
    kern = functools.partial(_dn_kernel, tt=tt, heads=heads, hp=hp, ncast=ncast)
    outs = pl.pallas_call(
        kern,
        grid=(bsz, nh, nt),
        in_specs=[act(0), act(1), act(2), act(3),
                  pl.BlockSpec((tt, LANES), lambda b, h, i: (b * nt + i, 0)),
                  cws(0), cws(1), cws(2), row, row, row] + cast_specs,
        out_specs=[pl.BlockSpec((tt, hw), lambda b, h, i: (b * nt + i, h))] + cast_specs,
        out_shape=([jax.ShapeDtypeStruct((t, heads * hd), BF16)]
                   + [jax.ShapeDtypeStruct(w.shape, BF16) for w in cast_weights]),
        scratch_shapes=([pltpu.VMEM((hd, hd), F32)] * hp
                        + [pltpu.VMEM((3, tt + DN_HALO, hd), F32)] * hp),
        compiler_params=_cparams(("arbitrary", "arbitrary", "arbitrary")),
        name="deltanet",
    )(proj, proj, proj, proj, ab, cw, cw, cw, al, dtb, nw, *cast_weights)
    return outs[0], outs[1:]


def _merge_kernel(ys_ref, yd_ref, gs_ref, gd_ref, ws_ref, wd_ref, o_ref):
    ps = jnp.dot(ys_ref[...], ws_ref[...], preferred_element_type=F32)
    pd = jnp.dot(yd_ref[...], wd_ref[...], preferred_element_type=F32)
    o_ref[...] = (jax.nn.sigmoid(gs_ref[...]) * ps + jax.nn.sigmoid(gd_ref[...]) * pd).astype(o_ref.dtype)


def _merge(ys, yd, gates, w_ps, w_pd):
    t, ns = ys.shape
    d = w_ps.shape[1]
    tm = _pick(t, 512)
    tn = _pick(ns, 1024)
    nb = d // tn
    return pl.pallas_call(
        _merge_kernel,
        grid=(nb, t // tm),
        in_specs=[pl.BlockSpec((tm, ns), lambda j, i: (i, 0)),
                  pl.BlockSpec((tm, d), lambda j, i: (i, 0)),
                  pl.BlockSpec((tm, tn), lambda j, i: (i, j)),
                  pl.BlockSpec((tm, tn), lambda j, i: (i, nb + j)),
                  pl.BlockSpec((ns, tn), lambda j, i: (0, j)),
                  pl.BlockSpec((d, tn), lambda j, i: (0, j))],
        out_specs=pl.BlockSpec((tm, tn), lambda j, i: (i, j)),
        out_shape=jax.ShapeDtypeStruct((t, d), BF16),
        compiler_params=_cparams(("arbitrary", "arbitrary")),
        name="merge",
    )(ys, yd, gates, gates, w_ps, w_pd)


def _pack_bf16_pair(x):
    half = x.shape[1] // 2
    lo = pltpu.bitcast(x[:, :half].astype(BF16).astype(F32), jnp.uint32)
    hi = pltpu.bitcast(x[:, half:].astype(BF16).astype(F32), jnp.uint32)
    return (lo >> 16) | (hi & jnp.uint32(0xFFFF0000))


def _unpack_bf16_pair(u):
    lo = pltpu.bitcast(u << 16, F32).astype(BF16)
    hi = pltpu.bitcast(u & jnp.uint32(0xFFFF0000), F32).astype(BF16)
    return lo, hi


def _out_ln_kernel(m_ref, w_ref, h_ref, g_ref, b_ref, h1_ref, hp_ref):
    mix = jnp.dot(m_ref[...], w_ref[...], preferred_element_type=F32)
    h1 = _ln(DEEPNORM_ALPHA * h_ref[...] + mix, g_ref[...], b_ref[...])
    h1_ref[...] = h1
    hp_ref[...] = _pack_bf16_pair(h1)


def _out_ln(merged, w_out, h, g, b):
    t, d = h.shape
    tm = _pick(t, 512)
    return pl.pallas_call(
        _out_ln_kernel,
        grid=(t // tm,),
        in_specs=[pl.BlockSpec((tm, d), lambda i: (i, 0)),
                  pl.BlockSpec((d, d), lambda i: (0, 0)),
                  pl.BlockSpec((tm, d), lambda i: (i, 0)),
                  pl.BlockSpec((1, d), lambda i: (0, 0)),
                  pl.BlockSpec((1, d), lambda i: (0, 0))],
        out_specs=[pl.BlockSpec((tm, d), lambda i: (i, 0)),
                   pl.BlockSpec((tm, d // 2), lambda i: (i, 0))],
        out_shape=[jax.ShapeDtypeStruct((t, d), F32), jax.ShapeDtypeStruct((t, d // 2), jnp.uint32)],
        compiler_params=_cparams(("arbitrary",)),
        name="out_ln1",
    )(merged, w_out, h, g.reshape(1, d), b.reshape(1, d))


def _router_kernel(h_ref, w_ref, b_ref, idx_ref, gate_ref, rank_ref, cnt_ref, base_ref, *, n_exp, tm):
    i = pl.program_id(0)

    @pl.when(i == 0)
    def _():
        base_ref[...] = jnp.zeros_like(base_ref)

    logits = _dot_hi(h_ref[...], w_ref[...]) + b_ref[...]
    lane = lax.broadcasted_iota(jnp.int32, logits.shape, 1)
    vals = jnp.where(lane < n_exp, logits, -jnp.inf)
    sels, tops, ams = [], [], []
    for _ in range(TOP_K):
        m = jnp.max(vals, axis=-1, keepdims=True)
        am = jnp.min(jnp.where(vals == m, lane, LANES), axis=-1, keepdims=True)
        sel = lane == am
        vals = jnp.where(sel, -jnp.inf, vals)
        sels.append(sel)
        tops.append(m)
        ams.append(am)
    es = [jnp.exp(v - tops[0]) for v in tops]
    den = es[0] + es[1] + es[2] + es[3]
    onehot = sels[0] | sels[1] | sels[2] | sels[3]
    mt = onehot.astype(BF16)
    ri = lax.broadcasted_iota(jnp.int32, (tm, tm), 0)
    ci = lax.broadcasted_iota(jnp.int32, (tm, tm), 1)
    before = (ri > ci).astype(BF16)
    prior = jnp.dot(before, mt, preferred_element_type=F32) + base_ref[...]
    idx_o = jnp.zeros(logits.shape, jnp.int32)
    gate_o = jnp.zeros(logits.shape, F32)
    rank_o = jnp.zeros(logits.shape, jnp.int32)
    for k in range(TOP_K):
        rk = jnp.sum(jnp.where(sels[k], prior, 0.0), axis=-1, keepdims=True).astype(jnp.int32)
        idx_o = jnp.where(lane == k, ams[k], idx_o)
        gate_o = jnp.where(lane == k, es[k] / den, gate_o)
        rank_o = jnp.where(lane == k, rk, rank_o)
    idx_ref[...] = idx_o
    gate_ref[...] = gate_o
    rank_ref[...] = rank_o
    base_ref[...] += jnp.sum(onehot.astype(F32), axis=0, keepdims=True)
    cnt_ref[...] = base_ref[...]


def _router(h1, w_router, b_router):
    t, d = h1.shape
    n_exp = w_router.shape[1]
    tm = _pick(t, 256)
    w = jnp.pad(w_router.astype(F32), ((0, 0), (0, LANES - n_exp)))
    b = jnp.pad(b_router.astype(F32), (0, LANES - n_exp)).reshape(1, LANES)
    tok = pl.BlockSpec((tm, LANES), lambda i: (i, 0))
    kern = functools.partial(_router_kernel, n_exp=n_exp, tm=tm)
    return pl.pallas_call(
        kern,
        grid=(t // tm,),
        in_specs=[pl.BlockSpec((tm, d), lambda i: (i, 0)),
                  pl.BlockSpec((d, LANES), lambda i: (0, 0)),
                  pl.BlockSpec((1, LANES), lambda i: (0, 0))],
        out_specs=[tok, tok, tok, pl.BlockSpec((1, LANES), lambda i: (0, 0))],
        out_shape=[jax.ShapeDtypeStruct((t, LANES), jnp.int32), jax.ShapeDtypeStruct((t, LANES), F32),
                   jax.ShapeDtypeStruct((t, LANES), jnp.int32), jax.ShapeDtypeStruct((1, LANES), F32)],
        scratch_shapes=[pltpu.VMEM((1, LANES), F32)],
        compiler_params=_cparams(("arbitrary",)),
        name="router",
    )(h1, w, b)


def _dispatch_kernel(dest_ref, pend_ref, nused_ref, x_ref, xs_ref, zero_ref, sem, zsem, *, tt, n_exp, nblk):
    i = pl.program_id(0)
    base = i * (tt * TOP_K)

    @pl.when(i == 0)
    def _():
        zero_ref[...] = jnp.zeros_like(zero_ref)

        def zero_block(start):
            start = pl.multiple_of(start, ROW_BLOCK)
            return pltpu.make_async_copy(zero_ref, xs_ref.at[pl.ds(start, ROW_BLOCK)], zsem)

        def expert_has_rows(e):
            prev = jnp.where(e == 0, 0, pend_ref[jnp.maximum(e - 1, 0)])
            return pend_ref[e] > prev

        def start_expert(e, carry):
            @pl.when(expert_has_rows(e))
            def _():
                zero_block(pend_ref[e] - ROW_BLOCK).start()
            return carry

        def wait_expert(e, carry):
            @pl.when(expert_has_rows(e))
            def _():
                zero_block(0).wait()
            return carry

        def start_tail(b, carry):
            zero_block(b * ROW_BLOCK).start()
            return carry

        def wait_tail(b, carry):
            zero_block(0).wait()
            return carry

        lax.fori_loop(0, n_exp, start_expert, 0)
        lax.fori_loop(nused_ref[0], nblk, start_tail, 0)
        lax.fori_loop(0, n_exp, wait_expert, 0)
        lax.fori_loop(nused_ref[0], nblk, wait_tail, 0)

    def issue(r, carry):
        for k in range(TOP_K):
            row = dest_ref[base + r * TOP_K + k]
            pltpu.make_async_copy(x_ref.at[pl.ds(r, 1)], xs_ref.at[pl.ds(row, 1)], sem).start()
        return carry

    lax.fori_loop(0, tt, issue, 0, unroll=4)

    for k in range(TOP_K):
        pltpu.make_async_copy(x_ref, xs_ref.at[pl.ds(0, tt)], sem).wait()


def _dispatch(xp, dest, pad_ends, n_used, n_rows):
    t, dh = xp.shape
    tt = _pick(t, 256)
    kern = functools.partial(_dispatch_kernel, tt=tt, n_exp=pad_ends.shape[0], nblk=n_rows // ROW_BLOCK)
    return pl.pallas_call(
        kern,
        grid_spec=pltpu.PrefetchScalarGridSpec(
            num_scalar_prefetch=3,
            grid=(t // tt,),
            in_specs=[pl.BlockSpec((tt, dh), lambda i, *_: (i, 0))],
            out_specs=pl.BlockSpec(memory_space=pl.ANY),
            scratch_shapes=[pltpu.VMEM((ROW_BLOCK, dh), jnp.uint32),
                            pltpu.SemaphoreType.DMA(()), pltpu.SemaphoreType.DMA(())]),
        out_shape=jax.ShapeDtypeStruct((n_rows, dh), jnp.uint32),
        compiler_params=_cparams(("arbitrary",)),
        name="dispatch",
    )(dest, pad_ends.astype(jnp.int32), n_used, xp)


def _expert_kernel(be_ref, nb_ref, x_ref, wg_ref, wu_ref, bg_ref, bu_ref, wd_ref, bd_ref, o_ref, *, half):
    i = pl.program_id(0)
    j = pl.program_id(1)

    @pl.when(i < nb_ref[0])
    def _():
        lo, hi = _unpack_bf16_pair(x_ref[...])
        gate = (jnp.dot(lo, wg_ref[0, :half, :], preferred_element_type=F32)
                + jnp.dot(hi, wg_ref[0, half:, :], preferred_element_type=F32) + bg_ref[0])
        up = (jnp.dot(lo, wu_ref[0, :half, :], preferred_element_type=F32)
              + jnp.dot(hi, wu_ref[0, half:, :], preferred_element_type=F32) + bu_ref[0])
        gate = jnp.minimum(gate, SWIGLU_LIMIT)
        up = jnp.clip(up, -SWIGLU_LIMIT, SWIGLU_LIMIT)
        act = gate * jax.nn.sigmoid(SWIGLU_ALPHA * gate) * (up + 1.0)
        part = jnp.dot(act.astype(BF16), wd_ref[0], preferred_element_type=F32)

        @pl.when(j == 0)
        def _():
            o_ref[...] = part + bd_ref[0]

        @pl.when(j > 0)
        def _():
            o_ref[...] += part

    @pl.when((i >= nb_ref[0]) & (j == 0))
    def _():
        o_ref[...] = jnp.zeros_like(o_ref)


def _experts(xs, block_expert, n_used, w_gu, b_gu, w_down, b_down):
    n_rows, dh = xs.shape
    n_exp, d, ff2 = w_gu.shape
    ff = ff2 // 2
    tf = _pick(ff, 1024)
    nj = ff // tf
    nblk = n_rows // ROW_BLOCK
    b_gu3 = b_gu.astype(F32).reshape(n_exp, 1, ff2)
    b_d3 = b_down.astype(F32).reshape(n_exp, 1, d)

    def blk(i, nb):
        return jnp.maximum(jnp.minimum(i, nb[0] - 1), 0)

    def jj(i, j, nb):
        return jnp.where(i < nb[0], j, nj - 1)

    kern = functools.partial(_expert_kernel, half=d // 2)
    return pl.pallas_call(
        kern,
        grid_spec=pltpu.PrefetchScalarGridSpec(
            num_scalar_prefetch=2,
            grid=(nblk, nj),
            in_specs=[
                pl.BlockSpec((ROW_BLOCK, dh), lambda i, j, be, nb: (blk(i, nb), 0)),
                pl.BlockSpec((1, d, tf), lambda i, j, be, nb: (be[blk(i, nb)], 0, jj(i, j, nb))),
                pl.BlockSpec((1, d, tf), lambda i, j, be, nb: (be[blk(i, nb)], 0, nj + jj(i, j, nb))),
                pl.BlockSpec((1, 1, tf), lambda i, j, be, nb: (be[blk(i, nb)], 0, jj(i, j, nb))),
                pl.BlockSpec((1, 1, tf), lambda i, j, be, nb: (be[blk(i, nb)], 0, nj + jj(i, j, nb))),
                pl.BlockSpec((1, tf, d), lambda i, j, be, nb: (be[blk(i, nb)], jj(i, j, nb), 0)),
                pl.BlockSpec((1, 1, d), lambda i, j, be, nb: (be[blk(i, nb)], 0, 0)),
            ],
            out_specs=pl.BlockSpec((ROW_BLOCK, d), lambda i, j, be, nb: (i, 0))),
        out_shape=jax.ShapeDtypeStruct((n_rows, d), F32),
        compiler_params=_cparams(("arbitrary", "arbitrary")),
        name="experts",
    )(block_expert, n_used, xs, w_gu, w_gu, b_gu3, b_gu3, w_down, b_d3)


def _combine_kernel(dest_ref, y_ref, h_ref, gate_ref, g_ref, b_ref, o_ref, buf_ref, sem, *, tt):
    i = pl.program_id(0)
    base = i * (tt * TOP_K)

    def issue(r, carry):
        for k in range(TOP_K):
            row = dest_ref[base + r * TOP_K + k]
            pltpu.make_async_copy(y_ref.at[pl.ds(row, 1)], buf_ref.at[k, pl.ds(r, 1)], sem).start()
        return carry

    lax.fori_loop(0, tt, issue, 0, unroll=4)

    for k in range(TOP_K):
        pltpu.make_async_copy(y_ref.at[pl.ds(0, tt)], buf_ref.at[k], sem).wait()
    gates = gate_ref[...]
    ffn = gates[:, 0:1] * buf_ref[0]
    for k in range(1, TOP_K):
        ffn = ffn + gates[:, k:k + 1] * buf_ref[k]
    o_ref[...] = _ln(DEEPNORM_ALPHA * h_ref[...] + ffn, g_ref[...], b_ref[...])


def _combine(y, dest, h1, gates, g, b):
    t, d = h1.shape
    tt = _pick(t, 128)
    kern = functools.partial(_combine_kernel, tt=tt)
    return pl.pallas_call(
        kern,
        grid_spec=pltpu.PrefetchScalarGridSpec(
            num_scalar_prefetch=1,
            grid=(t // tt,),
            in_specs=[pl.BlockSpec(memory_space=pl.ANY),
                      pl.BlockSpec((tt, d), lambda i, dst: (i, 0)),
                      pl.BlockSpec((tt, LANES), lambda i, dst: (i, 0)),
                      pl.BlockSpec((1, d), lambda i, dst: (0, 0)),
                      pl.BlockSpec((1, d), lambda i, dst: (0, 0))],
            out_specs=pl.BlockSpec((tt, d), lambda i, dst: (i, 0)),
            scratch_shapes=[pltpu.VMEM((TOP_K, tt, d), F32), pltpu.SemaphoreType.DMA(())]),
        out_shape=jax.ShapeDtypeStruct((t, d), F32),
        compiler_params=_cparams(("arbitrary",)),
        name="combine",
    )(dest, y, h1, gates, g.reshape(1, d), b.reshape(1, d))


def kernel(x, ln_in_g, ln_in_b, w_in, lam_re, lam_im, log_step, ssm_b_re, ssm_b_im, ssm_c_re, ssm_c_im, ssm_d, w_glu, b_glu, conv_w, a_log, dt_bias, dn_norm_w, w_proj_ssm, w_proj_dn, w_out, ln1_g, ln1_b, w_router, b_router, w_gate_up, b_gate_up, w_down, b_down, ln2_g, ln2_b):
    bsz, seq, d = x.shape
    t = bsz * seq
    sw = d // 2
    heads = d // DN_HEAD_DIM
    dn = heads * DN_HEAD_DIM
    groups = sw // SSM_GROUP_DIM
    n_state = lam_re.shape[-1]
    n_exp = w_router.shape[-1]
    assert w_in.shape[0] == DEPTH == 1
    assert seq % (CHUNK * 4) == 0 and sw % LANES == 0 and groups % 2 == 0 and 2 * heads <= LANES

    h, hb = _ln_in(x.reshape(t, d), ln_in_g, ln_in_b)

    wi = w_in[0]
    c_ab = sw + 4 * dn
    w_ab = jnp.pad(wi[:, c_ab:c_ab + 2 * heads], ((0, 0), (0, LANES - 2 * heads))).astype(BF16)
    proj = _matmul(hb, wi, F32, "proj_in", ncols=c_ab)
    ab = _matmul(hb, w_ab, F32, "proj_ab")
    gates = _matmul(hb, wi[:, c_ab + 2 * heads:], F32, "proj_gates")

    ops = _s5_operators(lam_re[0], lam_im[0], log_step[0], ssm_b_re[0], ssm_b_im[0],
                        ssm_c_re[0], ssm_c_im[0], ssm_d[0])
    y_s = _s5_scan(proj, ops, bsz, seq, sw, n_state)
    y_ssm = _ssm_post(y_s, w_glu[0].astype(BF16), b_glu[0].astype(F32))

    ff2 = w_gate_up.shape[-1]
    y_dn, (w_gu_bf, w_dn_bf) = _deltanet(
        proj, ab, conv_w[0], a_log[0], dt_bias[0], dn_norm_w[0], bsz, seq, heads, off_q=sw // LANES,
        cast_weights=(w_gate_up[0].reshape(n_exp * d, ff2), w_down[0].reshape(n_exp * (ff2 // 2), d)))
    w_gu_bf = w_gu_bf.reshape(n_exp, d, ff2)
    w_dn_bf = w_dn_bf.reshape(n_exp, ff2 // 2, d)

    merged = _merge(y_ssm, y_dn, gates, w_proj_ssm[0].astype(BF16), w_proj_dn[0].astype(BF16))
    h1, h1p = _out_ln(merged, w_out[0].astype(BF16), h, ln1_g[0], ln1_b[0])

    idx, gates, rank, cnt = _router(h1, w_router[0], b_router[0])
    counts = cnt[0, :n_exp].astype(jnp.int32)
    padded = (counts + ROW_BLOCK - 1) // ROW_BLOCK * ROW_BLOCK
    pad_ends = jnp.cumsum(padded)
    pad_starts = pad_ends - padded
    dest = (pad_starts[idx[:, :TOP_K]] + rank[:, :TOP_K]).astype(jnp.int32).reshape(t * TOP_K)
    n_rows = t * TOP_K + n_exp * ROW_BLOCK
    nblk = n_rows // ROW_BLOCK
    blk_start = jnp.arange(nblk, dtype=jnp.int32) * ROW_BLOCK
    block_expert = jnp.minimum(jnp.sum((pad_ends[None, :] <= blk_start[:, None]).astype(jnp.int32), axis=1),
                               n_exp - 1).astype(jnp.int32)
    n_used = (pad_ends[-1:] // ROW_BLOCK).astype(jnp.int32)
    xs = _dispatch(h1p, dest, pad_ends, n_used, n_rows)
    y = _experts(xs, block_expert, n_used, w_gu_bf, b_gate_up[0], w_dn_bf, b_down[0])
    out = _combine(y, dest, h1, gates, ln2_g[0], ln2_b[0])
    return out.reshape(bsz, seq, d)
```

```python
import functools
import itertools
import math

import jax
import jax.numpy as jnp
from jax import lax
from jax.experimental import pallas as pl
from jax.experimental.pallas import tpu as pltpu

F32 = jnp.float32
BF16 = jnp.bfloat16
HI = lax.Precision.HIGHEST

LANES = 128
CHUNK = 64
LN_EPS = 1e-5
NORM_EPS = 1e-6
SSM_GROUP_DIM = 16
SSM_TC = 16
DN_HEAD_DIM = 128
CONV_WIDTH = 4
TOP_K = 4
SWIGLU_LIMIT = 7.0
SWIGLU_ALPHA = 1.702
DEPTH = 1
DEEPNORM_ALPHA = (2 * DEPTH) ** 0.25
ROW_BLOCK = 512
VMEM_LIMIT = 56 * 1024 * 1024


def _cparams(sem):
    return pltpu.CompilerParams(dimension_semantics=sem, vmem_limit_bytes=VMEM_LIMIT)


def _pick(n, pref):
    t = min(n, pref)
    while n % t:
        t //= 2
    return t


def _ln(x, g, b):
    mu = jnp.mean(x, axis=-1, keepdims=True)
    xc = x - mu
    var = jnp.mean(xc * xc, axis=-1, keepdims=True)
    return xc * lax.rsqrt(var + LN_EPS) * g + b


def _ln_in_kernel(x_ref, g_ref, b_ref, h_ref, hb_ref):
    h = _ln(x_ref[...], g_ref[...], b_ref[...])
    h_ref[...] = h
    hb_ref[...] = h.astype(BF16)


def _ln_in(x, g, b):
    t, d = x.shape
    tm = _pick(t, 512)
    return pl.pallas_call(
        _ln_in_kernel,
        grid=(t // tm,),
        in_specs=[pl.BlockSpec((tm, d), lambda i: (i, 0)),
                  pl.BlockSpec((1, d), lambda i: (0, 0)),
                  pl.BlockSpec((1, d), lambda i: (0, 0))],
        out_specs=[pl.BlockSpec((tm, d), lambda i: (i, 0)),
                   pl.BlockSpec((tm, d), lambda i: (i, 0))],
        out_shape=[jax.ShapeDtypeStruct((t, d), F32), jax.ShapeDtypeStruct((t, d), BF16)],
        compiler_params=_cparams(("arbitrary",)),
        name="ln_in",
    )(x, g.reshape(1, d), b.reshape(1, d))


def _mm_kernel(x_ref, w_ref, o_ref):
    o_ref[...] = jnp.dot(x_ref[...], w_ref[...].astype(BF16), preferred_element_type=F32).astype(o_ref.dtype)


def _matmul(x, w, out_dtype, name, ncols=None):
    m, k = x.shape
    n = w.shape[1] if ncols is None else ncols
    tm = _pick(m, 1024)
    tn = _pick(n, 1024)
    return pl.pallas_call(
        _mm_kernel,
        grid=(n // tn, m // tm),
        in_specs=[pl.BlockSpec((tm, k), lambda j, i: (i, 0)),
                  pl.BlockSpec((k, tn), lambda j, i: (0, j))],
        out_specs=pl.BlockSpec((tm, tn), lambda j, i: (i, j)),
        out_shape=jax.ShapeDtypeStruct((m, n), out_dtype),
        compiler_params=_cparams(("arbitrary", "arbitrary")),
        name=name,
    )(x, w)


def _s5_operators(lam_re, lam_im, log_step, b_re, b_im, c_re, c_im, d_skip):
    g, p = lam_re.shape
    hd, tc = SSM_GROUP_DIM, SSM_TC
    lr, li = lam_re.astype(F32), lam_im.astype(F32)
    step = jnp.exp(log_step.astype(F32))[:, None]
    tau = jnp.arange(tc + 1, dtype=F32)[:, None, None]
    mag = jnp.exp(lr * step * tau)
    pr, pi = mag * jnp.cos(li * step * tau), mag * jnp.sin(li * step * tau)
    a_re, a_im = pr[1], pi[1]
    den = lr * lr + li * li
    nr, ni = a_re - 1.0, a_im
    f_re = (nr * lr + ni * li) / den
    f_im = (ni * lr - nr * li) / den
    br, bi = b_re.astype(F32), b_im.astype(F32)
    bb_re = f_re[..., None] * br - f_im[..., None] * bi
    bb_im = f_re[..., None] * bi + f_im[..., None] * br
    cr, ci = c_re.astype(F32), c_im.astype(F32)
    ca_re = cr[None] * pr[:, :, None, :] - ci[None] * pi[:, :, None, :]
    ca_im = cr[None] * pi[:, :, None, :] + ci[None] * pr[:, :, None, :]
    ca_g = jnp.concatenate([jnp.transpose(ca_re[:tc], (1, 0, 2, 3)).reshape(g, tc * hd, p),
                            -jnp.transpose(ca_im[:tc], (1, 0, 2, 3)).reshape(g, tc * hd, p)], axis=2)
    bb_g = jnp.concatenate([bb_re, bb_im], axis=1)
    kk = jnp.einsum('gxp,gpi->gxi', ca_g, bb_g, precision=HI)
    kk = jnp.transpose(kk.reshape(g, tc, hd, hd), (1, 0, 2, 3))
    kk = kk.at[0].add(d_skip.astype(F32)[:, :, None] * jnp.eye(hd, dtype=F32)[None])
    gs = LANES // hd
    ns = g // gs
    dup = LANES // p
    eye = jnp.eye(gs, dtype=F32)
    k5 = jnp.transpose(kk.reshape(tc, ns, gs, hd, hd), (1, 0, 2, 4, 3))
    bm = (k5[:, :, :, :, None, :] * eye[None, None, :, None, :, None]).reshape(ns, tc, LANES, LANES)
    dt = jnp.arange(tc)

    def rows(x_re, x_im):
        x = jnp.stack([x_re, x_im], axis=1).reshape(tc, 2, ns, gs * hd, p)
        x = jnp.transpose(x, (2, 0, 1, 3, 4))
        return jnp.concatenate([x] * dup, axis=-1)

    prr, pir = pr[tc - 1 - dt], pi[tc - 1 - dt]
    bc_re = prr[:, :, None, :] * jnp.transpose(bb_re, (0, 2, 1))[None] - pir[:, :, None, :] * jnp.transpose(bb_im, (0, 2, 1))[None]
    bc_im = prr[:, :, None, :] * jnp.transpose(bb_im, (0, 2, 1))[None] + pir[:, :, None, :] * jnp.transpose(bb_re, (0, 2, 1))[None]
    bcc = rows(bc_re, bc_im)
    cct = rows(ca_re[1:], -ca_im[1:])
    a16r = pr[tc].reshape(1, g * p)
    a16i = pi[tc].reshape(1, g * p)
    return (bm.astype(BF16), bcc.astype(BF16), cct.astype(BF16), a16r, a16i)


def _s5_kernel(x_ref, bm_ref, bcc_ref, cct_ref, ar_ref, ai_ref, y_ref,
               mf_ref, bcf_ref, ccf_ref, st_ref, cr_ref, ci_ref, *, ncb, sl, n_state):
    tc = SSM_TC

    @pl.when((pl.program_id(1) == 0) & (pl.program_id(2) == 0))
    def _():
        zero = jnp.zeros((LANES, LANES), BF16)
        for di in range(tc):
            for do in range(tc):
                mf_ref[di * LANES:(di + 1) * LANES, do * LANES:(do + 1) * LANES] = (
                    bm_ref[0, do - di] if do >= di else zero)
        r = lax.broadcasted_iota(jnp.int32, (LANES, sl), 0)
        c = lax.broadcasted_iota(jnp.int32, (LANES, sl), 1)
        same_group = (r // SSM_GROUP_DIM) == (c // n_state)

        def widen(blk):
            wide = jnp.concatenate([blk.astype(F32)] * (sl // LANES), axis=1)
            return jnp.where(same_group, wide, 0.0).astype(BF16)

        for dt in range(tc):
            for ri in range(2):
                bcf_ref[dt * LANES:(dt + 1) * LANES, ri * sl:(ri + 1) * sl] = widen(bcc_ref[0, dt, ri])
                ccf_ref[dt * LANES:(dt + 1) * LANES, ri * sl:(ri + 1) * sl] = widen(cct_ref[0, dt, ri])

    @pl.when(pl.program_id(2) == 0)
    def _():
        cr_ref[...] = jnp.zeros_like(cr_ref)
        ci_ref[...] = jnp.zeros_like(ci_ref)

    xs = jnp.concatenate([x_ref[pl.ds(dt, ncb, stride=SSM_TC), :] for dt in range(SSM_TC)],
                         axis=1).astype(BF16)
    st_ref[...] = jnp.dot(xs, bcf_ref[...], preferred_element_type=F32)
    ar = ar_ref[...]
    ai = ai_ref[...]

    def body(c, carry):
        sr, si = carry
        lr = st_ref[pl.ds(c, 1), :sl]
        li = st_ref[pl.ds(c, 1), sl:]
        st_ref[pl.ds(c, 1), :sl] = sr
        st_ref[pl.ds(c, 1), sl:] = si
        return ar * sr - ai * si + lr, ar * si + ai * sr + li

    sr, si = lax.fori_loop(0, ncb, body, (cr_ref[...], ci_ref[...]), unroll=8)
    cr_ref[...] = sr
    ci_ref[...] = si
    y = (jnp.dot(xs, mf_ref[...], preferred_element_type=F32)
         + _dot_nt(st_ref[...].astype(BF16), ccf_ref[...]))
    for dt in range(SSM_TC):
        y_ref[pl.ds(dt, ncb, stride=SSM_TC), :] = y[:, dt * LANES:(dt + 1) * LANES]


def _s5_scan(proj, ops, bsz, seq, sw, n_state):
    bm, bcc, cct, a16r, a16i = ops
    ns = sw // LANES
    sl = (LANES // SSM_GROUP_DIM) * n_state
    ncb = _pick(seq // SSM_TC, 128)
    rt = ncb * SSM_TC
    nt = seq // rt
    kw = SSM_TC * LANES
    kern = functools.partial(_s5_kernel, ncb=ncb, sl=sl, n_state=n_state)
    return pl.pallas_call(
        kern,
        grid=(ns, bsz, nt),
        in_specs=[pl.BlockSpec((rt, LANES), lambda s, b, i: (b * nt + i, s)),
                  pl.BlockSpec((1, SSM_TC, LANES, LANES), lambda s, b, i: (s, 0, 0, 0)),
                  pl.BlockSpec((1, SSM_TC, 2, LANES, LANES), lambda s, b, i: (s, 0, 0, 0, 0)),
                  pl.BlockSpec((1, SSM_TC, 2, LANES, LANES), lambda s, b, i: (s, 0, 0, 0, 0)),
                  pl.BlockSpec((1, sl), lambda s, b, i: (0, s)),
                  pl.BlockSpec((1, sl), lambda s, b, i: (0, s))],
        out_specs=pl.BlockSpec((rt, LANES), lambda s, b, i: (b * nt + i, s)),
        out_shape=jax.ShapeDtypeStruct((bsz * seq, sw), F32),
        scratch_shapes=[pltpu.VMEM((kw, kw), BF16), pltpu.VMEM((kw, 2 * sl), BF16), pltpu.VMEM((kw, 2 * sl), BF16),
                        pltpu.VMEM((ncb, 2 * sl), F32), pltpu.VMEM((1, sl), F32), pltpu.VMEM((1, sl), F32)],
        compiler_params=_cparams(("arbitrary", "arbitrary", "arbitrary")),
        name="s5_scan",
    )(proj, bm, bcc, cct, a16r, a16i)


def _ssm_post_kernel(y_ref, w_ref, b_ref, o_ref):
    y = y_ref[...]
    yg = 0.5 * y * (1.0 + lax.erf(y * (1.0 / math.sqrt(2.0))))
    s = jnp.dot(yg.astype(BF16), w_ref[...], preferred_element_type=F32) + b_ref[...]
    o_ref[...] = (yg * jax.nn.sigmoid(s)).astype(o_ref.dtype)


def _ssm_post(y, w_glu, b_glu):
    t, n = y.shape
    tm = _pick(t, 512)
    return pl.pallas_call(
        _ssm_post_kernel,
        grid=(t // tm,),
        in_specs=[pl.BlockSpec((tm, n), lambda i: (i, 0)),
                  pl.BlockSpec((n, n), lambda i: (0, 0)),
                  pl.BlockSpec((1, n), lambda i: (0, 0))],
        out_specs=pl.BlockSpec((tm, n), lambda i: (i, 0)),
        out_shape=jax.ShapeDtypeStruct((t, n), BF16),
        compiler_params=_cparams(("arbitrary",)),
        name="ssm_post",
    )(y, w_glu, b_glu.reshape(1, n))


def _dot_nt(a, b):
    return lax.dot_general(a, b, (((1,), (1,)), ((), ())), preferred_element_type=F32)


def _dot_tn(a, b):
    return lax.dot_general(a, b, (((0,), (0,)), ((), ())), preferred_element_type=F32)


def _dot_hi(a, b):
    return jnp.dot(a, b, preferred_element_type=F32, precision=HI)


def _sigmoid(x):
    return 0.5 * jnp.tanh(0.5 * x) + 0.5


def _silu(x):
    return x * _sigmoid(x)


DN_HALO = 8


CAST_SLICES = 16


def _cast_slices(src_refs, dst_refs):
    for sl in range(CAST_SLICES):
        for src, dst in zip(src_refs, dst_refs):
            n = src.shape[0] // CAST_SLICES
            dst[sl * n:(sl + 1) * n, :] = src[sl * n:(sl + 1) * n, :].astype(BF16)
        yield


def _dn_kernel(q_ref, k_ref, v_ref, z_ref, ab_ref, cq_ref, ck_ref, cv_ref, al_ref, dtb_ref, nw_ref,
               *rest, tt, heads, hp, ncast):
    cast_src, o_ref, cast_dst, scratch = rest[:ncast], rest[ncast], rest[ncast + 1:2 * ncast + 1], rest[2 * ncast + 1:]
    s_refs, xx_refs = scratch[:hp], scratch[hp:]

    @pl.when(pl.program_id(2) == 0)
    def _():
        for s_ref, xx_ref in zip(s_refs, xx_refs):
            s_ref[...] = jnp.zeros_like(s_ref)
            xx_ref[:, 0:DN_HALO, :] = jnp.zeros((3, DN_HALO, DN_HEAD_DIM), F32)

    ab = ab_ref[...]
    xa = ab + dtb_ref[...]
    sp = jnp.maximum(xa, 0.0) + jnp.log1p(jnp.exp(-jnp.abs(xa)))
    gc_all = -jnp.exp(al_ref[...]) * sp
    pos = lax.broadcasted_iota(jnp.int32, gc_all.shape, 0) & (CHUNK - 1)
    sh = 1
    while sh < CHUNK:
        gc_all = gc_all + jnp.where(pos >= sh, pltpu.roll(gc_all, sh, axis=0), 0.0)
        sh *= 2
    gate_vals = (gc_all, jnp.exp(gc_all), _sigmoid(ab))
    gens = [_dn_head(hh, pl.program_id(1) * hp + hh, q_ref, k_ref, v_ref, z_ref, gate_vals, cq_ref, ck_ref, cv_ref,
                     nw_ref, o_ref, s_refs[hh], xx_refs[hh], tt=tt, heads=heads)
            for hh in range(hp)]
    if ncast:
        gens.append(_cast_slices(cast_src, cast_dst))
    for _ in itertools.zip_longest(*gens):
        pass


def _dn_head(hh, h, q_ref, k_ref, v_ref, z_ref, gate_vals, cq_ref, ck_ref, cv_ref, nw_ref,
             o_ref, s_ref, xx_ref, *, tt, heads):
    hal = DN_HALO
    ls = slice(hh * DN_HEAD_DIM, (hh + 1) * DN_HEAD_DIM)

    def conv(idx, x_ref, cw_ref):
        x = x_ref[:, ls]
        xx_ref[idx, hal:, :] = x
        w = cw_ref[:, ls]
        acc = w[CONV_WIDTH - 1:CONV_WIDTH, :] * x
        for j in range(CONV_WIDTH - 1):
            off = hal - (CONV_WIDTH - 1) + j
            acc = acc + w[j:j + 1, :] * xx_ref[idx, off:off + tt, :]
        xx_ref[idx, 0:hal, :] = x[tt - hal:, :]
        return _silu(acc)

    qc = conv(0, q_ref, cq_ref)
    kc = conv(1, k_ref, ck_ref)
    vc = conv(2, v_ref, cv_ref)
    qn = qc * lax.rsqrt(jnp.sum(qc * qc, axis=-1, keepdims=True) + NORM_EPS) * (DN_HEAD_DIM ** -0.5)
    kn = kc * lax.rsqrt(jnp.sum(kc * kc, axis=-1, keepdims=True) + NORM_EPS)

    gc_all, egc_all, beta_all = gate_vals
    lane = lax.broadcasted_iota(jnp.int32, gc_all.shape, 1)

    def pick(vals, ln):
        col = jnp.sum(jnp.where(lane == ln, vals, 0.0), axis=-1, keepdims=True)
        return jnp.broadcast_to(col, (tt, DN_HEAD_DIM))

    beta_col = pick(beta_all, h + heads)
    gc = pick(gc_all, h)
    egc = pick(egc_all, h)
    nch = tt // CHUNK
    gc_row = jnp.transpose(gc)[0:1, :]

    ri = lax.broadcasted_iota(jnp.int32, (tt, tt), 0)
    ci = lax.broadcasted_iota(jnp.int32, (tt, tt), 1)
    same = (ri // CHUNK) == (ci // CHUNK)
    causal = same & (ri >= ci)
    strict = same & (ri > ci)
    gc_wide = jnp.concatenate([gc] * (tt // DN_HEAD_DIM), axis=1)
    decay = jnp.where(causal, jnp.exp(jnp.where(causal, gc_wide - gc_row, 0.0)), 0.0)
    kb = kn * beta_col
    knb = kn.astype(BF16)
    yield
    a_raw = _dot_nt(kb.astype(BF16), knb)
    qk_raw = _dot_nt(qn.astype(BF16), knb)
    yield
    a_bd = jnp.where(strict, a_raw * decay, 0.0)
    qk_bd = jnp.where(causal, qk_raw * decay, 0.0).astype(BF16)

    def fold(m):
        out = m[0:CHUNK]
        for c in range(1, nch):
            out = out + m[c * CHUNK:(c + 1) * CHUNK]
        return out

    def spread(m):
        return jnp.where(same, jnp.concatenate([m] * nch, axis=0), 0.0)

    r64 = lax.broadcasted_iota(jnp.int32, (CHUNK, tt), 0)
    c64 = lax.broadcasted_iota(jnp.int32, (CHUNK, tt), 1)
    eye_cat = (r64 == (c64 & (CHUNK - 1))).astype(F32)
    pw_cat = fold(a_bd)
    inv_cat = eye_cat - pw_cat
    pw_bd = a_bd.astype(BF16)
    for _ in range(5):
        pw_cat = jnp.dot(pw_cat.astype(BF16), pw_bd, preferred_element_type=F32)
        yield
        pw_bd = spread(pw_cat).astype(BF16)
        inv_add = jnp.dot(inv_cat.astype(BF16), pw_bd, preferred_element_type=F32)
        yield
        inv_cat = inv_cat + inv_add
    inv_bd = spread(inv_cat).astype(BF16)
    rhs = jnp.concatenate([vc * beta_col, kb * egc], axis=1).astype(BF16)
    sol = jnp.dot(inv_bd, rhs, preferred_element_type=F32)
    yield
    u_all, w_all = sol[:, :DN_HEAD_DIM], sol[:, DN_HEAD_DIM:]
    qe = qn * egc

    outs = []
    zblk = jnp.zeros((CHUNK, DN_HEAD_DIM), BF16)
    s = s_ref[...]
    for c in range(nch):
        sl = slice(c * CHUNK, (c + 1) * CHUNK)
        wq = jnp.concatenate([w_all[sl], qe[sl]], axis=0).astype(BF16)
        ws = jnp.dot(wq, s.astype(BF16), preferred_element_type=F32)
        yield
        v_new = u_all[sl] - ws[:CHUNK]
        vnb = v_new.astype(BF16)
        v_pad = jnp.concatenate([zblk] * c + [vnb] + [zblk] * (nch - 1 - c), axis=0)
        gc_c = gc[sl]
        g_last = gc_c[CHUNK - 1:CHUNK, :]
        k_dec = (kn[sl] * jnp.exp(g_last - gc_c)).astype(BF16)
        s_add = _dot_tn(k_dec, vnb)
        o_add = jnp.dot(qk_bd[sl], v_pad, preferred_element_type=F32)
        yield
        s = s * jnp.exp(g_last) + s_add
        outs.append(ws[CHUNK:] + o_add)
    s_ref[...] = s
    o = jnp.concatenate(outs, axis=0)
    o = o * lax.rsqrt(jnp.mean(o * o, axis=-1, keepdims=True) + NORM_EPS) * nw_ref[...]
    o_ref[:, ls] = (o * _silu(z_ref[:, ls])).astype(o_ref.dtype)


def _deltanet(proj, ab, conv_w, a_log, dt_bias, norm_w, bsz, seq, heads, off_q, cast_weights=()):
    t = bsz * seq
    tt = _pick(seq, 256)
    nt = seq // tt
    hd = DN_HEAD_DIM
    cw = conv_w.reshape(CONV_WIDTH, 3 * heads * hd).astype(F32)
    pad = LANES - heads
    al = jnp.pad(a_log.astype(F32), (0, pad)).reshape(1, LANES)
    dtb = jnp.pad(dt_bias.astype(F32), (0, pad)).reshape(1, LANES)
    nw = norm_w.astype(F32).reshape(1, hd)

    hp = math.gcd(math.gcd(heads, off_q), 4)
    hw = hp * hd
    oq, nh = off_q // hp, heads // hp

    def act(o):
        return pl.BlockSpec((tt, hw), lambda b, h, i: (b * nt + i, oq + o * nh + h))

    def cws(o):
        return pl.BlockSpec((CONV_WIDTH, hw), lambda b, h, i: (0, o * nh + h))

    row = pl.BlockSpec((1, LANES), lambda b, h, i: (0, 0))
    nsteps = bsz * nh * nt
    ncast = len(cast_weights)
    cast_specs = []
    for w in cast_weights:
        rows, cols = w.shape
        assert rows % (nsteps * CAST_SLICES * 16) == 0, (rows, nsteps)
        cast_specs.append(pl.BlockSpec((rows // nsteps, cols), lambda b, h, i: ((b * nh + h) * nt + i, 0)))
    kern = functools.partial(_dn_kernel, tt=tt, heads=heads, hp=hp, ncast=ncast)
    outs = pl.pallas_call(
        kern,
        grid=(bsz, nh, nt),
        in_specs=[act(0), act(1), act(2), act(3),
                  pl.BlockSpec((tt, LANES), lambda b, h, i: (b * nt + i, 0)),
                  cws(0), cws(1), cws(2), row, row, row] + cast_specs,
        out_specs=[pl.BlockSpec((tt, hw), lambda b, h, i: (b * nt + i, h))] + cast_specs,
        out_shape=([jax.ShapeDtypeStruct((t, heads * hd), BF16)]
                   + [jax.ShapeDtypeStruct(w.shape, BF16) for w in cast_weights]),
        scratch_shapes=([pltpu.VMEM((hd, hd), F32)] * hp
                        + [pltpu.VMEM((3, tt + DN_HALO, hd), F32)] * hp),
        compiler_params=_cparams(("arbitrary", "arbitrary", "arbitrary")),
        name="deltanet",
    )(proj, proj, proj, proj, ab, cw, cw, cw, al, dtb, nw, *cast_weights)
    return outs[0], outs[1:]


def _merge_kernel(ys_ref, yd_ref, gs_ref, gd_ref, ws_ref, wd_ref, o_ref):
    ps = jnp.dot(ys_ref[...], ws_ref[...], preferred_element_type=F32)
    pd = jnp.dot(yd_ref[...], wd_ref[...], preferred_element_type=F32)
    o_ref[...] = (jax.nn.sigmoid(gs_ref[...]) * ps + jax.nn.sigmoid(gd_ref[...]) * pd).astype(o_ref.dtype)


def _merge(ys, yd, gates, gate_col, w_ps, w_pd):
    t, ns = ys.shape
    d = w_ps.shape[1]
    tm = _pick(t, 512)
    tn = _pick(ns, 1024)
    nb = d // tn
    off = gate_col // tn
    assert gate_col % tn == 0
    return pl.pallas_call(
        _merge_kernel,
        grid=(nb, t // tm),
        in_specs=[pl.BlockSpec((tm, ns), lambda j, i: (i, 0)),
                  pl.BlockSpec((tm, d), lambda j, i: (i, 0)),
                  pl.BlockSpec((tm, tn), lambda j, i: (i, off + j)),
                  pl.BlockSpec((tm, tn), lambda j, i: (i, off + nb + j)),
                  pl.BlockSpec((ns, tn), lambda j, i: (0, j)),
                  pl.BlockSpec((d, tn), lambda j, i: (0, j))],
        out_specs=pl.BlockSpec((tm, tn), lambda j, i: (i, j)),
        out_shape=jax.ShapeDtypeStruct((t, d), BF16),
        compiler_params=_cparams(("arbitrary", "arbitrary")),
        name="merge",
    )(ys, yd, gates, gates, w_ps, w_pd)


def _pack_bf16_pair(x):
    half = x.shape[1] // 2
    lo = pltpu.bitcast(x[:, :half].astype(BF16).astype(F32), jnp.uint32)
    hi = pltpu.bitcast(x[:, half:].astype(BF16).astype(F32), jnp.uint32)
    return (lo >> 16) | (hi & jnp.uint32(0xFFFF0000))


def _unpack_bf16_pair(u):
    lo = pltpu.bitcast(u << 16, F32).astype(BF16)
    hi = pltpu.bitcast(u & jnp.uint32(0xFFFF0000), F32).astype(BF16)
    return lo, hi


def _out_ln_kernel(m_ref, w_ref, h_ref, g_ref, b_ref, h1_ref, hp_ref):
    mix = jnp.dot(m_ref[...], w_ref[...], preferred_element_type=F32)
    h1 = _ln(DEEPNORM_ALPHA * h_ref[...] + mix, g_ref[...], b_ref[...])
    h1_ref[...] = h1
    hp_ref[...] = _pack_bf16_pair(h1)


def _out_ln(merged, w_out, h, g, b):
    t, d = h.shape
    tm = _pick(t, 512)
    return pl.pallas_call(
        _out_ln_kernel,
        grid=(t // tm,),
        in_specs=[pl.BlockSpec((tm, d), lambda i: (i, 0)),
                  pl.BlockSpec((d, d), lambda i: (0, 0)),
                  pl.BlockSpec((tm, d), lambda i: (i, 0)),
                  pl.BlockSpec((1, d), lambda i: (0, 0)),
                  pl.BlockSpec((1, d), lambda i: (0, 0))],
        out_specs=[pl.BlockSpec((tm, d), lambda i: (i, 0)),
                   pl.BlockSpec((tm, d // 2), lambda i: (i, 0))],
        out_shape=[jax.ShapeDtypeStruct((t, d), F32), jax.ShapeDtypeStruct((t, d // 2), jnp.uint32)],
        compiler_params=_cparams(("arbitrary",)),
        name="out_ln1",
    )(merged, w_out, h, g.reshape(1, d), b.reshape(1, d))


def _router_kernel(h_ref, w_ref, b_ref, idx_ref, gate_ref, rank_ref, cnt_ref, base_ref, *, n_exp, tm):
    i = pl.program_id(0)

    @pl.when(i == 0)
    def _():
        base_ref[...] = jnp.zeros_like(base_ref)

    h = h_ref[...]
    w = w_ref[...]
    h_hi = h.astype(BF16)
    h_lo = (h - h_hi.astype(F32)).astype(BF16)
    w_hi = w.astype(BF16)
    w_lo = (w - w_hi.astype(F32)).astype(BF16)
    logits = (jnp.dot(h_hi, w_hi, preferred_element_type=F32) + jnp.dot(h_hi, w_lo, preferred_element_type=F32)
              + jnp.dot(h_lo, w_hi, preferred_element_type=F32) + b_ref[...])
    lane = lax.broadcasted_iota(jnp.int32, logits.shape, 1)
    vals = jnp.where(lane < n_exp, logits, -jnp.inf)
    sels, tops, ams = [], [], []
    for _ in range(TOP_K):
        m = jnp.max(vals, axis=-1, keepdims=True)
        am = jnp.min(jnp.where(vals == m, lane, LANES), axis=-1, keepdims=True)
        sel = lane == am
        vals = jnp.where(sel, -jnp.inf, vals)
        sels.append(sel)
        tops.append(m)
        ams.append(am)
    es = [jnp.exp(v - tops[0]) for v in tops]
    den = es[0] + es[1] + es[2] + es[3]
    onehot = sels[0] | sels[1] | sels[2] | sels[3]
    mt = onehot.astype(BF16)
    ri = lax.broadcasted_iota(jnp.int32, (tm, tm), 0)
    ci = lax.broadcasted_iota(jnp.int32, (tm, tm), 1)
    before = (ri > ci).astype(BF16)
    prior = jnp.dot(before, mt, preferred_element_type=F32) + base_ref[...]
    idx_o = jnp.zeros(logits.shape, jnp.int32)
    gate_o = jnp.zeros(logits.shape, F32)
    rank_o = jnp.zeros(logits.shape, jnp.int32)
    for k in range(TOP_K):
        rk = jnp.sum(jnp.where(sels[k], prior, 0.0), axis=-1, keepdims=True).astype(jnp.int32)
        idx_o = jnp.where(lane == k, ams[k], idx_o)
        gate_o = jnp.where(lane == k, es[k] / den, gate_o)
        rank_o = jnp.where(lane == k, rk, rank_o)
    idx_ref[...] = idx_o
    gate_ref[...] = gate_o
    rank_ref[...] = rank_o
    base_ref[...] += jnp.sum(onehot.astype(F32), axis=0, keepdims=True)
    cnt_ref[...] = base_ref[...]


def _router(h1, w_router, b_router):
    t, d = h1.shape
    n_exp = w_router.shape[1]
    tm = _pick(t, 256)
    w = jnp.pad(w_router.astype(F32), ((0, 0), (0, LANES - n_exp)))
    b = jnp.pad(b_router.astype(F32), (0, LANES - n_exp)).reshape(1, LANES)
    tok = pl.BlockSpec((tm, LANES), lambda i: (i, 0))
    kern = functools.partial(_router_kernel, n_exp=n_exp, tm=tm)
    return pl.pallas_call(
        kern,
        grid=(t // tm,),
        in_specs=[pl.BlockSpec((tm, d), lambda i: (i, 0)),
                  pl.BlockSpec((d, LANES), lambda i: (0, 0)),
                  pl.BlockSpec((1, LANES), lambda i: (0, 0))],
        out_specs=[tok, tok, tok, pl.BlockSpec((1, LANES), lambda i: (0, 0))],
        out_shape=[jax.ShapeDtypeStruct((t, LANES), jnp.int32), jax.ShapeDtypeStruct((t, LANES), F32),
                   jax.ShapeDtypeStruct((t, LANES), jnp.int32), jax.ShapeDtypeStruct((1, LANES), F32)],
        scratch_shapes=[pltpu.VMEM((1, LANES), F32)],
        compiler_params=_cparams(("arbitrary",)),
        name="router",
    )(h1, w, b)


def _dispatch_kernel(dest_ref, pend_ref, nused_ref, x_ref, xs_ref, zero_ref, sem, zsem, *, tt, n_exp, nblk):
    i = pl.program_id(0)
    base = i * (tt * TOP_K)

    @pl.when(i == 0)
    def _():
        zero_ref[...] = jnp.zeros_like(zero_ref)

        def zero_block(start):
            start = pl.multiple_of(start, ROW_BLOCK)
            return pltpu.make_async_copy(zero_ref, xs_ref.at[pl.ds(start, ROW_BLOCK)], zsem)

        def expert_has_rows(e):
            prev = jnp.where(e == 0, 0, pend_ref[jnp.maximum(e - 1, 0)])
            return pend_ref[e] > prev

        def start_expert(e, carry):
            @pl.when(expert_has_rows(e))
            def _():
                zero_block(pend_ref[e] - ROW_BLOCK).start()
            return carry

        def wait_expert(e, carry):
            @pl.when(expert_has_rows(e))
            def _():
                zero_block(0).wait()
            return carry

        def start_tail(b, carry):
            zero_block(b * ROW_BLOCK).start()
            return carry

        def wait_tail(b, carry):
            zero_block(0).wait()
            return carry

        lax.fori_loop(0, n_exp, start_expert, 0)
        lax.fori_loop(nused_ref[0], nblk, start_tail, 0)
        lax.fori_loop(0, n_exp, wait_expert, 0)
        lax.fori_loop(nused_ref[0], nblk, wait_tail, 0)

    def issue(r, carry):
        for k in range(TOP_K):
            row = dest_ref[base + r * TOP_K + k]
            pltpu.make_async_copy(x_ref.at[pl.ds(r, 1)], xs_ref.at[pl.ds(row, 1)], sem).start()
        return carry

    lax.fori_loop(0, tt, issue, 0, unroll=4)

    for k in range(TOP_K):
        pltpu.make_async_copy(x_ref, xs_ref.at[pl.ds(0, tt)], sem).wait()


def _dispatch(xp, dest, pad_ends, n_used, n_rows):
    t, dh = xp.shape
    tt = _pick(t, 256)
    kern = functools.partial(_dispatch_kernel, tt=tt, n_exp=pad_ends.shape[0], nblk=n_rows // ROW_BLOCK)
    return pl.pallas_call(
        kern,
        grid_spec=pltpu.PrefetchScalarGridSpec(
            num_scalar_prefetch=3,
            grid=(t // tt,),
            in_specs=[pl.BlockSpec((tt, dh), lambda i, *_: (i, 0))],
            out_specs=pl.BlockSpec(memory_space=pl.ANY),
            scratch_shapes=[pltpu.VMEM((ROW_BLOCK, dh), jnp.uint32),
                            pltpu.SemaphoreType.DMA(()), pltpu.SemaphoreType.DMA(())]),
        out_shape=jax.ShapeDtypeStruct((n_rows, dh), jnp.uint32),
        compiler_params=_cparams(("arbitrary",)),
        name="dispatch",
    )(dest, pad_ends.astype(jnp.int32), n_used, xp)


def _expert_kernel(be_ref, nb_ref, x_ref, wg_ref, wu_ref, bg_ref, bu_ref, wd_ref, bd_ref, o_ref, *, half):
    i = pl.program_id(0)
    j = pl.program_id(1)

    @pl.when(i < nb_ref[0])
    def _():
        lo, hi = _unpack_bf16_pair(x_ref[...])
        gate = (jnp.dot(lo, wg_ref[0, :half, :], preferred_element_type=F32)
                + jnp.dot(hi, wg_ref[0, half:, :], preferred_element_type=F32) + bg_ref[0])
        up = (jnp.dot(lo, wu_ref[0, :half, :], preferred_element_type=F32)
              + jnp.dot(hi, wu_ref[0, half:, :], preferred_element_type=F32) + bu_ref[0])
        gate = jnp.minimum(gate, SWIGLU_LIMIT)
        up = jnp.clip(up, -SWIGLU_LIMIT, SWIGLU_LIMIT)
        act = gate * _sigmoid(SWIGLU_ALPHA * gate) * (up + 1.0)
        part = jnp.dot(act.astype(BF16), wd_ref[0], preferred_element_type=F32)

        @pl.when(j == 0)
        def _():
            o_ref[...] = part + bd_ref[0]

        @pl.when(j > 0)
        def _():
            o_ref[...] += part

    @pl.when((i >= nb_ref[0]) & (j == 0))
    def _():
        o_ref[...] = jnp.zeros_like(o_ref)


def _experts(xs, block_expert, n_used, w_gu, b_gu, w_down, b_down):
    n_rows, dh = xs.shape
    n_exp, d, ff2 = w_gu.shape
    ff = ff2 // 2
    tf = _pick(ff, 1024)
    nj = ff // tf
    nblk = n_rows // ROW_BLOCK
    b_gu3 = b_gu.astype(F32).reshape(n_exp, 1, ff2)
    b_d3 = b_down.astype(F32).reshape(n_exp, 1, d)

    def blk(i, nb):
        return jnp.maximum(jnp.minimum(i, nb[0] - 1), 0)

    def jj(i, j, nb):
        return jnp.where(i < nb[0], j, nj - 1)

    kern = functools.partial(_expert_kernel, half=d // 2)
    return pl.pallas_call(
        kern,
        grid_spec=pltpu.PrefetchScalarGridSpec(
            num_scalar_prefetch=2,
            grid=(nblk, nj),
            in_specs=[
                pl.BlockSpec((ROW_BLOCK, dh), lambda i, j, be, nb: (blk(i, nb), 0)),
                pl.BlockSpec((1, d, tf), lambda i, j, be, nb: (be[blk(i, nb)], 0, jj(i, j, nb))),
                pl.BlockSpec((1, d, tf), lambda i, j, be, nb: (be[blk(i, nb)], 0, nj + jj(i, j, nb))),
                pl.BlockSpec((1, 1, tf), lambda i, j, be, nb: (be[blk(i, nb)], 0, jj(i, j, nb))),
                pl.BlockSpec((1, 1, tf), lambda i, j, be, nb: (be[blk(i, nb)], 0, nj + jj(i, j, nb))),
                pl.BlockSpec((1, tf, d), lambda i, j, be, nb: (be[blk(i, nb)], jj(i, j, nb), 0)),
                pl.BlockSpec((1, 1, d), lambda i, j, be, nb: (be[blk(i, nb)], 0, 0)),
            ],
            out_specs=pl.BlockSpec((ROW_BLOCK, d), lambda i, j, be, nb: (i, 0))),
        out_shape=jax.ShapeDtypeStruct((n_rows, d), F32),
        compiler_params=_cparams(("arbitrary", "arbitrary")),
        name="experts",
    )(block_expert, n_used, xs, w_gu, w_gu, b_gu3, b_gu3, w_down, b_d3)


def _combine_kernel(dest_ref, y_ref, h_ref, gate_ref, g_ref, b_ref, o_ref, buf_ref, sem, *, tt):
    i = pl.program_id(0)
    base = i * (tt * TOP_K)

    def issue(r, carry):
        for k in range(TOP_K):
            row = dest_ref[base + r * TOP_K + k]
            pltpu.make_async_copy(y_ref.at[pl.ds(row, 1)], buf_ref.at[k, pl.ds(r, 1)], sem).start()
        return carry

    lax.fori_loop(0, tt, issue, 0, unroll=4)

    for k in range(TOP_K):
        pltpu.make_async_copy(y_ref.at[pl.ds(0, tt)], buf_ref.at[k], sem).wait()
    gates = gate_ref[...]
    ffn = gates[:, 0:1] * buf_ref[0]
    for k in range(1, TOP_K):
        ffn = ffn + gates[:, k:k + 1] * buf_ref[k]
    o_ref[...] = _ln(DEEPNORM_ALPHA * h_ref[...] + ffn, g_ref[...], b_ref[...])


def _combine(y, dest, h1, gates, g, b):
    t, d = h1.shape
    tt = _pick(t, 128)
    kern = functools.partial(_combine_kernel, tt=tt)
    return pl.pallas_call(
        kern,
        grid_spec=pltpu.PrefetchScalarGridSpec(
            num_scalar_prefetch=1,
            grid=(t // tt,),
            in_specs=[pl.BlockSpec(memory_space=pl.ANY),
                      pl.BlockSpec((tt, d), lambda i, dst: (i, 0)),
                      pl.BlockSpec((tt, LANES), lambda i, dst: (i, 0)),
                      pl.BlockSpec((1, d), lambda i, dst: (0, 0)),
                      pl.BlockSpec((1, d), lambda i, dst: (0, 0))],
            out_specs=pl.BlockSpec((tt, d), lambda i, dst: (i, 0)),
            scratch_shapes=[pltpu.VMEM((TOP_K, tt, d), F32), pltpu.SemaphoreType.DMA(())]),
        out_shape=jax.ShapeDtypeStruct((t, d), F32),
        compiler_params=_cparams(("arbitrary",)),
        name="combine",
    )(dest, y, h1, gates, g.reshape(1, d), b.reshape(1, d))


def kernel(x, ln_in_g, ln_in_b, w_in, lam_re, lam_im, log_step, ssm_b_re, ssm_b_im, ssm_c_re, ssm_c_im, ssm_d, w_glu, b_glu, conv_w, a_log, dt_bias, dn_norm_w, w_proj_ssm, w_proj_dn, w_out, ln1_g, ln1_b, w_router, b_router, w_gate_up, b_gate_up, w_down, b_down, ln2_g, ln2_b):
    bsz, seq, d = x.shape
    t = bsz * seq
    sw = d // 2
    heads = d // DN_HEAD_DIM
    dn = heads * DN_HEAD_DIM
    groups = sw // SSM_GROUP_DIM
    n_state = lam_re.shape[-1]
    n_exp = w_router.shape[-1]
    assert w_in.shape[0] == DEPTH == 1
    assert seq % (CHUNK * 4) == 0 and sw % LANES == 0 and groups % 2 == 0 and 2 * heads <= LANES

    h, hb = _ln_in(x.reshape(t, d), ln_in_g, ln_in_b)

    wi = w_in[0]
    c_ab = sw + 4 * dn
    w_main = jnp.concatenate([wi[:, :c_ab], wi[:, c_ab + 2 * heads:]], axis=1).astype(BF16)
    w_ab = jnp.pad(wi[:, c_ab:c_ab + 2 * heads], ((0, 0), (0, LANES - 2 * heads))).astype(BF16)
    proj = _matmul(hb, w_main, F32, "proj_in")
    ab = _matmul(hb, w_ab, F32, "proj_ab")

    ops = _s5_operators(lam_re[0], lam_im[0], log_step[0], ssm_b_re[0], ssm_b_im[0],
                        ssm_c_re[0], ssm_c_im[0], ssm_d[0])
    y_s = _s5_scan(proj, ops, bsz, seq, sw, n_state)
    y_ssm = _ssm_post(y_s, w_glu[0].astype(BF16), b_glu[0].astype(F32))

    ff2 = w_gate_up.shape[-1]
    y_dn, (w_gu_bf, w_dn_bf) = _deltanet(
        proj, ab, conv_w[0], a_log[0], dt_bias[0], dn_norm_w[0], bsz, seq, heads, off_q=sw // LANES,
        cast_weights=(w_gate_up[0].reshape(n_exp * d, ff2), w_down[0].reshape(n_exp * (ff2 // 2), d)))
    w_gu_bf = w_gu_bf.reshape(n_exp, d, ff2)
    w_dn_bf = w_dn_bf.reshape(n_exp, ff2 // 2, d)

    merged = _merge(y_ssm, y_dn, proj, sw + 4 * dn, w_proj_ssm[0].astype(BF16), w_proj_dn[0].astype(BF16))
    h1, h1p = _out_ln(merged, w_out[0].astype(BF16), h, ln1_g[0], ln1_b[0])

    idx, gates, rank, cnt = _router(h1, w_router[0], b_router[0])
    counts = cnt[0, :n_exp].astype(jnp.int32)
    padded = (counts + ROW_BLOCK - 1) // ROW_BLOCK * ROW_BLOCK
    pad_ends = jnp.cumsum(padded)
    pad_starts = pad_ends - padded
    dest = (pad_starts[idx[:, :TOP_K]] + rank[:, :TOP_K]).astype(jnp.int32).reshape(t * TOP_K)
    n_rows = t * TOP_K + n_exp * ROW_BLOCK
    nblk = n_rows // ROW_BLOCK
    blk_start = jnp.arange(nblk, dtype=jnp.int32) * ROW_BLOCK
    block_expert = jnp.minimum(jnp.sum((pad_ends[None, :] <= blk_start[:, None]).astype(jnp.int32), axis=1),
                               n_exp - 1).astype(jnp.int32)
    n_used = (pad_ends[-1:] // ROW_BLOCK).astype(jnp.int32)
    xs = _dispatch(h1p, dest, pad_ends, n_used, n_rows)
    y = _experts(xs, block_expert, n_used, w_gu_bf, b_gate_up[0], w_dn_bf, b_down[0])
    out = _combine(y, dest, h1, gates, ln2_g[0], ln2_b[0])
    return out.reshape(bsz, seq, d)
```

```python
import functools
import itertools
import math

import jax
import jax.numpy as jnp
from jax import lax
from jax.experimental import pallas as pl
from jax.experimental.pallas import tpu as pltpu

F32 = jnp.float32
BF16 = jnp.bfloat16
HI = lax.Precision.HIGHEST

LANES = 128
CHUNK = 64
LN_EPS = 1e-5
NORM_EPS = 1e-6
SSM_GROUP_DIM = 16
SSM_TC = 16
DN_HEAD_DIM = 128
CONV_WIDTH = 4
TOP_K = 4
SWIGLU_LIMIT = 7.0
SWIGLU_ALPHA = 1.702
DEPTH = 1
DEEPNORM_ALPHA = (2 * DEPTH) ** 0.25
ROW_BLOCK = 512
VMEM_LIMIT = 56 * 1024 * 1024


def _cparams(sem):
    return pltpu.CompilerParams(dimension_semantics=sem, vmem_limit_bytes=VMEM_LIMIT)


def _pick(n, pref):
    t = min(n, pref)
    while n % t:
        t //= 2
    return t


def _ln(x, g, b):
    mu = jnp.mean(x, axis=-1, keepdims=True)
    xc = x - mu
    var = jnp.mean(xc * xc, axis=-1, keepdims=True)
    return xc * lax.rsqrt(var + LN_EPS) * g + b


def _ln_in_kernel(x_ref, g_ref, b_ref, h_ref, hb_ref):
    h = _ln(x_ref[...], g_ref[...], b_ref[...])
    h_ref[...] = h
    hb_ref[...] = h.astype(BF16)


def _ln_in(x, g, b):
    t, d = x.shape
    tm = _pick(t, 512)
    return pl.pallas_call(
        _ln_in_kernel,
        grid=(t // tm,),
        in_specs=[pl.BlockSpec((tm, d), lambda i: (i, 0)),
                  pl.BlockSpec((1, d), lambda i: (0, 0)),
                  pl.BlockSpec((1, d), lambda i: (0, 0))],
        out_specs=[pl.BlockSpec((tm, d), lambda i: (i, 0)),
                   pl.BlockSpec((tm, d), lambda i: (i, 0))],
        out_shape=[jax.ShapeDtypeStruct((t, d), F32), jax.ShapeDtypeStruct((t, d), BF16)],
        compiler_params=_cparams(("arbitrary",)),
        name="ln_in",
    )(x, g.reshape(1, d), b.reshape(1, d))


def _mm_kernel(x_ref, w_ref, o_ref):
    o_ref[...] = jnp.dot(x_ref[...], w_ref[...].astype(BF16), preferred_element_type=F32).astype(o_ref.dtype)


def _matmul(x, w, out_dtype, name, ncols=None):
    m, k = x.shape
    n = w.shape[1] if ncols is None else ncols
    tm = _pick(m, 1024)
    tn = _pick(n, 1024)
    return pl.pallas_call(
        _mm_kernel,
        grid=(n // tn, m // tm),
        in_specs=[pl.BlockSpec((tm, k), lambda j, i: (i, 0)),
                  pl.BlockSpec((k, tn), lambda j, i: (0, j))],
        out_specs=pl.BlockSpec((tm, tn), lambda j, i: (i, j)),
        out_shape=jax.ShapeDtypeStruct((m, n), out_dtype),
        compiler_params=_cparams(("arbitrary", "arbitrary")),
        name=name,
    )(x, w)


CAST_SLICES = 16


def _cast_slices(src_refs, dst_refs):
    for sl in range(CAST_SLICES):
        for src, dst in zip(src_refs, dst_refs):
            n = src.shape[0] // CAST_SLICES
            dst[sl * n:(sl + 1) * n, :] = src[sl * n:(sl + 1) * n, :].astype(BF16)
        yield


def _cast_specs(weights, grid):
    g0, g1, g2 = grid
    nsteps = g0 * g1 * g2
    specs = []
    for w in weights:
        rows, cols = w.shape
        assert rows % (nsteps * CAST_SLICES * 16) == 0, (rows, nsteps)
        specs.append(pl.BlockSpec((rows // nsteps, cols), lambda a, b, c: ((a * g1 + b) * g2 + c, 0)))
    return specs


def _s5_operators(lam_re, lam_im, log_step, b_re, b_im, c_re, c_im, d_skip):
    g, p = lam_re.shape
    hd, tc = SSM_GROUP_DIM, SSM_TC
    lr, li = lam_re.astype(F32), lam_im.astype(F32)
    step = jnp.exp(log_step.astype(F32))[:, None]
    tau = jnp.arange(tc + 1, dtype=F32)[:, None, None]
    mag = jnp.exp(lr * step * tau)
    pr, pi = mag * jnp.cos(li * step * tau), mag * jnp.sin(li * step * tau)
    a_re, a_im = pr[1], pi[1]
    den = lr * lr + li * li
    nr, ni = a_re - 1.0, a_im
    f_re = (nr * lr + ni * li) / den
    f_im = (ni * lr - nr * li) / den
    br, bi = b_re.astype(F32), b_im.astype(F32)
    bb_re = f_re[..., None] * br - f_im[..., None] * bi
    bb_im = f_re[..., None] * bi + f_im[..., None] * br
    cr, ci = c_re.astype(F32), c_im.astype(F32)
    ca_re = cr[None] * pr[:, :, None, :] - ci[None] * pi[:, :, None, :]
    ca_im = cr[None] * pi[:, :, None, :] + ci[None] * pr[:, :, None, :]
    ca_g = jnp.concatenate([jnp.transpose(ca_re[:tc], (1, 0, 2, 3)).reshape(g, tc * hd, p),
                            -jnp.transpose(ca_im[:tc], (1, 0, 2, 3)).reshape(g, tc * hd, p)], axis=2)
    bb_g = jnp.concatenate([bb_re, bb_im], axis=1)
    kk = jnp.einsum('gxp,gpi->gxi', ca_g, bb_g, precision=HI)
    kk = jnp.transpose(kk.reshape(g, tc, hd, hd), (1, 0, 2, 3))
    kk = kk.at[0].add(d_skip.astype(F32)[:, :, None] * jnp.eye(hd, dtype=F32)[None])
    gs = LANES // hd
    ns = g // gs
    dup = LANES // p
    eye = jnp.eye(gs, dtype=F32)
    k5 = jnp.transpose(kk.reshape(tc, ns, gs, hd, hd), (1, 0, 2, 4, 3))
    bm = (k5[:, :, :, :, None, :] * eye[None, None, :, None, :, None]).reshape(ns, tc, LANES, LANES)
    dt = jnp.arange(tc)

    def rows(x_re, x_im):
        x = jnp.stack([x_re, x_im], axis=1).reshape(tc, 2, ns, gs * hd, p)
        x = jnp.transpose(x, (2, 0, 1, 3, 4))
        return jnp.concatenate([x] * dup, axis=-1)

    prr, pir = pr[tc - 1 - dt], pi[tc - 1 - dt]
    bc_re = prr[:, :, None, :] * jnp.transpose(bb_re, (0, 2, 1))[None] - pir[:, :, None, :] * jnp.transpose(bb_im, (0, 2, 1))[None]
    bc_im = prr[:, :, None, :] * jnp.transpose(bb_im, (0, 2, 1))[None] + pir[:, :, None, :] * jnp.transpose(bb_re, (0, 2, 1))[None]
    bcc = rows(bc_re, bc_im)
    cct = rows(ca_re[1:], -ca_im[1:])
    a16r = pr[tc].reshape(1, g * p)
    a16i = pi[tc].reshape(1, g * p)
    return (bm.astype(BF16), bcc.astype(BF16), cct.astype(BF16), a16r, a16i)


def _s5_kernel(x_ref, bm_ref, bcc_ref, cct_ref, ar_ref, ai_ref, *rest, ncb, sl, n_state, ncast):
    cast_src, y_ref, cast_dst = rest[:ncast], rest[ncast], rest[ncast + 1:2 * ncast + 1]
    mf_ref, bcf_ref, ccf_ref, st_ref, cr_ref, ci_ref = rest[2 * ncast + 1:]
    tc = SSM_TC
    for _ in _cast_slices(cast_src, cast_dst):
        pass

    @pl.when((pl.program_id(1) == 0) & (pl.program_id(2) == 0))
    def _():
        zero = jnp.zeros((LANES, LANES), BF16)
        for di in range(tc):
            for do in range(tc):
                mf_ref[di * LANES:(di + 1) * LANES, do * LANES:(do + 1) * LANES] = (
                    bm_ref[0, do - di] if do >= di else zero)
        r = lax.broadcasted_iota(jnp.int32, (LANES, sl), 0)
        c = lax.broadcasted_iota(jnp.int32, (LANES, sl), 1)
        same_group = (r // SSM_GROUP_DIM) == (c // n_state)

        def widen(blk):
            wide = jnp.concatenate([blk.astype(F32)] * (sl // LANES), axis=1)
            return jnp.where(same_group, wide, 0.0).astype(BF16)

        for dt in range(tc):
            for ri in range(2):
                bcf_ref[dt * LANES:(dt + 1) * LANES, ri * sl:(ri + 1) * sl] = widen(bcc_ref[0, dt, ri])
                ccf_ref[dt * LANES:(dt + 1) * LANES, ri * sl:(ri + 1) * sl] = widen(cct_ref[0, dt, ri])

    @pl.when(pl.program_id(2) == 0)
    def _():
        cr_ref[...] = jnp.zeros_like(cr_ref)
        ci_ref[...] = jnp.zeros_like(ci_ref)

    xs = jnp.concatenate([x_ref[pl.ds(dt, ncb, stride=SSM_TC), :] for dt in range(SSM_TC)],
                         axis=1).astype(BF16)
    st_ref[...] = jnp.dot(xs, bcf_ref[...], preferred_element_type=F32)
    ar = ar_ref[...]
    ai = ai_ref[...]

    def body(c, carry):
        sr, si = carry
        lr = st_ref[pl.ds(c, 1), :sl]
        li = st_ref[pl.ds(c, 1), sl:]
        st_ref[pl.ds(c, 1), :sl] = sr
        st_ref[pl.ds(c, 1), sl:] = si
        return ar * sr - ai * si + lr, ar * si + ai * sr + li

    sr, si = lax.fori_loop(0, ncb, body, (cr_ref[...], ci_ref[...]), unroll=8)
    cr_ref[...] = sr
    ci_ref[...] = si
    y = (jnp.dot(xs, mf_ref[...], preferred_element_type=F32)
         + _dot_nt(st_ref[...].astype(BF16), ccf_ref[...]))
    for dt in range(SSM_TC):
        y_ref[pl.ds(dt, ncb, stride=SSM_TC), :] = y[:, dt * LANES:(dt + 1) * LANES]


def _s5_scan(proj, ops, bsz, seq, sw, n_state, cast_weights=()):
    bm, bcc, cct, a16r, a16i = ops
    ns = sw // LANES
    sl = (LANES // SSM_GROUP_DIM) * n_state
    ncb = _pick(seq // SSM_TC, 128)
    rt = ncb * SSM_TC
    nt = seq // rt
    kw = SSM_TC * LANES
    cast_specs = _cast_specs(cast_weights, (ns, bsz, nt))
    kern = functools.partial(_s5_kernel, ncb=ncb, sl=sl, n_state=n_state, ncast=len(cast_weights))
    outs = pl.pallas_call(
        kern,
        grid=(ns, bsz, nt),
        in_specs=[pl.BlockSpec((rt, LANES), lambda s, b, i: (b * nt + i, s)),
                  pl.BlockSpec((1, SSM_TC, LANES, LANES), lambda s, b, i: (s, 0, 0, 0)),
                  pl.BlockSpec((1, SSM_TC, 2, LANES, LANES), lambda s, b, i: (s, 0, 0, 0, 0)),
                  pl.BlockSpec((1, SSM_TC, 2, LANES, LANES), lambda s, b, i: (s, 0, 0, 0, 0)),
                  pl.BlockSpec((1, sl), lambda s, b, i: (0, s)),
                  pl.BlockSpec((1, sl), lambda s, b, i: (0, s))] + cast_specs,
        out_specs=[pl.BlockSpec((rt, LANES), lambda s, b, i: (b * nt + i, s))] + cast_specs,
        out_shape=([jax.ShapeDtypeStruct((bsz * seq, sw), F32)]
                   + [jax.ShapeDtypeStruct(w.shape, BF16) for w in cast_weights]),
        scratch_shapes=[pltpu.VMEM((kw, kw), BF16), pltpu.VMEM((kw, 2 * sl), BF16), pltpu.VMEM((kw, 2 * sl), BF16),
                        pltpu.VMEM((ncb, 2 * sl), F32), pltpu.VMEM((1, sl), F32), pltpu.VMEM((1, sl), F32)],
        compiler_params=_cparams(("arbitrary", "arbitrary", "arbitrary")),
        name="s5_scan",
    )(proj, bm, bcc, cct, a16r, a16i, *cast_weights)
    return outs[0], outs[1:]


def _ssm_post_kernel(y_ref, w_ref, b_ref, o_ref):
    y = y_ref[...]
    yg = 0.5 * y * (1.0 + lax.erf(y * (1.0 / math.sqrt(2.0))))
    s = jnp.dot(yg.astype(BF16), w_ref[...], preferred_element_type=F32) + b_ref[...]
    o_ref[...] = (yg * jax.nn.sigmoid(s)).astype(o_ref.dtype)


def _ssm_post(y, w_glu, b_glu):
    t, n = y.shape
    tm = _pick(t, 512)
    return pl.pallas_call(
        _ssm_post_kernel,
        grid=(t // tm,),
        in_specs=[pl.BlockSpec((tm, n), lambda i: (i, 0)),
                  pl.BlockSpec((n, n), lambda i: (0, 0)),
                  pl.BlockSpec((1, n), lambda i: (0, 0))],
        out_specs=pl.BlockSpec((tm, n), lambda i: (i, 0)),
        out_shape=jax.ShapeDtypeStruct((t, n), BF16),
        compiler_params=_cparams(("arbitrary",)),
        name="ssm_post",
    )(y, w_glu, b_glu.reshape(1, n))


def _dot_nt(a, b):
    return lax.dot_general(a, b, (((1,), (1,)), ((), ())), preferred_element_type=F32)


def _dot_tn(a, b):
    return lax.dot_general(a, b, (((0,), (0,)), ((), ())), preferred_element_type=F32)


def _dot_hi(a, b):
    return jnp.dot(a, b, preferred_element_type=F32, precision=HI)


def _sigmoid(x):
    return 0.5 * jnp.tanh(0.5 * x) + 0.5


def _silu(x):
    return x * _sigmoid(x)


DN_HALO = 8
DN_HEADS_PER_STEP = 8


def _dn_kernel(q_ref, k_ref, v_ref, z_ref, ab_ref, cq_ref, ck_ref, cv_ref, al_ref, dtb_ref, nw_ref,
               *rest, tt, heads, hp, ncast):
    cast_src, o_ref, cast_dst, scratch = rest[:ncast], rest[ncast], rest[ncast + 1:2 * ncast + 1], rest[2 * ncast + 1:]
    s_refs, xx_refs = scratch[:hp], scratch[hp:]

    @pl.when(pl.program_id(2) == 0)
    def _():
        for s_ref, xx_ref in zip(s_refs, xx_refs):
            s_ref[...] = jnp.zeros_like(s_ref)
            xx_ref[:, 0:DN_HALO, :] = jnp.zeros((3, DN_HALO, DN_HEAD_DIM), F32)

    ab = ab_ref[...]
    xa = ab + dtb_ref[...]
    sp = jnp.maximum(xa, 0.0) + jnp.log1p(jnp.exp(-jnp.abs(xa)))
    gc_all = -jnp.exp(al_ref[...]) * sp
    pos = lax.broadcasted_iota(jnp.int32, gc_all.shape, 0) & (CHUNK - 1)
    sh = 1
    while sh < CHUNK:
        gc_all = gc_all + jnp.where(pos >= sh, pltpu.roll(gc_all, sh, axis=0), 0.0)
        sh *= 2
    gate_vals = (gc_all, jnp.exp(gc_all), _sigmoid(ab))
    gens = [_dn_head(hh, pl.program_id(1) * hp + hh, q_ref, k_ref, v_ref, z_ref, gate_vals, cq_ref, ck_ref, cv_ref,
                     nw_ref, o_ref, s_refs[hh], xx_refs[hh], tt=tt, heads=heads)
            for hh in range(hp)]
    if ncast:
        gens.append(_cast_slices(cast_src, cast_dst))
    for _ in itertools.zip_longest(*gens):
        pass


def _dn_head(hh, h, q_ref, k_ref, v_ref, z_ref, gate_vals, cq_ref, ck_ref, cv_ref, nw_ref,
             o_ref, s_ref, xx_ref, *, tt, heads):
    hal = DN_HALO
    ls = slice(hh * DN_HEAD_DIM, (hh + 1) * DN_HEAD_DIM)

    def conv(idx, x_ref, cw_ref):
        x = x_ref[:, ls]
        xx_ref[idx, hal:, :] = x
        w = cw_ref[:, ls]
        acc = w[CONV_WIDTH - 1:CONV_WIDTH, :] * x
        for j in range(CONV_WIDTH - 1):
            off = hal - (CONV_WIDTH - 1) + j
            acc = acc + w[j:j + 1, :] * xx_ref[idx, off:off + tt, :]
        xx_ref[idx, 0:hal, :] = x[tt - hal:, :]
        return _silu(acc)

    qc = conv(0, q_ref, cq_ref)
    kc = conv(1, k_ref, ck_ref)
    vc = conv(2, v_ref, cv_ref)
    qn = qc * lax.rsqrt(jnp.sum(qc * qc, axis=-1, keepdims=True) + NORM_EPS) * (DN_HEAD_DIM ** -0.5)
    kn = kc * lax.rsqrt(jnp.sum(kc * kc, axis=-1, keepdims=True) + NORM_EPS)

    gc_all, egc_all, beta_all = gate_vals
    lane = lax.broadcasted_iota(jnp.int32, gc_all.shape, 1)

    def pick(vals, ln):
        col = jnp.sum(jnp.where(lane == ln, vals, 0.0), axis=-1, keepdims=True)
        return jnp.broadcast_to(col, (tt, DN_HEAD_DIM))

    beta_col = pick(beta_all, h + heads)
    gc = pick(gc_all, h)
    egc = pick(egc_all, h)
    nch = tt // CHUNK
    gc_row = jnp.transpose(gc)[0:1, :]

    ri = lax.broadcasted_iota(jnp.int32, (tt, tt), 0)
    ci = lax.broadcasted_iota(jnp.int32, (tt, tt), 1)
    same = (ri // CHUNK) == (ci // CHUNK)
    causal = same & (ri >= ci)
    strict = same & (ri > ci)
    gc_wide = jnp.concatenate([gc] * (tt // DN_HEAD_DIM), axis=1)
    decay = jnp.where(causal, jnp.exp(jnp.where(causal, gc_wide - gc_row, 0.0)), 0.0)
    kb = kn * beta_col
    knb = kn.astype(BF16)
    yield
    a_raw = _dot_nt(kb.astype(BF16), knb)
    qk_raw = _dot_nt(qn.astype(BF16), knb)
    yield
    a_bd = jnp.where(strict, a_raw * decay, 0.0)
    qk_bd = jnp.where(causal, qk_raw * decay, 0.0).astype(BF16)

    def fold(m):
        out = m[0:CHUNK]
        for c in range(1, nch):
            out = out + m[c * CHUNK:(c + 1) * CHUNK]
        return out

    def spread(m):
        return jnp.where(same, jnp.concatenate([m] * nch, axis=0), 0.0)

    r64 = lax.broadcasted_iota(jnp.int32, (CHUNK, tt), 0)
    c64 = lax.broadcasted_iota(jnp.int32, (CHUNK, tt), 1)
    eye_cat = (r64 == (c64 & (CHUNK - 1))).astype(F32)
    pw_cat = fold(a_bd)
    inv_cat = eye_cat - pw_cat
    pw_bd = a_bd.astype(BF16)
    for _ in range(5):
        pw_cat = jnp.dot(pw_cat.astype(BF16), pw_bd, preferred_element_type=F32)
        yield
        pw_bd = spread(pw_cat).astype(BF16)
        inv_add = jnp.dot(inv_cat.astype(BF16), pw_bd, preferred_element_type=F32)
        yield
        inv_cat = inv_cat + inv_add
    inv_bd = spread(inv_cat).astype(BF16)
    rhs = jnp.concatenate([vc * beta_col, kb * egc], axis=1).astype(BF16)
    sol = jnp.dot(inv_bd, rhs, preferred_element_type=F32)
    yield
    u_all, w_all = sol[:, :DN_HEAD_DIM], sol[:, DN_HEAD_DIM:]
    qe = qn * egc

    outs = []
    zblk = jnp.zeros((CHUNK, DN_HEAD_DIM), BF16)
    s = s_ref[...]
    for c in range(nch):
        sl = slice(c * CHUNK, (c + 1) * CHUNK)
        wq = jnp.concatenate([w_all[sl], qe[sl]], axis=0).astype(BF16)
        ws = jnp.dot(wq, s.astype(BF16), preferred_element_type=F32)
        yield
        v_new = u_all[sl] - ws[:CHUNK]
        vnb = v_new.astype(BF16)
        v_pad = jnp.concatenate([zblk] * c + [vnb] + [zblk] * (nch - 1 - c), axis=0)
        gc_c = gc[sl]
        g_last = gc_c[CHUNK - 1:CHUNK, :]
        k_dec = (kn[sl] * jnp.exp(g_last - gc_c)).astype(BF16)
        s_add = _dot_tn(k_dec, vnb)
        o_add = jnp.dot(qk_bd[sl], v_pad, preferred_element_type=F32)
        yield
        s = s * jnp.exp(g_last) + s_add
        outs.append(ws[CHUNK:] + o_add)
    s_ref[...] = s
    o = jnp.concatenate(outs, axis=0)
    o = o * lax.rsqrt(jnp.mean(o * o, axis=-1, keepdims=True) + NORM_EPS) * nw_ref[...]
    o_ref[:, ls] = (o * _silu(z_ref[:, ls])).astype(o_ref.dtype)


def _deltanet(proj, ab, conv_w, a_log, dt_bias, norm_w, bsz, seq, heads, off_q, cast_weights=()):
    t = bsz * seq
    tt = _pick(seq, 256)
    nt = seq // tt
    hd = DN_HEAD_DIM
    cw = conv_w.reshape(CONV_WIDTH, 3 * heads * hd).astype(F32)
    pad = LANES - heads
    al = jnp.pad(a_log.astype(F32), (0, pad)).reshape(1, LANES)
    dtb = jnp.pad(dt_bias.astype(F32), (0, pad)).reshape(1, LANES)
    nw = norm_w.astype(F32).reshape(1, hd)

    hp = math.gcd(math.gcd(heads, off_q), DN_HEADS_PER_STEP)
    hw = hp * hd
    oq, nh = off_q // hp, heads // hp

    def act(o):
        return pl.BlockSpec((tt, hw), lambda b, h, i: (b * nt + i, oq + o * nh + h))

    def cws(o):
        return pl.BlockSpec((CONV_WIDTH, hw), lambda b, h, i: (0, o * nh + h))

    row = pl.BlockSpec((1, LANES), lambda b, h, i: (0, 0))
    ncast = len(cast_weights)
    cast_specs = _cast_specs(cast_weights, (bsz, nh, nt))
    kern = functools.partial(_dn_kernel, tt=tt, heads=heads, hp=hp, ncast=ncast)
    outs = pl.pallas_call(
        kern,
        grid=(bsz, nh, nt),
        in_specs=[act(0), act(1), act(2), act(3),
                  pl.BlockSpec((tt, LANES), lambda b, h, i: (b * nt + i, 0)),
                  cws(0), cws(1), cws(2), row, row, row] + cast_specs,
        out_specs=[pl.BlockSpec((tt, hw), lambda b, h, i: (b * nt + i, h))] + cast_specs,
        out_shape=([jax.ShapeDtypeStruct((t, heads * hd), BF16)]
                   + [jax.ShapeDtypeStruct(w.shape, BF16) for w in cast_weights]),
        scratch_shapes=([pltpu.VMEM((hd, hd), F32)] * hp
                        + [pltpu.VMEM((3, tt + DN_HALO, hd), F32)] * hp),
        compiler_params=_cparams(("arbitrary", "arbitrary", "arbitrary")),
        name="deltanet",
    )(proj, proj, proj, proj, ab, cw, cw, cw, al, dtb, nw, *cast_weights)
    return outs[0], outs[1:]


def _merge_kernel(ys_ref, yd_ref, gs_ref, gd_ref, ws_ref, wd_ref, o_ref):
    ps = jnp.dot(ys_ref[...], ws_ref[...], preferred_element_type=F32)
    pd = jnp.dot(yd_ref[...], wd_ref[...], preferred_element_type=F32)
    o_ref[...] = (jax.nn.sigmoid(gs_ref[...]) * ps + jax.nn.sigmoid(gd_ref[...]) * pd).astype(o_ref.dtype)


def _merge(ys, yd, gates, gate_col, w_ps, w_pd):
    t, ns = ys.shape
    d = w_ps.shape[1]
    tm = _pick(t, 512)
    tn = _pick(ns, 1024)
    nb = d // tn
    off = gate_col // tn
    assert gate_col % tn == 0
    return pl.pallas_call(
        _merge_kernel,
        grid=(nb, t // tm),
        in_specs=[pl.BlockSpec((tm, ns), lambda j, i: (i, 0)),
                  pl.BlockSpec((tm, d), lambda j, i: (i, 0)),
                  pl.BlockSpec((tm, tn), lambda j, i: (i, off + j)),
                  pl.BlockSpec((tm, tn), lambda j, i: (i, off + nb + j)),
                  pl.BlockSpec((ns, tn), lambda j, i: (0, j)),
                  pl.BlockSpec((d, tn), lambda j, i: (0, j))],
        out_specs=pl.BlockSpec((tm, tn), lambda j, i: (i, j)),
        out_shape=jax.ShapeDtypeStruct((t, d), BF16),
        compiler_params=_cparams(("arbitrary", "arbitrary")),
        name="merge",
    )(ys, yd, gates, gates, w_ps, w_pd)


def _pack_bf16_pair(x):
    half = x.shape[1] // 2
    lo = pltpu.bitcast(x[:, :half].astype(BF16).astype(F32), jnp.uint32)
    hi = pltpu.bitcast(x[:, half:].astype(BF16).astype(F32), jnp.uint32)
    return (lo >> 16) | (hi & jnp.uint32(0xFFFF0000))


def _unpack_bf16_pair(u):
    lo = pltpu.bitcast(u << 16, F32).astype(BF16)
    hi = pltpu.bitcast(u & jnp.uint32(0xFFFF0000), F32).astype(BF16)
    return lo, hi


def _out_ln_kernel(m_ref, w_ref, h_ref, g_ref, b_ref, h1_ref, hp_ref):
    mix = jnp.dot(m_ref[...], w_ref[...], preferred_element_type=F32)
    h1 = _ln(DEEPNORM_ALPHA * h_ref[...] + mix, g_ref[...], b_ref[...])
    h1_ref[...] = h1
    hp_ref[...] = _pack_bf16_pair(h1)


def _out_ln(merged, w_out, h, g, b):
    t, d = h.shape
    tm = _pick(t, 512)
    return pl.pallas_call(
        _out_ln_kernel,
        grid=(t // tm,),
        in_specs=[pl.BlockSpec((tm, d), lambda i: (i, 0)),
                  pl.BlockSpec((d, d), lambda i: (0, 0)),
                  pl.BlockSpec((tm, d), lambda i: (i, 0)),
                  pl.BlockSpec((1, d), lambda i: (0, 0)),
                  pl.BlockSpec((1, d), lambda i: (0, 0))],
        out_specs=[pl.BlockSpec((tm, d), lambda i: (i, 0)),
                   pl.BlockSpec((tm, d // 2), lambda i: (i, 0))],
        out_shape=[jax.ShapeDtypeStruct((t, d), F32), jax.ShapeDtypeStruct((t, d // 2), jnp.uint32)],
        compiler_params=_cparams(("arbitrary",)),
        name="out_ln1",
    )(merged, w_out, h, g.reshape(1, d), b.reshape(1, d))


def _router_kernel(h_ref, w_ref, b_ref, idx_ref, gate_ref, rank_ref, cnt_ref, base_ref, *, n_exp, tm):
    i = pl.program_id(0)

    @pl.when(i == 0)
    def _():
        base_ref[...] = jnp.zeros_like(base_ref)

    h = h_ref[...]
    w = w_ref[...]
    h_hi = h.astype(BF16)
    h_lo = (h - h_hi.astype(F32)).astype(BF16)
    w_hi = w.astype(BF16)
    w_lo = (w - w_hi.astype(F32)).astype(BF16)
    logits = (jnp.dot(h_hi, w_hi, preferred_element_type=F32) + jnp.dot(h_hi, w_lo, preferred_element_type=F32)
              + jnp.dot(h_lo, w_hi, preferred_element_type=F32) + b_ref[...])
    lane = lax.broadcasted_iota(jnp.int32, logits.shape, 1)
    vals = jnp.where(lane < n_exp, logits, -jnp.inf)
    sels, tops, ams = [], [], []
    for _ in range(TOP_K):
        m = jnp.max(vals, axis=-1, keepdims=True)
        am = jnp.min(jnp.where(vals == m, lane, LANES), axis=-1, keepdims=True)
        sel = lane == am
        vals = jnp.where(sel, -jnp.inf, vals)
        sels.append(sel)
        tops.append(m)
        ams.append(am)
    es = [jnp.exp(v - tops[0]) for v in tops]
    den = es[0] + es[1] + es[2] + es[3]
    onehot = sels[0] | sels[1] | sels[2] | sels[3]
    mt = onehot.astype(BF16)
    ri = lax.broadcasted_iota(jnp.int32, (tm, tm), 0)
    ci = lax.broadcasted_iota(jnp.int32, (tm, tm), 1)
    before = (ri > ci).astype(BF16)
    prior = jnp.dot(before, mt, preferred_element_type=F32) + base_ref[...]
    idx_o = jnp.zeros(logits.shape, jnp.int32)
    gate_o = jnp.zeros(logits.shape, F32)
    rank_o = jnp.zeros(logits.shape, jnp.int32)
    for k in range(TOP_K):
        rk = jnp.sum(jnp.where(sels[k], prior, 0.0), axis=-1, keepdims=True).astype(jnp.int32)
        idx_o = jnp.where(lane == k, ams[k], idx_o)
        gate_o = jnp.where(lane == k, es[k] / den, gate_o)
        rank_o = jnp.where(lane == k, rk, rank_o)
    idx_ref[...] = idx_o
    gate_ref[...] = gate_o
    rank_ref[...] = rank_o
    base_ref[...] += jnp.sum(onehot.astype(F32), axis=0, keepdims=True)
    cnt_ref[...] = base_ref[...]


def _router(h1, w_router, b_router):
    t, d = h1.shape
    n_exp = w_router.shape[1]
    tm = _pick(t, 256)
    w = jnp.pad(w_router.astype(F32), ((0, 0), (0, LANES - n_exp)))
    b = jnp.pad(b_router.astype(F32), (0, LANES - n_exp)).reshape(1, LANES)
    tok = pl.BlockSpec((tm, LANES), lambda i: (i, 0))
    kern = functools.partial(_router_kernel, n_exp=n_exp, tm=tm)
    return pl.pallas_call(
        kern,
        grid=(t // tm,),
        in_specs=[pl.BlockSpec((tm, d), lambda i: (i, 0)),
                  pl.BlockSpec((d, LANES), lambda i: (0, 0)),
                  pl.BlockSpec((1, LANES), lambda i: (0, 0))],
        out_specs=[tok, tok, tok, pl.BlockSpec((1, LANES), lambda i: (0, 0))],
        out_shape=[jax.ShapeDtypeStruct((t, LANES), jnp.int32), jax.ShapeDtypeStruct((t, LANES), F32),
                   jax.ShapeDtypeStruct((t, LANES), jnp.int32), jax.ShapeDtypeStruct((1, LANES), F32)],
        scratch_shapes=[pltpu.VMEM((1, LANES), F32)],
        compiler_params=_cparams(("arbitrary",)),
        name="router",
    )(h1, w, b)


def _dispatch_kernel(dest_ref, pend_ref, nused_ref, x_ref, xs_ref, zero_ref, sem, zsem, *, tt, n_exp, nblk):
    i = pl.program_id(0)
    base = i * (tt * TOP_K)

    @pl.when(i == 0)
    def _():
        zero_ref[...] = jnp.zeros_like(zero_ref)

        def zero_block(start):
            start = pl.multiple_of(start, ROW_BLOCK)
            return pltpu.make_async_copy(zero_ref, xs_ref.at[pl.ds(start, ROW_BLOCK)], zsem)

        def expert_has_rows(e):
            prev = jnp.where(e == 0, 0, pend_ref[jnp.maximum(e - 1, 0)])
            return pend_ref[e] > prev

        def start_expert(e, carry):
            @pl.when(expert_has_rows(e))
            def _():
                zero_block(pend_ref[e] - ROW_BLOCK).start()
            return carry

        def wait_expert(e, carry):
            @pl.when(expert_has_rows(e))
            def _():
                zero_block(0).wait()
            return carry

        def start_tail(b, carry):
            zero_block(b * ROW_BLOCK).start()
            return carry

        def wait_tail(b, carry):
            zero_block(0).wait()
            return carry

        lax.fori_loop(0, n_exp, start_expert, 0)
        lax.fori_loop(nused_ref[0], nblk, start_tail, 0)
        lax.fori_loop(0, n_exp, wait_expert, 0)
        lax.fori_loop(nused_ref[0], nblk, wait_tail, 0)

    def issue(r, carry):
        for k in range(TOP_K):
            row = dest_ref[base + r * TOP_K + k]
            pltpu.make_async_copy(x_ref.at[pl.ds(r, 1)], xs_ref.at[pl.ds(row, 1)], sem).start()
        return carry

    lax.fori_loop(0, tt, issue, 0, unroll=4)

    for k in range(TOP_K):
        pltpu.make_async_copy(x_ref, xs_ref.at[pl.ds(0, tt)], sem).wait()


def _dispatch(xp, dest, pad_ends, n_used, n_rows):
    t, dh = xp.shape
    tt = _pick(t, 256)
    kern = functools.partial(_dispatch_kernel, tt=tt, n_exp=pad_ends.shape[0], nblk=n_rows // ROW_BLOCK)
    return pl.pallas_call(
        kern,
        grid_spec=pltpu.PrefetchScalarGridSpec(
            num_scalar_prefetch=3,
            grid=(t // tt,),
            in_specs=[pl.BlockSpec((tt, dh), lambda i, *_: (i, 0))],
            out_specs=pl.BlockSpec(memory_space=pl.ANY),
            scratch_shapes=[pltpu.VMEM((ROW_BLOCK, dh), jnp.uint32),
                            pltpu.SemaphoreType.DMA(()), pltpu.SemaphoreType.DMA(())]),
        out_shape=jax.ShapeDtypeStruct((n_rows, dh), jnp.uint32),
        compiler_params=_cparams(("arbitrary",)),
        name="dispatch",
    )(dest, pad_ends.astype(jnp.int32), n_used, xp)


def _expert_kernel(be_ref, nb_ref, x_ref, wg_ref, wu_ref, bg_ref, bu_ref, wd_ref, bd_ref, o_ref, *, half):
    i = pl.program_id(0)
    j = pl.program_id(1)

    @pl.when(i < nb_ref[0])
    def _():
        lo, hi = _unpack_bf16_pair(x_ref[...])
        gate = (jnp.dot(lo, wg_ref[0, :half, :], preferred_element_type=F32)
                + jnp.dot(hi, wg_ref[0, half:, :], preferred_element_type=F32) + bg_ref[0])
        up = (jnp.dot(lo, wu_ref[0, :half, :], preferred_element_type=F32)
              + jnp.dot(hi, wu_ref[0, half:, :], preferred_element_type=F32) + bu_ref[0])
        gate = jnp.minimum(gate, SWIGLU_LIMIT)
        up = jnp.clip(up, -SWIGLU_LIMIT, SWIGLU_LIMIT)
        act = gate * _sigmoid(SWIGLU_ALPHA * gate) * (up + 1.0)
        part = jnp.dot(act.astype(BF16), wd_ref[0], preferred_element_type=F32)

        @pl.when(j == 0)
        def _():
            o_ref[...] = part + bd_ref[0]

        @pl.when(j > 0)
        def _():
            o_ref[...] += part

    @pl.when((i >= nb_ref[0]) & (j == 0))
    def _():
        o_ref[...] = jnp.zeros_like(o_ref)


def _experts(xs, block_expert, n_used, w_gu, b_gu, w_down, b_down):
    n_rows, dh = xs.shape
    n_exp, d, ff2 = w_gu.shape
    ff = ff2 // 2
    tf = _pick(ff, 1024)
    nj = ff // tf
    nblk = n_rows // ROW_BLOCK
    b_gu3 = b_gu.astype(F32).reshape(n_exp, 1, ff2)
    b_d3 = b_down.astype(F32).reshape(n_exp, 1, d)

    def blk(i, nb):
        return jnp.maximum(jnp.minimum(i, nb[0] - 1), 0)

    def jj(i, j, nb):
        return jnp.where(i < nb[0], j, nj - 1)

    kern = functools.partial(_expert_kernel, half=d // 2)
    return pl.pallas_call(
        kern,
        grid_spec=pltpu.PrefetchScalarGridSpec(
            num_scalar_prefetch=2,
            grid=(nblk, nj),
            in_specs=[
                pl.BlockSpec((ROW_BLOCK, dh), lambda i, j, be, nb: (blk(i, nb), 0)),
                pl.BlockSpec((1, d, tf), lambda i, j, be, nb: (be[blk(i, nb)], 0, jj(i, j, nb))),
                pl.BlockSpec((1, d, tf), lambda i, j, be, nb: (be[blk(i, nb)], 0, nj + jj(i, j, nb))),
                pl.BlockSpec((1, 1, tf), lambda i, j, be, nb: (be[blk(i, nb)], 0, jj(i, j, nb))),
                pl.BlockSpec((1, 1, tf), lambda i, j, be, nb: (be[blk(i, nb)], 0, nj + jj(i, j, nb))),
                pl.BlockSpec((1, tf, d), lambda i, j, be, nb: (be[blk(i, nb)], jj(i, j, nb), 0)),
                pl.BlockSpec((1, 1, d), lambda i, j, be, nb: (be[blk(i, nb)], 0, 0)),
            ],
            out_specs=pl.BlockSpec((ROW_BLOCK, d), lambda i, j, be, nb: (i, 0))),
        out_shape=jax.ShapeDtypeStruct((n_rows, d), F32),
        compiler_params=_cparams(("arbitrary", "arbitrary")),
        name="experts",
    )(block_expert, n_used, xs, w_gu, w_gu, b_gu3, b_gu3, w_down, b_d3)


def _combine_kernel(dest_ref, y_ref, h_ref, gate_ref, g_ref, b_ref, o_ref, buf_ref, sem, *, tt, nsteps):
    i = pl.program_id(0)
    cur = i % 2

    def gather_tile(tile, buf):
        base = tile * (tt * TOP_K)

        def issue(r, carry):
            for k in range(TOP_K):
                row = dest_ref[base + r * TOP_K + k]
                pltpu.make_async_copy(y_ref.at[pl.ds(row, 1)], buf_ref.at[buf, k, pl.ds(r, 1)],
                                      sem.at[buf]).start()
            return carry

        lax.fori_loop(0, tt, issue, 0, unroll=4)

    @pl.when(i == 0)
    def _():
        gather_tile(0, 0)

    @pl.when(i + 1 < nsteps)
    def _():
        gather_tile(i + 1, 1 - cur)

    for k in range(TOP_K):
        pltpu.make_async_copy(y_ref.at[pl.ds(0, tt)], buf_ref.at[cur, k], sem.at[cur]).wait()
    gates = gate_ref[...]
    ffn = gates[:, 0:1] * buf_ref[cur, 0]
    for k in range(1, TOP_K):
        ffn = ffn + gates[:, k:k + 1] * buf_ref[cur, k]
    o_ref[...] = _ln(DEEPNORM_ALPHA * h_ref[...] + ffn, g_ref[...], b_ref[...])


def _combine(y, dest, h1, gates, g, b):
    t, d = h1.shape
    tt = _pick(t, 128)
    kern = functools.partial(_combine_kernel, tt=tt, nsteps=t // tt)
    return pl.pallas_call(
        kern,
        grid_spec=pltpu.PrefetchScalarGridSpec(
            num_scalar_prefetch=1,
            grid=(t // tt,),
            in_specs=[pl.BlockSpec(memory_space=pl.ANY),
                      pl.BlockSpec((tt, d), lambda i, dst: (i, 0)),
                      pl.BlockSpec((tt, LANES), lambda i, dst: (i, 0)),
                      pl.BlockSpec((1, d), lambda i, dst: (0, 0)),
                      pl.BlockSpec((1, d), lambda i, dst: (0, 0))],
            out_specs=pl.BlockSpec((tt, d), lambda i, dst: (i, 0)),
            scratch_shapes=[pltpu.VMEM((2, TOP_K, tt, d), F32), pltpu.SemaphoreType.DMA((2,))]),
        out_shape=jax.ShapeDtypeStruct((t, d), F32),
        compiler_params=_cparams(("arbitrary",)),
        name="combine",
    )(dest, y, h1, gates, g.reshape(1, d), b.reshape(1, d))


def kernel(x, ln_in_g, ln_in_b, w_in, lam_re, lam_im, log_step, ssm_b_re, ssm_b_im, ssm_c_re, ssm_c_im, ssm_d, w_glu, b_glu, conv_w, a_log, dt_bias, dn_norm_w, w_proj_ssm, w_proj_dn, w_out, ln1_g, ln1_b, w_router, b_router, w_gate_up, b_gate_up, w_down, b_down, ln2_g, ln2_b):
    bsz, seq, d = x.shape
    t = bsz * seq
    sw = d // 2
    heads = d // DN_HEAD_DIM
    dn = heads * DN_HEAD_DIM
    groups = sw // SSM_GROUP_DIM
    n_state = lam_re.shape[-1]
    n_exp = w_router.shape[-1]
    assert w_in.shape[0] == DEPTH == 1
    assert seq % (CHUNK * 4) == 0 and sw % LANES == 0 and groups % 2 == 0 and 2 * heads <= LANES

    h, hb = _ln_in(x.reshape(t, d), ln_in_g, ln_in_b)

    wi = w_in[0]
    c_ab = sw + 4 * dn
    w_main = jnp.concatenate([wi[:, :c_ab], wi[:, c_ab + 2 * heads:]], axis=1).astype(BF16)
    w_ab = jnp.pad(wi[:, c_ab:c_ab + 2 * heads], ((0, 0), (0, LANES - 2 * heads))).astype(BF16)
    proj = _matmul(hb, w_main, F32, "proj_in")
    ab = _matmul(hb, w_ab, F32, "proj_ab")

    ops = _s5_operators(lam_re[0], lam_im[0], log_step[0], ssm_b_re[0], ssm_b_im[0],
                        ssm_c_re[0], ssm_c_im[0], ssm_d[0])
    ff2 = w_gate_up.shape[-1]
    y_s, (w_dn_bf,) = _s5_scan(proj, ops, bsz, seq, sw, n_state,
                               cast_weights=(w_down[0].reshape(n_exp * (ff2 // 2), d),))
    w_dn_bf = w_dn_bf.reshape(n_exp, ff2 // 2, d)
    y_ssm = _ssm_post(y_s, w_glu[0].astype(BF16), b_glu[0].astype(F32))

    y_dn, (w_gu_bf,) = _deltanet(
        proj, ab, conv_w[0], a_log[0], dt_bias[0], dn_norm_w[0], bsz, seq, heads, off_q=sw // LANES,
        cast_weights=(w_gate_up[0].reshape(n_exp * d, ff2),))
    w_gu_bf = w_gu_bf.reshape(n_exp, d, ff2)

    merged = _merge(y_ssm, y_dn, proj, sw + 4 * dn, w_proj_ssm[0].astype(BF16), w_proj_dn[0].astype(BF16))
    h1, h1p = _out_ln(merged, w_out[0].astype(BF16), h, ln1_g[0], ln1_b[0])

    idx, gates, rank, cnt = _router(h1, w_router[0], b_router[0])
    counts = cnt[0, :n_exp].astype(jnp.int32)
    padded = (counts + ROW_BLOCK - 1) // ROW_BLOCK * ROW_BLOCK
    pad_ends = jnp.cumsum(padded)
    pad_starts = pad_ends - padded
    dest = (pad_starts[idx[:, :TOP_K]] + rank[:, :TOP_K]).astype(jnp.int32).reshape(t * TOP_K)
    n_rows = t * TOP_K + n_exp * ROW_BLOCK
    nblk = n_rows // ROW_BLOCK
    blk_start = jnp.arange(nblk, dtype=jnp.int32) * ROW_BLOCK
    block_expert = jnp.minimum(jnp.sum((pad_ends[None, :] <= blk_start[:, None]).astype(jnp.int32), axis=1),
                               n_exp - 1).astype(jnp.int32)
    n_used = (pad_ends[-1:] // ROW_BLOCK).astype(jnp.int32)
    xs = _dispatch(h1p, dest, pad_ends, n_used, n_rows)
    y = _experts(xs, block_expert, n_used, w_gu_bf, b_gate_up[0], w_dn_bf, b_down[0])
    out = _combine(y, dest, h1, gates, ln2_g[0], ln2_b[0])
    return out.reshape(bsz, seq, d)
```

```python
import functools
import itertools
import math

import jax
import jax.numpy as jnp
from jax import lax
from jax.experimental import pallas as pl
from jax.experimental.pallas import tpu as pltpu

F32 = jnp.float32
BF16 = jnp.bfloat16
HI = lax.Precision.HIGHEST

LANES = 128
CHUNK = 64
LN_EPS = 1e-5
NORM_EPS = 1e-6
SSM_GROUP_DIM = 16
SSM_TC = 16
DN_HEAD_DIM = 128
CONV_WIDTH = 4
TOP_K = 4
SWIGLU_LIMIT = 7.0
SWIGLU_ALPHA = 1.702
DEPTH = 1
DEEPNORM_ALPHA = (2 * DEPTH) ** 0.25
ROW_BLOCK = 512
VMEM_LIMIT = 56 * 1024 * 1024


def _cparams(sem):
    return pltpu.CompilerParams(dimension_semantics=sem, vmem_limit_bytes=VMEM_LIMIT)


def _pick(n, pref):
    t = min(n, pref)
    while n % t:
        t //= 2
    return t


def _ln(x, g, b):
    mu = jnp.mean(x, axis=-1, keepdims=True)
    xc = x - mu
    var = jnp.mean(xc * xc, axis=-1, keepdims=True)
    return xc * lax.rsqrt(var + LN_EPS) * g + b


def _ln_in_kernel(x_ref, g_ref, b_ref, h_ref, hb_ref):
    h = _ln(x_ref[...], g_ref[...], b_ref[...])
    h_ref[...] = h
    hb_ref[...] = h.astype(BF16)


def _ln_in(x, g, b):
    t, d = x.shape
    tm = _pick(t, 512)
    return pl.pallas_call(
        _ln_in_kernel,
        grid=(t // tm,),
        in_specs=[pl.BlockSpec((tm, d), lambda i: (i, 0)),
                  pl.BlockSpec((1, d), lambda i: (0, 0)),
                  pl.BlockSpec((1, d), lambda i: (0, 0))],
        out_specs=[pl.BlockSpec((tm, d), lambda i: (i, 0)),
                   pl.BlockSpec((tm, d), lambda i: (i, 0))],
        out_shape=[jax.ShapeDtypeStruct((t, d), F32), jax.ShapeDtypeStruct((t, d), BF16)],
        compiler_params=_cparams(("arbitrary",)),
        name="ln_in",
    )(x, g.reshape(1, d), b.reshape(1, d))


def _mm_kernel(x_ref, w_ref, o_ref):
    o_ref[...] = jnp.dot(x_ref[...], w_ref[...].astype(BF16), preferred_element_type=F32).astype(o_ref.dtype)


def _matmul(x, w, out_dtype, name, ncols=None):
    m, k = x.shape
    n = w.shape[1] if ncols is None else ncols
    tm = _pick(m, 1024)
    tn = _pick(n, 1024)
    return pl.pallas_call(
        _mm_kernel,
        grid=(n // tn, m // tm),
        in_specs=[pl.BlockSpec((tm, k), lambda j, i: (i, 0)),
                  pl.BlockSpec((k, tn), lambda j, i: (0, j))],
        out_specs=pl.BlockSpec((tm, tn), lambda j, i: (i, j)),
        out_shape=jax.ShapeDtypeStruct((m, n), out_dtype),
        compiler_params=_cparams(("arbitrary", "arbitrary")),
        name=name,
    )(x, w)


CAST_SLICES = 16


def _cast_slices(src_refs, dst_refs):
    for sl in range(CAST_SLICES):
        for src, dst in zip(src_refs, dst_refs):
            n = src.shape[0] // CAST_SLICES
            dst[sl * n:(sl + 1) * n, :] = src[sl * n:(sl + 1) * n, :].astype(BF16)
        yield


def _cast_specs(weights, grid):
    g0, g1, g2 = grid
    nsteps = g0 * g1 * g2
    specs = []
    for w in weights:
        rows, cols = w.shape
        assert rows % (nsteps * CAST_SLICES * 16) == 0, (rows, nsteps)
        specs.append(pl.BlockSpec((rows // nsteps, cols), lambda a, b, c: ((a * g1 + b) * g2 + c, 0)))
    return specs


def _s5_operators(lam_re, lam_im, log_step, b_re, b_im, c_re, c_im, d_skip):
    g, p = lam_re.shape
    hd, tc = SSM_GROUP_DIM, SSM_TC
    lr, li = lam_re.astype(F32), lam_im.astype(F32)
    step = jnp.exp(log_step.astype(F32))[:, None]
    tau = jnp.arange(tc + 1, dtype=F32)[:, None, None]
    mag = jnp.exp(lr * step * tau)
    pr, pi = mag * jnp.cos(li * step * tau), mag * jnp.sin(li * step * tau)
    a_re, a_im = pr[1], pi[1]
    den = lr * lr + li * li
    nr, ni = a_re - 1.0, a_im
    f_re = (nr * lr + ni * li) / den
    f_im = (ni * lr - nr * li) / den
    br, bi = b_re.astype(F32), b_im.astype(F32)
    bb_re = f_re[..., None] * br - f_im[..., None] * bi
    bb_im = f_re[..., None] * bi + f_im[..., None] * br
    cr, ci = c_re.astype(F32), c_im.astype(F32)
    ca_re = cr[None] * pr[:, :, None, :] - ci[None] * pi[:, :, None, :]
    ca_im = cr[None] * pi[:, :, None, :] + ci[None] * pr[:, :, None, :]
    ca_g = jnp.concatenate([jnp.transpose(ca_re[:tc], (1, 0, 2, 3)).reshape(g, tc * hd, p),
                            -jnp.transpose(ca_im[:tc], (1, 0, 2, 3)).reshape(g, tc * hd, p)], axis=2)
    bb_g = jnp.concatenate([bb_re, bb_im], axis=1)
    kk = jnp.einsum('gxp,gpi->gxi', ca_g, bb_g, precision=HI)
    kk = jnp.transpose(kk.reshape(g, tc, hd, hd), (1, 0, 2, 3))
    kk = kk.at[0].add(d_skip.astype(F32)[:, :, None] * jnp.eye(hd, dtype=F32)[None])
    gs = LANES // hd
    ns = g // gs
    dup = LANES // p
    eye = jnp.eye(gs, dtype=F32)
    k5 = jnp.transpose(kk.reshape(tc, ns, gs, hd, hd), (1, 0, 2, 4, 3))
    bm = (k5[:, :, :, :, None, :] * eye[None, None, :, None, :, None]).reshape(ns, tc, LANES, LANES)
    dt = jnp.arange(tc)

    def rows(x_re, x_im):
        x = jnp.stack([x_re, x_im], axis=1).reshape(tc, 2, ns, gs * hd, p)
        x = jnp.transpose(x, (2, 0, 1, 3, 4))
        return jnp.concatenate([x] * dup, axis=-1)

    prr, pir = pr[tc - 1 - dt], pi[tc - 1 - dt]
    bc_re = prr[:, :, None, :] * jnp.transpose(bb_re, (0, 2, 1))[None] - pir[:, :, None, :] * jnp.transpose(bb_im, (0, 2, 1))[None]
    bc_im = prr[:, :, None, :] * jnp.transpose(bb_im, (0, 2, 1))[None] + pir[:, :, None, :] * jnp.transpose(bb_re, (0, 2, 1))[None]
    bcc = rows(bc_re, bc_im)
    cct = rows(ca_re[1:], -ca_im[1:])
    a16r = pr[tc].reshape(1, g * p)
    a16i = pi[tc].reshape(1, g * p)
    return (bm.astype(BF16), bcc.astype(BF16), cct.astype(BF16), a16r, a16i)


def _s5_kernel(x_ref, bm_ref, bcc_ref, cct_ref, ar_ref, ai_ref, *rest, ncb, sl, n_state, ncast):
    cast_src, y_ref, cast_dst = rest[:ncast], rest[ncast], rest[ncast + 1:2 * ncast + 1]
    mf_ref, bcf_ref, ccf_ref, st_ref, cr_ref, ci_ref = rest[2 * ncast + 1:]
    tc = SSM_TC
    for _ in _cast_slices(cast_src, cast_dst):
        pass

    @pl.when((pl.program_id(1) == 0) & (pl.program_id(2) == 0))
    def _():
        zero = jnp.zeros((LANES, LANES), BF16)
        for di in range(tc):
            for do in range(tc):
                mf_ref[di * LANES:(di + 1) * LANES, do * LANES:(do + 1) * LANES] = (
                    bm_ref[0, do - di] if do >= di else zero)
        r = lax.broadcasted_iota(jnp.int32, (LANES, sl), 0)
        c = lax.broadcasted_iota(jnp.int32, (LANES, sl), 1)
        same_group = (r // SSM_GROUP_DIM) == (c // n_state)

        def widen(blk):
            wide = jnp.concatenate([blk.astype(F32)] * (sl // LANES), axis=1)
            return jnp.where(same_group, wide, 0.0).astype(BF16)

        for dt in range(tc):
            for ri in range(2):
                bcf_ref[dt * LANES:(dt + 1) * LANES, ri * sl:(ri + 1) * sl] = widen(bcc_ref[0, dt, ri])
                ccf_ref[dt * LANES:(dt + 1) * LANES, ri * sl:(ri + 1) * sl] = widen(cct_ref[0, dt, ri])

    @pl.when(pl.program_id(2) == 0)
    def _():
        cr_ref[...] = jnp.zeros_like(cr_ref)
        ci_ref[...] = jnp.zeros_like(ci_ref)

    xs = jnp.concatenate([x_ref[pl.ds(dt, ncb, stride=SSM_TC), :] for dt in range(SSM_TC)],
                         axis=1).astype(BF16)
    st_ref[...] = jnp.dot(xs, bcf_ref[...], preferred_element_type=F32)
    ar = ar_ref[...]
    ai = ai_ref[...]

    def body(c, carry):
        sr, si = carry
        lr = st_ref[pl.ds(c, 1), :sl]
        li = st_ref[pl.ds(c, 1), sl:]
        st_ref[pl.ds(c, 1), :sl] = sr
        st_ref[pl.ds(c, 1), sl:] = si
        return ar * sr - ai * si + lr, ar * si + ai * sr + li

    sr, si = lax.fori_loop(0, ncb, body, (cr_ref[...], ci_ref[...]), unroll=8)
    cr_ref[...] = sr
    ci_ref[...] = si
    y = (jnp.dot(xs, mf_ref[...], preferred_element_type=F32)
         + _dot_nt(st_ref[...].astype(BF16), ccf_ref[...]))
    for dt in range(SSM_TC):
        y_ref[pl.ds(dt, ncb, stride=SSM_TC), :] = y[:, dt * LANES:(dt + 1) * LANES]


def _s5_scan(proj, ops, bsz, seq, sw, n_state, cast_weights=()):
    bm, bcc, cct, a16r, a16i = ops
    ns = sw // LANES
    sl = (LANES // SSM_GROUP_DIM) * n_state
    ncb = _pick(seq // SSM_TC, 128)
    rt = ncb * SSM_TC
    nt = seq // rt
    kw = SSM_TC * LANES
    cast_specs = _cast_specs(cast_weights, (ns, bsz, nt))
    kern = functools.partial(_s5_kernel, ncb=ncb, sl=sl, n_state=n_state, ncast=len(cast_weights))
    outs = pl.pallas_call(
        kern,
        grid=(ns, bsz, nt),
        in_specs=[pl.BlockSpec((rt, LANES), lambda s, b, i: (b * nt + i, s)),
                  pl.BlockSpec((1, SSM_TC, LANES, LANES), lambda s, b, i: (s, 0, 0, 0)),
                  pl.BlockSpec((1, SSM_TC, 2, LANES, LANES), lambda s, b, i: (s, 0, 0, 0, 0)),
                  pl.BlockSpec((1, SSM_TC, 2, LANES, LANES), lambda s, b, i: (s, 0, 0, 0, 0)),
                  pl.BlockSpec((1, sl), lambda s, b, i: (0, s)),
                  pl.BlockSpec((1, sl), lambda s, b, i: (0, s))] + cast_specs,
        out_specs=[pl.BlockSpec((rt, LANES), lambda s, b, i: (b * nt + i, s))] + cast_specs,
        out_shape=([jax.ShapeDtypeStruct((bsz * seq, sw), F32)]
                   + [jax.ShapeDtypeStruct(w.shape, BF16) for w in cast_weights]),
        scratch_shapes=[pltpu.VMEM((kw, kw), BF16), pltpu.VMEM((kw, 2 * sl), BF16), pltpu.VMEM((kw, 2 * sl), BF16),
                        pltpu.VMEM((ncb, 2 * sl), F32), pltpu.VMEM((1, sl), F32), pltpu.VMEM((1, sl), F32)],
        compiler_params=_cparams(("arbitrary", "arbitrary", "arbitrary")),
        name="s5_scan",
    )(proj, bm, bcc, cct, a16r, a16i, *cast_weights)
    return outs[0], outs[1:]


def _ssm_post_kernel(y_ref, w_ref, b_ref, o_ref):
    y = y_ref[...]
    yg = 0.5 * y * (1.0 + lax.erf(y * (1.0 / math.sqrt(2.0))))
    s = jnp.dot(yg.astype(BF16), w_ref[...], preferred_element_type=F32) + b_ref[...]
    o_ref[...] = (yg * jax.nn.sigmoid(s)).astype(o_ref.dtype)


def _ssm_post(y, w_glu, b_glu):
    t, n = y.shape
    tm = _pick(t, 512)
    return pl.pallas_call(
        _ssm_post_kernel,
        grid=(t // tm,),
        in_specs=[pl.BlockSpec((tm, n), lambda i: (i, 0)),
                  pl.BlockSpec((n, n), lambda i: (0, 0)),
                  pl.BlockSpec((1, n), lambda i: (0, 0))],
        out_specs=pl.BlockSpec((tm, n), lambda i: (i, 0)),
        out_shape=jax.ShapeDtypeStruct((t, n), BF16),
        compiler_params=_cparams(("arbitrary",)),
        name="ssm_post",
    )(y, w_glu, b_glu.reshape(1, n))


def _dot_nt(a, b):
    return lax.dot_general(a, b, (((1,), (1,)), ((), ())), preferred_element_type=F32)


def _dot_tn(a, b):
    return lax.dot_general(a, b, (((0,), (0,)), ((), ())), preferred_element_type=F32)


def _dot_hi(a, b):
    return jnp.dot(a, b, preferred_element_type=F32, precision=HI)


def _sigmoid(x):
    return 0.5 * jnp.tanh(0.5 * x) + 0.5


def _silu(x):
    hx = 0.5 * x
    return hx * jnp.tanh(hx) + hx


DN_HALO = 8
DN_HEADS_PER_STEP = 8


def _dn_kernel(q_ref, k_ref, v_ref, z_ref, ab_ref, cq_ref, ck_ref, cv_ref, al_ref, dtb_ref, nw_ref,
               *rest, tt, heads, hp, ncast):
    cast_src, o_ref, cast_dst, scratch = rest[:ncast], rest[ncast], rest[ncast + 1:2 * ncast + 1], rest[2 * ncast + 1:]
    s_refs, xx_refs = scratch[:hp], scratch[hp:]

    @pl.when(pl.program_id(2) == 0)
    def _():
        for s_ref, xx_ref in zip(s_refs, xx_refs):
            s_ref[...] = jnp.zeros_like(s_ref)
            xx_ref[:, 0:DN_HALO, :] = jnp.zeros((3, DN_HALO, DN_HEAD_DIM), F32)

    ab = ab_ref[...]
    xa = ab + dtb_ref[...]
    sp = jnp.maximum(xa, 0.0) + jnp.log(1.0 + jnp.exp(-jnp.abs(xa)))
    gc_all = -jnp.exp(al_ref[...]) * sp
    pos = lax.broadcasted_iota(jnp.int32, gc_all.shape, 0) & (CHUNK - 1)
    sh = 1
    while sh < CHUNK:
        gc_all = gc_all + jnp.where(pos >= sh, pltpu.roll(gc_all, sh, axis=0), 0.0)
        sh *= 2
    gate_vals = (gc_all, jnp.exp(gc_all), _sigmoid(ab))
    gens = [_dn_head(hh, pl.program_id(1) * hp + hh, q_ref, k_ref, v_ref, z_ref, gate_vals, cq_ref, ck_ref, cv_ref,
                     nw_ref, o_ref, s_refs[hh], xx_refs[hh], tt=tt, heads=heads)
            for hh in range(hp)]
    if ncast:
        gens.append(_cast_slices(cast_src, cast_dst))
    for _ in itertools.zip_longest(*gens):
        pass


def _dn_head(hh, h, q_ref, k_ref, v_ref, z_ref, gate_vals, cq_ref, ck_ref, cv_ref, nw_ref,
             o_ref, s_ref, xx_ref, *, tt, heads):
    hal = DN_HALO
    ls = slice(hh * DN_HEAD_DIM, (hh + 1) * DN_HEAD_DIM)

    def conv(idx, x_ref, cw_ref):
        x = x_ref[:, ls]
        xx_ref[idx, hal:, :] = x
        w = cw_ref[:, ls]
        acc = w[CONV_WIDTH - 1:CONV_WIDTH, :] * x
        for j in range(CONV_WIDTH - 1):
            off = hal - (CONV_WIDTH - 1) + j
            acc = acc + w[j:j + 1, :] * xx_ref[idx, off:off + tt, :]
        xx_ref[idx, 0:hal, :] = x[tt - hal:, :]
        return _silu(acc)

    qc = conv(0, q_ref, cq_ref)
    kc = conv(1, k_ref, ck_ref)
    vc = conv(2, v_ref, cv_ref)
    qn = qc * lax.rsqrt(jnp.sum(qc * qc, axis=-1, keepdims=True) + NORM_EPS) * (DN_HEAD_DIM ** -0.5)
    kn = kc * lax.rsqrt(jnp.sum(kc * kc, axis=-1, keepdims=True) + NORM_EPS)

    gc_all, egc_all, beta_all = gate_vals
    lane = lax.broadcasted_iota(jnp.int32, gc_all.shape, 1)

    def pick(vals, ln):
        col = jnp.sum(jnp.where(lane == ln, vals, 0.0), axis=-1, keepdims=True)
        return jnp.broadcast_to(col, (tt, DN_HEAD_DIM))

    beta_col = pick(beta_all, h + heads)
    gc = pick(gc_all, h)
    egc = pick(egc_all, h)
    nch = tt // CHUNK
    gc_row = jnp.transpose(gc)[0:1, :]

    ri = lax.broadcasted_iota(jnp.int32, (tt, tt), 0)
    ci = lax.broadcasted_iota(jnp.int32, (tt, tt), 1)
    same = (ri // CHUNK) == (ci // CHUNK)
    causal = same & (ri >= ci)
    strict = same & (ri > ci)
    gc_wide = jnp.concatenate([gc] * (tt // DN_HEAD_DIM), axis=1)
    decay = jnp.where(causal, jnp.exp(jnp.where(causal, gc_wide - gc_row, 0.0)), 0.0)
    kb = kn * beta_col
    knb = kn.astype(BF16)
    yield
    a_raw = _dot_nt(kb.astype(BF16), knb)
    qk_raw = _dot_nt(qn.astype(BF16), knb)
    yield
    a_bd = jnp.where(strict, a_raw * decay, 0.0)
    qk_bd = jnp.where(causal, qk_raw * decay, 0.0).astype(BF16)

    def fold(m):
        out = m[0:CHUNK]
        for c in range(1, nch):
            out = out + m[c * CHUNK:(c + 1) * CHUNK]
        return out

    def spread(m):
        return jnp.where(same, jnp.concatenate([m] * nch, axis=0), 0.0)

    r64 = lax.broadcasted_iota(jnp.int32, (CHUNK, tt), 0)
    c64 = lax.broadcasted_iota(jnp.int32, (CHUNK, tt), 1)
    eye_cat = (r64 == (c64 & (CHUNK - 1))).astype(F32)
    pw_cat = fold(a_bd)
    inv_cat = eye_cat - pw_cat
    pw_bd = a_bd.astype(BF16)
    for _ in range(5):
        pw_cat = jnp.dot(pw_cat.astype(BF16), pw_bd, preferred_element_type=F32)
        yield
        pw_bd = spread(pw_cat).astype(BF16)
        inv_add = jnp.dot(inv_cat.astype(BF16), pw_bd, preferred_element_type=F32)
        yield
        inv_cat = inv_cat + inv_add
    inv_bd = spread(inv_cat).astype(BF16)
    rhs = jnp.concatenate([vc * beta_col, kb * egc], axis=1).astype(BF16)
    sol = jnp.dot(inv_bd, rhs, preferred_element_type=F32)
    yield
    u_all, w_all = sol[:, :DN_HEAD_DIM], sol[:, DN_HEAD_DIM:]
    qe = qn * egc

    outs = []
    zblk = jnp.zeros((CHUNK, DN_HEAD_DIM), BF16)
    s = s_ref[...]
    for c in range(nch):
        sl = slice(c * CHUNK, (c + 1) * CHUNK)
        wq = jnp.concatenate([w_all[sl], qe[sl]], axis=0).astype(BF16)
        ws = jnp.dot(wq, s.astype(BF16), preferred_element_type=F32)
        yield
        v_new = u_all[sl] - ws[:CHUNK]
        vnb = v_new.astype(BF16)
        v_pad = jnp.concatenate([zblk] * c + [vnb] + [zblk] * (nch - 1 - c), axis=0)
        gc_c = gc[sl]
        g_last = gc_c[CHUNK - 1:CHUNK, :]
        k_dec = (kn[sl] * jnp.exp(g_last - gc_c)).astype(BF16)
        s_add = _dot_tn(k_dec, vnb)
        o_add = jnp.dot(qk_bd[sl], v_pad, preferred_element_type=F32)
        yield
        s = s * jnp.exp(g_last) + s_add
        outs.append(ws[CHUNK:] + o_add)
    s_ref[...] = s
    o = jnp.concatenate(outs, axis=0)
    o = o * lax.rsqrt(jnp.mean(o * o, axis=-1, keepdims=True) + NORM_EPS) * nw_ref[...]
    o_ref[:, ls] = (o * _silu(z_ref[:, ls])).astype(o_ref.dtype)


def _deltanet(proj, ab, conv_w, a_log, dt_bias, norm_w, bsz, seq, heads, off_q, cast_weights=()):
    t = bsz * seq
    tt = _pick(seq, 256)
    nt = seq // tt
    hd = DN_HEAD_DIM
    cw = conv_w.reshape(CONV_WIDTH, 3 * heads * hd).astype(F32)
    pad = LANES - heads
    al = jnp.pad(a_log.astype(F32), (0, pad)).reshape(1, LANES)
    dtb = jnp.pad(dt_bias.astype(F32), (0, pad)).reshape(1, LANES)
    nw = norm_w.astype(F32).reshape(1, hd)

    hp = math.gcd(math.gcd(heads, off_q), DN_HEADS_PER_STEP)
    hw = hp * hd
    oq, nh = off_q // hp, heads // hp

    def act(o):
        return pl.BlockSpec((tt, hw), lambda b, h, i: (b * nt + i, oq + o * nh + h))

    def cws(o):
        return pl.BlockSpec((CONV_WIDTH, hw), lambda b, h, i: (0, o * nh + h))

    row = pl.BlockSpec((1, LANES), lambda b, h, i: (0, 0))
    ncast = len(cast_weights)
    cast_specs = _cast_specs(cast_weights, (bsz, nh, nt))
    kern = functools.partial(_dn_kernel, tt=tt, heads=heads, hp=hp, ncast=ncast)
    outs = pl.pallas_call(
        kern,
        grid=(bsz, nh, nt),
        in_specs=[act(0), act(1), act(2), act(3),
                  pl.BlockSpec((tt, LANES), lambda b, h, i: (b * nt + i, 0)),
                  cws(0), cws(1), cws(2), row, row, row] + cast_specs,
        out_specs=[pl.BlockSpec((tt, hw), lambda b, h, i: (b * nt + i, h))] + cast_specs,
        out_shape=([jax.ShapeDtypeStruct((t, heads * hd), BF16)]
                   + [jax.ShapeDtypeStruct(w.shape, BF16) for w in cast_weights]),
        scratch_shapes=([pltpu.VMEM((hd, hd), F32)] * hp
                        + [pltpu.VMEM((3, tt + DN_HALO, hd), F32)] * hp),
        compiler_params=_cparams(("arbitrary", "arbitrary", "arbitrary")),
        name="deltanet",
    )(proj, proj, proj, proj, ab, cw, cw, cw, al, dtb, nw, *cast_weights)
    return outs[0], outs[1:]


def _merge_kernel(ys_ref, yd_ref, gs_ref, gd_ref, ws_ref, wd_ref, o_ref):
    ps = jnp.dot(ys_ref[...], ws_ref[...], preferred_element_type=F32)
    pd = jnp.dot(yd_ref[...], wd_ref[...], preferred_element_type=F32)
    o_ref[...] = (jax.nn.sigmoid(gs_ref[...]) * ps + jax.nn.sigmoid(gd_ref[...]) * pd).astype(o_ref.dtype)


def _merge(ys, yd, gates, gate_col, w_ps, w_pd):
    t, ns = ys.shape
    d = w_ps.shape[1]
    tm = _pick(t, 512)
    tn = _pick(ns, 1024)
    nb = d // tn
    off = gate_col // tn
    assert gate_col % tn == 0
    return pl.pallas_call(
        _merge_kernel,
        grid=(nb, t // tm),
        in_specs=[pl.BlockSpec((tm, ns), lambda j, i: (i, 0)),
                  pl.BlockSpec((tm, d), lambda j, i: (i, 0)),
                  pl.BlockSpec((tm, tn), lambda j, i: (i, off + j)),
                  pl.BlockSpec((tm, tn), lambda j, i: (i, off + nb + j)),
                  pl.BlockSpec((ns, tn), lambda j, i: (0, j)),
                  pl.BlockSpec((d, tn), lambda j, i: (0, j))],
        out_specs=pl.BlockSpec((tm, tn), lambda j, i: (i, j)),
        out_shape=jax.ShapeDtypeStruct((t, d), BF16),
        compiler_params=_cparams(("arbitrary", "arbitrary")),
        name="merge",
    )(ys, yd, gates, gates, w_ps, w_pd)


def _pack_bf16_pair(x):
    half = x.shape[1] // 2
    lo = pltpu.bitcast(x[:, :half].astype(BF16).astype(F32), jnp.uint32)
    hi = pltpu.bitcast(x[:, half:].astype(BF16).astype(F32), jnp.uint32)
    return (lo >> 16) | (hi & jnp.uint32(0xFFFF0000))


def _unpack_bf16_pair(u):
    lo = pltpu.bitcast(u << 16, F32).astype(BF16)
    hi = pltpu.bitcast(u & jnp.uint32(0xFFFF0000), F32).astype(BF16)
    return lo, hi


def _out_ln_kernel(m_ref, w_ref, h_ref, g_ref, b_ref, h1_ref, hp_ref):
    mix = jnp.dot(m_ref[...], w_ref[...], preferred_element_type=F32)
    h1 = _ln(DEEPNORM_ALPHA * h_ref[...] + mix, g_ref[...], b_ref[...])
    h1_ref[...] = h1
    hp_ref[...] = _pack_bf16_pair(h1)


def _out_ln(merged, w_out, h, g, b):
    t, d = h.shape
    tm = _pick(t, 512)
    return pl.pallas_call(
        _out_ln_kernel,
        grid=(t // tm,),
        in_specs=[pl.BlockSpec((tm, d), lambda i: (i, 0)),
                  pl.BlockSpec((d, d), lambda i: (0, 0)),
                  pl.BlockSpec((tm, d), lambda i: (i, 0)),
                  pl.BlockSpec((1, d), lambda i: (0, 0)),
                  pl.BlockSpec((1, d), lambda i: (0, 0))],
        out_specs=[pl.BlockSpec((tm, d), lambda i: (i, 0)),
                   pl.BlockSpec((tm, d // 2), lambda i: (i, 0))],
        out_shape=[jax.ShapeDtypeStruct((t, d), F32), jax.ShapeDtypeStruct((t, d // 2), jnp.uint32)],
        compiler_params=_cparams(("arbitrary",)),
        name="out_ln1",
    )(merged, w_out, h, g.reshape(1, d), b.reshape(1, d))


def _router_kernel(h_ref, w_ref, b_ref, idx_ref, gate_ref, rank_ref, cnt_ref, base_ref, *, n_exp, tm):
    i = pl.program_id(0)

    @pl.when(i == 0)
    def _():
        base_ref[...] = jnp.zeros_like(base_ref)

    h = h_ref[...]
    w = w_ref[...]
    h_hi = h.astype(BF16)
    h_lo = (h - h_hi.astype(F32)).astype(BF16)
    w_hi = w.astype(BF16)
    w_lo = (w - w_hi.astype(F32)).astype(BF16)
    logits = (jnp.dot(h_hi, w_hi, preferred_element_type=F32) + jnp.dot(h_hi, w_lo, preferred_element_type=F32)
              + jnp.dot(h_lo, w_hi, preferred_element_type=F32) + b_ref[...])
    lane = lax.broadcasted_iota(jnp.int32, logits.shape, 1)
    vals = jnp.where(lane < n_exp, logits, -jnp.inf)
    sels, tops, ams = [], [], []
    for _ in range(TOP_K):
        m = jnp.max(vals, axis=-1, keepdims=True)
        am = jnp.min(jnp.where(vals == m, lane, LANES), axis=-1, keepdims=True)
        sel = lane == am
        vals = jnp.where(sel, -jnp.inf, vals)
        sels.append(sel)
        tops.append(m)
        ams.append(am)
    es = [jnp.exp(v - tops[0]) for v in tops]
    den = es[0] + es[1] + es[2] + es[3]
    onehot = sels[0] | sels[1] | sels[2] | sels[3]
    mt = onehot.astype(BF16)
    ri = lax.broadcasted_iota(jnp.int32, (tm, tm), 0)
    ci = lax.broadcasted_iota(jnp.int32, (tm, tm), 1)
    before = (ri > ci).astype(BF16)
    prior = jnp.dot(before, mt, preferred_element_type=F32) + base_ref[...]
    idx_o = jnp.zeros(logits.shape, jnp.int32)
    gate_o = jnp.zeros(logits.shape, F32)
    rank_o = jnp.zeros(logits.shape, jnp.int32)
    for k in range(TOP_K):
        rk = jnp.sum(jnp.where(sels[k], prior, 0.0), axis=-1, keepdims=True).astype(jnp.int32)
        idx_o = jnp.where(lane == k, ams[k], idx_o)
        gate_o = jnp.where(lane == k, es[k] / den, gate_o)
        rank_o = jnp.where(lane == k, rk, rank_o)
    idx_ref[...] = idx_o
    gate_ref[...] = gate_o
    rank_ref[...] = rank_o
    base_ref[...] += jnp.sum(onehot.astype(F32), axis=0, keepdims=True)
    cnt_ref[...] = base_ref[...]


def _router(h1, w_router, b_router):
    t, d = h1.shape
    n_exp = w_router.shape[1]
    tm = _pick(t, 256)
    w = jnp.pad(w_router.astype(F32), ((0, 0), (0, LANES - n_exp)))
    b = jnp.pad(b_router.astype(F32), (0, LANES - n_exp)).reshape(1, LANES)
    tok = pl.BlockSpec((tm, LANES), lambda i: (i, 0))
    kern = functools.partial(_router_kernel, n_exp=n_exp, tm=tm)
    return pl.pallas_call(
        kern,
        grid=(t // tm,),
        in_specs=[pl.BlockSpec((tm, d), lambda i: (i, 0)),
                  pl.BlockSpec((d, LANES), lambda i: (0, 0)),
                  pl.BlockSpec((1, LANES), lambda i: (0, 0))],
        out_specs=[tok, tok, tok, pl.BlockSpec((1, LANES), lambda i: (0, 0))],
        out_shape=[jax.ShapeDtypeStruct((t, LANES), jnp.int32), jax.ShapeDtypeStruct((t, LANES), F32),
                   jax.ShapeDtypeStruct((t, LANES), jnp.int32), jax.ShapeDtypeStruct((1, LANES), F32)],
        scratch_shapes=[pltpu.VMEM((1, LANES), F32)],
        compiler_params=_cparams(("arbitrary",)),
        name="router",
    )(h1, w, b)


def _dispatch_kernel(dest_ref, pend_ref, nused_ref, x_ref, xs_ref, zero_ref, sem, zsem, *, tt, n_exp, nblk):
    i = pl.program_id(0)
    base = i * (tt * TOP_K)

    @pl.when(i == 0)
    def _():
        zero_ref[...] = jnp.zeros_like(zero_ref)

        def zero_block(start):
            start = pl.multiple_of(start, ROW_BLOCK)
            return pltpu.make_async_copy(zero_ref, xs_ref.at[pl.ds(start, ROW_BLOCK)], zsem)

        def expert_has_rows(e):
            prev = jnp.where(e == 0, 0, pend_ref[jnp.maximum(e - 1, 0)])
            return pend_ref[e] > prev

        def start_expert(e, carry):
            @pl.when(expert_has_rows(e))
            def _():
                zero_block(pend_ref[e] - ROW_BLOCK).start()
            return carry

        def wait_expert(e, carry):
            @pl.when(expert_has_rows(e))
            def _():
                zero_block(0).wait()
            return carry

        def start_tail(b, carry):
            zero_block(b * ROW_BLOCK).start()
            return carry

        def wait_tail(b, carry):
            zero_block(0).wait()
            return carry

        lax.fori_loop(0, n_exp, start_expert, 0)
        lax.fori_loop(nused_ref[0], nblk, start_tail, 0)
        lax.fori_loop(0, n_exp, wait_expert, 0)
        lax.fori_loop(nused_ref[0], nblk, wait_tail, 0)

    def issue(r, carry):
        for k in range(TOP_K):
            row = dest_ref[base + r * TOP_K + k]
            pltpu.make_async_copy(x_ref.at[pl.ds(r, 1)], xs_ref.at[pl.ds(row, 1)], sem).start()
        return carry

    lax.fori_loop(0, tt, issue, 0, unroll=4)

    for k in range(TOP_K):
        pltpu.make_async_copy(x_ref, xs_ref.at[pl.ds(0, tt)], sem).wait()


def _dispatch(xp, dest, pad_ends, n_used, n_rows):
    t, dh = xp.shape
    tt = _pick(t, 512)
    kern = functools.partial(_dispatch_kernel, tt=tt, n_exp=pad_ends.shape[0], nblk=n_rows // ROW_BLOCK)
    return pl.pallas_call(
        kern,
        grid_spec=pltpu.PrefetchScalarGridSpec(
            num_scalar_prefetch=3,
            grid=(t // tt,),
            in_specs=[pl.BlockSpec((tt, dh), lambda i, *_: (i, 0))],
            out_specs=pl.BlockSpec(memory_space=pl.ANY),
            scratch_shapes=[pltpu.VMEM((ROW_BLOCK, dh), jnp.uint32),
                            pltpu.SemaphoreType.DMA(()), pltpu.SemaphoreType.DMA(())]),
        out_shape=jax.ShapeDtypeStruct((n_rows, dh), jnp.uint32),
        compiler_params=_cparams(("arbitrary",)),
        name="dispatch",
    )(dest, pad_ends.astype(jnp.int32), n_used, xp)


def _expert_kernel(be_ref, nb_ref, x_ref, wg_ref, wu_ref, bg_ref, bu_ref, wd_ref, bd_ref, o_ref, *, half):
    i = pl.program_id(0)
    j = pl.program_id(1)

    @pl.when(i < nb_ref[0])
    def _():
        lo, hi = _unpack_bf16_pair(x_ref[...])
        gate = (jnp.dot(lo, wg_ref[0, :half, :], preferred_element_type=F32)
                + jnp.dot(hi, wg_ref[0, half:, :], preferred_element_type=F32) + bg_ref[0])
        up = (jnp.dot(lo, wu_ref[0, :half, :], preferred_element_type=F32)
              + jnp.dot(hi, wu_ref[0, half:, :], preferred_element_type=F32) + bu_ref[0])
        gate = jnp.minimum(gate, SWIGLU_LIMIT)
        up = jnp.clip(up, -SWIGLU_LIMIT, SWIGLU_LIMIT)
        act = gate * _sigmoid(SWIGLU_ALPHA * gate) * (up + 1.0)
        part = jnp.dot(act.astype(BF16), wd_ref[0], preferred_element_type=F32)

        @pl.when(j == 0)
        def _():
            o_ref[...] = part + bd_ref[0]

        @pl.when(j > 0)
        def _():
            o_ref[...] += part

    @pl.when((i >= nb_ref[0]) & (j == 0))
    def _():
        o_ref[...] = jnp.zeros_like(o_ref)


def _experts(xs, block_expert, n_used, w_gu, b_gu, w_down, b_down):
    n_rows, dh = xs.shape
    n_exp, d, ff2 = w_gu.shape
    ff = ff2 // 2
    tf = _pick(ff, 1024)
    nj = ff // tf
    nblk = n_rows // ROW_BLOCK
    b_gu3 = b_gu.astype(F32).reshape(n_exp, 1, ff2)
    b_d3 = b_down.astype(F32).reshape(n_exp, 1, d)

    def blk(i, nb):
        return jnp.maximum(jnp.minimum(i, nb[0] - 1), 0)

    def jj(i, j, nb):
        return jnp.where(i < nb[0], j, nj - 1)

    kern = functools.partial(_expert_kernel, half=d // 2)
    return pl.pallas_call(
        kern,
        grid_spec=pltpu.PrefetchScalarGridSpec(
            num_scalar_prefetch=2,
            grid=(nblk, nj),
            in_specs=[
                pl.BlockSpec((ROW_BLOCK, dh), lambda i, j, be, nb: (blk(i, nb), 0)),
                pl.BlockSpec((1, d, tf), lambda i, j, be, nb: (be[blk(i, nb)], 0, jj(i, j, nb))),
                pl.BlockSpec((1, d, tf), lambda i, j, be, nb: (be[blk(i, nb)], 0, nj + jj(i, j, nb))),
                pl.BlockSpec((1, 1, tf), lambda i, j, be, nb: (be[blk(i, nb)], 0, jj(i, j, nb))),
                pl.BlockSpec((1, 1, tf), lambda i, j, be, nb: (be[blk(i, nb)], 0, nj + jj(i, j, nb))),
                pl.BlockSpec((1, tf, d), lambda i, j, be, nb: (be[blk(i, nb)], jj(i, j, nb), 0)),
                pl.BlockSpec((1, 1, d), lambda i, j, be, nb: (be[blk(i, nb)], 0, 0)),
            ],
            out_specs=pl.BlockSpec((ROW_BLOCK, d), lambda i, j, be, nb: (i, 0))),
        out_shape=jax.ShapeDtypeStruct((n_rows, d), F32),
        compiler_params=_cparams(("arbitrary", "arbitrary")),
        name="experts",
    )(block_expert, n_used, xs, w_gu, w_gu, b_gu3, b_gu3, w_down, b_d3)


def _combine_kernel(dest_ref, y_ref, h_ref, gate_ref, g_ref, b_ref, o_ref, buf_ref, sem, *, tt, nsteps):
    i = pl.program_id(0)
    cur = i % 2

    def gather_tile(tile, buf):
        base = tile * (tt * TOP_K)

        def issue(r, carry):
            for k in range(TOP_K):
                row = dest_ref[base + r * TOP_K + k]
                pltpu.make_async_copy(y_ref.at[pl.ds(row, 1)], buf_ref.at[buf, k, pl.ds(r, 1)],
                                      sem.at[buf]).start()
            return carry

        lax.fori_loop(0, tt, issue, 0, unroll=4)

    @pl.when(i == 0)
    def _():
        gather_tile(0, 0)

    @pl.when(i + 1 < nsteps)
    def _():
        gather_tile(i + 1, 1 - cur)

    for k in range(TOP_K):
        pltpu.make_async_copy(y_ref.at[pl.ds(0, tt)], buf_ref.at[cur, k], sem.at[cur]).wait()
    gates = gate_ref[...]
    ffn = gates[:, 0:1] * buf_ref[cur, 0]
    for k in range(1, TOP_K):
        ffn = ffn + gates[:, k:k + 1] * buf_ref[cur, k]
    o_ref[...] = _ln(DEEPNORM_ALPHA * h_ref[...] + ffn, g_ref[...], b_ref[...])


def _combine(y, dest, h1, gates, g, b):
    t, d = h1.shape
    tt = _pick(t, 256)
    kern = functools.partial(_combine_kernel, tt=tt, nsteps=t // tt)
    return pl.pallas_call(
        kern,
        grid_spec=pltpu.PrefetchScalarGridSpec(
            num_scalar_prefetch=1,
            grid=(t // tt,),
            in_specs=[pl.BlockSpec(memory_space=pl.ANY),
                      pl.BlockSpec((tt, d), lambda i, dst: (i, 0)),
                      pl.BlockSpec((tt, LANES), lambda i, dst: (i, 0)),
                      pl.BlockSpec((1, d), lambda i, dst: (0, 0)),
                      pl.BlockSpec((1, d), lambda i, dst: (0, 0))],
            out_specs=pl.BlockSpec((tt, d), lambda i, dst: (i, 0)),
            scratch_shapes=[pltpu.VMEM((2, TOP_K, tt, d), F32), pltpu.SemaphoreType.DMA((2,))]),
        out_shape=jax.ShapeDtypeStruct((t, d), F32),
        compiler_params=_cparams(("arbitrary",)),
        name="combine",
    )(dest, y, h1, gates, g.reshape(1, d), b.reshape(1, d))


def kernel(x, ln_in_g, ln_in_b, w_in, lam_re, lam_im, log_step, ssm_b_re, ssm_b_im, ssm_c_re, ssm_c_im, ssm_d, w_glu, b_glu, conv_w, a_log, dt_bias, dn_norm_w, w_proj_ssm, w_proj_dn, w_out, ln1_g, ln1_b, w_router, b_router, w_gate_up, b_gate_up, w_down, b_down, ln2_g, ln2_b):
    bsz, seq, d = x.shape
    t = bsz * seq
    sw = d // 2
    heads = d // DN_HEAD_DIM
    dn = heads * DN_HEAD_DIM
    groups = sw // SSM_GROUP_DIM
    n_state = lam_re.shape[-1]
    n_exp = w_router.shape[-1]
    assert w_in.shape[0] == DEPTH == 1
    assert seq % (CHUNK * 4) == 0 and sw % LANES == 0 and groups % 2 == 0 and 2 * heads <= LANES

    h, hb = _ln_in(x.reshape(t, d), ln_in_g, ln_in_b)

    wi = w_in[0]
    c_ab = sw + 4 * dn
    w_main = jnp.concatenate([wi[:, :c_ab], wi[:, c_ab + 2 * heads:]], axis=1).astype(BF16)
    w_ab = jnp.pad(wi[:, c_ab:c_ab + 2 * heads], ((0, 0), (0, LANES - 2 * heads))).astype(BF16)
    proj = _matmul(hb, w_main, F32, "proj_in")
    ab = _matmul(hb, w_ab, F32, "proj_ab")

    ops = _s5_operators(lam_re[0], lam_im[0], log_step[0], ssm_b_re[0], ssm_b_im[0],
                        ssm_c_re[0], ssm_c_im[0], ssm_d[0])
    ff2 = w_gate_up.shape[-1]
    y_s, (w_dn_bf,) = _s5_scan(proj, ops, bsz, seq, sw, n_state,
                               cast_weights=(w_down[0].reshape(n_exp * (ff2 // 2), d),))
    w_dn_bf = w_dn_bf.reshape(n_exp, ff2 // 2, d)
    y_ssm = _ssm_post(y_s, w_glu[0].astype(BF16), b_glu[0].astype(F32))

    y_dn, (w_gu_bf,) = _deltanet(
        proj, ab, conv_w[0], a_log[0], dt_bias[0], dn_norm_w[0], bsz, seq, heads, off_q=sw // LANES,
        cast_weights=(w_gate_up[0].reshape(n_exp * d, ff2),))
    w_gu_bf = w_gu_bf.reshape(n_exp, d, ff2)

    merged = _merge(y_ssm, y_dn, proj, sw + 4 * dn, w_proj_ssm[0].astype(BF16), w_proj_dn[0].astype(BF16))
    h1, h1p = _out_ln(merged, w_out[0].astype(BF16), h, ln1_g[0], ln1_b[0])

    idx, gates, rank, cnt = _router(h1, w_router[0], b_router[0])
    counts = cnt[0, :n_exp].astype(jnp.int32)
    padded = (counts + ROW_BLOCK - 1) // ROW_BLOCK * ROW_BLOCK
    pad_ends = jnp.cumsum(padded)
    pad_starts = pad_ends - padded
    dest = (pad_starts[idx[:, :TOP_K]] + rank[:, :TOP_K]).astype(jnp.int32).reshape(t * TOP_K)
    n_rows = t * TOP_K + n_exp * ROW_BLOCK
    nblk = n_rows // ROW_BLOCK
    blk_start = jnp.arange(nblk, dtype=jnp.int32) * ROW_BLOCK
    block_expert = jnp.minimum(jnp.sum((pad_ends[None, :] <= blk_start[:, None]).astype(jnp.int32), axis=1),
                               n_exp - 1).astype(jnp.int32)
    n_used = (pad_ends[-1:] // ROW_BLOCK).astype(jnp.int32)
    xs = _dispatch(h1p, dest, pad_ends, n_used, n_rows)
    y = _experts(xs, block_expert, n_used, w_gu_bf, b_gate_up[0], w_dn_bf, b_down[0])
    out = _combine(y, dest, h1, gates, ln2_g[0], ln2_b[0])
    return out.reshape(bsz, seq, d)
```

```python
import functools
import itertools
import math

import jax
import jax.numpy as jnp
from jax import lax
from jax.experimental import pallas as pl
from jax.experimental.pallas import tpu as pltpu

F32 = jnp.float32
BF16 = jnp.bfloat16
HI = lax.Precision.HIGHEST

LANES = 128
CHUNK = 64
LN_EPS = 1e-5
NORM_EPS = 1e-6
SSM_GROUP_DIM = 16
SSM_TC = 16
DN_HEAD_DIM = 128
CONV_WIDTH = 4
TOP_K = 4
SWIGLU_LIMIT = 7.0
SWIGLU_ALPHA = 1.702
DEPTH = 1
DEEPNORM_ALPHA = (2 * DEPTH) ** 0.25
ROW_BLOCK = 512
VMEM_LIMIT = 56 * 1024 * 1024
VMEM_LIMIT_DN = 62 * 1024 * 1024


def _cparams(sem, vmem_limit=VMEM_LIMIT):
    return pltpu.CompilerParams(dimension_semantics=sem, vmem_limit_bytes=vmem_limit)


def _pick(n, pref):
    t = min(n, pref)
    while n % t:
        t //= 2
    return t


def _ln(x, g, b):
    mu = jnp.mean(x, axis=-1, keepdims=True)
    xc = x - mu
    var = jnp.mean(xc * xc, axis=-1, keepdims=True)
    return xc * lax.rsqrt(var + LN_EPS) * g + b


def _ln_in_kernel(x_ref, g_ref, b_ref, h_ref, hb_ref):
    h = _ln(x_ref[...], g_ref[...], b_ref[...])
    h_ref[...] = h
    hb_ref[...] = h.astype(BF16)


def _ln_in(x, g, b):
    t, d = x.shape
    tm = _pick(t, 512)
    return pl.pallas_call(
        _ln_in_kernel,
        grid=(t // tm,),
        in_specs=[pl.BlockSpec((tm, d), lambda i: (i, 0)),
                  pl.BlockSpec((1, d), lambda i: (0, 0)),
                  pl.BlockSpec((1, d), lambda i: (0, 0))],
        out_specs=[pl.BlockSpec((tm, d), lambda i: (i, 0)),
                   pl.BlockSpec((tm, d), lambda i: (i, 0))],
        out_shape=[jax.ShapeDtypeStruct((t, d), F32), jax.ShapeDtypeStruct((t, d), BF16)],
        compiler_params=_cparams(("arbitrary",)),
        name="ln_in",
    )(x, g.reshape(1, d), b.reshape(1, d))


def _mm_kernel(x_ref, w_ref, o_ref):
    o_ref[...] = jnp.dot(x_ref[...], w_ref[...].astype(BF16), preferred_element_type=F32).astype(o_ref.dtype)


def _matmul(x, w, out_dtype, name, ncols=None):
    m, k = x.shape
    n = w.shape[1] if ncols is None else ncols
    tm = _pick(m, 1024)
    tn = _pick(n, 1024)
    return pl.pallas_call(
        _mm_kernel,
        grid=(n // tn, m // tm),
        in_specs=[pl.BlockSpec((tm, k), lambda j, i: (i, 0)),
                  pl.BlockSpec((k, tn), lambda j, i: (0, j))],
        out_specs=pl.BlockSpec((tm, tn), lambda j, i: (i, j)),
        out_shape=jax.ShapeDtypeStruct((m, n), out_dtype),
        compiler_params=_cparams(("arbitrary", "arbitrary")),
        name=name,
    )(x, w)


CAST_SLICES = 16


def _cast_slices(src_refs, dst_refs):
    for sl in range(CAST_SLICES):
        for src, dst in zip(src_refs, dst_refs):
            n = src.shape[0] // CAST_SLICES
            dst[sl * n:(sl + 1) * n, :] = src[sl * n:(sl + 1) * n, :].astype(BF16)
        yield


def _cast_specs(weights, grid):
    g0, g1, g2 = grid
    nsteps = g0 * g1 * g2
    specs = []
    for w in weights:
        rows, cols = w.shape
        assert rows % (nsteps * CAST_SLICES * 16) == 0, (rows, nsteps)
        specs.append(pl.BlockSpec((rows // nsteps, cols), lambda a, b, c: ((a * g1 + b) * g2 + c, 0)))
    return specs


def _s5_operators(lam_re, lam_im, log_step, b_re, b_im, c_re, c_im, d_skip):
    g, p = lam_re.shape
    hd, tc = SSM_GROUP_DIM, SSM_TC
    lr, li = lam_re.astype(F32), lam_im.astype(F32)
    step = jnp.exp(log_step.astype(F32))[:, None]
    tau = jnp.arange(tc + 1, dtype=F32)[:, None, None]
    mag = jnp.exp(lr * step * tau)
    pr, pi = mag * jnp.cos(li * step * tau), mag * jnp.sin(li * step * tau)
    a_re, a_im = pr[1], pi[1]
    den = lr * lr + li * li
    nr, ni = a_re - 1.0, a_im
    f_re = (nr * lr + ni * li) / den
    f_im = (ni * lr - nr * li) / den
    br, bi = b_re.astype(F32), b_im.astype(F32)
    bb_re = f_re[..., None] * br - f_im[..., None] * bi
    bb_im = f_re[..., None] * bi + f_im[..., None] * br
    cr, ci = c_re.astype(F32), c_im.astype(F32)
    ca_re = cr[None] * pr[:, :, None, :] - ci[None] * pi[:, :, None, :]
    ca_im = cr[None] * pi[:, :, None, :] + ci[None] * pr[:, :, None, :]
    ca_g = jnp.concatenate([jnp.transpose(ca_re[:tc], (1, 0, 2, 3)).reshape(g, tc * hd, p),
                            -jnp.transpose(ca_im[:tc], (1, 0, 2, 3)).reshape(g, tc * hd, p)], axis=2)
    bb_g = jnp.concatenate([bb_re, bb_im], axis=1)
    kk = jnp.einsum('gxp,gpi->gxi', ca_g, bb_g, precision=HI)
    kk = jnp.transpose(kk.reshape(g, tc, hd, hd), (1, 0, 2, 3))
    kk = kk.at[0].add(d_skip.astype(F32)[:, :, None] * jnp.eye(hd, dtype=F32)[None])
    gs = LANES // hd
    ns = g // gs
    dup = LANES // p
    eye = jnp.eye(gs, dtype=F32)
    k5 = jnp.transpose(kk.reshape(tc, ns, gs, hd, hd), (1, 0, 2, 4, 3))
    bm = (k5[:, :, :, :, None, :] * eye[None, None, :, None, :, None]).reshape(ns, tc, LANES, LANES)
    dt = jnp.arange(tc)

    def rows(x_re, x_im):
        x = jnp.stack([x_re, x_im], axis=1).reshape(tc, 2, ns, gs * hd, p)
        x = jnp.transpose(x, (2, 0, 1, 3, 4))
        return jnp.concatenate([x] * dup, axis=-1)

    prr, pir = pr[tc - 1 - dt], pi[tc - 1 - dt]
    bc_re = prr[:, :, None, :] * jnp.transpose(bb_re, (0, 2, 1))[None] - pir[:, :, None, :] * jnp.transpose(bb_im, (0, 2, 1))[None]
    bc_im = prr[:, :, None, :] * jnp.transpose(bb_im, (0, 2, 1))[None] + pir[:, :, None, :] * jnp.transpose(bb_re, (0, 2, 1))[None]
    bcc = rows(bc_re, bc_im)
    cct = rows(ca_re[1:], -ca_im[1:])
    a16r = pr[tc].reshape(1, g * p)
    a16i = pi[tc].reshape(1, g * p)
    return (bm.astype(BF16), bcc.astype(BF16), cct.astype(BF16), a16r, a16i)


def _s5_kernel(x_ref, bm_ref, bcc_ref, cct_ref, ar_ref, ai_ref, *rest, ncb, sl, n_state, ncast):
    cast_src, y_ref, cast_dst = rest[:ncast], rest[ncast], rest[ncast + 1:2 * ncast + 1]
    mf_ref, bcf_ref, ccf_ref, st_ref, cr_ref, ci_ref = rest[2 * ncast + 1:]
    tc = SSM_TC
    for _ in _cast_slices(cast_src, cast_dst):
        pass

    @pl.when((pl.program_id(1) == 0) & (pl.program_id(2) == 0))
    def _():
        zero = jnp.zeros((LANES, LANES), BF16)
        for di in range(tc):
            for do in range(tc):
                mf_ref[di * LANES:(di + 1) * LANES, do * LANES:(do + 1) * LANES] = (
                    bm_ref[0, do - di] if do >= di else zero)
        r = lax.broadcasted_iota(jnp.int32, (LANES, sl), 0)
        c = lax.broadcasted_iota(jnp.int32, (LANES, sl), 1)
        same_group = (r // SSM_GROUP_DIM) == (c // n_state)

        def widen(blk):
            wide = jnp.concatenate([blk.astype(F32)] * (sl // LANES), axis=1)
            return jnp.where(same_group, wide, 0.0).astype(BF16)

        for dt in range(tc):
            for ri in range(2):
                bcf_ref[dt * LANES:(dt + 1) * LANES, ri * sl:(ri + 1) * sl] = widen(bcc_ref[0, dt, ri])
                ccf_ref[dt * LANES:(dt + 1) * LANES, ri * sl:(ri + 1) * sl] = widen(cct_ref[0, dt, ri])

    @pl.when(pl.program_id(2) == 0)
    def _():
        cr_ref[...] = jnp.zeros_like(cr_ref)
        ci_ref[...] = jnp.zeros_like(ci_ref)

    xs = jnp.concatenate([x_ref[pl.ds(dt, ncb, stride=SSM_TC), :] for dt in range(SSM_TC)],
                         axis=1).astype(BF16)
    st_ref[...] = jnp.dot(xs, bcf_ref[...], preferred_element_type=F32)
    ar = ar_ref[...]
    ai = ai_ref[...]

    def body(c, carry):
        sr, si = carry
        lr = st_ref[pl.ds(c, 1), :sl]
        li = st_ref[pl.ds(c, 1), sl:]
        st_ref[pl.ds(c, 1), :sl] = sr
        st_ref[pl.ds(c, 1), sl:] = si
        return ar * sr - ai * si + lr, ar * si + ai * sr + li

    sr, si = lax.fori_loop(0, ncb, body, (cr_ref[...], ci_ref[...]), unroll=8)
    cr_ref[...] = sr
    ci_ref[...] = si
    y = (jnp.dot(xs, mf_ref[...], preferred_element_type=F32)
         + _dot_nt(st_ref[...].astype(BF16), ccf_ref[...]))
    for dt in range(SSM_TC):
        y_ref[pl.ds(dt, ncb, stride=SSM_TC), :] = y[:, dt * LANES:(dt + 1) * LANES]


def _s5_scan(proj, ops, bsz, seq, sw, n_state, cast_weights=()):
    bm, bcc, cct, a16r, a16i = ops
    ns = sw // LANES
    sl = (LANES // SSM_GROUP_DIM) * n_state
    ncb = _pick(seq // SSM_TC, 512)
    rt = ncb * SSM_TC
    nt = seq // rt
    kw = SSM_TC * LANES
    cast_specs = _cast_specs(cast_weights, (ns, bsz, nt))
    kern = functools.partial(_s5_kernel, ncb=ncb, sl=sl, n_state=n_state, ncast=len(cast_weights))
    outs = pl.pallas_call(
        kern,
        grid=(ns, bsz, nt),
        in_specs=[pl.BlockSpec((rt, LANES), lambda s, b, i: (b * nt + i, s)),
                  pl.BlockSpec((1, SSM_TC, LANES, LANES), lambda s, b, i: (s, 0, 0, 0)),
                  pl.BlockSpec((1, SSM_TC, 2, LANES, LANES), lambda s, b, i: (s, 0, 0, 0, 0)),
                  pl.BlockSpec((1, SSM_TC, 2, LANES, LANES), lambda s, b, i: (s, 0, 0, 0, 0)),
                  pl.BlockSpec((1, sl), lambda s, b, i: (0, s)),
                  pl.BlockSpec((1, sl), lambda s, b, i: (0, s))] + cast_specs,
        out_specs=[pl.BlockSpec((rt, LANES), lambda s, b, i: (b * nt + i, s))] + cast_specs,
        out_shape=([jax.ShapeDtypeStruct((bsz * seq, sw), F32)]
                   + [jax.ShapeDtypeStruct(w.shape, BF16) for w in cast_weights]),
        scratch_shapes=[pltpu.VMEM((kw, kw), BF16), pltpu.VMEM((kw, 2 * sl), BF16), pltpu.VMEM((kw, 2 * sl), BF16),
                        pltpu.VMEM((ncb, 2 * sl), F32), pltpu.VMEM((1, sl), F32), pltpu.VMEM((1, sl), F32)],
        compiler_params=_cparams(("arbitrary", "arbitrary", "arbitrary")),
        name="s5_scan",
    )(proj, bm, bcc, cct, a16r, a16i, *cast_weights)
    return outs[0], outs[1:]


def _ssm_post_kernel(y_ref, w_ref, b_ref, o_ref):
    y = y_ref[...]
    yg = 0.5 * y * (1.0 + lax.erf(y * (1.0 / math.sqrt(2.0))))
    s = jnp.dot(yg.astype(BF16), w_ref[...], preferred_element_type=F32) + b_ref[...]
    o_ref[...] = (yg * jax.nn.sigmoid(s)).astype(o_ref.dtype)


def _ssm_post(y, w_glu, b_glu):
    t, n = y.shape
    tm = _pick(t, 512)
    return pl.pallas_call(
        _ssm_post_kernel,
        grid=(t // tm,),
        in_specs=[pl.BlockSpec((tm, n), lambda i: (i, 0)),
                  pl.BlockSpec((n, n), lambda i: (0, 0)),
                  pl.BlockSpec((1, n), lambda i: (0, 0))],
        out_specs=pl.BlockSpec((tm, n), lambda i: (i, 0)),
        out_shape=jax.ShapeDtypeStruct((t, n), BF16),
        compiler_params=_cparams(("arbitrary",)),
        name="ssm_post",
    )(y, w_glu, b_glu.reshape(1, n))


def _dot_nt(a, b):
    return lax.dot_general(a, b, (((1,), (1,)), ((), ())), preferred_element_type=F32)


def _dot_tn(a, b):
    return lax.dot_general(a, b, (((0,), (0,)), ((), ())), preferred_element_type=F32)


def _dot_hi(a, b):
    return jnp.dot(a, b, preferred_element_type=F32, precision=HI)


def _sigmoid(x):
    return 0.5 * jnp.tanh(0.5 * x) + 0.5


def _silu(x):
    hx = 0.5 * x
    return hx * jnp.tanh(hx) + hx


DN_HALO = 8
DN_HEADS_PER_STEP = 8


def _dn_kernel(q_ref, k_ref, v_ref, z_ref, ab_ref, cq_ref, ck_ref, cv_ref, al_ref, dtb_ref, nw_ref,
               *rest, tt, heads, hp, ncast):
    cast_src, o_ref, cast_dst, scratch = rest[:ncast], rest[ncast], rest[ncast + 1:2 * ncast + 1], rest[2 * ncast + 1:]
    s_refs, xx_refs = scratch[:hp], scratch[hp:]

    @pl.when(pl.program_id(2) == 0)
    def _():
        for s_ref, xx_ref in zip(s_refs, xx_refs):
            s_ref[...] = jnp.zeros_like(s_ref)
            xx_ref[:, 0:DN_HALO, :] = jnp.zeros((3, DN_HALO, DN_HEAD_DIM), F32)

    ab = ab_ref[...]
    xa = ab + dtb_ref[...]
    sp = jnp.maximum(xa, 0.0) + jnp.log(1.0 + jnp.exp(-jnp.abs(xa)))
    gc_all = -jnp.exp(al_ref[...]) * sp
    pos = lax.broadcasted_iota(jnp.int32, gc_all.shape, 0) & (CHUNK - 1)
    sh = 1
    while sh < CHUNK:
        gc_all = gc_all + jnp.where(pos >= sh, pltpu.roll(gc_all, sh, axis=0), 0.0)
        sh *= 2
    gate_vals = (gc_all, jnp.exp(gc_all), _sigmoid(ab))
    gens = [_dn_head(hh, pl.program_id(1) * hp + hh, q_ref, k_ref, v_ref, z_ref, gate_vals, cq_ref, ck_ref, cv_ref,
                     nw_ref, o_ref, s_refs[hh], xx_refs[hh], tt=tt, heads=heads)
            for hh in range(hp)]
    if ncast:
        gens.append(_cast_slices(cast_src, cast_dst))
    for _ in itertools.zip_longest(*gens):
        pass


def _dn_head(hh, h, q_ref, k_ref, v_ref, z_ref, gate_vals, cq_ref, ck_ref, cv_ref, nw_ref,
             o_ref, s_ref, xx_ref, *, tt, heads):
    hal = DN_HALO
    ls = slice(hh * DN_HEAD_DIM, (hh + 1) * DN_HEAD_DIM)

    def conv(idx, x_ref, cw_ref):
        x = x_ref[:, ls]
        xx_ref[idx, hal:, :] = x
        w = cw_ref[:, ls]
        acc = w[CONV_WIDTH - 1:CONV_WIDTH, :] * x
        for j in range(CONV_WIDTH - 1):
            off = hal - (CONV_WIDTH - 1) + j
            acc = acc + w[j:j + 1, :] * xx_ref[idx, off:off + tt, :]
        xx_ref[idx, 0:hal, :] = x[tt - hal:, :]
        return _silu(acc)

    qc = conv(0, q_ref, cq_ref)
    kc = conv(1, k_ref, ck_ref)
    vc = conv(2, v_ref, cv_ref)
    qn = qc * lax.rsqrt(jnp.sum(qc * qc, axis=-1, keepdims=True) + NORM_EPS) * (DN_HEAD_DIM ** -0.5)
    kn = kc * lax.rsqrt(jnp.sum(kc * kc, axis=-1, keepdims=True) + NORM_EPS)

    gc_all, egc_all, beta_all = gate_vals
    lane = lax.broadcasted_iota(jnp.int32, gc_all.shape, 1)

    def pick(vals, ln):
        col = jnp.sum(jnp.where(lane == ln, vals, 0.0), axis=-1, keepdims=True)
        return jnp.broadcast_to(col, (tt, DN_HEAD_DIM))

    beta_col = pick(beta_all, h + heads)
    gc = pick(gc_all, h)
    egc = pick(egc_all, h)
    nch = tt // CHUNK
    gc_row = jnp.transpose(gc)[0:1, :]

    ri = lax.broadcasted_iota(jnp.int32, (tt, tt), 0)
    ci = lax.broadcasted_iota(jnp.int32, (tt, tt), 1)
    same = (ri // CHUNK) == (ci // CHUNK)
    causal = same & (ri >= ci)
    strict = same & (ri > ci)
    gc_wide = jnp.concatenate([gc] * (tt // DN_HEAD_DIM), axis=1)
    decay = jnp.where(causal, jnp.exp(jnp.where(causal, gc_wide - gc_row, 0.0)), 0.0)
    kb = kn * beta_col
    knb = kn.astype(BF16)
    yield
    a_raw = _dot_nt(kb.astype(BF16), knb)
    qk_raw = _dot_nt(qn.astype(BF16), knb)
    yield
    a_bd = jnp.where(strict, a_raw * decay, 0.0)
    qk_bd = jnp.where(causal, qk_raw * decay, 0.0).astype(BF16)

    def fold(m):
        out = m[0:CHUNK]
        for c in range(1, nch):
            out = out + m[c * CHUNK:(c + 1) * CHUNK]
        return out

    def spread(m):
        return jnp.where(same, jnp.concatenate([m] * nch, axis=0), 0.0)

    r64 = lax.broadcasted_iota(jnp.int32, (CHUNK, tt), 0)
    c64 = lax.broadcasted_iota(jnp.int32, (CHUNK, tt), 1)
    eye_cat = (r64 == (c64 & (CHUNK - 1))).astype(F32)
    pw_cat = fold(a_bd)
    inv_cat = eye_cat - pw_cat
    pw_bd = a_bd.astype(BF16)
    for _ in range(5):
        pw_cat = jnp.dot(pw_cat.astype(BF16), pw_bd, preferred_element_type=F32)
        yield
        pw_bd = spread(pw_cat).astype(BF16)
        inv_add = jnp.dot(inv_cat.astype(BF16), pw_bd, preferred_element_type=F32)
        yield
        inv_cat = inv_cat + inv_add
    inv_bd = spread(inv_cat).astype(BF16)
    rhs = jnp.concatenate([vc * beta_col, kb * egc], axis=1).astype(BF16)
    sol = jnp.dot(inv_bd, rhs, preferred_element_type=F32)
    yield
    u_all, w_all = sol[:, :DN_HEAD_DIM], sol[:, DN_HEAD_DIM:]
    qe = qn * egc

    outs = []
    zblk = jnp.zeros((CHUNK, DN_HEAD_DIM), BF16)
    s = s_ref[...]
    for c in range(nch):
        sl = slice(c * CHUNK, (c + 1) * CHUNK)
        wq = jnp.concatenate([w_all[sl], qe[sl]], axis=0).astype(BF16)
        ws = jnp.dot(wq, s.astype(BF16), preferred_element_type=F32)
        yield
        v_new = u_all[sl] - ws[:CHUNK]
        vnb = v_new.astype(BF16)
        v_pad = jnp.concatenate([zblk] * c + [vnb] + [zblk] * (nch - 1 - c), axis=0)
        gc_c = gc[sl]
        g_last = gc_c[CHUNK - 1:CHUNK, :]
        k_dec = (kn[sl] * jnp.exp(g_last - gc_c)).astype(BF16)
        s_add = _dot_tn(k_dec, vnb)
        o_add = jnp.dot(qk_bd[sl], v_pad, preferred_element_type=F32)
        yield
        s = s * jnp.exp(g_last) + s_add
        outs.append(ws[CHUNK:] + o_add)
    s_ref[...] = s
    o = jnp.concatenate(outs, axis=0)
    o = o * lax.rsqrt(jnp.mean(o * o, axis=-1, keepdims=True) + NORM_EPS) * nw_ref[...]
    o_ref[:, ls] = (o * _silu(z_ref[:, ls])).astype(o_ref.dtype)


def _deltanet(proj, ab, conv_w, a_log, dt_bias, norm_w, bsz, seq, heads, off_q, cast_weights=()):
    t = bsz * seq
    tt = _pick(seq, 256)
    nt = seq // tt
    hd = DN_HEAD_DIM
    cw = conv_w.reshape(CONV_WIDTH, 3 * heads * hd).astype(F32)
    pad = LANES - heads
    al = jnp.pad(a_log.astype(F32), (0, pad)).reshape(1, LANES)
    dtb = jnp.pad(dt_bias.astype(F32), (0, pad)).reshape(1, LANES)
    nw = norm_w.astype(F32).reshape(1, hd)

    hp = math.gcd(math.gcd(heads, off_q), DN_HEADS_PER_STEP)
    hw = hp * hd
    oq, nh = off_q // hp, heads // hp

    def act(o):
        return pl.BlockSpec((tt, hw), lambda b, h, i: (b * nt + i, oq + o * nh + h))

    def cws(o):
        return pl.BlockSpec((CONV_WIDTH, hw), lambda b, h, i: (0, o * nh + h))

    row = pl.BlockSpec((1, LANES), lambda b, h, i: (0, 0))
    ncast = len(cast_weights)
    cast_specs = _cast_specs(cast_weights, (bsz, nh, nt))
    kern = functools.partial(_dn_kernel, tt=tt, heads=heads, hp=hp, ncast=ncast)
    outs = pl.pallas_call(
        kern,
        grid=(bsz, nh, nt),
        in_specs=[act(0), act(1), act(2), act(3),
                  pl.BlockSpec((tt, LANES), lambda b, h, i: (b * nt + i, 0)),
                  cws(0), cws(1), cws(2), row, row, row] + cast_specs,
        out_specs=[pl.BlockSpec((tt, hw), lambda b, h, i: (b * nt + i, h))] + cast_specs,
        out_shape=([jax.ShapeDtypeStruct((t, heads * hd), BF16)]
                   + [jax.ShapeDtypeStruct(w.shape, BF16) for w in cast_weights]),
        scratch_shapes=([pltpu.VMEM((hd, hd), F32)] * hp
                        + [pltpu.VMEM((3, tt + DN_HALO, hd), F32)] * hp),
        compiler_params=_cparams(("arbitrary", "arbitrary", "arbitrary"), VMEM_LIMIT_DN),
        name="deltanet",
    )(proj, proj, proj, proj, ab, cw, cw, cw, al, dtb, nw, *cast_weights)
    return outs[0], outs[1:]


def _merge_kernel(ys_ref, yd_ref, gs_ref, gd_ref, ws_ref, wd_ref, o_ref):
    ps = jnp.dot(ys_ref[...], ws_ref[...], preferred_element_type=F32)
    pd = jnp.dot(yd_ref[...], wd_ref[...], preferred_element_type=F32)
    o_ref[...] = (jax.nn.sigmoid(gs_ref[...]) * ps + jax.nn.sigmoid(gd_ref[...]) * pd).astype(o_ref.dtype)


def _merge(ys, yd, gates, gate_col, w_ps, w_pd):
    t, ns = ys.shape
    d = w_ps.shape[1]
    tm = _pick(t, 512)
    tn = _pick(ns, 1024)
    nb = d // tn
    off = gate_col // tn
    assert gate_col % tn == 0
    return pl.pallas_call(
        _merge_kernel,
        grid=(nb, t // tm),
        in_specs=[pl.BlockSpec((tm, ns), lambda j, i: (i, 0)),
                  pl.BlockSpec((tm, d), lambda j, i: (i, 0)),
                  pl.BlockSpec((tm, tn), lambda j, i: (i, off + j)),
                  pl.BlockSpec((tm, tn), lambda j, i: (i, off + nb + j)),
                  pl.BlockSpec((ns, tn), lambda j, i: (0, j)),
                  pl.BlockSpec((d, tn), lambda j, i: (0, j))],
        out_specs=pl.BlockSpec((tm, tn), lambda j, i: (i, j)),
        out_shape=jax.ShapeDtypeStruct((t, d), BF16),
        compiler_params=_cparams(("arbitrary", "arbitrary")),
        name="merge",
    )(ys, yd, gates, gates, w_ps, w_pd)


def _pack_bf16_pair(x):
    half = x.shape[1] // 2
    lo = pltpu.bitcast(x[:, :half].astype(BF16).astype(F32), jnp.uint32)
    hi = pltpu.bitcast(x[:, half:].astype(BF16).astype(F32), jnp.uint32)
    return (lo >> 16) | (hi & jnp.uint32(0xFFFF0000))


def _unpack_bf16_pair(u):
    lo = pltpu.bitcast(u << 16, F32).astype(BF16)
    hi = pltpu.bitcast(u & jnp.uint32(0xFFFF0000), F32).astype(BF16)
    return lo, hi


def _out_ln_kernel(m_ref, w_ref, h_ref, g_ref, b_ref, h1_ref, hp_ref):
    mix = jnp.dot(m_ref[...], w_ref[...], preferred_element_type=F32)
    h1 = _ln(DEEPNORM_ALPHA * h_ref[...] + mix, g_ref[...], b_ref[...])
    h1_ref[...] = h1
    hp_ref[...] = _pack_bf16_pair(h1)


def _out_ln(merged, w_out, h, g, b):
    t, d = h.shape
    tm = _pick(t, 512)
    return pl.pallas_call(
        _out_ln_kernel,
        grid=(t // tm,),
        in_specs=[pl.BlockSpec((tm, d), lambda i: (i, 0)),
                  pl.BlockSpec((d, d), lambda i: (0, 0)),
                  pl.BlockSpec((tm, d), lambda i: (i, 0)),
                  pl.BlockSpec((1, d), lambda i: (0, 0)),
                  pl.BlockSpec((1, d), lambda i: (0, 0))],
        out_specs=[pl.BlockSpec((tm, d), lambda i: (i, 0)),
                   pl.BlockSpec((tm, d // 2), lambda i: (i, 0))],
        out_shape=[jax.ShapeDtypeStruct((t, d), F32), jax.ShapeDtypeStruct((t, d // 2), jnp.uint32)],
        compiler_params=_cparams(("arbitrary",)),
        name="out_ln1",
    )(merged, w_out, h, g.reshape(1, d), b.reshape(1, d))


def _router_kernel(h_ref, w_ref, b_ref, idx_ref, gate_ref, rank_ref, cnt_ref, base_ref, *, n_exp, tm):
    i = pl.program_id(0)

    @pl.when(i == 0)
    def _():
        base_ref[...] = jnp.zeros_like(base_ref)

    h = h_ref[...]
    w = w_ref[...]
    h_hi = h.astype(BF16)
    h_lo = (h - h_hi.astype(F32)).astype(BF16)
    w_hi = w.astype(BF16)
    w_lo = (w - w_hi.astype(F32)).astype(BF16)
    logits = (jnp.dot(h_hi, w_hi, preferred_element_type=F32) + jnp.dot(h_hi, w_lo, preferred_element_type=F32)
              + jnp.dot(h_lo, w_hi, preferred_element_type=F32) + b_ref[...])
    lane = lax.broadcasted_iota(jnp.int32, logits.shape, 1)
    vals = jnp.where(lane < n_exp, logits, -jnp.inf)
    sels, tops, ams = [], [], []
    for _ in range(TOP_K):
        m = jnp.max(vals, axis=-1, keepdims=True)
        am = jnp.min(jnp.where(vals == m, lane, LANES), axis=-1, keepdims=True)
        sel = lane == am
        vals = jnp.where(sel, -jnp.inf, vals)
        sels.append(sel)
        tops.append(m)
        ams.append(am)
    es = [jnp.exp(v - tops[0]) for v in tops]
    den = es[0] + es[1] + es[2] + es[3]
    onehot = sels[0] | sels[1] | sels[2] | sels[3]
    mt = onehot.astype(BF16)
    ri = lax.broadcasted_iota(jnp.int32, (tm, tm), 0)
    ci = lax.broadcasted_iota(jnp.int32, (tm, tm), 1)
    before = (ri > ci).astype(BF16)
    prior = jnp.dot(before, mt, preferred_element_type=F32) + base_ref[...]
    idx_o = jnp.zeros(logits.shape, jnp.int32)
    gate_o = jnp.zeros(logits.shape, F32)
    rank_o = jnp.zeros(logits.shape, jnp.int32)
    for k in range(TOP_K):
        rk = jnp.sum(jnp.where(sels[k], prior, 0.0), axis=-1, keepdims=True).astype(jnp.int32)
        idx_o = jnp.where(lane == k, ams[k], idx_o)
        gate_o = jnp.where(lane == k, es[k] / den, gate_o)
        rank_o = jnp.where(lane == k, rk, rank_o)
    idx_ref[...] = idx_o
    gate_ref[...] = gate_o
    rank_ref[...] = rank_o
    base_ref[...] += jnp.sum(onehot.astype(F32), axis=0, keepdims=True)
    cnt_ref[...] = base_ref[...]


def _router(h1, w_router, b_router):
    t, d = h1.shape
    n_exp = w_router.shape[1]
    tm = _pick(t, 256)
    w = jnp.pad(w_router.astype(F32), ((0, 0), (0, LANES - n_exp)))
    b = jnp.pad(b_router.astype(F32), (0, LANES - n_exp)).reshape(1, LANES)
    tok = pl.BlockSpec((tm, LANES), lambda i: (i, 0))
    kern = functools.partial(_router_kernel, n_exp=n_exp, tm=tm)
    return pl.pallas_call(
        kern,
        grid=(t // tm,),
        in_specs=[pl.BlockSpec((tm, d), lambda i: (i, 0)),
                  pl.BlockSpec((d, LANES), lambda i: (0, 0)),
                  pl.BlockSpec((1, LANES), lambda i: (0, 0))],
        out_specs=[tok, tok, tok, pl.BlockSpec((1, LANES), lambda i: (0, 0))],
        out_shape=[jax.ShapeDtypeStruct((t, LANES), jnp.int32), jax.ShapeDtypeStruct((t, LANES), F32),
                   jax.ShapeDtypeStruct((t, LANES), jnp.int32), jax.ShapeDtypeStruct((1, LANES), F32)],
        scratch_shapes=[pltpu.VMEM((1, LANES), F32)],
        compiler_params=_cparams(("arbitrary",)),
        name="router",
    )(h1, w, b)


def _dispatch_kernel(dest_ref, pend_ref, nused_ref, x_ref, xs_ref, zero_ref, sem, zsem, *, tt, n_exp, nblk):
    i = pl.program_id(0)
    base = i * (tt * TOP_K)

    @pl.when(i == 0)
    def _():
        zero_ref[...] = jnp.zeros_like(zero_ref)

        def zero_block(start):
            start = pl.multiple_of(start, ROW_BLOCK)
            return pltpu.make_async_copy(zero_ref, xs_ref.at[pl.ds(start, ROW_BLOCK)], zsem)

        def expert_has_rows(e):
            prev = jnp.where(e == 0, 0, pend_ref[jnp.maximum(e - 1, 0)])
            return pend_ref[e] > prev

        def start_expert(e, carry):
            @pl.when(expert_has_rows(e))
            def _():
                zero_block(pend_ref[e] - ROW_BLOCK).start()
            return carry

        def wait_expert(e, carry):
            @pl.when(expert_has_rows(e))
            def _():
                zero_block(0).wait()
            return carry

        def start_tail(b, carry):
            zero_block(b * ROW_BLOCK).start()
            return carry

        def wait_tail(b, carry):
            zero_block(0).wait()
            return carry

        lax.fori_loop(0, n_exp, start_expert, 0)
        lax.fori_loop(nused_ref[0], nblk, start_tail, 0)
        lax.fori_loop(0, n_exp, wait_expert, 0)
        lax.fori_loop(nused_ref[0], nblk, wait_tail, 0)

    def issue(r, carry):
        for k in range(TOP_K):
            row = dest_ref[base + r * TOP_K + k]
            pltpu.make_async_copy(x_ref.at[pl.ds(r, 1)], xs_ref.at[pl.ds(row, 1)], sem).start()
        return carry

    lax.fori_loop(0, tt, issue, 0, unroll=4)

    for k in range(TOP_K):
        pltpu.make_async_copy(x_ref, xs_ref.at[pl.ds(0, tt)], sem).wait()


def _dispatch(xp, dest, pad_ends, n_used, n_rows):
    t, dh = xp.shape
    tt = _pick(t, 512)
    kern = functools.partial(_dispatch_kernel, tt=tt, n_exp=pad_ends.shape[0], nblk=n_rows // ROW_BLOCK)
    return pl.pallas_call(
        kern,
        grid_spec=pltpu.PrefetchScalarGridSpec(
            num_scalar_prefetch=3,
            grid=(t // tt,),
            in_specs=[pl.BlockSpec((tt, dh), lambda i, *_: (i, 0))],
            out_specs=pl.BlockSpec(memory_space=pl.ANY),
            scratch_shapes=[pltpu.VMEM((ROW_BLOCK, dh), jnp.uint32),
                            pltpu.SemaphoreType.DMA(()), pltpu.SemaphoreType.DMA(())]),
        out_shape=jax.ShapeDtypeStruct((n_rows, dh), jnp.uint32),
        compiler_params=_cparams(("arbitrary",)),
        name="dispatch",
    )(dest, pad_ends.astype(jnp.int32), n_used, xp)


def _expert_kernel(be_ref, nb_ref, x_ref, wg_ref, wu_ref, bg_ref, bu_ref, wd_ref, bd_ref, o_ref, *, half):
    i = pl.program_id(0)
    j = pl.program_id(1)

    @pl.when(i < nb_ref[0])
    def _():
        lo, hi = _unpack_bf16_pair(x_ref[...])
        gate = (jnp.dot(lo, wg_ref[0, :half, :], preferred_element_type=F32)
                + jnp.dot(hi, wg_ref[0, half:, :], preferred_element_type=F32) + bg_ref[0])
        up = (jnp.dot(lo, wu_ref[0, :half, :], preferred_element_type=F32)
              + jnp.dot(hi, wu_ref[0, half:, :], preferred_element_type=F32) + bu_ref[0])
        gate = jnp.minimum(gate, SWIGLU_LIMIT)
        up = jnp.clip(up, -SWIGLU_LIMIT, SWIGLU_LIMIT)
        act = gate * _sigmoid(SWIGLU_ALPHA * gate) * (up + 1.0)
        part = jnp.dot(act.astype(BF16), wd_ref[0], preferred_element_type=F32)

        @pl.when(j == 0)
        def _():
            o_ref[...] = part + bd_ref[0]

        @pl.when(j > 0)
        def _():
            o_ref[...] += part

    @pl.when((i >= nb_ref[0]) & (j == 0))
    def _():
        o_ref[...] = jnp.zeros_like(o_ref)


def _experts(xs, block_expert, n_used, w_gu, b_gu, w_down, b_down):
    n_rows, dh = xs.shape
    n_exp, d, ff2 = w_gu.shape
    ff = ff2 // 2
    tf = _pick(ff, 1024)
    nj = ff // tf
    nblk = n_rows // ROW_BLOCK
    b_gu3 = b_gu.astype(F32).reshape(n_exp, 1, ff2)
    b_d3 = b_down.astype(F32).reshape(n_exp, 1, d)

    def blk(i, nb):
        return jnp.maximum(jnp.minimum(i, nb[0] - 1), 0)

    def jj(i, j, nb):
        return jnp.where(i < nb[0], j, nj - 1)

    kern = functools.partial(_expert_kernel, half=d // 2)
    return pl.pallas_call(
        kern,
        grid_spec=pltpu.PrefetchScalarGridSpec(
            num_scalar_prefetch=2,
            grid=(nblk, nj),
            in_specs=[
                pl.BlockSpec((ROW_BLOCK, dh), lambda i, j, be, nb: (blk(i, nb), 0)),
                pl.BlockSpec((1, d, tf), lambda i, j, be, nb: (be[blk(i, nb)], 0, jj(i, j, nb))),
                pl.BlockSpec((1, d, tf), lambda i, j, be, nb: (be[blk(i, nb)], 0, nj + jj(i, j, nb))),
                pl.BlockSpec((1, 1, tf), lambda i, j, be, nb: (be[blk(i, nb)], 0, jj(i, j, nb))),
                pl.BlockSpec((1, 1, tf), lambda i, j, be, nb: (be[blk(i, nb)], 0, nj + jj(i, j, nb))),
                pl.BlockSpec((1, tf, d), lambda i, j, be, nb: (be[blk(i, nb)], jj(i, j, nb), 0)),
                pl.BlockSpec((1, 1, d), lambda i, j, be, nb: (be[blk(i, nb)], 0, 0)),
            ],
            out_specs=pl.BlockSpec((ROW_BLOCK, d), lambda i, j, be, nb: (i, 0))),
        out_shape=jax.ShapeDtypeStruct((n_rows, d), F32),
        compiler_params=_cparams(("arbitrary", "arbitrary")),
        name="experts",
    )(block_expert, n_used, xs, w_gu, w_gu, b_gu3, b_gu3, w_down, b_d3)


def _combine_kernel(dest_ref, y_ref, h_ref, gate_ref, g_ref, b_ref, o_ref, buf_ref, sem, *, tt, nsteps):
    i = pl.program_id(0)
    cur = i % 2

    def gather_tile(tile, buf):
        base = tile * (tt * TOP_K)

        def issue(r, carry):
            for k in range(TOP_K):
                row = dest_ref[base + r * TOP_K + k]
                pltpu.make_async_copy(y_ref.at[pl.ds(row, 1)], buf_ref.at[buf, k, pl.ds(r, 1)],
                                      sem.at[buf]).start()
            return carry

        lax.fori_loop(0, tt, issue, 0, unroll=4)

    @pl.when(i == 0)
    def _():
        gather_tile(0, 0)

    @pl.when(i + 1 < nsteps)
    def _():
        gather_tile(i + 1, 1 - cur)

    for k in range(TOP_K):
        pltpu.make_async_copy(y_ref.at[pl.ds(0, tt)], buf_ref.at[cur, k], sem.at[cur]).wait()
    gates = gate_ref[...]
    ffn = gates[:, 0:1] * buf_ref[cur, 0]
    for k in range(1, TOP_K):
        ffn = ffn + gates[:, k:k + 1] * buf_ref[cur, k]
    o_ref[...] = _ln(DEEPNORM_ALPHA * h_ref[...] + ffn, g_ref[...], b_ref[...])


def _combine(y, dest, h1, gates, g, b):
    t, d = h1.shape
    tt = _pick(t, 256)
    kern = functools.partial(_combine_kernel, tt=tt, nsteps=t // tt)
    return pl.pallas_call(
        kern,
        grid_spec=pltpu.PrefetchScalarGridSpec(
            num_scalar_prefetch=1,
            grid=(t // tt,),
            in_specs=[pl.BlockSpec(memory_space=pl.ANY),
                      pl.BlockSpec((tt, d), lambda i, dst: (i, 0)),
                      pl.BlockSpec((tt, LANES), lambda i, dst: (i, 0)),
                      pl.BlockSpec((1, d), lambda i, dst: (0, 0)),
                      pl.BlockSpec((1, d), lambda i, dst: (0, 0))],
            out_specs=pl.BlockSpec((tt, d), lambda i, dst: (i, 0)),
            scratch_shapes=[pltpu.VMEM((2, TOP_K, tt, d), F32), pltpu.SemaphoreType.DMA((2,))]),
        out_shape=jax.ShapeDtypeStruct((t, d), F32),
        compiler_params=_cparams(("arbitrary",)),
        name="combine",
    )(dest, y, h1, gates, g.reshape(1, d), b.reshape(1, d))


def kernel(x, ln_in_g, ln_in_b, w_in, lam_re, lam_im, log_step, ssm_b_re, ssm_b_im, ssm_c_re, ssm_c_im, ssm_d, w_glu, b_glu, conv_w, a_log, dt_bias, dn_norm_w, w_proj_ssm, w_proj_dn, w_out, ln1_g, ln1_b, w_router, b_router, w_gate_up, b_gate_up, w_down, b_down, ln2_g, ln2_b):
    bsz, seq, d = x.shape
    t = bsz * seq
    sw = d // 2
    heads = d // DN_HEAD_DIM
    dn = heads * DN_HEAD_DIM
    groups = sw // SSM_GROUP_DIM
    n_state = lam_re.shape[-1]
    n_exp = w_router.shape[-1]
    assert w_in.shape[0] == DEPTH == 1
    assert seq % (CHUNK * 4) == 0 and sw % LANES == 0 and groups % 2 == 0 and 2 * heads <= LANES

    h, hb = _ln_in(x.reshape(t, d), ln_in_g, ln_in_b)

    wi = w_in[0]
    c_ab = sw + 4 * dn
    w_main = jnp.concatenate([wi[:, :c_ab], wi[:, c_ab + 2 * heads:]], axis=1).astype(BF16)
    w_ab = jnp.pad(wi[:, c_ab:c_ab + 2 * heads], ((0, 0), (0, LANES - 2 * heads))).astype(BF16)
    proj = _matmul(hb, w_main, F32, "proj_in")
    ab = _matmul(hb, w_ab, F32, "proj_ab")

    ops = _s5_operators(lam_re[0], lam_im[0], log_step[0], ssm_b_re[0], ssm_b_im[0],
                        ssm_c_re[0], ssm_c_im[0], ssm_d[0])
    y_s, _ = _s5_scan(proj, ops, bsz, seq, sw, n_state)
    y_ssm = _ssm_post(y_s, w_glu[0].astype(BF16), b_glu[0].astype(F32))

    ff2 = w_gate_up.shape[-1]
    y_dn, (w_gu_bf, w_dn_bf) = _deltanet(
        proj, ab, conv_w[0], a_log[0], dt_bias[0], dn_norm_w[0], bsz, seq, heads, off_q=sw // LANES,
        cast_weights=(w_gate_up[0].reshape(n_exp * d, ff2), w_down[0].reshape(n_exp * (ff2 // 2), d)))
    w_gu_bf = w_gu_bf.reshape(n_exp, d, ff2)
    w_dn_bf = w_dn_bf.reshape(n_exp, ff2 // 2, d)

    merged = _merge(y_ssm, y_dn, proj, sw + 4 * dn, w_proj_ssm[0].astype(BF16), w_proj_dn[0].astype(BF16))
    h1, h1p = _out_ln(merged, w_out[0].astype(BF16), h, ln1_g[0], ln1_b[0])

    idx, gates, rank, cnt = _router(h1, w_router[0], b_router[0])
    counts = cnt[0, :n_exp].astype(jnp.int32)
    padded = (counts + ROW_BLOCK - 1) // ROW_BLOCK * ROW_BLOCK
    pad_ends = jnp.cumsum(padded)
    pad_starts = pad_ends - padded
    dest = (pad_starts[idx[:, :TOP_K]] + rank[:, :TOP_K]).astype(jnp.int32).reshape(t * TOP_K)
    n_rows = t * TOP_K + n_exp * ROW_BLOCK
    nblk = n_rows // ROW_BLOCK
    blk_start = jnp.arange(nblk, dtype=jnp.int32) * ROW_BLOCK
    block_expert = jnp.minimum(jnp.sum((pad_ends[None, :] <= blk_start[:, None]).astype(jnp.int32), axis=1),
                               n_exp - 1).astype(jnp.int32)
    n_used = (pad_ends[-1:] // ROW_BLOCK).astype(jnp.int32)
    xs = _dispatch(h1p, dest, pad_ends, n_used, n_rows)
    y = _experts(xs, block_expert, n_used, w_gu_bf, b_gate_up[0], w_dn_bf, b_down[0])
    out = _combine(y, dest, h1, gates, ln2_g[0], ln2_b[0])
    return out.reshape(bsz, seq, d)
```

```python
import functools
import itertools
import math

import jax
import jax.numpy as jnp
from jax import lax
from jax.experimental import pallas as pl
from jax.experimental.pallas import tpu as pltpu

F32 = jnp.float32
BF16 = jnp.bfloat16
HI = lax.Precision.HIGHEST

LANES = 128
CHUNK = 64
LN_EPS = 1e-5
NORM_EPS = 1e-6
SSM_GROUP_DIM = 16
SSM_TC = 16
DN_HEAD_DIM = 128
CONV_WIDTH = 4
TOP_K = 4
SWIGLU_LIMIT = 7.0
SWIGLU_ALPHA = 1.702
DEPTH = 1
DEEPNORM_ALPHA = (2 * DEPTH) ** 0.25
ROW_BLOCK = 512
VMEM_LIMIT = 56 * 1024 * 1024
VMEM_LIMIT_DN = 62 * 1024 * 1024


def _cparams(sem, vmem_limit=VMEM_LIMIT):
    return pltpu.CompilerParams(dimension_semantics=sem, vmem_limit_bytes=vmem_limit)


def _pick(n, pref):
    t = min(n, pref)
    while n % t:
        t //= 2
    return t


def _ln(x, g, b):
    mu = jnp.mean(x, axis=-1, keepdims=True)
    xc = x - mu
    var = jnp.mean(xc * xc, axis=-1, keepdims=True)
    return xc * lax.rsqrt(var + LN_EPS) * g + b


def _ln_in_kernel(x_ref, g_ref, b_ref, h_ref, hb_ref):
    h = _ln(x_ref[...], g_ref[...], b_ref[...])
    h_ref[...] = h
    hb_ref[...] = h.astype(BF16)


def _ln_in(x, g, b):
    t, d = x.shape
    tm = _pick(t, 512)
    return pl.pallas_call(
        _ln_in_kernel,
        grid=(t // tm,),
        in_specs=[pl.BlockSpec((tm, d), lambda i: (i, 0)),
                  pl.BlockSpec((1, d), lambda i: (0, 0)),
                  pl.BlockSpec((1, d), lambda i: (0, 0))],
        out_specs=[pl.BlockSpec((tm, d), lambda i: (i, 0)),
                   pl.BlockSpec((tm, d), lambda i: (i, 0))],
        out_shape=[jax.ShapeDtypeStruct((t, d), F32), jax.ShapeDtypeStruct((t, d), BF16)],
        compiler_params=_cparams(("arbitrary",)),
        name="ln_in",
    )(x, g.reshape(1, d), b.reshape(1, d))


def _mm_kernel(x_ref, w_ref, o_ref):
    o_ref[...] = jnp.dot(x_ref[...], w_ref[...].astype(BF16), preferred_element_type=F32).astype(o_ref.dtype)


def _matmul(x, w, out_dtype, name, ncols=None):
    m, k = x.shape
    n = w.shape[1] if ncols is None else ncols
    tm = _pick(m, 1024)
    tn = _pick(n, 1024)
    return pl.pallas_call(
        _mm_kernel,
        grid=(n // tn, m // tm),
        in_specs=[pl.BlockSpec((tm, k), lambda j, i: (i, 0)),
                  pl.BlockSpec((k, tn), lambda j, i: (0, j))],
        out_specs=pl.BlockSpec((tm, tn), lambda j, i: (i, j)),
        out_shape=jax.ShapeDtypeStruct((m, n), out_dtype),
        compiler_params=_cparams(("arbitrary", "arbitrary")),
        name=name,
    )(x, w)


CAST_SLICES = 16


def _cast_slices(src_refs, dst_refs):
    for sl in range(CAST_SLICES):
        for src, dst in zip(src_refs, dst_refs):
            n = src.shape[0] // CAST_SLICES
            dst[sl * n:(sl + 1) * n, :] = src[sl * n:(sl + 1) * n, :].astype(BF16)
        yield


def _cast_specs(weights, grid):
    g0, g1, g2 = grid
    nsteps = g0 * g1 * g2
    specs = []
    for w in weights:
        rows, cols = w.shape
        assert rows % (nsteps * CAST_SLICES * 16) == 0, (rows, nsteps)
        specs.append(pl.BlockSpec((rows // nsteps, cols), lambda a, b, c: ((a * g1 + b) * g2 + c, 0)))
    return specs


def _s5_operators(lam_re, lam_im, log_step, b_re, b_im, c_re, c_im, d_skip):
    g, p = lam_re.shape
    hd, tc = SSM_GROUP_DIM, SSM_TC
    lr, li = lam_re.astype(F32), lam_im.astype(F32)
    step = jnp.exp(log_step.astype(F32))[:, None]
    tau = jnp.arange(tc + 1, dtype=F32)[:, None, None]
    mag = jnp.exp(lr * step * tau)
    pr, pi = mag * jnp.cos(li * step * tau), mag * jnp.sin(li * step * tau)
    a_re, a_im = pr[1], pi[1]
    den = lr * lr + li * li
    nr, ni = a_re - 1.0, a_im
    f_re = (nr * lr + ni * li) / den
    f_im = (ni * lr - nr * li) / den
    br, bi = b_re.astype(F32), b_im.astype(F32)
    bb_re = f_re[..., None] * br - f_im[..., None] * bi
    bb_im = f_re[..., None] * bi + f_im[..., None] * br
    cr, ci = c_re.astype(F32), c_im.astype(F32)
    ca_re = cr[None] * pr[:, :, None, :] - ci[None] * pi[:, :, None, :]
    ca_im = cr[None] * pi[:, :, None, :] + ci[None] * pr[:, :, None, :]
    ca_g = jnp.concatenate([jnp.transpose(ca_re[:tc], (1, 0, 2, 3)).reshape(g, tc * hd, p),
                            -jnp.transpose(ca_im[:tc], (1, 0, 2, 3)).reshape(g, tc * hd, p)], axis=2)
    bb_g = jnp.concatenate([bb_re, bb_im], axis=1)
    kk = jnp.einsum('gxp,gpi->gxi', ca_g, bb_g, precision=HI)
    kk = jnp.transpose(kk.reshape(g, tc, hd, hd), (1, 0, 2, 3))
    kk = kk.at[0].add(d_skip.astype(F32)[:, :, None] * jnp.eye(hd, dtype=F32)[None])
    gs = LANES // hd
    ns = g // gs
    bm = jnp.transpose(kk.reshape(tc, ns, gs, hd, hd), (1, 0, 2, 4, 3)).reshape(ns, tc, LANES, hd)
    dt = jnp.arange(tc)

    def rows(x_re, x_im):
        x = jnp.stack([x_re, x_im], axis=1).reshape(tc, 2, ns, gs * hd, p)
        return jnp.transpose(x, (2, 0, 1, 3, 4))

    prr, pir = pr[tc - 1 - dt], pi[tc - 1 - dt]
    bc_re = prr[:, :, None, :] * jnp.transpose(bb_re, (0, 2, 1))[None] - pir[:, :, None, :] * jnp.transpose(bb_im, (0, 2, 1))[None]
    bc_im = prr[:, :, None, :] * jnp.transpose(bb_im, (0, 2, 1))[None] + pir[:, :, None, :] * jnp.transpose(bb_re, (0, 2, 1))[None]
    bcc = rows(bc_re, bc_im)
    cct = rows(ca_re[1:], -ca_im[1:])
    a16r = pr[tc].reshape(1, g * p)
    a16i = pi[tc].reshape(1, g * p)
    return (bm.astype(BF16), bcc.astype(BF16), cct.astype(BF16), a16r, a16i)


def _s5_kernel(x_ref, bm_ref, bcc_ref, cct_ref, ar_ref, ai_ref, *rest, ncb, sl, n_state, ncast):
    cast_src, y_ref, cast_dst = rest[:ncast], rest[ncast], rest[ncast + 1:2 * ncast + 1]
    mf_ref, bcf_ref, ccf_ref, st_ref, cr_ref, ci_ref = rest[2 * ncast + 1:]
    tc = SSM_TC
    for _ in _cast_slices(cast_src, cast_dst):
        pass

    @pl.when((pl.program_id(1) == 0) & (pl.program_id(2) == 0))
    def _():
        def block_diag(blk, width):
            n = blk.shape[1]
            rep = (lax.broadcasted_iota(jnp.int32, (n, width), 0)
                   == lax.broadcasted_iota(jnp.int32, (n, width), 1) % n).astype(BF16)
            wide = jnp.dot(blk, rep, preferred_element_type=F32)
            r = lax.broadcasted_iota(jnp.int32, (LANES, width), 0)
            c = lax.broadcasted_iota(jnp.int32, (LANES, width), 1)
            return jnp.where((r // SSM_GROUP_DIM) == (c // n), wide, 0.0).astype(BF16)

        zero = jnp.zeros((LANES, LANES), BF16)
        tiles = [block_diag(bm_ref[0, tau], LANES) for tau in range(tc)]
        for di in range(tc):
            for do in range(tc):
                mf_ref[di * LANES:(di + 1) * LANES, do * LANES:(do + 1) * LANES] = (
                    tiles[do - di] if do >= di else zero)

        def widen(blk):
            return block_diag(blk, sl)

        for dt in range(tc):
            for ri in range(2):
                bcf_ref[dt * LANES:(dt + 1) * LANES, ri * sl:(ri + 1) * sl] = widen(bcc_ref[0, dt, ri])
                ccf_ref[dt * LANES:(dt + 1) * LANES, ri * sl:(ri + 1) * sl] = widen(cct_ref[0, dt, ri])

    @pl.when(pl.program_id(2) == 0)
    def _():
        cr_ref[...] = jnp.zeros_like(cr_ref)
        ci_ref[...] = jnp.zeros_like(ci_ref)

    xs = jnp.concatenate([x_ref[pl.ds(dt, ncb, stride=SSM_TC), :] for dt in range(SSM_TC)],
                         axis=1).astype(BF16)
    st_ref[...] = jnp.dot(xs, bcf_ref[...], preferred_element_type=F32)
    ar = ar_ref[...]
    ai = ai_ref[...]

    def body(c, carry):
        sr, si = carry
        lr = st_ref[pl.ds(c, 1), :sl]
        li = st_ref[pl.ds(c, 1), sl:]
        st_ref[pl.ds(c, 1), :sl] = sr
        st_ref[pl.ds(c, 1), sl:] = si
        return ar * sr - ai * si + lr, ar * si + ai * sr + li

    sr, si = lax.fori_loop(0, ncb, body, (cr_ref[...], ci_ref[...]), unroll=8)
    cr_ref[...] = sr
    ci_ref[...] = si
    y = (jnp.dot(xs, mf_ref[...], preferred_element_type=F32)
         + _dot_nt(st_ref[...].astype(BF16), ccf_ref[...]))
    for dt in range(SSM_TC):
        y_ref[pl.ds(dt, ncb, stride=SSM_TC), :] = y[:, dt * LANES:(dt + 1) * LANES]


def _s5_scan(proj, ops, bsz, seq, sw, n_state, cast_weights=()):
    bm, bcc, cct, a16r, a16i = ops
    ns = sw // LANES
    sl = (LANES // SSM_GROUP_DIM) * n_state
    ncb = _pick(seq // SSM_TC, 512)
    rt = ncb * SSM_TC
    nt = seq // rt
    kw = SSM_TC * LANES
    cast_specs = _cast_specs(cast_weights, (ns, bsz, nt))
    kern = functools.partial(_s5_kernel, ncb=ncb, sl=sl, n_state=n_state, ncast=len(cast_weights))
    outs = pl.pallas_call(
        kern,
        grid=(ns, bsz, nt),
        in_specs=[pl.BlockSpec((rt, LANES), lambda s, b, i: (b * nt + i, s)),
                  pl.BlockSpec((1, SSM_TC, LANES, SSM_GROUP_DIM), lambda s, b, i: (s, 0, 0, 0)),
                  pl.BlockSpec((1, SSM_TC, 2, LANES, n_state), lambda s, b, i: (s, 0, 0, 0, 0)),
                  pl.BlockSpec((1, SSM_TC, 2, LANES, n_state), lambda s, b, i: (s, 0, 0, 0, 0)),
                  pl.BlockSpec((1, sl), lambda s, b, i: (0, s)),
                  pl.BlockSpec((1, sl), lambda s, b, i: (0, s))] + cast_specs,
        out_specs=[pl.BlockSpec((rt, LANES), lambda s, b, i: (b * nt + i, s))] + cast_specs,
        out_shape=([jax.ShapeDtypeStruct((bsz * seq, sw), F32)]
                   + [jax.ShapeDtypeStruct(w.shape, BF16) for w in cast_weights]),
        scratch_shapes=[pltpu.VMEM((kw, kw), BF16), pltpu.VMEM((kw, 2 * sl), BF16), pltpu.VMEM((kw, 2 * sl), BF16),
                        pltpu.VMEM((ncb, 2 * sl), F32), pltpu.VMEM((1, sl), F32), pltpu.VMEM((1, sl), F32)],
        compiler_params=_cparams(("arbitrary", "arbitrary", "arbitrary")),
        name="s5_scan",
    )(proj, bm, bcc, cct, a16r, a16i, *cast_weights)
    return outs[0], outs[1:]


def _ssm_post_kernel(y_ref, w_ref, b_ref, o_ref):
    y = y_ref[...]
    yg = 0.5 * y * (1.0 + lax.erf(y * (1.0 / math.sqrt(2.0))))
    s = jnp.dot(yg.astype(BF16), w_ref[...], preferred_element_type=F32) + b_ref[...]
    o_ref[...] = (yg * jax.nn.sigmoid(s)).astype(o_ref.dtype)


def _ssm_post(y, w_glu, b_glu):
    t, n = y.shape
    tm = _pick(t, 512)
    return pl.pallas_call(
        _ssm_post_kernel,
        grid=(t // tm,),
        in_specs=[pl.BlockSpec((tm, n), lambda i: (i, 0)),
                  pl.BlockSpec((n, n), lambda i: (0, 0)),
                  pl.BlockSpec((1, n), lambda i: (0, 0))],
        out_specs=pl.BlockSpec((tm, n), lambda i: (i, 0)),
        out_shape=jax.ShapeDtypeStruct((t, n), BF16),
        compiler_params=_cparams(("arbitrary",)),
        name="ssm_post",
    )(y, w_glu, b_glu.reshape(1, n))


def _dot_nt(a, b):
    return lax.dot_general(a, b, (((1,), (1,)), ((), ())), preferred_element_type=F32)


def _dot_tn(a, b):
    return lax.dot_general(a, b, (((0,), (0,)), ((), ())), preferred_element_type=F32)


def _dot_hi(a, b):
    return jnp.dot(a, b, preferred_element_type=F32, precision=HI)


def _sigmoid(x):
    return 0.5 * jnp.tanh(0.5 * x) + 0.5


def _silu(x):
    hx = 0.5 * x
    return hx * jnp.tanh(hx) + hx


DN_HALO = 8
DN_HEADS_PER_STEP = 8


def _dn_kernel(q_ref, k_ref, v_ref, z_ref, ab_ref, cq_ref, ck_ref, cv_ref, al_ref, dtb_ref, nw_ref,
               *rest, tt, heads, hp, ncast):
    cast_src, o_ref, cast_dst, scratch = rest[:ncast], rest[ncast], rest[ncast + 1:2 * ncast + 1], rest[2 * ncast + 1:]
    s_refs, xx_refs = scratch[:hp], scratch[hp:]

    @pl.when(pl.program_id(2) == 0)
    def _():
        for s_ref, xx_ref in zip(s_refs, xx_refs):
            s_ref[...] = jnp.zeros_like(s_ref)
            xx_ref[:, 0:DN_HALO, :] = jnp.zeros((3, DN_HALO, DN_HEAD_DIM), F32)

    ab = ab_ref[...]
    xa = ab + dtb_ref[...]
    sp = jnp.maximum(xa, 0.0) + jnp.log(1.0 + jnp.exp(-jnp.abs(xa)))
    gc_all = -jnp.exp(al_ref[...]) * sp
    pos = lax.broadcasted_iota(jnp.int32, gc_all.shape, 0) & (CHUNK - 1)
    sh = 1
    while sh < CHUNK:
        gc_all = gc_all + jnp.where(pos >= sh, pltpu.roll(gc_all, sh, axis=0), 0.0)
        sh *= 2
    gate_vals = (gc_all, jnp.exp(gc_all), _sigmoid(ab))
    gens = [_dn_head(hh, pl.program_id(1) * hp + hh, q_ref, k_ref, v_ref, z_ref, gate_vals, cq_ref, ck_ref, cv_ref,
                     nw_ref, o_ref, s_refs[hh], xx_refs[hh], tt=tt, heads=heads)
            for hh in range(hp)]
    if ncast:
        gens.append(_cast_slices(cast_src, cast_dst))
    for _ in itertools.zip_longest(*gens):
        pass


def _dn_head(hh, h, q_ref, k_ref, v_ref, z_ref, gate_vals, cq_ref, ck_ref, cv_ref, nw_ref,
             o_ref, s_ref, xx_ref, *, tt, heads):
    hal = DN_HALO
    ls = slice(hh * DN_HEAD_DIM, (hh + 1) * DN_HEAD_DIM)

    def conv(idx, x_ref, cw_ref):
        x = x_ref[:, ls]
        xx_ref[idx, hal:, :] = x
        w = cw_ref[:, ls]
        acc = w[CONV_WIDTH - 1:CONV_WIDTH, :] * x
        for j in range(CONV_WIDTH - 1):
            off = hal - (CONV_WIDTH - 1) + j
            acc = acc + w[j:j + 1, :] * xx_ref[idx, off:off + tt, :]
        xx_ref[idx, 0:hal, :] = x[tt - hal:, :]
        return _silu(acc)

    qc = conv(0, q_ref, cq_ref)
    kc = conv(1, k_ref, ck_ref)
    vc = conv(2, v_ref, cv_ref)
    qn = qc * lax.rsqrt(jnp.sum(qc * qc, axis=-1, keepdims=True) + NORM_EPS) * (DN_HEAD_DIM ** -0.5)
    kn = kc * lax.rsqrt(jnp.sum(kc * kc, axis=-1, keepdims=True) + NORM_EPS)

    gc_all, egc_all, beta_all = gate_vals
    lane = lax.broadcasted_iota(jnp.int32, gc_all.shape, 1)

    def pick(vals, ln):
        col = jnp.sum(jnp.where(lane == ln, vals, 0.0), axis=-1, keepdims=True)
        return jnp.broadcast_to(col, (tt, DN_HEAD_DIM))

    beta_col = pick(beta_all, h + heads)
    gc = pick(gc_all, h)
    egc = pick(egc_all, h)
    nch = tt // CHUNK
    gc_row = jnp.transpose(gc)[0:1, :]

    ri = lax.broadcasted_iota(jnp.int32, (tt, tt), 0)
    ci = lax.broadcasted_iota(jnp.int32, (tt, tt), 1)
    same = (ri // CHUNK) == (ci // CHUNK)
    causal = same & (ri >= ci)
    strict = same & (ri > ci)
    gc_wide = jnp.concatenate([gc] * (tt // DN_HEAD_DIM), axis=1)
    decay = jnp.where(causal, jnp.exp(jnp.where(causal, gc_wide - gc_row, 0.0)), 0.0)
    kb = kn * beta_col
    knb = kn.astype(BF16)
    yield
    a_raw = _dot_nt(kb.astype(BF16), knb)
    qk_raw = _dot_nt(qn.astype(BF16), knb)
    yield
    a_bd = jnp.where(strict, a_raw * decay, 0.0)
    qk_bd = jnp.where(causal, qk_raw * decay, 0.0).astype(BF16)

    def fold(m):
        out = m[0:CHUNK]
        for c in range(1, nch):
            out = out + m[c * CHUNK:(c + 1) * CHUNK]
        return out

    def spread(m):
        return jnp.where(same, jnp.concatenate([m] * nch, axis=0), 0.0)

    r64 = lax.broadcasted_iota(jnp.int32, (CHUNK, tt), 0)
    c64 = lax.broadcasted_iota(jnp.int32, (CHUNK, tt), 1)
    eye_cat = (r64 == (c64 & (CHUNK - 1))).astype(F32)
    pw_cat = fold(a_bd)
    inv_cat = eye_cat - pw_cat
    pw_bd = a_bd.astype(BF16)
    for _ in range(5):
        pw_cat = jnp.dot(pw_cat.astype(BF16), pw_bd, preferred_element_type=F32)
        yield
        pw_bd = spread(pw_cat).astype(BF16)
        inv_add = jnp.dot(inv_cat.astype(BF16), pw_bd, preferred_element_type=F32)
        yield
        inv_cat = inv_cat + inv_add
    inv_bd = spread(inv_cat).astype(BF16)
    rhs = jnp.concatenate([vc * beta_col, kb * egc], axis=1).astype(BF16)
    sol = jnp.dot(inv_bd, rhs, preferred_element_type=F32)
    yield
    u_all, w_all = sol[:, :DN_HEAD_DIM], sol[:, DN_HEAD_DIM:]
    qe = qn * egc

    outs = []
    zblk = jnp.zeros((CHUNK, DN_HEAD_DIM), BF16)
    s = s_ref[...]
    for c in range(nch):
        sl = slice(c * CHUNK, (c + 1) * CHUNK)
        wq = jnp.concatenate([w_all[sl], qe[sl]], axis=0).astype(BF16)
        ws = jnp.dot(wq, s.astype(BF16), preferred_element_type=F32)
        yield
        v_new = u_all[sl] - ws[:CHUNK]
        vnb = v_new.astype(BF16)
        v_pad = jnp.concatenate([zblk] * c + [vnb] + [zblk] * (nch - 1 - c), axis=0)
        gc_c = gc[sl]
        g_last = gc_c[CHUNK - 1:CHUNK, :]
        k_dec = (kn[sl] * jnp.exp(g_last - gc_c)).astype(BF16)
        s_add = _dot_tn(k_dec, vnb)
        o_add = jnp.dot(qk_bd[sl], v_pad, preferred_element_type=F32)
        yield
        s = s * jnp.exp(g_last) + s_add
        outs.append(ws[CHUNK:] + o_add)
    s_ref[...] = s
    o = jnp.concatenate(outs, axis=0)
    o = o * lax.rsqrt(jnp.mean(o * o, axis=-1, keepdims=True) + NORM_EPS) * nw_ref[...]
    o_ref[:, ls] = (o * _silu(z_ref[:, ls])).astype(o_ref.dtype)


def _deltanet(proj, ab, conv_w, a_log, dt_bias, norm_w, bsz, seq, heads, off_q, cast_weights=()):
    t = bsz * seq
    tt = _pick(seq, 256)
    nt = seq // tt
    hd = DN_HEAD_DIM
    cw = conv_w.reshape(CONV_WIDTH, 3 * heads * hd).astype(F32)
    pad = LANES - heads
    al = jnp.pad(a_log.astype(F32), (0, pad)).reshape(1, LANES)
    dtb = jnp.pad(dt_bias.astype(F32), (0, pad)).reshape(1, LANES)
    nw = norm_w.astype(F32).reshape(1, hd)

    hp = math.gcd(math.gcd(heads, off_q), DN_HEADS_PER_STEP)
    hw = hp * hd
    oq, nh = off_q // hp, heads // hp

    def act(o):
        return pl.BlockSpec((tt, hw), lambda b, h, i: (b * nt + i, oq + o * nh + h))

    def cws(o):
        return pl.BlockSpec((CONV_WIDTH, hw), lambda b, h, i: (0, o * nh + h))

    row = pl.BlockSpec((1, LANES), lambda b, h, i: (0, 0))
    ncast = len(cast_weights)
    cast_specs = _cast_specs(cast_weights, (bsz, nh, nt))
    kern = functools.partial(_dn_kernel, tt=tt, heads=heads, hp=hp, ncast=ncast)
    outs = pl.pallas_call(
        kern,
        grid=(bsz, nh, nt),
        in_specs=[act(0), act(1), act(2), act(3),
                  pl.BlockSpec((tt, LANES), lambda b, h, i: (b * nt + i, 0)),
                  cws(0), cws(1), cws(2), row, row, row] + cast_specs,
        out_specs=[pl.BlockSpec((tt, hw), lambda b, h, i: (b * nt + i, h))] + cast_specs,
        out_shape=([jax.ShapeDtypeStruct((t, heads * hd), BF16)]
                   + [jax.ShapeDtypeStruct(w.shape, BF16) for w in cast_weights]),
        scratch_shapes=([pltpu.VMEM((hd, hd), F32)] * hp
                        + [pltpu.VMEM((3, tt + DN_HALO, hd), F32)] * hp),
        compiler_params=_cparams(("arbitrary", "arbitrary", "arbitrary"), VMEM_LIMIT_DN),
        name="deltanet",
    )(proj, proj, proj, proj, ab, cw, cw, cw, al, dtb, nw, *cast_weights)
    return outs[0], outs[1:]


def _merge_kernel(ys_ref, yd_ref, gs_ref, gd_ref, ws_ref, wd_ref, o_ref):
    ps = jnp.dot(ys_ref[...], ws_ref[...], preferred_element_type=F32)
    pd = jnp.dot(yd_ref[...], wd_ref[...], preferred_element_type=F32)
    o_ref[...] = (jax.nn.sigmoid(gs_ref[...]) * ps + jax.nn.sigmoid(gd_ref[...]) * pd).astype(o_ref.dtype)


def _merge(ys, yd, gates, gate_col, w_ps, w_pd):
    t, ns = ys.shape
    d = w_ps.shape[1]
    tm = _pick(t, 512)
    tn = _pick(ns, 1024)
    nb = d // tn
    off = gate_col // tn
    assert gate_col % tn == 0
    return pl.pallas_call(
        _merge_kernel,
        grid=(nb, t // tm),
        in_specs=[pl.BlockSpec((tm, ns), lambda j, i: (i, 0)),
                  pl.BlockSpec((tm, d), lambda j, i: (i, 0)),
                  pl.BlockSpec((tm, tn), lambda j, i: (i, off + j)),
                  pl.BlockSpec((tm, tn), lambda j, i: (i, off + nb + j)),
                  pl.BlockSpec((ns, tn), lambda j, i: (0, j)),
                  pl.BlockSpec((d, tn), lambda j, i: (0, j))],
        out_specs=pl.BlockSpec((tm, tn), lambda j, i: (i, j)),
        out_shape=jax.ShapeDtypeStruct((t, d), BF16),
        compiler_params=_cparams(("arbitrary", "arbitrary")),
        name="merge",
    )(ys, yd, gates, gates, w_ps, w_pd)


def _pack_bf16_pair(x):
    half = x.shape[1] // 2
    lo = pltpu.bitcast(x[:, :half].astype(BF16).astype(F32), jnp.uint32)
    hi = pltpu.bitcast(x[:, half:].astype(BF16).astype(F32), jnp.uint32)
    return (lo >> 16) | (hi & jnp.uint32(0xFFFF0000))


def _unpack_bf16_pair(u):
    lo = pltpu.bitcast(u << 16, F32).astype(BF16)
    hi = pltpu.bitcast(u & jnp.uint32(0xFFFF0000), F32).astype(BF16)
    return lo, hi


def _out_ln_kernel(m_ref, w_ref, h_ref, g_ref, b_ref, h1_ref, hp_ref):
    mix = jnp.dot(m_ref[...], w_ref[...], preferred_element_type=F32)
    h1 = _ln(DEEPNORM_ALPHA * h_ref[...] + mix, g_ref[...], b_ref[...])
    h1_ref[...] = h1
    hp_ref[...] = _pack_bf16_pair(h1)


def _out_ln(merged, w_out, h, g, b):
    t, d = h.shape
    tm = _pick(t, 512)
    return pl.pallas_call(
        _out_ln_kernel,
        grid=(t // tm,),
        in_specs=[pl.BlockSpec((tm, d), lambda i: (i, 0)),
                  pl.BlockSpec((d, d), lambda i: (0, 0)),
                  pl.BlockSpec((tm, d), lambda i: (i, 0)),
                  pl.BlockSpec((1, d), lambda i: (0, 0)),
                  pl.BlockSpec((1, d), lambda i: (0, 0))],
        out_specs=[pl.BlockSpec((tm, d), lambda i: (i, 0)),
                   pl.BlockSpec((tm, d // 2), lambda i: (i, 0))],
        out_shape=[jax.ShapeDtypeStruct((t, d), F32), jax.ShapeDtypeStruct((t, d // 2), jnp.uint32)],
        compiler_params=_cparams(("arbitrary",)),
        name="out_ln1",
    )(merged, w_out, h, g.reshape(1, d), b.reshape(1, d))


def _router_kernel(h_ref, w_ref, b_ref, ir_ref, gate_ref, cnt_ref, base_ref, *, n_exp, tm):
    i = pl.program_id(0)

    @pl.when(i == 0)
    def _():
        base_ref[...] = jnp.zeros_like(base_ref)

    h = h_ref[...]
    w = w_ref[...]
    h_hi = h.astype(BF16)
    h_lo = (h - h_hi.astype(F32)).astype(BF16)
    w_hi = w.astype(BF16)
    w_lo = (w - w_hi.astype(F32)).astype(BF16)
    logits = (jnp.dot(h_hi, w_hi, preferred_element_type=F32) + jnp.dot(h_hi, w_lo, preferred_element_type=F32)
              + jnp.dot(h_lo, w_hi, preferred_element_type=F32) + b_ref[...])
    lane = lax.broadcasted_iota(jnp.int32, logits.shape, 1)
    vals = jnp.where(lane < n_exp, logits, -jnp.inf)
    sels, tops, ams = [], [], []
    for _ in range(TOP_K):
        m = jnp.max(vals, axis=-1, keepdims=True)
        am = jnp.min(jnp.where(vals == m, lane, LANES), axis=-1, keepdims=True)
        sel = lane == am
        vals = jnp.where(sel, -jnp.inf, vals)
        sels.append(sel)
        tops.append(m)
        ams.append(am)
    es = [jnp.exp(v - tops[0]) for v in tops]
    den = es[0] + es[1] + es[2] + es[3]
    onehot = sels[0] | sels[1] | sels[2] | sels[3]
    mt = onehot.astype(BF16)
    ri = lax.broadcasted_iota(jnp.int32, (tm, tm), 0)
    ci = lax.broadcasted_iota(jnp.int32, (tm, tm), 1)
    before = (ri > ci).astype(BF16)
    prior = jnp.dot(before, mt, preferred_element_type=F32) + base_ref[...]
    ir = jnp.zeros(logits.shape, F32)
    gate_o = jnp.zeros(logits.shape, F32)
    for k in range(TOP_K):
        rk = jnp.sum(jnp.where(sels[k], prior, 0.0), axis=-1, keepdims=True)
        ir = jnp.where(lane == k, ams[k].astype(F32), ir)
        ir = jnp.where(lane == TOP_K + k, rk, ir)
        gate_o = jnp.where(lane == k, es[k] / den, gate_o)
    ir_ref[...] = jnp.transpose(ir)[0:2 * TOP_K, :].astype(jnp.int32)
    gate_ref[...] = gate_o
    base_ref[...] += jnp.sum(onehot.astype(F32), axis=0, keepdims=True)
    cnt_ref[...] = base_ref[...]


def _router(h1, w_router, b_router):
    t, d = h1.shape
    n_exp = w_router.shape[1]
    tm = _pick(t, 512)
    w = jnp.pad(w_router.astype(F32), ((0, 0), (0, LANES - n_exp)))
    b = jnp.pad(b_router.astype(F32), (0, LANES - n_exp)).reshape(1, LANES)
    kern = functools.partial(_router_kernel, n_exp=n_exp, tm=tm)
    return pl.pallas_call(
        kern,
        grid=(t // tm,),
        in_specs=[pl.BlockSpec((tm, d), lambda i: (i, 0)),
                  pl.BlockSpec((d, LANES), lambda i: (0, 0)),
                  pl.BlockSpec((1, LANES), lambda i: (0, 0))],
        out_specs=[pl.BlockSpec((2 * TOP_K, tm), lambda i: (0, i)),
                   pl.BlockSpec((tm, LANES), lambda i: (i, 0)),
                   pl.BlockSpec((1, LANES), lambda i: (0, 0))],
        out_shape=[jax.ShapeDtypeStruct((2 * TOP_K, t), jnp.int32), jax.ShapeDtypeStruct((t, LANES), F32),
                   jax.ShapeDtypeStruct((1, LANES), F32)],
        scratch_shapes=[pltpu.VMEM((1, LANES), F32)],
        compiler_params=_cparams(("arbitrary",)),
        name="router",
    )(h1, w, b)


def _dispatch_kernel(dest_ref, pend_ref, nused_ref, x_ref, xs_ref, zero_ref, sem, zsem, *, tt, n_tok, n_exp, nblk):
    i = pl.program_id(0)
    base = i * tt

    @pl.when(i == 0)
    def _():
        zero_ref[...] = jnp.zeros_like(zero_ref)

        def zero_block(start):
            start = pl.multiple_of(start, ROW_BLOCK)
            return pltpu.make_async_copy(zero_ref, xs_ref.at[pl.ds(start, ROW_BLOCK)], zsem)

        def expert_has_rows(e):
            prev = jnp.where(e == 0, 0, pend_ref[jnp.maximum(e - 1, 0)])
            return pend_ref[e] > prev

        def start_expert(e, carry):
            @pl.when(expert_has_rows(e))
            def _():
                zero_block(pend_ref[e] - ROW_BLOCK).start()
            return carry

        def wait_expert(e, carry):
            @pl.when(expert_has_rows(e))
            def _():
                zero_block(0).wait()
            return carry

        def start_tail(b, carry):
            zero_block(b * ROW_BLOCK).start()
            return carry

        def wait_tail(b, carry):
            zero_block(0).wait()
            return carry

        lax.fori_loop(0, n_exp, start_expert, 0)
        lax.fori_loop(nused_ref[0], nblk, start_tail, 0)
        lax.fori_loop(0, n_exp, wait_expert, 0)
        lax.fori_loop(nused_ref[0], nblk, wait_tail, 0)

    def issue(r, carry):
        for k in range(TOP_K):
            row = dest_ref[k * n_tok + base + r]
            pltpu.make_async_copy(x_ref.at[pl.ds(r, 1)], xs_ref.at[pl.ds(row, 1)], sem).start()
        return carry

    lax.fori_loop(0, tt, issue, 0, unroll=4)

    for k in range(TOP_K):
        pltpu.make_async_copy(x_ref, xs_ref.at[pl.ds(0, tt)], sem).wait()


def _dispatch(xp, dest, pad_ends, n_used, n_rows):
    t, dh = xp.shape
    tt = _pick(t, 512)
    kern = functools.partial(_dispatch_kernel, tt=tt, n_tok=t, n_exp=pad_ends.shape[0], nblk=n_rows // ROW_BLOCK)
    return pl.pallas_call(
        kern,
        grid_spec=pltpu.PrefetchScalarGridSpec(
            num_scalar_prefetch=3,
            grid=(t // tt,),
            in_specs=[pl.BlockSpec((tt, dh), lambda i, *_: (i, 0))],
            out_specs=pl.BlockSpec(memory_space=pl.ANY),
            scratch_shapes=[pltpu.VMEM((ROW_BLOCK, dh), jnp.uint32),
                            pltpu.SemaphoreType.DMA(()), pltpu.SemaphoreType.DMA(())]),
        out_shape=jax.ShapeDtypeStruct((n_rows, dh), jnp.uint32),
        compiler_params=_cparams(("arbitrary",)),
        name="dispatch",
    )(dest, pad_ends.astype(jnp.int32), n_used, xp)


def _expert_kernel(be_ref, nb_ref, x_ref, wg_ref, wu_ref, bg_ref, bu_ref, wd_ref, bd_ref, o_ref, *, half):
    i = pl.program_id(0)
    j = pl.program_id(1)

    @pl.when(i < nb_ref[0])
    def _():
        lo, hi = _unpack_bf16_pair(x_ref[...])
        gate = (jnp.dot(lo, wg_ref[0, :half, :], preferred_element_type=F32)
                + jnp.dot(hi, wg_ref[0, half:, :], preferred_element_type=F32) + bg_ref[0])
        up = (jnp.dot(lo, wu_ref[0, :half, :], preferred_element_type=F32)
              + jnp.dot(hi, wu_ref[0, half:, :], preferred_element_type=F32) + bu_ref[0])
        gate = jnp.minimum(gate, SWIGLU_LIMIT)
        up = jnp.clip(up, -SWIGLU_LIMIT, SWIGLU_LIMIT)
        act = gate * _sigmoid(SWIGLU_ALPHA * gate) * (up + 1.0)
        part = jnp.dot(act.astype(BF16), wd_ref[0], preferred_element_type=F32)

        @pl.when(j == 0)
        def _():
            o_ref[...] = part + bd_ref[0]

        @pl.when(j > 0)
        def _():
            o_ref[...] += part

    @pl.when((i >= nb_ref[0]) & (j == 0))
    def _():
        o_ref[...] = jnp.zeros_like(o_ref)


def _experts(xs, block_expert, n_used, w_gu, b_gu, w_down, b_down):
    n_rows, dh = xs.shape
    n_exp, d, ff2 = w_gu.shape
    ff = ff2 // 2
    tf = _pick(ff, 1024)
    nj = ff // tf
    nblk = n_rows // ROW_BLOCK
    b_gu3 = b_gu.astype(F32).reshape(n_exp, 1, ff2)
    b_d3 = b_down.astype(F32).reshape(n_exp, 1, d)

    def blk(i, nb):
        return jnp.maximum(jnp.minimum(i, nb[0] - 1), 0)

    def jj(i, j, nb):
        return jnp.where(i < nb[0], j, nj - 1)

    kern = functools.partial(_expert_kernel, half=d // 2)
    return pl.pallas_call(
        kern,
        grid_spec=pltpu.PrefetchScalarGridSpec(
            num_scalar_prefetch=2,
            grid=(nblk, nj),
            in_specs=[
                pl.BlockSpec((ROW_BLOCK, dh), lambda i, j, be, nb: (blk(i, nb), 0)),
                pl.BlockSpec((1, d, tf), lambda i, j, be, nb: (be[blk(i, nb)], 0, jj(i, j, nb))),
                pl.BlockSpec((1, d, tf), lambda i, j, be, nb: (be[blk(i, nb)], 0, nj + jj(i, j, nb))),
                pl.BlockSpec((1, 1, tf), lambda i, j, be, nb: (be[blk(i, nb)], 0, jj(i, j, nb))),
                pl.BlockSpec((1, 1, tf), lambda i, j, be, nb: (be[blk(i, nb)], 0, nj + jj(i, j, nb))),
                pl.BlockSpec((1, tf, d), lambda i, j, be, nb: (be[blk(i, nb)], jj(i, j, nb), 0)),
                pl.BlockSpec((1, 1, d), lambda i, j, be, nb: (be[blk(i, nb)], 0, 0)),
            ],
            out_specs=pl.BlockSpec((ROW_BLOCK, d), lambda i, j, be, nb: (i, 0))),
        out_shape=jax.ShapeDtypeStruct((n_rows, d), F32),
        compiler_params=_cparams(("arbitrary", "arbitrary")),
        name="experts",
    )(block_expert, n_used, xs, w_gu, w_gu, b_gu3, b_gu3, w_down, b_d3)


def _combine_kernel(dest_ref, y_ref, h_ref, gate_ref, g_ref, b_ref, o_ref, buf_ref, sem, *, tt, n_tok):
    nsteps = n_tok // tt
    i = pl.program_id(0)
    cur = i % 2

    def gather_tile(tile, buf):
        base = tile * tt

        def issue(r, carry):
            for k in range(TOP_K):
                row = dest_ref[k * n_tok + base + r]
                pltpu.make_async_copy(y_ref.at[pl.ds(row, 1)], buf_ref.at[buf, k, pl.ds(r, 1)],
                                      sem.at[buf]).start()
            return carry

        lax.fori_loop(0, tt, issue, 0, unroll=4)

    @pl.when(i == 0)
    def _():
        gather_tile(0, 0)

    @pl.when(i + 1 < nsteps)
    def _():
        gather_tile(i + 1, 1 - cur)

    for k in range(TOP_K):
        pltpu.make_async_copy(y_ref.at[pl.ds(0, tt)], buf_ref.at[cur, k], sem.at[cur]).wait()
    gates = gate_ref[...]
    ffn = gates[:, 0:1] * buf_ref[cur, 0]
    for k in range(1, TOP_K):
        ffn = ffn + gates[:, k:k + 1] * buf_ref[cur, k]
    o_ref[...] = _ln(DEEPNORM_ALPHA * h_ref[...] + ffn, g_ref[...], b_ref[...])


def _combine(y, dest, h1, gates, g, b):
    t, d = h1.shape
    tt = _pick(t, 256)
    kern = functools.partial(_combine_kernel, tt=tt, n_tok=t)
    return pl.pallas_call(
        kern,
        grid_spec=pltpu.PrefetchScalarGridSpec(
            num_scalar_prefetch=1,
            grid=(t // tt,),
            in_specs=[pl.BlockSpec(memory_space=pl.ANY),
                      pl.BlockSpec((tt, d), lambda i, dst: (i, 0)),
                      pl.BlockSpec((tt, LANES), lambda i, dst: (i, 0)),
                      pl.BlockSpec((1, d), lambda i, dst: (0, 0)),
                      pl.BlockSpec((1, d), lambda i, dst: (0, 0))],
            out_specs=pl.BlockSpec((tt, d), lambda i, dst: (i, 0)),
            scratch_shapes=[pltpu.VMEM((2, TOP_K, tt, d), F32), pltpu.SemaphoreType.DMA((2,))]),
        out_shape=jax.ShapeDtypeStruct((t, d), F32),
        compiler_params=_cparams(("arbitrary",)),
        name="combine",
    )(dest, y, h1, gates, g.reshape(1, d), b.reshape(1, d))


def kernel(x, ln_in_g, ln_in_b, w_in, lam_re, lam_im, log_step, ssm_b_re, ssm_b_im, ssm_c_re, ssm_c_im, ssm_d, w_glu, b_glu, conv_w, a_log, dt_bias, dn_norm_w, w_proj_ssm, w_proj_dn, w_out, ln1_g, ln1_b, w_router, b_router, w_gate_up, b_gate_up, w_down, b_down, ln2_g, ln2_b):
    bsz, seq, d = x.shape
    t = bsz * seq
    sw = d // 2
    heads = d // DN_HEAD_DIM
    dn = heads * DN_HEAD_DIM
    groups = sw // SSM_GROUP_DIM
    n_state = lam_re.shape[-1]
    n_exp = w_router.shape[-1]
    assert w_in.shape[0] == DEPTH == 1
    assert seq % (CHUNK * 4) == 0 and sw % LANES == 0 and groups % 2 == 0 and 2 * heads <= LANES

    h, hb = _ln_in(x.reshape(t, d), ln_in_g, ln_in_b)

    wi = w_in[0]
    c_ab = sw + 4 * dn
    w_main = jnp.concatenate([wi[:, :c_ab], wi[:, c_ab + 2 * heads:]], axis=1).astype(BF16)
    w_ab = jnp.pad(wi[:, c_ab:c_ab + 2 * heads], ((0, 0), (0, LANES - 2 * heads))).astype(BF16)
    proj = _matmul(hb, w_main, F32, "proj_in")
    ab = _matmul(hb, w_ab, F32, "proj_ab")

    ops = _s5_operators(lam_re[0], lam_im[0], log_step[0], ssm_b_re[0], ssm_b_im[0],
                        ssm_c_re[0], ssm_c_im[0], ssm_d[0])
    y_s, _ = _s5_scan(proj, ops, bsz, seq, sw, n_state)
    y_ssm = _ssm_post(y_s, w_glu[0].astype(BF16), b_glu[0].astype(F32))

    ff2 = w_gate_up.shape[-1]
    y_dn, (w_gu_bf, w_dn_bf) = _deltanet(
        proj, ab, conv_w[0], a_log[0], dt_bias[0], dn_norm_w[0], bsz, seq, heads, off_q=sw // LANES,
        cast_weights=(w_gate_up[0].reshape(n_exp * d, ff2), w_down[0].reshape(n_exp * (ff2 // 2), d)))
    w_gu_bf = w_gu_bf.reshape(n_exp, d, ff2)
    w_dn_bf = w_dn_bf.reshape(n_exp, ff2 // 2, d)

    merged = _merge(y_ssm, y_dn, proj, sw + 4 * dn, w_proj_ssm[0].astype(BF16), w_proj_dn[0].astype(BF16))
    h1, h1p = _out_ln(merged, w_out[0].astype(BF16), h, ln1_g[0], ln1_b[0])

    idx_rank, gates, cnt = _router(h1, w_router[0], b_router[0])
    counts = cnt[0, :n_exp].astype(jnp.int32)
    padded = (counts + ROW_BLOCK - 1) // ROW_BLOCK * ROW_BLOCK
    pad_ends = jnp.cumsum(padded)
    pad_starts = pad_ends - padded
    dest = (pad_starts[idx_rank[:TOP_K]] + idx_rank[TOP_K:]).astype(jnp.int32).reshape(TOP_K * t)
    n_rows = t * TOP_K + n_exp * ROW_BLOCK
    nblk = n_rows // ROW_BLOCK
    blk_start = jnp.arange(nblk, dtype=jnp.int32) * ROW_BLOCK
    block_expert = jnp.minimum(jnp.sum((pad_ends[None, :] <= blk_start[:, None]).astype(jnp.int32), axis=1),
                               n_exp - 1).astype(jnp.int32)
    n_used = (pad_ends[-1:] // ROW_BLOCK).astype(jnp.int32)
    xs = _dispatch(h1p, dest, pad_ends, n_used, n_rows)
    y = _experts(xs, block_expert, n_used, w_gu_bf, b_gate_up[0], w_dn_bf, b_down[0])
    out = _combine(y, dest, h1, gates, ln2_g[0], ln2_b[0])
    return out.reshape(bsz, seq, d)
```

```python
import functools
import itertools
import math

import jax
import jax.numpy as jnp
from jax import lax
from jax.experimental import pallas as pl
from jax.experimental.pallas import tpu as pltpu

F32 = jnp.float32
BF16 = jnp.bfloat16
HI = lax.Precision.HIGHEST

LANES = 128
CHUNK = 64
LN_EPS = 1e-5
NORM_EPS = 1e-6
SSM_GROUP_DIM = 16
SSM_TC = 16
DN_HEAD_DIM = 128
CONV_WIDTH = 4
TOP_K = 4
SWIGLU_LIMIT = 7.0
SWIGLU_ALPHA = 1.702
DEPTH = 1
DEEPNORM_ALPHA = (2 * DEPTH) ** 0.25
ROW_BLOCK = 512
VMEM_LIMIT = 56 * 1024 * 1024
VMEM_LIMIT_DN = 62 * 1024 * 1024


def _cparams(sem, vmem_limit=VMEM_LIMIT):
    return pltpu.CompilerParams(dimension_semantics=sem, vmem_limit_bytes=vmem_limit)


def _pick(n, pref):
    t = min(n, pref)
    while n % t:
        t //= 2
    return t


def _ln(x, g, b):
    mu = jnp.mean(x, axis=-1, keepdims=True)
    xc = x - mu
    var = jnp.mean(xc * xc, axis=-1, keepdims=True)
    return xc * lax.rsqrt(var + LN_EPS) * g + b


def _ln_in_kernel(x_ref, g_ref, b_ref, h_ref, hb_ref):
    h = _ln(x_ref[...], g_ref[...], b_ref[...])
    h_ref[...] = h
    hb_ref[...] = h.astype(BF16)


def _ln_in(x, g, b):
    t, d = x.shape
    tm = _pick(t, 512)
    return pl.pallas_call(
        _ln_in_kernel,
        grid=(t // tm,),
        in_specs=[pl.BlockSpec((tm, d), lambda i: (i, 0)),
                  pl.BlockSpec((1, d), lambda i: (0, 0)),
                  pl.BlockSpec((1, d), lambda i: (0, 0))],
        out_specs=[pl.BlockSpec((tm, d), lambda i: (i, 0)),
                   pl.BlockSpec((tm, d), lambda i: (i, 0))],
        out_shape=[jax.ShapeDtypeStruct((t, d), F32), jax.ShapeDtypeStruct((t, d), BF16)],
        compiler_params=_cparams(("arbitrary",)),
        name="ln_in",
    )(x, g.reshape(1, d), b.reshape(1, d))


def _mm_kernel(x_ref, w_ref, o_ref):
    o_ref[...] = jnp.dot(x_ref[...], w_ref[...].astype(BF16), preferred_element_type=F32).astype(o_ref.dtype)


def _matmul(x, w, out_dtype, name, ncols=None):
    m, k = x.shape
    n = w.shape[1] if ncols is None else ncols
    tm = _pick(m, 1024)
    tn = _pick(n, 1024)
    return pl.pallas_call(
        _mm_kernel,
        grid=(n // tn, m // tm),
        in_specs=[pl.BlockSpec((tm, k), lambda j, i: (i, 0)),
                  pl.BlockSpec((k, tn), lambda j, i: (0, j))],
        out_specs=pl.BlockSpec((tm, tn), lambda j, i: (i, j)),
        out_shape=jax.ShapeDtypeStruct((m, n), out_dtype),
        compiler_params=_cparams(("arbitrary", "arbitrary")),
        name=name,
    )(x, w)


def _proj2_kernel(x_ref, wa_ref, wb_ref, o_ref, *, na):
    j = pl.program_id(0)

    @pl.when(j < na)
    def _():
        o_ref[...] = jnp.dot(x_ref[...], wa_ref[...].astype(BF16), preferred_element_type=F32)

    @pl.when(j >= na)
    def _():
        o_ref[...] = jnp.dot(x_ref[...], wb_ref[...].astype(BF16), preferred_element_type=F32)


def _proj2(x, wa, ncols_a, wb, name):
    m, k = x.shape
    nb = wb.shape[1]
    tm = _pick(m, 1024)
    tn = _pick(math.gcd(ncols_a, nb), 1024)
    na, ng = ncols_a // tn, nb // tn
    return pl.pallas_call(
        functools.partial(_proj2_kernel, na=na),
        grid=(na + ng, m // tm),
        in_specs=[pl.BlockSpec((tm, k), lambda j, i: (i, 0)),
                  pl.BlockSpec((k, tn), lambda j, i: (0, jnp.minimum(j, na - 1))),
                  pl.BlockSpec((k, tn), lambda j, i: (0, jnp.maximum(j - na, 0)))],
        out_specs=pl.BlockSpec((tm, tn), lambda j, i: (i, j)),
        out_shape=jax.ShapeDtypeStruct((m, ncols_a + nb), F32),
        compiler_params=_cparams(("arbitrary", "arbitrary")),
        name=name,
    )(x, wa, wb)


CAST_SLICES = 16


def _cast_slices(src_refs, dst_refs):
    for sl in range(CAST_SLICES):
        for src, dst in zip(src_refs, dst_refs):
            n = src.shape[0] // CAST_SLICES
            dst[sl * n:(sl + 1) * n, :] = src[sl * n:(sl + 1) * n, :].astype(BF16)
        yield


def _cast_specs(weights, grid):
    g0, g1, g2 = grid
    nsteps = g0 * g1 * g2
    specs = []
    for w in weights:
        rows, cols = w.shape
        assert rows % (nsteps * CAST_SLICES * 16) == 0, (rows, nsteps)
        specs.append(pl.BlockSpec((rows // nsteps, cols), lambda a, b, c: ((a * g1 + b) * g2 + c, 0)))
    return specs


def _s5_operators(lam_re, lam_im, log_step, b_re, b_im, c_re, c_im, d_skip):
    g, p = lam_re.shape
    hd, tc = SSM_GROUP_DIM, SSM_TC
    lr, li = lam_re.astype(F32), lam_im.astype(F32)
    step = jnp.exp(log_step.astype(F32))[:, None]
    tau = jnp.arange(tc + 1, dtype=F32)[:, None, None]
    mag = jnp.exp(lr * step * tau)
    pr, pi = mag * jnp.cos(li * step * tau), mag * jnp.sin(li * step * tau)
    a_re, a_im = pr[1], pi[1]
    den = lr * lr + li * li
    nr, ni = a_re - 1.0, a_im
    f_re = (nr * lr + ni * li) / den
    f_im = (ni * lr - nr * li) / den
    br, bi = b_re.astype(F32), b_im.astype(F32)
    bb_re = f_re[..., None] * br - f_im[..., None] * bi
    bb_im = f_re[..., None] * bi + f_im[..., None] * br
    cr, ci = c_re.astype(F32), c_im.astype(F32)
    ca_re = cr[None] * pr[:, :, None, :] - ci[None] * pi[:, :, None, :]
    ca_im = cr[None] * pi[:, :, None, :] + ci[None] * pr[:, :, None, :]
    ca_g = jnp.concatenate([jnp.transpose(ca_re[:tc], (1, 0, 2, 3)).reshape(g, tc * hd, p),
                            -jnp.transpose(ca_im[:tc], (1, 0, 2, 3)).reshape(g, tc * hd, p)], axis=2)
    bb_g = jnp.concatenate([bb_re, bb_im], axis=1)
    kk = jnp.einsum('gxp,gpi->gxi', ca_g, bb_g, precision=HI)
    kk = jnp.transpose(kk.reshape(g, tc, hd, hd), (1, 0, 2, 3))
    kk = kk.at[0].add(d_skip.astype(F32)[:, :, None] * jnp.eye(hd, dtype=F32)[None])
    gs = LANES // hd
    ns = g // gs
    bm = jnp.transpose(kk.reshape(tc, ns, gs, hd, hd), (1, 0, 2, 4, 3)).reshape(ns, tc, LANES, hd)
    dt = jnp.arange(tc)

    def rows(x_re, x_im):
        x = jnp.stack([x_re, x_im], axis=1).reshape(tc, 2, ns, gs * hd, p)
        return jnp.transpose(x, (2, 0, 1, 3, 4))

    prr, pir = pr[tc - 1 - dt], pi[tc - 1 - dt]
    bc_re = prr[:, :, None, :] * jnp.transpose(bb_re, (0, 2, 1))[None] - pir[:, :, None, :] * jnp.transpose(bb_im, (0, 2, 1))[None]
    bc_im = prr[:, :, None, :] * jnp.transpose(bb_im, (0, 2, 1))[None] + pir[:, :, None, :] * jnp.transpose(bb_re, (0, 2, 1))[None]
    bcc = rows(bc_re, bc_im)
    cct = rows(ca_re[1:], -ca_im[1:])
    a16r = pr[tc].reshape(1, g * p)
    a16i = pi[tc].reshape(1, g * p)
    return (bm.astype(BF16), bcc.astype(BF16), cct.astype(BF16), a16r, a16i)


def _s5_kernel(x_ref, bm_ref, bcc_ref, cct_ref, ar_ref, ai_ref, *rest, ncb, sl, n_state, ncast):
    cast_src, y_ref, cast_dst = rest[:ncast], rest[ncast], rest[ncast + 1:2 * ncast + 1]
    mf_ref, bcf_ref, ccf_ref, st_ref, cr_ref, ci_ref = rest[2 * ncast + 1:]
    tc = SSM_TC
    for _ in _cast_slices(cast_src, cast_dst):
        pass

    @pl.when((pl.program_id(1) == 0) & (pl.program_id(2) == 0))
    def _():
        def block_diag(blk, width):
            n = blk.shape[1]
            rep = (lax.broadcasted_iota(jnp.int32, (n, width), 0)
                   == lax.broadcasted_iota(jnp.int32, (n, width), 1) % n).astype(BF16)
            wide = jnp.dot(blk, rep, preferred_element_type=F32)
            r = lax.broadcasted_iota(jnp.int32, (LANES, width), 0)
            c = lax.broadcasted_iota(jnp.int32, (LANES, width), 1)
            return jnp.where((r // SSM_GROUP_DIM) == (c // n), wide, 0.0).astype(BF16)

        zero = jnp.zeros((LANES, LANES), BF16)
        tiles = [block_diag(bm_ref[0, tau], LANES) for tau in range(tc)]
        for di in range(tc):
            for do in range(tc):
                mf_ref[di * LANES:(di + 1) * LANES, do * LANES:(do + 1) * LANES] = (
                    tiles[do - di] if do >= di else zero)

        def widen(blk):
            return block_diag(blk, sl)

        for dt in range(tc):
            for ri in range(2):
                bcf_ref[dt * LANES:(dt + 1) * LANES, ri * sl:(ri + 1) * sl] = widen(bcc_ref[0, dt, ri])
                ccf_ref[dt * LANES:(dt + 1) * LANES, ri * sl:(ri + 1) * sl] = widen(cct_ref[0, dt, ri])

    @pl.when(pl.program_id(2) == 0)
    def _():
        cr_ref[...] = jnp.zeros_like(cr_ref)
        ci_ref[...] = jnp.zeros_like(ci_ref)

    xs = jnp.concatenate([x_ref[pl.ds(dt, ncb, stride=SSM_TC), :] for dt in range(SSM_TC)],
                         axis=1).astype(BF16)
    st_ref[...] = jnp.dot(xs, bcf_ref[...], preferred_element_type=F32)
    ar = ar_ref[...]
    ai = ai_ref[...]

    def body(c, carry):
        sr, si = carry
        lr = st_ref[pl.ds(c, 1), :sl]
        li = st_ref[pl.ds(c, 1), sl:]
        st_ref[pl.ds(c, 1), :sl] = sr
        st_ref[pl.ds(c, 1), sl:] = si
        return ar * sr - ai * si + lr, ar * si + ai * sr + li

    sr, si = lax.fori_loop(0, ncb, body, (cr_ref[...], ci_ref[...]), unroll=8)
    cr_ref[...] = sr
    ci_ref[...] = si
    y = (jnp.dot(xs, mf_ref[...], preferred_element_type=F32)
         + _dot_nt(st_ref[...].astype(BF16), ccf_ref[...]))
    for dt in range(SSM_TC):
        y_ref[pl.ds(dt, ncb, stride=SSM_TC), :] = y[:, dt * LANES:(dt + 1) * LANES]


def _s5_scan(proj, ops, bsz, seq, sw, n_state, cast_weights=()):
    bm, bcc, cct, a16r, a16i = ops
    ns = sw // LANES
    sl = (LANES // SSM_GROUP_DIM) * n_state
    ncb = _pick(seq // SSM_TC, 512)
    rt = ncb * SSM_TC
    nt = seq // rt
    kw = SSM_TC * LANES
    cast_specs = _cast_specs(cast_weights, (ns, bsz, nt))
    kern = functools.partial(_s5_kernel, ncb=ncb, sl=sl, n_state=n_state, ncast=len(cast_weights))
    outs = pl.pallas_call(
        kern,
        grid=(ns, bsz, nt),
        in_specs=[pl.BlockSpec((rt, LANES), lambda s, b, i: (b * nt + i, s)),
                  pl.BlockSpec((1, SSM_TC, LANES, SSM_GROUP_DIM), lambda s, b, i: (s, 0, 0, 0)),
                  pl.BlockSpec((1, SSM_TC, 2, LANES, n_state), lambda s, b, i: (s, 0, 0, 0, 0)),
                  pl.BlockSpec((1, SSM_TC, 2, LANES, n_state), lambda s, b, i: (s, 0, 0, 0, 0)),
                  pl.BlockSpec((1, sl), lambda s, b, i: (0, s)),
                  pl.BlockSpec((1, sl), lambda s, b, i: (0, s))] + cast_specs,
        out_specs=[pl.BlockSpec((rt, LANES), lambda s, b, i: (b * nt + i, s))] + cast_specs,
        out_shape=([jax.ShapeDtypeStruct((bsz * seq, sw), F32)]
                   + [jax.ShapeDtypeStruct(w.shape, BF16) for w in cast_weights]),
        scratch_shapes=[pltpu.VMEM((kw, kw), BF16), pltpu.VMEM((kw, 2 * sl), BF16), pltpu.VMEM((kw, 2 * sl), BF16),
                        pltpu.VMEM((ncb, 2 * sl), F32), pltpu.VMEM((1, sl), F32), pltpu.VMEM((1, sl), F32)],
        compiler_params=_cparams(("arbitrary", "arbitrary", "arbitrary")),
        name="s5_scan",
    )(proj, bm, bcc, cct, a16r, a16i, *cast_weights)
    return outs[0], outs[1:]


def _ssm_post_kernel(y_ref, w_ref, b_ref, o_ref):
    y = y_ref[...]
    yg = 0.5 * y * (1.0 + lax.erf(y * (1.0 / math.sqrt(2.0))))
    s = jnp.dot(yg.astype(BF16), w_ref[...], preferred_element_type=F32) + b_ref[...]
    o_ref[...] = (yg * jax.nn.sigmoid(s)).astype(o_ref.dtype)


def _ssm_post(y, w_glu, b_glu):
    t, n = y.shape
    tm = _pick(t, 512)
    return pl.pallas_call(
        _ssm_post_kernel,
        grid=(t // tm,),
        in_specs=[pl.BlockSpec((tm, n), lambda i: (i, 0)),
                  pl.BlockSpec((n, n), lambda i: (0, 0)),
                  pl.BlockSpec((1, n), lambda i: (0, 0))],
        out_specs=pl.BlockSpec((tm, n), lambda i: (i, 0)),
        out_shape=jax.ShapeDtypeStruct((t, n), BF16),
        compiler_params=_cparams(("arbitrary",)),
        name="ssm_post",
    )(y, w_glu, b_glu.reshape(1, n))


def _dot_nt(a, b):
    return lax.dot_general(a, b, (((1,), (1,)), ((), ())), preferred_element_type=F32)


def _dot_tn(a, b):
    return lax.dot_general(a, b, (((0,), (0,)), ((), ())), preferred_element_type=F32)


def _dot_hi(a, b):
    return jnp.dot(a, b, preferred_element_type=F32, precision=HI)


def _sigmoid(x):
    return 0.5 * jnp.tanh(0.5 * x) + 0.5


def _silu(x):
    hx = 0.5 * x
    return hx * jnp.tanh(hx) + hx


DN_HALO = 8
DN_HEADS_PER_STEP = 8


def _dn_kernel(q_ref, k_ref, v_ref, z_ref, ab_ref, cq_ref, ck_ref, cv_ref, al_ref, dtb_ref, nw_ref,
               *rest, tt, heads, hp, ncast):
    cast_src, o_ref, cast_dst, scratch = rest[:ncast], rest[ncast], rest[ncast + 1:2 * ncast + 1], rest[2 * ncast + 1:]
    s_refs, xx_refs = scratch[:hp], scratch[hp:]

    @pl.when(pl.program_id(2) == 0)
    def _():
        for s_ref, xx_ref in zip(s_refs, xx_refs):
            s_ref[...] = jnp.zeros_like(s_ref)
            xx_ref[:, 0:DN_HALO, :] = jnp.zeros((3, DN_HALO, DN_HEAD_DIM), F32)

    ab = ab_ref[...]
    xa = ab + dtb_ref[...]
    sp = jnp.maximum(xa, 0.0) + jnp.log(1.0 + jnp.exp(-jnp.abs(xa)))
    gc_all = -jnp.exp(al_ref[...]) * sp
    pos = lax.broadcasted_iota(jnp.int32, gc_all.shape, 0) & (CHUNK - 1)
    sh = 1
    while sh < CHUNK:
        gc_all = gc_all + jnp.where(pos >= sh, pltpu.roll(gc_all, sh, axis=0), 0.0)
        sh *= 2
    gate_vals = (gc_all, jnp.exp(gc_all), _sigmoid(ab))
    gens = [_dn_head(hh, pl.program_id(1) * hp + hh, q_ref, k_ref, v_ref, z_ref, gate_vals, cq_ref, ck_ref, cv_ref,
                     nw_ref, o_ref, s_refs[hh], xx_refs[hh], tt=tt, heads=heads)
            for hh in range(hp)]
    if ncast:
        gens.append(_cast_slices(cast_src, cast_dst))
    for _ in itertools.zip_longest(*gens):
        pass


def _dn_head(hh, h, q_ref, k_ref, v_ref, z_ref, gate_vals, cq_ref, ck_ref, cv_ref, nw_ref,
             o_ref, s_ref, xx_ref, *, tt, heads):
    hal = DN_HALO
    ls = slice(hh * DN_HEAD_DIM, (hh + 1) * DN_HEAD_DIM)

    def conv(idx, x_ref, cw_ref):
        x = x_ref[:, ls]
        xx_ref[idx, hal:, :] = x
        w = cw_ref[:, ls]
        acc = w[CONV_WIDTH - 1:CONV_WIDTH, :] * x
        for j in range(CONV_WIDTH - 1):
            off = hal - (CONV_WIDTH - 1) + j
            acc = acc + w[j:j + 1, :] * xx_ref[idx, off:off + tt, :]
        xx_ref[idx, 0:hal, :] = x[tt - hal:, :]
        return _silu(acc)

    qc = conv(0, q_ref, cq_ref)
    kc = conv(1, k_ref, ck_ref)
    vc = conv(2, v_ref, cv_ref)
    qn = qc * lax.rsqrt(jnp.sum(qc * qc, axis=-1, keepdims=True) + NORM_EPS) * (DN_HEAD_DIM ** -0.5)
    kn = kc * lax.rsqrt(jnp.sum(kc * kc, axis=-1, keepdims=True) + NORM_EPS)

    gc_all, egc_all, beta_all = gate_vals
    lane = lax.broadcasted_iota(jnp.int32, gc_all.shape, 1)

    def pick(vals, ln):
        col = jnp.sum(jnp.where(lane == ln, vals, 0.0), axis=-1, keepdims=True)
        return jnp.broadcast_to(col, (tt, DN_HEAD_DIM))

    beta_col = pick(beta_all, h + heads)
    gc = pick(gc_all, h)
    egc = pick(egc_all, h)
    nch = tt // CHUNK
    gc_row = jnp.transpose(gc)[0:1, :]

    ri = lax.broadcasted_iota(jnp.int32, (tt, tt), 0)
    ci = lax.broadcasted_iota(jnp.int32, (tt, tt), 1)
    same = (ri // CHUNK) == (ci // CHUNK)
    causal = same & (ri >= ci)
    strict = same & (ri > ci)
    gc_wide = jnp.concatenate([gc] * (tt // DN_HEAD_DIM), axis=1)
    decay = jnp.where(causal, jnp.exp(jnp.where(causal, gc_wide - gc_row, 0.0)), 0.0)
    kb = kn * beta_col
    knb = kn.astype(BF16)
    yield
    a_raw = _dot_nt(kb.astype(BF16), knb)
    qk_raw = _dot_nt(qn.astype(BF16), knb)
    yield
    a_bd = jnp.where(strict, a_raw * decay, 0.0)
    qk_bd = jnp.where(causal, qk_raw * decay, 0.0).astype(BF16)

    def fold(m):
        out = m[0:CHUNK]
        for c in range(1, nch):
            out = out + m[c * CHUNK:(c + 1) * CHUNK]
        return out

    def spread(m):
        return jnp.where(same, jnp.concatenate([m] * nch, axis=0), 0.0)

    r64 = lax.broadcasted_iota(jnp.int32, (CHUNK, tt), 0)
    c64 = lax.broadcasted_iota(jnp.int32, (CHUNK, tt), 1)
    eye_cat = (r64 == (c64 & (CHUNK - 1))).astype(F32)
    pw_cat = fold(a_bd)
    inv_cat = eye_cat - pw_cat
    pw_bd = a_bd.astype(BF16)
    for _ in range(5):
        pw_cat = jnp.dot(pw_cat.astype(BF16), pw_bd, preferred_element_type=F32)
        yield
        pw_bd = spread(pw_cat).astype(BF16)
        inv_add = jnp.dot(inv_cat.astype(BF16), pw_bd, preferred_element_type=F32)
        yield
        inv_cat = inv_cat + inv_add
    inv_bd = spread(inv_cat).astype(BF16)
    rhs = jnp.concatenate([vc * beta_col, kb * egc], axis=1).astype(BF16)
    sol = jnp.dot(inv_bd, rhs, preferred_element_type=F32)
    yield
    u_all, w_all = sol[:, :DN_HEAD_DIM], sol[:, DN_HEAD_DIM:]
    qe = qn * egc

    outs = []
    zblk = jnp.zeros((CHUNK, DN_HEAD_DIM), BF16)
    s = s_ref[...]
    for c in range(nch):
        sl = slice(c * CHUNK, (c + 1) * CHUNK)
        wq = jnp.concatenate([w_all[sl], qe[sl]], axis=0).astype(BF16)
        ws = jnp.dot(wq, s.astype(BF16), preferred_element_type=F32)
        yield
        v_new = u_all[sl] - ws[:CHUNK]
        vnb = v_new.astype(BF16)
        v_pad = jnp.concatenate([zblk] * c + [vnb] + [zblk] * (nch - 1 - c), axis=0)
        gc_c = gc[sl]
        g_last = gc_c[CHUNK - 1:CHUNK, :]
        k_dec = (kn[sl] * jnp.exp(g_last - gc_c)).astype(BF16)
        s_add = _dot_tn(k_dec, vnb)
        o_add = jnp.dot(qk_bd[sl], v_pad, preferred_element_type=F32)
        yield
        s = s * jnp.exp(g_last) + s_add
        outs.append(ws[CHUNK:] + o_add)
    s_ref[...] = s
    o = jnp.concatenate(outs, axis=0)
    o = o * lax.rsqrt(jnp.mean(o * o, axis=-1, keepdims=True) + NORM_EPS) * nw_ref[...]
    o_ref[:, ls] = (o * _silu(z_ref[:, ls])).astype(o_ref.dtype)


def _deltanet(proj, ab, conv_w, a_log, dt_bias, norm_w, bsz, seq, heads, off_q, cast_weights=()):
    t = bsz * seq
    tt = _pick(seq, 256)
    nt = seq // tt
    hd = DN_HEAD_DIM
    cw = conv_w.reshape(CONV_WIDTH, 3 * heads * hd).astype(F32)
    pad = LANES - heads
    al = jnp.pad(a_log.astype(F32), (0, pad)).reshape(1, LANES)
    dtb = jnp.pad(dt_bias.astype(F32), (0, pad)).reshape(1, LANES)
    nw = norm_w.astype(F32).reshape(1, hd)

    hp = math.gcd(math.gcd(heads, off_q), DN_HEADS_PER_STEP)
    hw = hp * hd
    oq, nh = off_q // hp, heads // hp

    def act(o):
        return pl.BlockSpec((tt, hw), lambda b, h, i: (b * nt + i, oq + o * nh + h))

    def cws(o):
        return pl.BlockSpec((CONV_WIDTH, hw), lambda b, h, i: (0, o * nh + h))

    row = pl.BlockSpec((1, LANES), lambda b, h, i: (0, 0))
    ncast = len(cast_weights)
    cast_specs = _cast_specs(cast_weights, (bsz, nh, nt))
    kern = functools.partial(_dn_kernel, tt=tt, heads=heads, hp=hp, ncast=ncast)
    outs = pl.pallas_call(
        kern,
        grid=(bsz, nh, nt),
        in_specs=[act(0), act(1), act(2), act(3),
                  pl.BlockSpec((tt, LANES), lambda b, h, i: (b * nt + i, 0)),
                  cws(0), cws(1), cws(2), row, row, row] + cast_specs,
        out_specs=[pl.BlockSpec((tt, hw), lambda b, h, i: (b * nt + i, h))] + cast_specs,
        out_shape=([jax.ShapeDtypeStruct((t, heads * hd), BF16)]
                   + [jax.ShapeDtypeStruct(w.shape, BF16) for w in cast_weights]),
        scratch_shapes=([pltpu.VMEM((hd, hd), F32)] * hp
                        + [pltpu.VMEM((3, tt + DN_HALO, hd), F32)] * hp),
        compiler_params=_cparams(("arbitrary", "arbitrary", "arbitrary"), VMEM_LIMIT_DN),
        name="deltanet",
    )(proj, proj, proj, proj, ab, cw, cw, cw, al, dtb, nw, *cast_weights)
    return outs[0], outs[1:]


def _merge_kernel(ys_ref, yd_ref, gs_ref, gd_ref, ws_ref, wd_ref, o_ref):
    ps = jnp.dot(ys_ref[...], ws_ref[...], preferred_element_type=F32)
    pd = jnp.dot(yd_ref[...], wd_ref[...], preferred_element_type=F32)
    o_ref[...] = (jax.nn.sigmoid(gs_ref[...]) * ps + jax.nn.sigmoid(gd_ref[...]) * pd).astype(o_ref.dtype)


def _merge(ys, yd, gates, gate_col, w_ps, w_pd):
    t, ns = ys.shape
    d = w_ps.shape[1]
    tm = _pick(t, 512)
    tn = _pick(ns, 1024)
    nb = d // tn
    off = gate_col // tn
    assert gate_col % tn == 0
    return pl.pallas_call(
        _merge_kernel,
        grid=(nb, t // tm),
        in_specs=[pl.BlockSpec((tm, ns), lambda j, i: (i, 0)),
                  pl.BlockSpec((tm, d), lambda j, i: (i, 0)),
                  pl.BlockSpec((tm, tn), lambda j, i: (i, off + j)),
                  pl.BlockSpec((tm, tn), lambda j, i: (i, off + nb + j)),
                  pl.BlockSpec((ns, tn), lambda j, i: (0, j)),
                  pl.BlockSpec((d, tn), lambda j, i: (0, j))],
        out_specs=pl.BlockSpec((tm, tn), lambda j, i: (i, j)),
        out_shape=jax.ShapeDtypeStruct((t, d), BF16),
        compiler_params=_cparams(("arbitrary", "arbitrary")),
        name="merge",
    )(ys, yd, gates, gates, w_ps, w_pd)


def _pack_bf16_pair(x):
    half = x.shape[1] // 2
    lo = pltpu.bitcast(x[:, :half].astype(BF16).astype(F32), jnp.uint32)
    hi = pltpu.bitcast(x[:, half:].astype(BF16).astype(F32), jnp.uint32)
    return (lo >> 16) | (hi & jnp.uint32(0xFFFF0000))


def _unpack_bf16_pair(u):
    lo = pltpu.bitcast(u << 16, F32).astype(BF16)
    hi = pltpu.bitcast(u & jnp.uint32(0xFFFF0000), F32).astype(BF16)
    return lo, hi


def _out_ln_kernel(m_ref, w_ref, h_ref, g_ref, b_ref, h1_ref, hp_ref):
    mix = jnp.dot(m_ref[...], w_ref[...], preferred_element_type=F32)
    h1 = _ln(DEEPNORM_ALPHA * h_ref[...] + mix, g_ref[...], b_ref[...])
    h1_ref[...] = h1
    hp_ref[...] = _pack_bf16_pair(h1)


def _out_ln(merged, w_out, h, g, b):
    t, d = h.shape
    tm = _pick(t, 512)
    return pl.pallas_call(
        _out_ln_kernel,
        grid=(t // tm,),
        in_specs=[pl.BlockSpec((tm, d), lambda i: (i, 0)),
                  pl.BlockSpec((d, d), lambda i: (0, 0)),
                  pl.BlockSpec((tm, d), lambda i: (i, 0)),
                  pl.BlockSpec((1, d), lambda i: (0, 0)),
                  pl.BlockSpec((1, d), lambda i: (0, 0))],
        out_specs=[pl.BlockSpec((tm, d), lambda i: (i, 0)),
                   pl.BlockSpec((tm, d // 2), lambda i: (i, 0))],
        out_shape=[jax.ShapeDtypeStruct((t, d), F32), jax.ShapeDtypeStruct((t, d // 2), jnp.uint32)],
        compiler_params=_cparams(("arbitrary",)),
        name="out_ln1",
    )(merged, w_out, h, g.reshape(1, d), b.reshape(1, d))


def _router_kernel(h_ref, w_ref, b_ref, ir_ref, gate_ref, cnt_ref, base_ref, *, n_exp, tm):
    i = pl.program_id(0)

    @pl.when(i == 0)
    def _():
        base_ref[...] = jnp.zeros_like(base_ref)

    h = h_ref[...]
    w = w_ref[...]
    h_hi = h.astype(BF16)
    h_lo = (h - h_hi.astype(F32)).astype(BF16)
    w_hi = w.astype(BF16)
    w_lo = (w - w_hi.astype(F32)).astype(BF16)
    logits = (jnp.dot(h_hi, w_hi, preferred_element_type=F32) + jnp.dot(h_hi, w_lo, preferred_element_type=F32)
              + jnp.dot(h_lo, w_hi, preferred_element_type=F32) + b_ref[...])
    lane = lax.broadcasted_iota(jnp.int32, logits.shape, 1)
    vals = jnp.where(lane < n_exp, logits, -jnp.inf)
    sels, tops, ams = [], [], []
    for _ in range(TOP_K):
        m = jnp.max(vals, axis=-1, keepdims=True)
        am = jnp.min(jnp.where(vals == m, lane, LANES), axis=-1, keepdims=True)
        sel = lane == am
        vals = jnp.where(sel, -jnp.inf, vals)
        sels.append(sel)
        tops.append(m)
        ams.append(am)
    es = [jnp.exp(v - tops[0]) for v in tops]
    den = es[0] + es[1] + es[2] + es[3]
    onehot = sels[0] | sels[1] | sels[2] | sels[3]
    mt = onehot.astype(BF16)
    ri = lax.broadcasted_iota(jnp.int32, (tm, tm), 0)
    ci = lax.broadcasted_iota(jnp.int32, (tm, tm), 1)
    before = (ri > ci).astype(BF16)
    prior = jnp.dot(before, mt, preferred_element_type=F32) + base_ref[...]
    ir = jnp.zeros(logits.shape, F32)
    gate_o = jnp.zeros(logits.shape, F32)
    for k in range(TOP_K):
        rk = jnp.sum(jnp.where(sels[k], prior, 0.0), axis=-1, keepdims=True)
        ir = jnp.where(lane == k, ams[k].astype(F32), ir)
        ir = jnp.where(lane == TOP_K + k, rk, ir)
        gate_o = jnp.where(lane == k, es[k] / den, gate_o)
    ir_ref[...] = jnp.transpose(ir)[0:2 * TOP_K, :].astype(jnp.int32)
    gate_ref[...] = gate_o
    base_ref[...] += jnp.sum(onehot.astype(F32), axis=0, keepdims=True)
    cnt_ref[...] = base_ref[...]


def _router(h1, w_router, b_router):
    t, d = h1.shape
    n_exp = w_router.shape[1]
    tm = _pick(t, 512)
    w = jnp.pad(w_router.astype(F32), ((0, 0), (0, LANES - n_exp)))
    b = jnp.pad(b_router.astype(F32), (0, LANES - n_exp)).reshape(1, LANES)
    kern = functools.partial(_router_kernel, n_exp=n_exp, tm=tm)
    return pl.pallas_call(
        kern,
        grid=(t // tm,),
        in_specs=[pl.BlockSpec((tm, d), lambda i: (i, 0)),
                  pl.BlockSpec((d, LANES), lambda i: (0, 0)),
                  pl.BlockSpec((1, LANES), lambda i: (0, 0))],
        out_specs=[pl.BlockSpec((2 * TOP_K, tm), lambda i: (0, i)),
                   pl.BlockSpec((tm, LANES), lambda i: (i, 0)),
                   pl.BlockSpec((1, LANES), lambda i: (0, 0))],
        out_shape=[jax.ShapeDtypeStruct((2 * TOP_K, t), jnp.int32), jax.ShapeDtypeStruct((t, LANES), F32),
                   jax.ShapeDtypeStruct((1, LANES), F32)],
        scratch_shapes=[pltpu.VMEM((1, LANES), F32)],
        compiler_params=_cparams(("arbitrary",)),
        name="router",
    )(h1, w, b)


def _dispatch_kernel(dest_ref, pend_ref, nused_ref, x_ref, xs_ref, zero_ref, sem, zsem, *, tt, n_tok, n_exp, nblk):
    i = pl.program_id(0)
    base = i * tt

    @pl.when(i == 0)
    def _():
        zero_ref[...] = jnp.zeros_like(zero_ref)

        def zero_block(start):
            start = pl.multiple_of(start, ROW_BLOCK)
            return pltpu.make_async_copy(zero_ref, xs_ref.at[pl.ds(start, ROW_BLOCK)], zsem)

        def expert_has_rows(e):
            prev = jnp.where(e == 0, 0, pend_ref[jnp.maximum(e - 1, 0)])
            return pend_ref[e] > prev

        def start_expert(e, carry):
            @pl.when(expert_has_rows(e))
            def _():
                zero_block(pend_ref[e] - ROW_BLOCK).start()
            return carry

        def wait_expert(e, carry):
            @pl.when(expert_has_rows(e))
            def _():
                zero_block(0).wait()
            return carry

        def start_tail(b, carry):
            zero_block(b * ROW_BLOCK).start()
            return carry

        def wait_tail(b, carry):
            zero_block(0).wait()
            return carry

        lax.fori_loop(0, n_exp, start_expert, 0)
        lax.fori_loop(nused_ref[0], nblk, start_tail, 0)
        lax.fori_loop(0, n_exp, wait_expert, 0)
        lax.fori_loop(nused_ref[0], nblk, wait_tail, 0)

    def issue(r, carry):
        for k in range(TOP_K):
            row = dest_ref[k * n_tok + base + r]
            pltpu.make_async_copy(x_ref.at[pl.ds(r, 1)], xs_ref.at[pl.ds(row, 1)], sem).start()
        return carry

    lax.fori_loop(0, tt, issue, 0, unroll=4)

    for k in range(TOP_K):
        pltpu.make_async_copy(x_ref, xs_ref.at[pl.ds(0, tt)], sem).wait()


def _dispatch(xp, dest, pad_ends, n_used, n_rows):
    t, dh = xp.shape
    tt = _pick(t, 512)
    kern = functools.partial(_dispatch_kernel, tt=tt, n_tok=t, n_exp=pad_ends.shape[0], nblk=n_rows // ROW_BLOCK)
    return pl.pallas_call(
        kern,
        grid_spec=pltpu.PrefetchScalarGridSpec(
            num_scalar_prefetch=3,
            grid=(t // tt,),
            in_specs=[pl.BlockSpec((tt, dh), lambda i, *_: (i, 0))],
            out_specs=pl.BlockSpec(memory_space=pl.ANY),
            scratch_shapes=[pltpu.VMEM((ROW_BLOCK, dh), jnp.uint32),
                            pltpu.SemaphoreType.DMA(()), pltpu.SemaphoreType.DMA(())]),
        out_shape=jax.ShapeDtypeStruct((n_rows, dh), jnp.uint32),
        compiler_params=_cparams(("arbitrary",)),
        name="dispatch",
    )(dest, pad_ends.astype(jnp.int32), n_used, xp)


def _expert_kernel(be_ref, nb_ref, x_ref, wg_ref, wu_ref, bg_ref, bu_ref, wd_ref, bd_ref, o_ref, *, half):
    i = pl.program_id(0)
    j = pl.program_id(1)

    @pl.when(i < nb_ref[0])
    def _():
        lo, hi = _unpack_bf16_pair(x_ref[...])
        gate = (jnp.dot(lo, wg_ref[0, :half, :], preferred_element_type=F32)
                + jnp.dot(hi, wg_ref[0, half:, :], preferred_element_type=F32) + bg_ref[0])
        up = (jnp.dot(lo, wu_ref[0, :half, :], preferred_element_type=F32)
              + jnp.dot(hi, wu_ref[0, half:, :], preferred_element_type=F32) + bu_ref[0])
        gate = jnp.minimum(gate, SWIGLU_LIMIT)
        up = jnp.clip(up, -SWIGLU_LIMIT, SWIGLU_LIMIT)
        act = gate * _sigmoid(SWIGLU_ALPHA * gate) * (up + 1.0)
        part = jnp.dot(act.astype(BF16), wd_ref[0], preferred_element_type=F32)

        @pl.when(j == 0)
        def _():
            o_ref[...] = part + bd_ref[0]

        @pl.when(j > 0)
        def _():
            o_ref[...] += part

    @pl.when((i >= nb_ref[0]) & (j == 0))
    def _():
        o_ref[...] = jnp.zeros_like(o_ref)


def _experts(xs, block_expert, n_used, w_gu, b_gu, w_down, b_down):
    n_rows, dh = xs.shape
    n_exp, d, ff2 = w_gu.shape
    ff = ff2 // 2
    tf = _pick(ff, 1024)
    nj = ff // tf
    nblk = n_rows // ROW_BLOCK
    b_gu3 = b_gu.astype(F32).reshape(n_exp, 1, ff2)
    b_d3 = b_down.astype(F32).reshape(n_exp, 1, d)

    def blk(i, nb):
        return jnp.maximum(jnp.minimum(i, nb[0] - 1), 0)

    def jj(i, j, nb):
        return jnp.where(i < nb[0], j, nj - 1)

    kern = functools.partial(_expert_kernel, half=d // 2)
    return pl.pallas_call(
        kern,
        grid_spec=pltpu.PrefetchScalarGridSpec(
            num_scalar_prefetch=2,
            grid=(nblk, nj),
            in_specs=[
                pl.BlockSpec((ROW_BLOCK, dh), lambda i, j, be, nb: (blk(i, nb), 0)),
                pl.BlockSpec((1, d, tf), lambda i, j, be, nb: (be[blk(i, nb)], 0, jj(i, j, nb))),
                pl.BlockSpec((1, d, tf), lambda i, j, be, nb: (be[blk(i, nb)], 0, nj + jj(i, j, nb))),
                pl.BlockSpec((1, 1, tf), lambda i, j, be, nb: (be[blk(i, nb)], 0, jj(i, j, nb))),
                pl.BlockSpec((1, 1, tf), lambda i, j, be, nb: (be[blk(i, nb)], 0, nj + jj(i, j, nb))),
                pl.BlockSpec((1, tf, d), lambda i, j, be, nb: (be[blk(i, nb)], jj(i, j, nb), 0)),
                pl.BlockSpec((1, 1, d), lambda i, j, be, nb: (be[blk(i, nb)], 0, 0)),
            ],
            out_specs=pl.BlockSpec((ROW_BLOCK, d), lambda i, j, be, nb: (i, 0))),
        out_shape=jax.ShapeDtypeStruct((n_rows, d), F32),
        compiler_params=_cparams(("arbitrary", "arbitrary")),
        name="experts",
    )(block_expert, n_used, xs, w_gu, w_gu, b_gu3, b_gu3, w_down, b_d3)


def _combine_kernel(dest_ref, y_ref, h_ref, gate_ref, g_ref, b_ref, o_ref, buf_ref, sem, *, tt, n_tok):
    nsteps = n_tok // tt
    i = pl.program_id(0)
    cur = i % 2

    def gather_tile(tile, buf):
        base = tile * tt

        def issue(r, carry):
            for k in range(TOP_K):
                row = dest_ref[k * n_tok + base + r]
                pltpu.make_async_copy(y_ref.at[pl.ds(row, 1)], buf_ref.at[buf, k, pl.ds(r, 1)],
                                      sem.at[buf]).start()
            return carry

        lax.fori_loop(0, tt, issue, 0, unroll=4)

    @pl.when(i == 0)
    def _():
        gather_tile(0, 0)

    @pl.when(i + 1 < nsteps)
    def _():
        gather_tile(i + 1, 1 - cur)

    for k in range(TOP_K):
        pltpu.make_async_copy(y_ref.at[pl.ds(0, tt)], buf_ref.at[cur, k], sem.at[cur]).wait()
    gates = gate_ref[...]
    ffn = gates[:, 0:1] * buf_ref[cur, 0]
    for k in range(1, TOP_K):
        ffn = ffn + gates[:, k:k + 1] * buf_ref[cur, k]
    o_ref[...] = _ln(DEEPNORM_ALPHA * h_ref[...] + ffn, g_ref[...], b_ref[...])


def _combine(y, dest, h1, gates, g, b):
    t, d = h1.shape
    tt = _pick(t, 256)
    kern = functools.partial(_combine_kernel, tt=tt, n_tok=t)
    return pl.pallas_call(
        kern,
        grid_spec=pltpu.PrefetchScalarGridSpec(
            num_scalar_prefetch=1,
            grid=(t // tt,),
            in_specs=[pl.BlockSpec(memory_space=pl.ANY),
                      pl.BlockSpec((tt, d), lambda i, dst: (i, 0)),
                      pl.BlockSpec((tt, LANES), lambda i, dst: (i, 0)),
                      pl.BlockSpec((1, d), lambda i, dst: (0, 0)),
                      pl.BlockSpec((1, d), lambda i, dst: (0, 0))],
            out_specs=pl.BlockSpec((tt, d), lambda i, dst: (i, 0)),
            scratch_shapes=[pltpu.VMEM((2, TOP_K, tt, d), F32), pltpu.SemaphoreType.DMA((2,))]),
        out_shape=jax.ShapeDtypeStruct((t, d), F32),
        compiler_params=_cparams(("arbitrary",)),
        name="combine",
    )(dest, y, h1, gates, g.reshape(1, d), b.reshape(1, d))


def kernel(x, ln_in_g, ln_in_b, w_in, lam_re, lam_im, log_step, ssm_b_re, ssm_b_im, ssm_c_re, ssm_c_im, ssm_d, w_glu, b_glu, conv_w, a_log, dt_bias, dn_norm_w, w_proj_ssm, w_proj_dn, w_out, ln1_g, ln1_b, w_router, b_router, w_gate_up, b_gate_up, w_down, b_down, ln2_g, ln2_b):
    bsz, seq, d = x.shape
    t = bsz * seq
    sw = d // 2
    heads = d // DN_HEAD_DIM
    dn = heads * DN_HEAD_DIM
    groups = sw // SSM_GROUP_DIM
    n_state = lam_re.shape[-1]
    n_exp = w_router.shape[-1]
    assert w_in.shape[0] == DEPTH == 1
    assert seq % (CHUNK * 4) == 0 and sw % LANES == 0 and groups % 2 == 0 and 2 * heads <= LANES

    h, hb = _ln_in(x.reshape(t, d), ln_in_g, ln_in_b)

    wi = w_in[0]
    c_ab = sw + 4 * dn
    w_gates = wi[:, c_ab + 2 * heads:].astype(BF16)
    w_ab = jnp.pad(wi[:, c_ab:c_ab + 2 * heads], ((0, 0), (0, LANES - 2 * heads))).astype(BF16)
    proj = _proj2(hb, wi, c_ab, w_gates, "proj_in")
    ab = _matmul(hb, w_ab, F32, "proj_ab")

    ops = _s5_operators(lam_re[0], lam_im[0], log_step[0], ssm_b_re[0], ssm_b_im[0],
                        ssm_c_re[0], ssm_c_im[0], ssm_d[0])
    y_s, _ = _s5_scan(proj, ops, bsz, seq, sw, n_state)
    y_ssm = _ssm_post(y_s, w_glu[0].astype(BF16), b_glu[0].astype(F32))

    ff2 = w_gate_up.shape[-1]
    y_dn, (w_gu_bf, w_dn_bf) = _deltanet(
        proj, ab, conv_w[0], a_log[0], dt_bias[0], dn_norm_w[0], bsz, seq, heads, off_q=sw // LANES,
        cast_weights=(w_gate_up[0].reshape(n_exp * d, ff2), w_down[0].reshape(n_exp * (ff2 // 2), d)))
    w_gu_bf = w_gu_bf.reshape(n_exp, d, ff2)
    w_dn_bf = w_dn_bf.reshape(n_exp, ff2 // 2, d)

    merged = _merge(y_ssm, y_dn, proj, sw + 4 * dn, w_proj_ssm[0].astype(BF16), w_proj_dn[0].astype(BF16))
    h1, h1p = _out_ln(merged, w_out[0].astype(BF16), h, ln1_g[0], ln1_b[0])

    idx_rank, gates, cnt = _router(h1, w_router[0], b_router[0])
    counts = cnt[0, :n_exp].astype(jnp.int32)
    padded = (counts + ROW_BLOCK - 1) // ROW_BLOCK * ROW_BLOCK
    pad_ends = jnp.cumsum(padded)
    pad_starts = pad_ends - padded
    slot_expert = idx_rank[:TOP_K]
    is_e = slot_expert[None] == jnp.arange(n_exp, dtype=jnp.int32)[:, None, None]
    start = jnp.sum(jnp.where(is_e, pad_starts.astype(jnp.int32)[:, None, None], 0), axis=0)
    dest = (start + idx_rank[TOP_K:]).astype(jnp.int32).reshape(TOP_K * t)
    n_rows = t * TOP_K + n_exp * ROW_BLOCK
    nblk = n_rows // ROW_BLOCK
    blk_start = jnp.arange(nblk, dtype=jnp.int32) * ROW_BLOCK
    block_expert = jnp.minimum(jnp.sum((pad_ends[None, :] <= blk_start[:, None]).astype(jnp.int32), axis=1),
                               n_exp - 1).astype(jnp.int32)
    n_used = (pad_ends[-1:] // ROW_BLOCK).astype(jnp.int32)
    xs = _dispatch(h1p, dest, pad_ends, n_used, n_rows)
    y = _experts(xs, block_expert, n_used, w_gu_bf, b_gate_up[0], w_dn_bf, b_down[0])
    out = _combine(y, dest, h1, gates, ln2_g[0], ln2_b[0])
    return out.reshape(bsz, seq, d)
```

```python
import functools
import itertools
import math

import jax
import jax.numpy as jnp
from jax import lax
from jax.experimental import pallas as pl
from jax.experimental.pallas import tpu as pltpu

F32 = jnp.float32
BF16 = jnp.bfloat16
HI = lax.Precision.HIGHEST

LANES = 128
CHUNK = 64
LN_EPS = 1e-5
NORM_EPS = 1e-6
SSM_GROUP_DIM = 16
SSM_TC = 16
DN_HEAD_DIM = 128
CONV_WIDTH = 4
TOP_K = 4
SWIGLU_LIMIT = 7.0
SWIGLU_ALPHA = 1.702
DEPTH = 1
DEEPNORM_ALPHA = (2 * DEPTH) ** 0.25
ROW_BLOCK = 512
VMEM_LIMIT = 56 * 1024 * 1024
VMEM_LIMIT_DN = 62 * 1024 * 1024


def _cparams(sem, vmem_limit=VMEM_LIMIT):
    return pltpu.CompilerParams(dimension_semantics=sem, vmem_limit_bytes=vmem_limit)


def _pick(n, pref):
    t = min(n, pref)
    while n % t:
        t //= 2
    return t


def _ln(x, g, b):
    mu = jnp.mean(x, axis=-1, keepdims=True)
    xc = x - mu
    var = jnp.mean(xc * xc, axis=-1, keepdims=True)
    return xc * lax.rsqrt(var + LN_EPS) * g + b


def _ln_in_kernel(x_ref, g_ref, b_ref, h_ref, hb_ref):
    h = _ln(x_ref[...], g_ref[...], b_ref[...])
    h_ref[...] = h
    hb_ref[...] = h.astype(BF16)


def _ln_in(x, g, b):
    t, d = x.shape
    tm = _pick(t, 512)
    return pl.pallas_call(
        _ln_in_kernel,
        grid=(t // tm,),
        in_specs=[pl.BlockSpec((tm, d), lambda i: (i, 0)),
                  pl.BlockSpec((1, d), lambda i: (0, 0)),
                  pl.BlockSpec((1, d), lambda i: (0, 0))],
        out_specs=[pl.BlockSpec((tm, d), lambda i: (i, 0)),
                   pl.BlockSpec((tm, d), lambda i: (i, 0))],
        out_shape=[jax.ShapeDtypeStruct((t, d), F32), jax.ShapeDtypeStruct((t, d), BF16)],
        compiler_params=_cparams(("arbitrary",)),
        name="ln_in",
    )(x, g.reshape(1, d), b.reshape(1, d))


def _mm_kernel(x_ref, w_ref, o_ref):
    o_ref[...] = jnp.dot(x_ref[...], w_ref[...].astype(BF16), preferred_element_type=F32).astype(o_ref.dtype)


def _matmul(x, w, out_dtype, name, ncols=None):
    m, k = x.shape
    n = w.shape[1] if ncols is None else ncols
    tm = _pick(m, 1024)
    tn = _pick(n, 1024)
    return pl.pallas_call(
        _mm_kernel,
        grid=(n // tn, m // tm),
        in_specs=[pl.BlockSpec((tm, k), lambda j, i: (i, 0)),
                  pl.BlockSpec((k, tn), lambda j, i: (0, j))],
        out_specs=pl.BlockSpec((tm, tn), lambda j, i: (i, j)),
        out_shape=jax.ShapeDtypeStruct((m, n), out_dtype),
        compiler_params=_cparams(("arbitrary", "arbitrary")),
        name=name,
    )(x, w)


def _realign_kernel(a_ref, b_ref, o_ref, *, shift):
    lane = lax.broadcasted_iota(jnp.int32, a_ref.shape, 1)
    a = pltpu.roll(a_ref[...], LANES - shift, axis=1)
    b = pltpu.roll(b_ref[...], LANES - shift, axis=1)
    o_ref[...] = jnp.where(lane < LANES - shift, a, b).astype(o_ref.dtype)


def _realign_cols(w, c0, ncols):
    k = w.shape[0]
    blk0, shift = divmod(c0, LANES)
    assert shift and ncols % LANES == 0
    return pl.pallas_call(
        functools.partial(_realign_kernel, shift=shift),
        grid=(ncols // LANES,),
        in_specs=[pl.BlockSpec((k, LANES), lambda j: (0, blk0 + j)),
                  pl.BlockSpec((k, LANES), lambda j: (0, blk0 + j + 1))],
        out_specs=pl.BlockSpec((k, LANES), lambda j: (0, j)),
        out_shape=jax.ShapeDtypeStruct((k, ncols), BF16),
        compiler_params=_cparams(("arbitrary",)),
        name="realign_cols",
    )(w, w)


def _proj2_kernel(x_ref, wa_ref, wb_ref, o_ref, *, na):
    j = pl.program_id(0)

    @pl.when(j < na)
    def _():
        o_ref[...] = jnp.dot(x_ref[...], wa_ref[...].astype(BF16), preferred_element_type=F32)

    @pl.when(j >= na)
    def _():
        o_ref[...] = jnp.dot(x_ref[...], wb_ref[...].astype(BF16), preferred_element_type=F32)


def _proj2(x, wa, ncols_a, wb, name):
    m, k = x.shape
    nb = wb.shape[1]
    tm = _pick(m, 1024)
    tn = _pick(math.gcd(ncols_a, nb), 1024)
    na, ng = ncols_a // tn, nb // tn
    return pl.pallas_call(
        functools.partial(_proj2_kernel, na=na),
        grid=(na + ng, m // tm),
        in_specs=[pl.BlockSpec((tm, k), lambda j, i: (i, 0)),
                  pl.BlockSpec((k, tn), lambda j, i: (0, jnp.minimum(j, na - 1))),
                  pl.BlockSpec((k, tn), lambda j, i: (0, jnp.maximum(j - na, 0)))],
        out_specs=pl.BlockSpec((tm, tn), lambda j, i: (i, j)),
        out_shape=jax.ShapeDtypeStruct((m, ncols_a + nb), F32),
        compiler_params=_cparams(("arbitrary", "arbitrary")),
        name=name,
    )(x, wa, wb)


CAST_SLICES = 16


def _cast_slices(src_refs, dst_refs):
    for sl in range(CAST_SLICES):
        for src, dst in zip(src_refs, dst_refs):
            n = src.shape[0] // CAST_SLICES
            dst[sl * n:(sl + 1) * n, :] = src[sl * n:(sl + 1) * n, :].astype(BF16)
        yield


def _cast_specs(weights, grid):
    g0, g1, g2 = grid
    nsteps = g0 * g1 * g2
    specs = []
    for w in weights:
        rows, cols = w.shape
        assert rows % (nsteps * CAST_SLICES * 16) == 0, (rows, nsteps)
        specs.append(pl.BlockSpec((rows // nsteps, cols), lambda a, b, c: ((a * g1 + b) * g2 + c, 0)))
    return specs


def _s5_operators(lam_re, lam_im, log_step, b_re, b_im, c_re, c_im, d_skip):
    g, p = lam_re.shape
    hd, tc = SSM_GROUP_DIM, SSM_TC
    lr, li = lam_re.astype(F32), lam_im.astype(F32)
    step = jnp.exp(log_step.astype(F32))[:, None]
    tau = jnp.arange(tc + 1, dtype=F32)[:, None, None]
    mag = jnp.exp(lr * step * tau)
    pr, pi = mag * jnp.cos(li * step * tau), mag * jnp.sin(li * step * tau)
    a_re, a_im = pr[1], pi[1]
    den = lr * lr + li * li
    nr, ni = a_re - 1.0, a_im
    f_re = (nr * lr + ni * li) / den
    f_im = (ni * lr - nr * li) / den
    br, bi = b_re.astype(F32), b_im.astype(F32)
    bb_re = f_re[..., None] * br - f_im[..., None] * bi
    bb_im = f_re[..., None] * bi + f_im[..., None] * br
    cr, ci = c_re.astype(F32), c_im.astype(F32)
    ca_re = cr[None] * pr[:, :, None, :] - ci[None] * pi[:, :, None, :]
    ca_im = cr[None] * pi[:, :, None, :] + ci[None] * pr[:, :, None, :]
    ca_g = jnp.concatenate([jnp.transpose(ca_re[:tc], (1, 0, 2, 3)).reshape(g, tc * hd, p),
                            -jnp.transpose(ca_im[:tc], (1, 0, 2, 3)).reshape(g, tc * hd, p)], axis=2)
    bb_g = jnp.concatenate([bb_re, bb_im], axis=1)
    kk = jnp.einsum('gxp,gpi->gxi', ca_g, bb_g, precision=HI)
    kk = jnp.transpose(kk.reshape(g, tc, hd, hd), (1, 0, 2, 3))
    kk = kk.at[0].add(d_skip.astype(F32)[:, :, None] * jnp.eye(hd, dtype=F32)[None])
    gs = LANES // hd
    ns = g // gs
    bm = jnp.transpose(kk.reshape(tc, ns, gs, hd, hd), (1, 0, 2, 4, 3)).reshape(ns, tc, LANES, hd)
    dt = jnp.arange(tc)

    def rows(x_re, x_im):
        x = jnp.stack([x_re, x_im], axis=1).reshape(tc, 2, ns, gs * hd, p)
        return jnp.transpose(x, (2, 0, 1, 3, 4))

    prr, pir = pr[tc - 1 - dt], pi[tc - 1 - dt]
    bc_re = prr[:, :, None, :] * jnp.transpose(bb_re, (0, 2, 1))[None] - pir[:, :, None, :] * jnp.transpose(bb_im, (0, 2, 1))[None]
    bc_im = prr[:, :, None, :] * jnp.transpose(bb_im, (0, 2, 1))[None] + pir[:, :, None, :] * jnp.transpose(bb_re, (0, 2, 1))[None]
    bcc = rows(bc_re, bc_im)
    cct = rows(ca_re[1:], -ca_im[1:])
    a16r = pr[tc].reshape(1, g * p)
    a16i = pi[tc].reshape(1, g * p)
    return (bm.astype(BF16), bcc.astype(BF16), cct.astype(BF16), a16r, a16i)


def _s5_kernel(x_ref, bm_ref, bcc_ref, cct_ref, ar_ref, ai_ref, *rest, ncb, sl, n_state, ncast):
    cast_src, y_ref, cast_dst = rest[:ncast], rest[ncast], rest[ncast + 1:2 * ncast + 1]
    mf_ref, bcf_ref, ccf_ref, st_ref, cr_ref, ci_ref = rest[2 * ncast + 1:]
    tc = SSM_TC
    for _ in _cast_slices(cast_src, cast_dst):
        pass

    @pl.when((pl.program_id(1) == 0) & (pl.program_id(2) == 0))
    def _():
        def block_diag(blk, width):
            n = blk.shape[1]
            rep = (lax.broadcasted_iota(jnp.int32, (n, width), 0)
                   == lax.broadcasted_iota(jnp.int32, (n, width), 1) % n).astype(BF16)
            wide = jnp.dot(blk, rep, preferred_element_type=F32)
            r = lax.broadcasted_iota(jnp.int32, (LANES, width), 0)
            c = lax.broadcasted_iota(jnp.int32, (LANES, width), 1)
            return jnp.where((r // SSM_GROUP_DIM) == (c // n), wide, 0.0).astype(BF16)

        zero = jnp.zeros((LANES, LANES), BF16)
        tiles = [block_diag(bm_ref[0, tau], LANES) for tau in range(tc)]
        for di in range(tc):
            for do in range(tc):
                mf_ref[di * LANES:(di + 1) * LANES, do * LANES:(do + 1) * LANES] = (
                    tiles[do - di] if do >= di else zero)

        def widen(blk):
            return block_diag(blk, sl)

        for dt in range(tc):
            for ri in range(2):
                bcf_ref[dt * LANES:(dt + 1) * LANES, ri * sl:(ri + 1) * sl] = widen(bcc_ref[0, dt, ri])
                ccf_ref[dt * LANES:(dt + 1) * LANES, ri * sl:(ri + 1) * sl] = widen(cct_ref[0, dt, ri])

    @pl.when(pl.program_id(2) == 0)
    def _():
        cr_ref[...] = jnp.zeros_like(cr_ref)
        ci_ref[...] = jnp.zeros_like(ci_ref)

    xs = jnp.concatenate([x_ref[pl.ds(dt, ncb, stride=SSM_TC), :] for dt in range(SSM_TC)],
                         axis=1).astype(BF16)
    st_ref[...] = jnp.dot(xs, bcf_ref[...], preferred_element_type=F32)
    ar = ar_ref[...]
    ai = ai_ref[...]

    def body(c, carry):
        sr, si = carry
        lr = st_ref[pl.ds(c, 1), :sl]
        li = st_ref[pl.ds(c, 1), sl:]
        st_ref[pl.ds(c, 1), :sl] = sr
        st_ref[pl.ds(c, 1), sl:] = si
        return ar * sr - ai * si + lr, ar * si + ai * sr + li

    sr, si = lax.fori_loop(0, ncb, body, (cr_ref[...], ci_ref[...]), unroll=8)
    cr_ref[...] = sr
    ci_ref[...] = si
    y = (jnp.dot(xs, mf_ref[...], preferred_element_type=F32)
         + _dot_nt(st_ref[...].astype(BF16), ccf_ref[...]))
    for dt in range(SSM_TC):
        y_ref[pl.ds(dt, ncb, stride=SSM_TC), :] = y[:, dt * LANES:(dt + 1) * LANES]


def _s5_scan(proj, ops, bsz, seq, sw, n_state, cast_weights=()):
    bm, bcc, cct, a16r, a16i = ops
    ns = sw // LANES
    sl = (LANES // SSM_GROUP_DIM) * n_state
    ncb = _pick(seq // SSM_TC, 512)
    rt = ncb * SSM_TC
    nt = seq // rt
    kw = SSM_TC * LANES
    cast_specs = _cast_specs(cast_weights, (ns, bsz, nt))
    kern = functools.partial(_s5_kernel, ncb=ncb, sl=sl, n_state=n_state, ncast=len(cast_weights))
    outs = pl.pallas_call(
        kern,
        grid=(ns, bsz, nt),
        in_specs=[pl.BlockSpec((rt, LANES), lambda s, b, i: (b * nt + i, s)),
                  pl.BlockSpec((1, SSM_TC, LANES, SSM_GROUP_DIM), lambda s, b, i: (s, 0, 0, 0)),
                  pl.BlockSpec((1, SSM_TC, 2, LANES, n_state), lambda s, b, i: (s, 0, 0, 0, 0)),
                  pl.BlockSpec((1, SSM_TC, 2, LANES, n_state), lambda s, b, i: (s, 0, 0, 0, 0)),
                  pl.BlockSpec((1, sl), lambda s, b, i: (0, s)),
                  pl.BlockSpec((1, sl), lambda s, b, i: (0, s))] + cast_specs,
        out_specs=[pl.BlockSpec((rt, LANES), lambda s, b, i: (b * nt + i, s))] + cast_specs,
        out_shape=([jax.ShapeDtypeStruct((bsz * seq, sw), F32)]
                   + [jax.ShapeDtypeStruct(w.shape, BF16) for w in cast_weights]),
        scratch_shapes=[pltpu.VMEM((kw, kw), BF16), pltpu.VMEM((kw, 2 * sl), BF16), pltpu.VMEM((kw, 2 * sl), BF16),
                        pltpu.VMEM((ncb, 2 * sl), F32), pltpu.VMEM((1, sl), F32), pltpu.VMEM((1, sl), F32)],
        compiler_params=_cparams(("arbitrary", "arbitrary", "arbitrary")),
        name="s5_scan",
    )(proj, bm, bcc, cct, a16r, a16i, *cast_weights)
    return outs[0], outs[1:]


def _ssm_post_kernel(y_ref, w_ref, b_ref, o_ref):
    y = y_ref[...]
    yg = 0.5 * y * (1.0 + lax.erf(y * (1.0 / math.sqrt(2.0))))
    s = jnp.dot(yg.astype(BF16), w_ref[...], preferred_element_type=F32) + b_ref[...]
    o_ref[...] = (yg * jax.nn.sigmoid(s)).astype(o_ref.dtype)


def _ssm_post(y, w_glu, b_glu):
    t, n = y.shape
    tm = _pick(t, 512)
    return pl.pallas_call(
        _ssm_post_kernel,
        grid=(t // tm,),
        in_specs=[pl.BlockSpec((tm, n), lambda i: (i, 0)),
                  pl.BlockSpec((n, n), lambda i: (0, 0)),
                  pl.BlockSpec((1, n), lambda i: (0, 0))],
        out_specs=pl.BlockSpec((tm, n), lambda i: (i, 0)),
        out_shape=jax.ShapeDtypeStruct((t, n), BF16),
        compiler_params=_cparams(("arbitrary",)),
        name="ssm_post",
    )(y, w_glu, b_glu.reshape(1, n))


def _dot_nt(a, b):
    return lax.dot_general(a, b, (((1,), (1,)), ((), ())), preferred_element_type=F32)


def _dot_tn(a, b):
    return lax.dot_general(a, b, (((0,), (0,)), ((), ())), preferred_element_type=F32)


def _dot_hi(a, b):
    return jnp.dot(a, b, preferred_element_type=F32, precision=HI)


def _sigmoid(x):
    return 0.5 * jnp.tanh(0.5 * x) + 0.5


def _silu(x):
    hx = 0.5 * x
    return hx * jnp.tanh(hx) + hx


DN_HALO = 8
DN_HEADS_PER_STEP = 8


def _dn_kernel(q_ref, k_ref, v_ref, z_ref, ab_ref, cq_ref, ck_ref, cv_ref, al_ref, dtb_ref, nw_ref,
               *rest, tt, heads, hp, ncast):
    cast_src, o_ref, cast_dst, scratch = rest[:ncast], rest[ncast], rest[ncast + 1:2 * ncast + 1], rest[2 * ncast + 1:]
    s_refs, xx_refs = scratch[:hp], scratch[hp:]

    @pl.when(pl.program_id(2) == 0)
    def _():
        for s_ref, xx_ref in zip(s_refs, xx_refs):
            s_ref[...] = jnp.zeros_like(s_ref)
            xx_ref[:, 0:DN_HALO, :] = jnp.zeros((3, DN_HALO, DN_HEAD_DIM), F32)

    ab = ab_ref[...]
    xa = ab + dtb_ref[...]
    sp = jnp.maximum(xa, 0.0) + jnp.log(1.0 + jnp.exp(-jnp.abs(xa)))
    gc_all = -jnp.exp(al_ref[...]) * sp
    pos = lax.broadcasted_iota(jnp.int32, gc_all.shape, 0) & (CHUNK - 1)
    sh = 1
    while sh < CHUNK:
        gc_all = gc_all + jnp.where(pos >= sh, pltpu.roll(gc_all, sh, axis=0), 0.0)
        sh *= 2
    gate_vals = (gc_all, jnp.exp(gc_all), _sigmoid(ab))
    gens = [_dn_head(hh, pl.program_id(1) * hp + hh, q_ref, k_ref, v_ref, z_ref, gate_vals, cq_ref, ck_ref, cv_ref,
                     nw_ref, o_ref, s_refs[hh], xx_refs[hh], tt=tt, heads=heads)
            for hh in range(hp)]
    if ncast:
        gens.append(_cast_slices(cast_src, cast_dst))
    for _ in itertools.zip_longest(*gens):
        pass


def _dn_head(hh, h, q_ref, k_ref, v_ref, z_ref, gate_vals, cq_ref, ck_ref, cv_ref, nw_ref,
             o_ref, s_ref, xx_ref, *, tt, heads):
    hal = DN_HALO
    ls = slice(hh * DN_HEAD_DIM, (hh + 1) * DN_HEAD_DIM)

    def conv(idx, x_ref, cw_ref):
        x = x_ref[:, ls]
        xx_ref[idx, hal:, :] = x
        w = cw_ref[:, ls]
        acc = w[CONV_WIDTH - 1:CONV_WIDTH, :] * x
        for j in range(CONV_WIDTH - 1):
            off = hal - (CONV_WIDTH - 1) + j
            acc = acc + w[j:j + 1, :] * xx_ref[idx, off:off + tt, :]
        xx_ref[idx, 0:hal, :] = x[tt - hal:, :]
        return _silu(acc)

    qc = conv(0, q_ref, cq_ref)
    kc = conv(1, k_ref, ck_ref)
    vc = conv(2, v_ref, cv_ref)
    qn = qc * lax.rsqrt(jnp.sum(qc * qc, axis=-1, keepdims=True) + NORM_EPS) * (DN_HEAD_DIM ** -0.5)
    kn = kc * lax.rsqrt(jnp.sum(kc * kc, axis=-1, keepdims=True) + NORM_EPS)

    gc_all, egc_all, beta_all = gate_vals
    lane = lax.broadcasted_iota(jnp.int32, gc_all.shape, 1)

    def pick(vals, ln):
        col = jnp.sum(jnp.where(lane == ln, vals, 0.0), axis=-1, keepdims=True)
        return jnp.broadcast_to(col, (tt, DN_HEAD_DIM))

    beta_col = pick(beta_all, h + heads)
    gc = pick(gc_all, h)
    egc = pick(egc_all, h)
    nch = tt // CHUNK
    gc_row = jnp.transpose(gc)[0:1, :]

    ri = lax.broadcasted_iota(jnp.int32, (tt, tt), 0)
    ci = lax.broadcasted_iota(jnp.int32, (tt, tt), 1)
    same = (ri // CHUNK) == (ci // CHUNK)
    causal = same & (ri >= ci)
    strict = same & (ri > ci)
    gc_wide = jnp.concatenate([gc] * (tt // DN_HEAD_DIM), axis=1)
    decay = jnp.where(causal, jnp.exp(jnp.where(causal, gc_wide - gc_row, 0.0)), 0.0)
    kb = kn * beta_col
    knb = kn.astype(BF16)
    yield
    a_raw = _dot_nt(kb.astype(BF16), knb)
    qk_raw = _dot_nt(qn.astype(BF16), knb)
    yield
    a_bd = jnp.where(strict, a_raw * decay, 0.0)
    qk_bd = jnp.where(causal, qk_raw * decay, 0.0).astype(BF16)

    def fold(m):
        out = m[0:CHUNK]
        for c in range(1, nch):
            out = out + m[c * CHUNK:(c + 1) * CHUNK]
        return out

    def spread(m):
        return jnp.where(same, jnp.concatenate([m] * nch, axis=0), 0.0)

    r64 = lax.broadcasted_iota(jnp.int32, (CHUNK, tt), 0)
    c64 = lax.broadcasted_iota(jnp.int32, (CHUNK, tt), 1)
    eye_cat = (r64 == (c64 & (CHUNK - 1))).astype(F32)
    pw_cat = fold(a_bd)
    inv_cat = eye_cat - pw_cat
    pw_bd = a_bd.astype(BF16)
    for _ in range(5):
        pw_cat = jnp.dot(pw_cat.astype(BF16), pw_bd, preferred_element_type=F32)
        yield
        pw_bd = spread(pw_cat).astype(BF16)
        inv_add = jnp.dot(inv_cat.astype(BF16), pw_bd, preferred_element_type=F32)
        yield
        inv_cat = inv_cat + inv_add
    inv_bd = spread(inv_cat).astype(BF16)
    rhs = jnp.concatenate([vc * beta_col, kb * egc], axis=1).astype(BF16)
    sol = jnp.dot(inv_bd, rhs, preferred_element_type=F32)
    yield
    u_all, w_all = sol[:, :DN_HEAD_DIM], sol[:, DN_HEAD_DIM:]
    qe = qn * egc

    outs = []
    zblk = jnp.zeros((CHUNK, DN_HEAD_DIM), BF16)
    s = s_ref[...]
    for c in range(nch):
        sl = slice(c * CHUNK, (c + 1) * CHUNK)
        wq = jnp.concatenate([w_all[sl], qe[sl]], axis=0).astype(BF16)
        ws = jnp.dot(wq, s.astype(BF16), preferred_element_type=F32)
        yield
        v_new = u_all[sl] - ws[:CHUNK]
        vnb = v_new.astype(BF16)
        v_pad = jnp.concatenate([zblk] * c + [vnb] + [zblk] * (nch - 1 - c), axis=0)
        gc_c = gc[sl]
        g_last = gc_c[CHUNK - 1:CHUNK, :]
        k_dec = (kn[sl] * jnp.exp(g_last - gc_c)).astype(BF16)
        s_add = _dot_tn(k_dec, vnb)
        o_add = jnp.dot(qk_bd[sl], v_pad, preferred_element_type=F32)
        yield
        s = s * jnp.exp(g_last) + s_add
        outs.append(ws[CHUNK:] + o_add)
    s_ref[...] = s
    o = jnp.concatenate(outs, axis=0)
    o = o * lax.rsqrt(jnp.mean(o * o, axis=-1, keepdims=True) + NORM_EPS) * nw_ref[...]
    o_ref[:, ls] = (o * _silu(z_ref[:, ls])).astype(o_ref.dtype)


def _deltanet(proj, ab, conv_w, a_log, dt_bias, norm_w, bsz, seq, heads, off_q, cast_weights=()):
    t = bsz * seq
    tt = _pick(seq, 256)
    nt = seq // tt
    hd = DN_HEAD_DIM
    cw = conv_w.reshape(CONV_WIDTH, 3 * heads * hd).astype(F32)
    pad = LANES - heads
    al = jnp.pad(a_log.astype(F32), (0, pad)).reshape(1, LANES)
    dtb = jnp.pad(dt_bias.astype(F32), (0, pad)).reshape(1, LANES)
    nw = norm_w.astype(F32).reshape(1, hd)

    hp = math.gcd(math.gcd(heads, off_q), DN_HEADS_PER_STEP)
    hw = hp * hd
    oq, nh = off_q // hp, heads // hp

    def act(o):
        return pl.BlockSpec((tt, hw), lambda b, h, i: (b * nt + i, oq + o * nh + h))

    def cws(o):
        return pl.BlockSpec((CONV_WIDTH, hw), lambda b, h, i: (0, o * nh + h))

    row = pl.BlockSpec((1, LANES), lambda b, h, i: (0, 0))
    ncast = len(cast_weights)
    cast_specs = _cast_specs(cast_weights, (bsz, nh, nt))
    kern = functools.partial(_dn_kernel, tt=tt, heads=heads, hp=hp, ncast=ncast)
    outs = pl.pallas_call(
        kern,
        grid=(bsz, nh, nt),
        in_specs=[act(0), act(1), act(2), act(3),
                  pl.BlockSpec((tt, LANES), lambda b, h, i: (b * nt + i, 0)),
                  cws(0), cws(1), cws(2), row, row, row] + cast_specs,
        out_specs=[pl.BlockSpec((tt, hw), lambda b, h, i: (b * nt + i, h))] + cast_specs,
        out_shape=([jax.ShapeDtypeStruct((t, heads * hd), BF16)]
                   + [jax.ShapeDtypeStruct(w.shape, BF16) for w in cast_weights]),
        scratch_shapes=([pltpu.VMEM((hd, hd), F32)] * hp
                        + [pltpu.VMEM((3, tt + DN_HALO, hd), F32)] * hp),
        compiler_params=_cparams(("arbitrary", "arbitrary", "arbitrary"), VMEM_LIMIT_DN),
        name="deltanet",
    )(proj, proj, proj, proj, ab, cw, cw, cw, al, dtb, nw, *cast_weights)
    return outs[0], outs[1:]


def _merge_kernel(ys_ref, yd_ref, gs_ref, gd_ref, ws_ref, wd_ref, o_ref):
    ps = jnp.dot(ys_ref[...], ws_ref[...], preferred_element_type=F32)
    pd = jnp.dot(yd_ref[...], wd_ref[...], preferred_element_type=F32)
    o_ref[...] = (jax.nn.sigmoid(gs_ref[...]) * ps + jax.nn.sigmoid(gd_ref[...]) * pd).astype(o_ref.dtype)


def _merge(ys, yd, gates, gate_col, w_ps, w_pd):
    t, ns = ys.shape
    d = w_ps.shape[1]
    tm = _pick(t, 512)
    tn = _pick(ns, 1024)
    nb = d // tn
    off = gate_col // tn
    assert gate_col % tn == 0
    return pl.pallas_call(
        _merge_kernel,
        grid=(nb, t // tm),
        in_specs=[pl.BlockSpec((tm, ns), lambda j, i: (i, 0)),
                  pl.BlockSpec((tm, d), lambda j, i: (i, 0)),
                  pl.BlockSpec((tm, tn), lambda j, i: (i, off + j)),
                  pl.BlockSpec((tm, tn), lambda j, i: (i, off + nb + j)),
                  pl.BlockSpec((ns, tn), lambda j, i: (0, j)),
                  pl.BlockSpec((d, tn), lambda j, i: (0, j))],
        out_specs=pl.BlockSpec((tm, tn), lambda j, i: (i, j)),
        out_shape=jax.ShapeDtypeStruct((t, d), BF16),
        compiler_params=_cparams(("arbitrary", "arbitrary")),
        name="merge",
    )(ys, yd, gates, gates, w_ps, w_pd)


def _pack_bf16_pair(x):
    half = x.shape[1] // 2
    lo = pltpu.bitcast(x[:, :half].astype(BF16).astype(F32), jnp.uint32)
    hi = pltpu.bitcast(x[:, half:].astype(BF16).astype(F32), jnp.uint32)
    return (lo >> 16) | (hi & jnp.uint32(0xFFFF0000))


def _unpack_bf16_pair(u):
    lo = pltpu.bitcast(u << 16, F32).astype(BF16)
    hi = pltpu.bitcast(u & jnp.uint32(0xFFFF0000), F32).astype(BF16)
    return lo, hi


def _out_ln_kernel(m_ref, w_ref, h_ref, g_ref, b_ref, h1_ref, hp_ref):
    mix = jnp.dot(m_ref[...], w_ref[...], preferred_element_type=F32)
    h1 = _ln(DEEPNORM_ALPHA * h_ref[...] + mix, g_ref[...], b_ref[...])
    h1_ref[...] = h1
    hp_ref[...] = _pack_bf16_pair(h1)


def _out_ln(merged, w_out, h, g, b):
    t, d = h.shape
    tm = _pick(t, 512)
    return pl.pallas_call(
        _out_ln_kernel,
        grid=(t // tm,),
        in_specs=[pl.BlockSpec((tm, d), lambda i: (i, 0)),
                  pl.BlockSpec((d, d), lambda i: (0, 0)),
                  pl.BlockSpec((tm, d), lambda i: (i, 0)),
                  pl.BlockSpec((1, d), lambda i: (0, 0)),
                  pl.BlockSpec((1, d), lambda i: (0, 0))],
        out_specs=[pl.BlockSpec((tm, d), lambda i: (i, 0)),
                   pl.BlockSpec((tm, d // 2), lambda i: (i, 0))],
        out_shape=[jax.ShapeDtypeStruct((t, d), F32), jax.ShapeDtypeStruct((t, d // 2), jnp.uint32)],
        compiler_params=_cparams(("arbitrary",)),
        name="out_ln1",
    )(merged, w_out, h, g.reshape(1, d), b.reshape(1, d))


def _router_kernel(h_ref, w_ref, b_ref, ir_ref, gate_ref, cnt_ref, base_ref, *, n_exp, tm):
    i = pl.program_id(0)

    @pl.when(i == 0)
    def _():
        base_ref[...] = jnp.zeros_like(base_ref)

    h = h_ref[...]
    w = w_ref[...]
    h_hi = h.astype(BF16)
    h_lo = (h - h_hi.astype(F32)).astype(BF16)
    w_hi = w.astype(BF16)
    w_lo = (w - w_hi.astype(F32)).astype(BF16)
    logits = (jnp.dot(h_hi, w_hi, preferred_element_type=F32) + jnp.dot(h_hi, w_lo, preferred_element_type=F32)
              + jnp.dot(h_lo, w_hi, preferred_element_type=F32) + b_ref[...])
    lane = lax.broadcasted_iota(jnp.int32, logits.shape, 1)
    vals = jnp.where(lane < n_exp, logits, -jnp.inf)
    sels, tops, ams = [], [], []
    for _ in range(TOP_K):
        m = jnp.max(vals, axis=-1, keepdims=True)
        am = jnp.min(jnp.where(vals == m, lane, LANES), axis=-1, keepdims=True)
        sel = lane == am
        vals = jnp.where(sel, -jnp.inf, vals)
        sels.append(sel)
        tops.append(m)
        ams.append(am)
    es = [jnp.exp(v - tops[0]) for v in tops]
    den = es[0] + es[1] + es[2] + es[3]
    onehot = sels[0] | sels[1] | sels[2] | sels[3]
    mt = onehot.astype(BF16)
    ri = lax.broadcasted_iota(jnp.int32, (tm, tm), 0)
    ci = lax.broadcasted_iota(jnp.int32, (tm, tm), 1)
    before = (ri > ci).astype(BF16)
    prior = jnp.dot(before, mt, preferred_element_type=F32) + base_ref[...]
    ir = jnp.zeros(logits.shape, F32)
    gate_o = jnp.zeros(logits.shape, F32)
    for k in range(TOP_K):
        rk = jnp.sum(jnp.where(sels[k], prior, 0.0), axis=-1, keepdims=True)
        ir = jnp.where(lane == k, ams[k].astype(F32), ir)
        ir = jnp.where(lane == TOP_K + k, rk, ir)
        gate_o = jnp.where(lane == k, es[k] / den, gate_o)
    ir_ref[...] = jnp.transpose(ir)[0:2 * TOP_K, :].astype(jnp.int32)
    gate_ref[...] = gate_o
    base_ref[...] += jnp.sum(onehot.astype(F32), axis=0, keepdims=True)
    cnt_ref[...] = base_ref[...]


def _router(h1, w_router, b_router):
    t, d = h1.shape
    n_exp = w_router.shape[1]
    tm = _pick(t, 512)
    w = jnp.pad(w_router.astype(F32), ((0, 0), (0, LANES - n_exp)))
    b = jnp.pad(b_router.astype(F32), (0, LANES - n_exp)).reshape(1, LANES)
    kern = functools.partial(_router_kernel, n_exp=n_exp, tm=tm)
    return pl.pallas_call(
        kern,
        grid=(t // tm,),
        in_specs=[pl.BlockSpec((tm, d), lambda i: (i, 0)),
                  pl.BlockSpec((d, LANES), lambda i: (0, 0)),
                  pl.BlockSpec((1, LANES), lambda i: (0, 0))],
        out_specs=[pl.BlockSpec((2 * TOP_K, tm), lambda i: (0, i)),
                   pl.BlockSpec((tm, LANES), lambda i: (i, 0)),
                   pl.BlockSpec((1, LANES), lambda i: (0, 0))],
        out_shape=[jax.ShapeDtypeStruct((2 * TOP_K, t), jnp.int32), jax.ShapeDtypeStruct((t, LANES), F32),
                   jax.ShapeDtypeStruct((1, LANES), F32)],
        scratch_shapes=[pltpu.VMEM((1, LANES), F32)],
        compiler_params=_cparams(("arbitrary",)),
        name="router",
    )(h1, w, b)


def _dispatch_kernel(dest_ref, pend_ref, nused_ref, x_ref, xs_ref, zero_ref, sem, zsem, *, tt, n_tok, n_exp, nblk):
    i = pl.program_id(0)
    base = i * tt

    @pl.when(i == 0)
    def _():
        zero_ref[...] = jnp.zeros_like(zero_ref)

        def zero_block(start):
            start = pl.multiple_of(start, ROW_BLOCK)
            return pltpu.make_async_copy(zero_ref, xs_ref.at[pl.ds(start, ROW_BLOCK)], zsem)

        def expert_has_rows(e):
            prev = jnp.where(e == 0, 0, pend_ref[jnp.maximum(e - 1, 0)])
            return pend_ref[e] > prev

        def start_expert(e, carry):
            @pl.when(expert_has_rows(e))
            def _():
                zero_block(pend_ref[e] - ROW_BLOCK).start()
            return carry

        def wait_expert(e, carry):
            @pl.when(expert_has_rows(e))
            def _():
                zero_block(0).wait()
            return carry

        def start_tail(b, carry):
            zero_block(b * ROW_BLOCK).start()
            return carry

        def wait_tail(b, carry):
            zero_block(0).wait()
            return carry

        lax.fori_loop(0, n_exp, start_expert, 0)
        lax.fori_loop(nused_ref[0], nblk, start_tail, 0)
        lax.fori_loop(0, n_exp, wait_expert, 0)
        lax.fori_loop(nused_ref[0], nblk, wait_tail, 0)

    def issue(r, carry):
        for k in range(TOP_K):
            row = dest_ref[k * n_tok + base + r]
            pltpu.make_async_copy(x_ref.at[pl.ds(r, 1)], xs_ref.at[pl.ds(row, 1)], sem).start()
        return carry

    lax.fori_loop(0, tt, issue, 0, unroll=4)

    for k in range(TOP_K):
        pltpu.make_async_copy(x_ref, xs_ref.at[pl.ds(0, tt)], sem).wait()


def _dispatch(xp, dest, pad_ends, n_used, n_rows):
    t, dh = xp.shape
    tt = _pick(t, 512)
    kern = functools.partial(_dispatch_kernel, tt=tt, n_tok=t, n_exp=pad_ends.shape[0], nblk=n_rows // ROW_BLOCK)
    return pl.pallas_call(
        kern,
        grid_spec=pltpu.PrefetchScalarGridSpec(
            num_scalar_prefetch=3,
            grid=(t // tt,),
            in_specs=[pl.BlockSpec((tt, dh), lambda i, *_: (i, 0))],
            out_specs=pl.BlockSpec(memory_space=pl.ANY),
            scratch_shapes=[pltpu.VMEM((ROW_BLOCK, dh), jnp.uint32),
                            pltpu.SemaphoreType.DMA(()), pltpu.SemaphoreType.DMA(())]),
        out_shape=jax.ShapeDtypeStruct((n_rows, dh), jnp.uint32),
        compiler_params=_cparams(("arbitrary",)),
        name="dispatch",
    )(dest, pad_ends.astype(jnp.int32), n_used, xp)


def _expert_kernel(be_ref, nb_ref, x_ref, wg_ref, wu_ref, bg_ref, bu_ref, wd_ref, bd_ref, o_ref, *, half):
    i = pl.program_id(0)
    j = pl.program_id(1)

    @pl.when(i < nb_ref[0])
    def _():
        lo, hi = _unpack_bf16_pair(x_ref[...])
        gate = (jnp.dot(lo, wg_ref[0, :half, :], preferred_element_type=F32)
                + jnp.dot(hi, wg_ref[0, half:, :], preferred_element_type=F32) + bg_ref[0])
        up = (jnp.dot(lo, wu_ref[0, :half, :], preferred_element_type=F32)
              + jnp.dot(hi, wu_ref[0, half:, :], preferred_element_type=F32) + bu_ref[0])
        gate = jnp.minimum(gate, SWIGLU_LIMIT)
        up = jnp.clip(up, -SWIGLU_LIMIT, SWIGLU_LIMIT)
        act = gate * _sigmoid(SWIGLU_ALPHA * gate) * (up + 1.0)
        part = jnp.dot(act.astype(BF16), wd_ref[0], preferred_element_type=F32)

        @pl.when(j == 0)
        def _():
            o_ref[...] = part + bd_ref[0]

        @pl.when(j > 0)
        def _():
            o_ref[...] += part

    @pl.when((i >= nb_ref[0]) & (j == 0))
    def _():
        o_ref[...] = jnp.zeros_like(o_ref)


def _experts(xs, block_expert, n_used, w_gu, b_gu, w_down, b_down):
    n_rows, dh = xs.shape
    n_exp, d, ff2 = w_gu.shape
    ff = ff2 // 2
    tf = _pick(ff, 1024)
    nj = ff // tf
    nblk = n_rows // ROW_BLOCK
    b_gu3 = b_gu.astype(F32).reshape(n_exp, 1, ff2)
    b_d3 = b_down.astype(F32).reshape(n_exp, 1, d)

    def blk(i, nb):
        return jnp.maximum(jnp.minimum(i, nb[0] - 1), 0)

    def jj(i, j, nb):
        return jnp.where(i < nb[0], j, nj - 1)

    kern = functools.partial(_expert_kernel, half=d // 2)
    return pl.pallas_call(
        kern,
        grid_spec=pltpu.PrefetchScalarGridSpec(
            num_scalar_prefetch=2,
            grid=(nblk, nj),
            in_specs=[
                pl.BlockSpec((ROW_BLOCK, dh), lambda i, j, be, nb: (blk(i, nb), 0)),
                pl.BlockSpec((1, d, tf), lambda i, j, be, nb: (be[blk(i, nb)], 0, jj(i, j, nb))),
                pl.BlockSpec((1, d, tf), lambda i, j, be, nb: (be[blk(i, nb)], 0, nj + jj(i, j, nb))),
                pl.BlockSpec((1, 1, tf), lambda i, j, be, nb: (be[blk(i, nb)], 0, jj(i, j, nb))),
                pl.BlockSpec((1, 1, tf), lambda i, j, be, nb: (be[blk(i, nb)], 0, nj + jj(i, j, nb))),
                pl.BlockSpec((1, tf, d), lambda i, j, be, nb: (be[blk(i, nb)], jj(i, j, nb), 0)),
                pl.BlockSpec((1, 1, d), lambda i, j, be, nb: (be[blk(i, nb)], 0, 0)),
            ],
            out_specs=pl.BlockSpec((ROW_BLOCK, d), lambda i, j, be, nb: (i, 0))),
        out_shape=jax.ShapeDtypeStruct((n_rows, d), F32),
        compiler_params=_cparams(("arbitrary", "arbitrary")),
        name="experts",
    )(block_expert, n_used, xs, w_gu, w_gu, b_gu3, b_gu3, w_down, b_d3)


def _combine_kernel(dest_ref, y_ref, h_ref, gate_ref, g_ref, b_ref, o_ref, buf_ref, sem, *, tt, n_tok):
    nsteps = n_tok // tt
    i = pl.program_id(0)
    cur = i % 2

    def gather_tile(tile, buf):
        base = tile * tt

        def issue(r, carry):
            for k in range(TOP_K):
                row = dest_ref[k * n_tok + base + r]
                pltpu.make_async_copy(y_ref.at[pl.ds(row, 1)], buf_ref.at[buf, k, pl.ds(r, 1)],
                                      sem.at[buf]).start()
            return carry

        lax.fori_loop(0, tt, issue, 0, unroll=4)

    @pl.when(i == 0)
    def _():
        gather_tile(0, 0)

    @pl.when(i + 1 < nsteps)
    def _():
        gather_tile(i + 1, 1 - cur)

    for k in range(TOP_K):
        pltpu.make_async_copy(y_ref.at[pl.ds(0, tt)], buf_ref.at[cur, k], sem.at[cur]).wait()
    gates = gate_ref[...]
    ffn = gates[:, 0:1] * buf_ref[cur, 0]
    for k in range(1, TOP_K):
        ffn = ffn + gates[:, k:k + 1] * buf_ref[cur, k]
    o_ref[...] = _ln(DEEPNORM_ALPHA * h_ref[...] + ffn, g_ref[...], b_ref[...])


def _combine(y, dest, h1, gates, g, b):
    t, d = h1.shape
    tt = _pick(t, 256)
    kern = functools.partial(_combine_kernel, tt=tt, n_tok=t)
    return pl.pallas_call(
        kern,
        grid_spec=pltpu.PrefetchScalarGridSpec(
            num_scalar_prefetch=1,
            grid=(t // tt,),
            in_specs=[pl.BlockSpec(memory_space=pl.ANY),
                      pl.BlockSpec((tt, d), lambda i, dst: (i, 0)),
                      pl.BlockSpec((tt, LANES), lambda i, dst: (i, 0)),
                      pl.BlockSpec((1, d), lambda i, dst: (0, 0)),
                      pl.BlockSpec((1, d), lambda i, dst: (0, 0))],
            out_specs=pl.BlockSpec((tt, d), lambda i, dst: (i, 0)),
            scratch_shapes=[pltpu.VMEM((2, TOP_K, tt, d), F32), pltpu.SemaphoreType.DMA((2,))]),
        out_shape=jax.ShapeDtypeStruct((t, d), F32),
        compiler_params=_cparams(("arbitrary",)),
        name="combine",
    )(dest, y, h1, gates, g.reshape(1, d), b.reshape(1, d))


def kernel(x, ln_in_g, ln_in_b, w_in, lam_re, lam_im, log_step, ssm_b_re, ssm_b_im, ssm_c_re, ssm_c_im, ssm_d, w_glu, b_glu, conv_w, a_log, dt_bias, dn_norm_w, w_proj_ssm, w_proj_dn, w_out, ln1_g, ln1_b, w_router, b_router, w_gate_up, b_gate_up, w_down, b_down, ln2_g, ln2_b):
    bsz, seq, d = x.shape
    t = bsz * seq
    sw = d // 2
    heads = d // DN_HEAD_DIM
    dn = heads * DN_HEAD_DIM
    groups = sw // SSM_GROUP_DIM
    n_state = lam_re.shape[-1]
    n_exp = w_router.shape[-1]
    assert w_in.shape[0] == DEPTH == 1
    assert seq % (CHUNK * 4) == 0 and sw % LANES == 0 and groups % 2 == 0 and 2 * heads <= LANES

    h, hb = _ln_in(x.reshape(t, d), ln_in_g, ln_in_b)

    wi = w_in[0]
    c_ab = sw + 4 * dn
    w_gates = _realign_cols(wi, c_ab + 2 * heads, 2 * d)
    w_ab = jnp.pad(wi[:, c_ab:c_ab + 2 * heads], ((0, 0), (0, LANES - 2 * heads))).astype(BF16)
    proj = _proj2(hb, wi, c_ab, w_gates, "proj_in")
    ab = _matmul(hb, w_ab, F32, "proj_ab")

    ops = _s5_operators(lam_re[0], lam_im[0], log_step[0], ssm_b_re[0], ssm_b_im[0],
                        ssm_c_re[0], ssm_c_im[0], ssm_d[0])
    y_s, _ = _s5_scan(proj, ops, bsz, seq, sw, n_state)
    y_ssm = _ssm_post(y_s, w_glu[0].astype(BF16), b_glu[0].astype(F32))

    ff2 = w_gate_up.shape[-1]
    y_dn, (w_gu_bf, w_dn_bf) = _deltanet(
        proj, ab, conv_w[0], a_log[0], dt_bias[0], dn_norm_w[0], bsz, seq, heads, off_q=sw // LANES,
        cast_weights=(w_gate_up[0].reshape(n_exp * d, ff2), w_down[0].reshape(n_exp * (ff2 // 2), d)))
    w_gu_bf = w_gu_bf.reshape(n_exp, d, ff2)
    w_dn_bf = w_dn_bf.reshape(n_exp, ff2 // 2, d)

    merged = _merge(y_ssm, y_dn, proj, sw + 4 * dn, w_proj_ssm[0].astype(BF16), w_proj_dn[0].astype(BF16))
    h1, h1p = _out_ln(merged, w_out[0].astype(BF16), h, ln1_g[0], ln1_b[0])

    idx_rank, gates, cnt = _router(h1, w_router[0], b_router[0])
    counts = cnt[0, :n_exp].astype(jnp.int32)
    padded = (counts + ROW_BLOCK - 1) // ROW_BLOCK * ROW_BLOCK
    pad_ends = jnp.cumsum(padded)
    pad_starts = pad_ends - padded
    slot_expert = idx_rank[:TOP_K]
    is_e = slot_expert[None] == jnp.arange(n_exp, dtype=jnp.int32)[:, None, None]
    start = jnp.sum(jnp.where(is_e, pad_starts.astype(jnp.int32)[:, None, None], 0), axis=0)
    dest = (start + idx_rank[TOP_K:]).astype(jnp.int32).reshape(TOP_K * t)
    n_rows = t * TOP_K + n_exp * ROW_BLOCK
    nblk = n_rows // ROW_BLOCK
    blk_start = jnp.arange(nblk, dtype=jnp.int32) * ROW_BLOCK
    block_expert = jnp.minimum(jnp.sum((pad_ends[None, :] <= blk_start[:, None]).astype(jnp.int32), axis=1),
                               n_exp - 1).astype(jnp.int32)
    n_used = (pad_ends[-1:] // ROW_BLOCK).astype(jnp.int32)
    xs = _dispatch(h1p, dest, pad_ends, n_used, n_rows)
    y = _experts(xs, block_expert, n_used, w_gu_bf, b_gate_up[0], w_dn_bf, b_down[0])
    out = _combine(y, dest, h1, gates, ln2_g[0], ln2_b[0])
    return out.reshape(bsz, seq, d)
```

```python
import functools
import itertools
import math

import jax
import jax.numpy as jnp
from jax import lax
from jax.experimental import pallas as pl
from jax.experimental.pallas import tpu as pltpu

F32 = jnp.float32
BF16 = jnp.bfloat16
HI = lax.Precision.HIGHEST

LANES = 128
CHUNK = 64
LN_EPS = 1e-5
NORM_EPS = 1e-6
SSM_GROUP_DIM = 16
SSM_TC = 16
DN_HEAD_DIM = 128
CONV_WIDTH = 4
TOP_K = 4
SWIGLU_LIMIT = 7.0
SWIGLU_ALPHA = 1.702
DEPTH = 1
DEEPNORM_ALPHA = (2 * DEPTH) ** 0.25
ROW_BLOCK = 512
VMEM_LIMIT = 56 * 1024 * 1024
VMEM_LIMIT_DN = 62 * 1024 * 1024


def _cparams(sem, vmem_limit=VMEM_LIMIT):
    return pltpu.CompilerParams(dimension_semantics=sem, vmem_limit_bytes=vmem_limit)


def _pick(n, pref):
    t = min(n, pref)
    while n % t:
        t //= 2
    return t


def _ln(x, g, b):
    mu = jnp.mean(x, axis=-1, keepdims=True)
    xc = x - mu
    var = jnp.mean(xc * xc, axis=-1, keepdims=True)
    return xc * lax.rsqrt(var + LN_EPS) * g + b


def _ln_in_kernel(x_ref, g_ref, b_ref, h_ref, hb_ref):
    h = _ln(x_ref[...], g_ref[...], b_ref[...])
    h_ref[...] = h
    hb_ref[...] = h.astype(BF16)


def _ln_in(x, g, b):
    t, d = x.shape
    tm = _pick(t, 512)
    return pl.pallas_call(
        _ln_in_kernel,
        grid=(t // tm,),
        in_specs=[pl.BlockSpec((tm, d), lambda i: (i, 0)),
                  pl.BlockSpec((1, d), lambda i: (0, 0)),
                  pl.BlockSpec((1, d), lambda i: (0, 0))],
        out_specs=[pl.BlockSpec((tm, d), lambda i: (i, 0)),
                   pl.BlockSpec((tm, d), lambda i: (i, 0))],
        out_shape=[jax.ShapeDtypeStruct((t, d), F32), jax.ShapeDtypeStruct((t, d), BF16)],
        compiler_params=_cparams(("arbitrary",)),
        name="ln_in",
    )(x, g.reshape(1, d), b.reshape(1, d))


def _mm_kernel(x_ref, w_ref, o_ref):
    o_ref[...] = jnp.dot(x_ref[...], w_ref[...].astype(BF16), preferred_element_type=F32).astype(o_ref.dtype)


def _matmul(x, w, out_dtype, name, ncols=None):
    m, k = x.shape
    n = w.shape[1] if ncols is None else ncols
    tm = _pick(m, 1024)
    tn = _pick(n, 1024)
    return pl.pallas_call(
        _mm_kernel,
        grid=(n // tn, m // tm),
        in_specs=[pl.BlockSpec((tm, k), lambda j, i: (i, 0)),
                  pl.BlockSpec((k, tn), lambda j, i: (0, j))],
        out_specs=pl.BlockSpec((tm, tn), lambda j, i: (i, j)),
        out_shape=jax.ShapeDtypeStruct((m, n), out_dtype),
        compiler_params=_cparams(("arbitrary", "arbitrary")),
        name=name,
    )(x, w)


def _mm_nt_kernel(x_ref, wt_ref, o_ref):
    o_ref[...] = _dot_nt(x_ref[...], wt_ref[...].astype(BF16)).astype(o_ref.dtype)


def _proj_rows(x, wt3, row0, name):
    m, k = x.shape
    tm = _pick(m, 1024)
    assert row0 % LANES == 0
    return pl.pallas_call(
        _mm_nt_kernel,
        grid=(m // tm,),
        in_specs=[pl.BlockSpec((tm, k), lambda i: (i, 0)),
                  pl.BlockSpec((None, LANES, k), lambda i: (0, row0 // LANES, 0))],
        out_specs=pl.BlockSpec((tm, LANES), lambda i: (i, 0)),
        out_shape=jax.ShapeDtypeStruct((m, LANES), F32),
        compiler_params=_cparams(("arbitrary",)),
        name=name,
    )(x, wt3)


def _proj2_kernel(x_ref, wa_ref, wb_ref, o_ref, *, na):
    j = pl.program_id(0)

    @pl.when(j < na)
    def _():
        o_ref[...] = _dot_nt(x_ref[...], wa_ref[...].astype(BF16))

    @pl.when(j >= na)
    def _():
        o_ref[...] = _dot_nt(x_ref[...], wb_ref[...])


def _proj2(x, wat, nrows_a, wbt, name):
    m, k = x.shape
    nb = wbt.shape[0]
    tm = _pick(m, 1024)
    tn = _pick(math.gcd(nrows_a, nb), 1024)
    na, ng = nrows_a // tn, nb // tn
    return pl.pallas_call(
        functools.partial(_proj2_kernel, na=na),
        grid=(na + ng, m // tm),
        in_specs=[pl.BlockSpec((tm, k), lambda j, i: (i, 0)),
                  pl.BlockSpec((None, tn, k), lambda j, i: (0, jnp.minimum(j, na - 1), 0)),
                  pl.BlockSpec((tn, k), lambda j, i: (jnp.maximum(j - na, 0), 0))],
        out_specs=pl.BlockSpec((tm, tn), lambda j, i: (i, j)),
        out_shape=jax.ShapeDtypeStruct((m, nrows_a + nb), F32),
        compiler_params=_cparams(("arbitrary", "arbitrary")),
        name=name,
    )(x, wat, wbt)


CAST_SLICES = 16


def _cast_slices(src_refs, dst_refs):
    for sl in range(CAST_SLICES):
        for src, dst in zip(src_refs, dst_refs):
            n = src.shape[0] // CAST_SLICES
            dst[sl * n:(sl + 1) * n, :] = src[sl * n:(sl + 1) * n, :].astype(BF16)
        yield


def _cast_specs(weights, grid):
    g0, g1, g2 = grid
    nsteps = g0 * g1 * g2
    specs = []
    for w in weights:
        rows, cols = w.shape
        assert rows % (nsteps * CAST_SLICES * 16) == 0, (rows, nsteps)
        specs.append(pl.BlockSpec((rows // nsteps, cols), lambda a, b, c: ((a * g1 + b) * g2 + c, 0)))
    return specs


def _s5_operators(lam_re, lam_im, log_step, b_re, b_im, c_re, c_im, d_skip):
    g, p = lam_re.shape
    hd, tc = SSM_GROUP_DIM, SSM_TC
    lr, li = lam_re.astype(F32), lam_im.astype(F32)
    step = jnp.exp(log_step.astype(F32))[:, None]
    tau = jnp.arange(tc + 1, dtype=F32)[:, None, None]
    mag = jnp.exp(lr * step * tau)
    pr, pi = mag * jnp.cos(li * step * tau), mag * jnp.sin(li * step * tau)
    a_re, a_im = pr[1], pi[1]
    den = lr * lr + li * li
    nr, ni = a_re - 1.0, a_im
    f_re = (nr * lr + ni * li) / den
    f_im = (ni * lr - nr * li) / den
    br, bi = b_re.astype(F32), b_im.astype(F32)
    bb_re = f_re[..., None] * br - f_im[..., None] * bi
    bb_im = f_re[..., None] * bi + f_im[..., None] * br
    cr, ci = c_re.astype(F32), c_im.astype(F32)
    ca_re = cr[None] * pr[:, :, None, :] - ci[None] * pi[:, :, None, :]
    ca_im = cr[None] * pi[:, :, None, :] + ci[None] * pr[:, :, None, :]
    ca_g = jnp.concatenate([jnp.transpose(ca_re[:tc], (1, 0, 2, 3)).reshape(g, tc * hd, p),
                            -jnp.transpose(ca_im[:tc], (1, 0, 2, 3)).reshape(g, tc * hd, p)], axis=2)
    bb_g = jnp.concatenate([bb_re, bb_im], axis=1)
    kk = jnp.einsum('gxp,gpi->gxi', ca_g, bb_g, precision=HI)
    kk = jnp.transpose(kk.reshape(g, tc, hd, hd), (1, 0, 2, 3))
    kk = kk.at[0].add(d_skip.astype(F32)[:, :, None] * jnp.eye(hd, dtype=F32)[None])
    gs = LANES // hd
    ns = g // gs
    bm = jnp.transpose(kk.reshape(tc, ns, gs, hd, hd), (1, 0, 2, 4, 3)).reshape(ns, tc, LANES, hd)
    dt = jnp.arange(tc)

    def rows(x_re, x_im):
        x = jnp.stack([x_re, x_im], axis=1).reshape(tc, 2, ns, gs * hd, p)
        return jnp.transpose(x, (2, 0, 1, 3, 4))

    prr, pir = pr[tc - 1 - dt], pi[tc - 1 - dt]
    bc_re = prr[:, :, None, :] * jnp.transpose(bb_re, (0, 2, 1))[None] - pir[:, :, None, :] * jnp.transpose(bb_im, (0, 2, 1))[None]
    bc_im = prr[:, :, None, :] * jnp.transpose(bb_im, (0, 2, 1))[None] + pir[:, :, None, :] * jnp.transpose(bb_re, (0, 2, 1))[None]
    bcc = rows(bc_re, bc_im)
    cct = rows(ca_re[1:], -ca_im[1:])
    a16r = pr[tc].reshape(1, g * p)
    a16i = pi[tc].reshape(1, g * p)
    return (bm.astype(BF16), bcc.astype(BF16), cct.astype(BF16), a16r, a16i)


def _s5_kernel(x_ref, bm_ref, bcc_ref, cct_ref, ar_ref, ai_ref, *rest, ncb, sl, n_state, ncast):
    cast_src, y_ref, cast_dst = rest[:ncast], rest[ncast], rest[ncast + 1:2 * ncast + 1]
    mf_ref, bcf_ref, ccf_ref, st_ref, cr_ref, ci_ref = rest[2 * ncast + 1:]
    tc = SSM_TC
    for _ in _cast_slices(cast_src, cast_dst):
        pass

    @pl.when((pl.program_id(1) == 0) & (pl.program_id(2) == 0))
    def _():
        def block_diag(blk, width):
            n = blk.shape[1]
            rep = (lax.broadcasted_iota(jnp.int32, (n, width), 0)
                   == lax.broadcasted_iota(jnp.int32, (n, width), 1) % n).astype(BF16)
            wide = jnp.dot(blk, rep, preferred_element_type=F32)
            r = lax.broadcasted_iota(jnp.int32, (LANES, width), 0)
            c = lax.broadcasted_iota(jnp.int32, (LANES, width), 1)
            return jnp.where((r // SSM_GROUP_DIM) == (c // n), wide, 0.0).astype(BF16)

        zero = jnp.zeros((LANES, LANES), BF16)
        tiles = [block_diag(bm_ref[0, tau], LANES) for tau in range(tc)]
        for di in range(tc):
            for do in range(tc):
                mf_ref[di * LANES:(di + 1) * LANES, do * LANES:(do + 1) * LANES] = (
                    tiles[do - di] if do >= di else zero)

        def widen(blk):
            return block_diag(blk, sl)

        for dt in range(tc):
            for ri in range(2):
                bcf_ref[dt * LANES:(dt + 1) * LANES, ri * sl:(ri + 1) * sl] = widen(bcc_ref[0, dt, ri])
                ccf_ref[dt * LANES:(dt + 1) * LANES, ri * sl:(ri + 1) * sl] = widen(cct_ref[0, dt, ri])

    @pl.when(pl.program_id(2) == 0)
    def _():
        cr_ref[...] = jnp.zeros_like(cr_ref)
        ci_ref[...] = jnp.zeros_like(ci_ref)

    xs = jnp.concatenate([x_ref[pl.ds(dt, ncb, stride=SSM_TC), :] for dt in range(SSM_TC)],
                         axis=1).astype(BF16)
    st_ref[...] = jnp.dot(xs, bcf_ref[...], preferred_element_type=F32)
    ar = ar_ref[...]
    ai = ai_ref[...]

    def body(c, carry):
        sr, si = carry
        lr = st_ref[pl.ds(c, 1), :sl]
        li = st_ref[pl.ds(c, 1), sl:]
        st_ref[pl.ds(c, 1), :sl] = sr
        st_ref[pl.ds(c, 1), sl:] = si
        return ar * sr - ai * si + lr, ar * si + ai * sr + li

    sr, si = lax.fori_loop(0, ncb, body, (cr_ref[...], ci_ref[...]), unroll=8)
    cr_ref[...] = sr
    ci_ref[...] = si
    y = (jnp.dot(xs, mf_ref[...], preferred_element_type=F32)
         + _dot_nt(st_ref[...].astype(BF16), ccf_ref[...]))
    for dt in range(SSM_TC):
        y_ref[pl.ds(dt, ncb, stride=SSM_TC), :] = y[:, dt * LANES:(dt + 1) * LANES]


def _s5_scan(proj, ops, bsz, seq, sw, n_state, cast_weights=()):
    bm, bcc, cct, a16r, a16i = ops
    ns = sw // LANES
    sl = (LANES // SSM_GROUP_DIM) * n_state
    ncb = _pick(seq // SSM_TC, 512)
    rt = ncb * SSM_TC
    nt = seq // rt
    kw = SSM_TC * LANES
    cast_specs = _cast_specs(cast_weights, (ns, bsz, nt))
    kern = functools.partial(_s5_kernel, ncb=ncb, sl=sl, n_state=n_state, ncast=len(cast_weights))
    outs = pl.pallas_call(
        kern,
        grid=(ns, bsz, nt),
        in_specs=[pl.BlockSpec((rt, LANES), lambda s, b, i: (b * nt + i, s)),
                  pl.BlockSpec((1, SSM_TC, LANES, SSM_GROUP_DIM), lambda s, b, i: (s, 0, 0, 0)),
                  pl.BlockSpec((1, SSM_TC, 2, LANES, n_state), lambda s, b, i: (s, 0, 0, 0, 0)),
                  pl.BlockSpec((1, SSM_TC, 2, LANES, n_state), lambda s, b, i: (s, 0, 0, 0, 0)),
                  pl.BlockSpec((1, sl), lambda s, b, i: (0, s)),
                  pl.BlockSpec((1, sl), lambda s, b, i: (0, s))] + cast_specs,
        out_specs=[pl.BlockSpec((rt, LANES), lambda s, b, i: (b * nt + i, s))] + cast_specs,
        out_shape=([jax.ShapeDtypeStruct((bsz * seq, sw), F32)]
                   + [jax.ShapeDtypeStruct(w.shape, BF16) for w in cast_weights]),
        scratch_shapes=[pltpu.VMEM((kw, kw), BF16), pltpu.VMEM((kw, 2 * sl), BF16), pltpu.VMEM((kw, 2 * sl), BF16),
                        pltpu.VMEM((ncb, 2 * sl), F32), pltpu.VMEM((1, sl), F32), pltpu.VMEM((1, sl), F32)],
        compiler_params=_cparams(("arbitrary", "arbitrary", "arbitrary")),
        name="s5_scan",
    )(proj, bm, bcc, cct, a16r, a16i, *cast_weights)
    return outs[0], outs[1:]


def _ssm_post_kernel(y_ref, w_ref, b_ref, o_ref):
    y = y_ref[...]
    yg = 0.5 * y * (1.0 + lax.erf(y * (1.0 / math.sqrt(2.0))))
    s = jnp.dot(yg.astype(BF16), w_ref[...], preferred_element_type=F32) + b_ref[...]
    o_ref[...] = (yg * jax.nn.sigmoid(s)).astype(o_ref.dtype)


def _ssm_post(y, w_glu, b_glu):
    t, n = y.shape
    tm = _pick(t, 512)
    return pl.pallas_call(
        _ssm_post_kernel,
        grid=(t // tm,),
        in_specs=[pl.BlockSpec((tm, n), lambda i: (i, 0)),
                  pl.BlockSpec((n, n), lambda i: (0, 0)),
                  pl.BlockSpec((1, n), lambda i: (0, 0))],
        out_specs=pl.BlockSpec((tm, n), lambda i: (i, 0)),
        out_shape=jax.ShapeDtypeStruct((t, n), BF16),
        compiler_params=_cparams(("arbitrary",)),
        name="ssm_post",
    )(y, w_glu, b_glu.reshape(1, n))


def _dot_nt(a, b):
    return lax.dot_general(a, b, (((1,), (1,)), ((), ())), preferred_element_type=F32)


def _dot_tn(a, b):
    return lax.dot_general(a, b, (((0,), (0,)), ((), ())), preferred_element_type=F32)


def _dot_hi(a, b):
    return jnp.dot(a, b, preferred_element_type=F32, precision=HI)


def _sigmoid(x):
    return 0.5 * jnp.tanh(0.5 * x) + 0.5


def _silu(x):
    hx = 0.5 * x
    return hx * jnp.tanh(hx) + hx


DN_HALO = 8
DN_HEADS_PER_STEP = 8


def _dn_kernel(q_ref, k_ref, v_ref, z_ref, ab_ref, cq_ref, ck_ref, cv_ref, al_ref, dtb_ref, nw_ref,
               *rest, tt, heads, hp, ncast):
    cast_src, o_ref, cast_dst, scratch = rest[:ncast], rest[ncast], rest[ncast + 1:2 * ncast + 1], rest[2 * ncast + 1:]
    s_refs, xx_refs = scratch[:hp], scratch[hp:]

    @pl.when(pl.program_id(2) == 0)
    def _():
        for s_ref, xx_ref in zip(s_refs, xx_refs):
            s_ref[...] = jnp.zeros_like(s_ref)
            xx_ref[:, 0:DN_HALO, :] = jnp.zeros((3, DN_HALO, DN_HEAD_DIM), F32)

    ab = ab_ref[...]
    xa = ab + dtb_ref[...]
    sp = jnp.maximum(xa, 0.0) + jnp.log(1.0 + jnp.exp(-jnp.abs(xa)))
    gc_all = -jnp.exp(al_ref[...]) * sp
    pos = lax.broadcasted_iota(jnp.int32, gc_all.shape, 0) & (CHUNK - 1)
    sh = 1
    while sh < CHUNK:
        gc_all = gc_all + jnp.where(pos >= sh, pltpu.roll(gc_all, sh, axis=0), 0.0)
        sh *= 2
    gate_vals = (gc_all, jnp.exp(gc_all), _sigmoid(ab))
    gens = [_dn_head(hh, pl.program_id(1) * hp + hh, q_ref, k_ref, v_ref, z_ref, gate_vals, cq_ref, ck_ref, cv_ref,
                     nw_ref, o_ref, s_refs[hh], xx_refs[hh], tt=tt, heads=heads)
            for hh in range(hp)]
    if ncast:
        gens.append(_cast_slices(cast_src, cast_dst))
    for _ in itertools.zip_longest(*gens):
        pass


def _dn_head(hh, h, q_ref, k_ref, v_ref, z_ref, gate_vals, cq_ref, ck_ref, cv_ref, nw_ref,
             o_ref, s_ref, xx_ref, *, tt, heads):
    hal = DN_HALO
    ls = slice(hh * DN_HEAD_DIM, (hh + 1) * DN_HEAD_DIM)

    def conv(idx, x_ref, cw_ref):
        x = x_ref[:, ls]
        xx_ref[idx, hal:, :] = x
        w = cw_ref[:, ls]
        acc = w[CONV_WIDTH - 1:CONV_WIDTH, :] * x
        for j in range(CONV_WIDTH - 1):
            off = hal - (CONV_WIDTH - 1) + j
            acc = acc + w[j:j + 1, :] * xx_ref[idx, off:off + tt, :]
        xx_ref[idx, 0:hal, :] = x[tt - hal:, :]
        return _silu(acc)

    qc = conv(0, q_ref, cq_ref)
    kc = conv(1, k_ref, ck_ref)
    vc = conv(2, v_ref, cv_ref)
    qn = qc * lax.rsqrt(jnp.sum(qc * qc, axis=-1, keepdims=True) + NORM_EPS) * (DN_HEAD_DIM ** -0.5)
    kn = kc * lax.rsqrt(jnp.sum(kc * kc, axis=-1, keepdims=True) + NORM_EPS)

    gc_all, egc_all, beta_all = gate_vals
    lane = lax.broadcasted_iota(jnp.int32, gc_all.shape, 1)

    def pick(vals, ln):
        col = jnp.sum(jnp.where(lane == ln, vals, 0.0), axis=-1, keepdims=True)
        return jnp.broadcast_to(col, (tt, DN_HEAD_DIM))

    beta_col = pick(beta_all, h + heads)
    gc = pick(gc_all, h)
    egc = pick(egc_all, h)
    nch = tt // CHUNK
    gc_row = jnp.transpose(gc)[0:1, :]

    ri = lax.broadcasted_iota(jnp.int32, (tt, tt), 0)
    ci = lax.broadcasted_iota(jnp.int32, (tt, tt), 1)
    same = (ri // CHUNK) == (ci // CHUNK)
    causal = same & (ri >= ci)
    strict = same & (ri > ci)
    gc_wide = jnp.concatenate([gc] * (tt // DN_HEAD_DIM), axis=1)
    decay = jnp.where(causal, jnp.exp(jnp.where(causal, gc_wide - gc_row, 0.0)), 0.0)
    kb = kn * beta_col
    knb = kn.astype(BF16)
    yield
    a_raw = _dot_nt(kb.astype(BF16), knb)
    qk_raw = _dot_nt(qn.astype(BF16), knb)
    yield
    a_bd = jnp.where(strict, a_raw * decay, 0.0)
    qk_bd = jnp.where(causal, qk_raw * decay, 0.0).astype(BF16)

    def fold(m):
        out = m[0:CHUNK]
        for c in range(1, nch):
            out = out + m[c * CHUNK:(c + 1) * CHUNK]
        return out

    def spread(m):
        return jnp.where(same, jnp.concatenate([m] * nch, axis=0), 0.0)

    r64 = lax.broadcasted_iota(jnp.int32, (CHUNK, tt), 0)
    c64 = lax.broadcasted_iota(jnp.int32, (CHUNK, tt), 1)
    eye_cat = (r64 == (c64 & (CHUNK - 1))).astype(F32)
    pw_cat = fold(a_bd)
    inv_cat = eye_cat - pw_cat
    pw_bd = a_bd.astype(BF16)
    for _ in range(5):
        pw_cat = jnp.dot(pw_cat.astype(BF16), pw_bd, preferred_element_type=F32)
        yield
        pw_bd = spread(pw_cat).astype(BF16)
        inv_add = jnp.dot(inv_cat.astype(BF16), pw_bd, preferred_element_type=F32)
        yield
        inv_cat = inv_cat + inv_add
    inv_bd = spread(inv_cat).astype(BF16)
    rhs = jnp.concatenate([vc * beta_col, kb * egc], axis=1).astype(BF16)
    sol = jnp.dot(inv_bd, rhs, preferred_element_type=F32)
    yield
    u_all, w_all = sol[:, :DN_HEAD_DIM], sol[:, DN_HEAD_DIM:]
    qe = qn * egc

    outs = []
    zblk = jnp.zeros((CHUNK, DN_HEAD_DIM), BF16)
    s = s_ref[...]
    for c in range(nch):
        sl = slice(c * CHUNK, (c + 1) * CHUNK)
        wq = jnp.concatenate([w_all[sl], qe[sl]], axis=0).astype(BF16)
        ws = jnp.dot(wq, s.astype(BF16), preferred_element_type=F32)
        yield
        v_new = u_all[sl] - ws[:CHUNK]
        vnb = v_new.astype(BF16)
        v_pad = jnp.concatenate([zblk] * c + [vnb] + [zblk] * (nch - 1 - c), axis=0)
        gc_c = gc[sl]
        g_last = gc_c[CHUNK - 1:CHUNK, :]
        k_dec = (kn[sl] * jnp.exp(g_last - gc_c)).astype(BF16)
        s_add = _dot_tn(k_dec, vnb)
        o_add = jnp.dot(qk_bd[sl], v_pad, preferred_element_type=F32)
        yield
        s = s * jnp.exp(g_last) + s_add
        outs.append(ws[CHUNK:] + o_add)
    s_ref[...] = s
    o = jnp.concatenate(outs, axis=0)
    o = o * lax.rsqrt(jnp.mean(o * o, axis=-1, keepdims=True) + NORM_EPS) * nw_ref[...]
    o_ref[:, ls] = (o * _silu(z_ref[:, ls])).astype(o_ref.dtype)


def _deltanet(proj, ab, conv_w, a_log, dt_bias, norm_w, bsz, seq, heads, off_q, cast_weights=()):
    t = bsz * seq
    tt = _pick(seq, 256)
    nt = seq // tt
    hd = DN_HEAD_DIM
    cw = conv_w.reshape(CONV_WIDTH, 3 * heads * hd).astype(F32)
    pad = LANES - heads
    al = jnp.pad(a_log.astype(F32), (0, pad)).reshape(1, LANES)
    dtb = jnp.pad(dt_bias.astype(F32), (0, pad)).reshape(1, LANES)
    nw = norm_w.astype(F32).reshape(1, hd)

    hp = math.gcd(math.gcd(heads, off_q), DN_HEADS_PER_STEP)
    hw = hp * hd
    oq, nh = off_q // hp, heads // hp

    def act(o):
        return pl.BlockSpec((tt, hw), lambda b, h, i: (b * nt + i, oq + o * nh + h))

    def cws(o):
        return pl.BlockSpec((CONV_WIDTH, hw), lambda b, h, i: (0, o * nh + h))

    row = pl.BlockSpec((1, LANES), lambda b, h, i: (0, 0))
    ncast = len(cast_weights)
    cast_specs = _cast_specs(cast_weights, (bsz, nh, nt))
    kern = functools.partial(_dn_kernel, tt=tt, heads=heads, hp=hp, ncast=ncast)
    outs = pl.pallas_call(
        kern,
        grid=(bsz, nh, nt),
        in_specs=[act(0), act(1), act(2), act(3),
                  pl.BlockSpec((tt, LANES), lambda b, h, i: (b * nt + i, 0)),
                  cws(0), cws(1), cws(2), row, row, row] + cast_specs,
        out_specs=[pl.BlockSpec((tt, hw), lambda b, h, i: (b * nt + i, h))] + cast_specs,
        out_shape=([jax.ShapeDtypeStruct((t, heads * hd), BF16)]
                   + [jax.ShapeDtypeStruct(w.shape, BF16) for w in cast_weights]),
        scratch_shapes=([pltpu.VMEM((hd, hd), F32)] * hp
                        + [pltpu.VMEM((3, tt + DN_HALO, hd), F32)] * hp),
        compiler_params=_cparams(("arbitrary", "arbitrary", "arbitrary"), VMEM_LIMIT_DN),
        name="deltanet",
    )(proj, proj, proj, proj, ab, cw, cw, cw, al, dtb, nw, *cast_weights)
    return outs[0], outs[1:]


def _merge_kernel(ys_ref, yd_ref, gs_ref, gd_ref, ws_ref, wd_ref, o_ref):
    ps = jnp.dot(ys_ref[...], ws_ref[...], preferred_element_type=F32)
    pd = jnp.dot(yd_ref[...], wd_ref[...], preferred_element_type=F32)
    o_ref[...] = (jax.nn.sigmoid(gs_ref[...]) * ps + jax.nn.sigmoid(gd_ref[...]) * pd).astype(o_ref.dtype)


def _merge(ys, yd, gates, gate_col, w_ps, w_pd):
    t, ns = ys.shape
    d = w_ps.shape[1]
    tm = _pick(t, 512)
    tn = _pick(ns, 1024)
    nb = d // tn
    off = gate_col // tn
    assert gate_col % tn == 0
    return pl.pallas_call(
        _merge_kernel,
        grid=(nb, t // tm),
        in_specs=[pl.BlockSpec((tm, ns), lambda j, i: (i, 0)),
                  pl.BlockSpec((tm, d), lambda j, i: (i, 0)),
                  pl.BlockSpec((tm, tn), lambda j, i: (i, off + j)),
                  pl.BlockSpec((tm, tn), lambda j, i: (i, off + nb + j)),
                  pl.BlockSpec((ns, tn), lambda j, i: (0, j)),
                  pl.BlockSpec((d, tn), lambda j, i: (0, j))],
        out_specs=pl.BlockSpec((tm, tn), lambda j, i: (i, j)),
        out_shape=jax.ShapeDtypeStruct((t, d), BF16),
        compiler_params=_cparams(("arbitrary", "arbitrary")),
        name="merge",
    )(ys, yd, gates, gates, w_ps, w_pd)


def _pack_bf16_pair(x):
    half = x.shape[1] // 2
    lo = pltpu.bitcast(x[:, :half].astype(BF16).astype(F32), jnp.uint32)
    hi = pltpu.bitcast(x[:, half:].astype(BF16).astype(F32), jnp.uint32)
    return (lo >> 16) | (hi & jnp.uint32(0xFFFF0000))


def _unpack_bf16_pair(u):
    lo = pltpu.bitcast(u << 16, F32).astype(BF16)
    hi = pltpu.bitcast(u & jnp.uint32(0xFFFF0000), F32).astype(BF16)
    return lo, hi


def _out_ln_kernel(m_ref, w_ref, h_ref, g_ref, b_ref, h1_ref, hp_ref):
    mix = jnp.dot(m_ref[...], w_ref[...], preferred_element_type=F32)
    h1 = _ln(DEEPNORM_ALPHA * h_ref[...] + mix, g_ref[...], b_ref[...])
    h1_ref[...] = h1
    hp_ref[...] = _pack_bf16_pair(h1)


def _out_ln(merged, w_out, h, g, b):
    t, d = h.shape
    tm = _pick(t, 512)
    return pl.pallas_call(
        _out_ln_kernel,
        grid=(t // tm,),
        in_specs=[pl.BlockSpec((tm, d), lambda i: (i, 0)),
                  pl.BlockSpec((d, d), lambda i: (0, 0)),
                  pl.BlockSpec((tm, d), lambda i: (i, 0)),
                  pl.BlockSpec((1, d), lambda i: (0, 0)),
                  pl.BlockSpec((1, d), lambda i: (0, 0))],
        out_specs=[pl.BlockSpec((tm, d), lambda i: (i, 0)),
                   pl.BlockSpec((tm, d // 2), lambda i: (i, 0))],
        out_shape=[jax.ShapeDtypeStruct((t, d), F32), jax.ShapeDtypeStruct((t, d // 2), jnp.uint32)],
        compiler_params=_cparams(("arbitrary",)),
        name="out_ln1",
    )(merged, w_out, h, g.reshape(1, d), b.reshape(1, d))


def _router_kernel(h_ref, w_ref, b_ref, ir_ref, gate_ref, cnt_ref, base_ref, *, n_exp, tm):
    i = pl.program_id(0)

    @pl.when(i == 0)
    def _():
        base_ref[...] = jnp.zeros_like(base_ref)

    h = h_ref[...]
    w = w_ref[...]
    h_hi = h.astype(BF16)
    h_lo = (h - h_hi.astype(F32)).astype(BF16)
    w_hi = w.astype(BF16)
    w_lo = (w - w_hi.astype(F32)).astype(BF16)
    logits = (jnp.dot(h_hi, w_hi, preferred_element_type=F32) + jnp.dot(h_hi, w_lo, preferred_element_type=F32)
              + jnp.dot(h_lo, w_hi, preferred_element_type=F32) + b_ref[...])
    lane = lax.broadcasted_iota(jnp.int32, logits.shape, 1)
    vals = jnp.where(lane < n_exp, logits, -jnp.inf)
    sels, tops, ams = [], [], []
    for _ in range(TOP_K):
        m = jnp.max(vals, axis=-1, keepdims=True)
        am = jnp.min(jnp.where(vals == m, lane, LANES), axis=-1, keepdims=True)
        sel = lane == am
        vals = jnp.where(sel, -jnp.inf, vals)
        sels.append(sel)
        tops.append(m)
        ams.append(am)
    es = [jnp.exp(v - tops[0]) for v in tops]
    den = es[0] + es[1] + es[2] + es[3]
    onehot = sels[0] | sels[1] | sels[2] | sels[3]
    mt = onehot.astype(BF16)
    ri = lax.broadcasted_iota(jnp.int32, (tm, tm), 0)
    ci = lax.broadcasted_iota(jnp.int32, (tm, tm), 1)
    before = (ri > ci).astype(BF16)
    prior = jnp.dot(before, mt, preferred_element_type=F32) + base_ref[...]
    ir = jnp.zeros(logits.shape, F32)
    gate_o = jnp.zeros(logits.shape, F32)
    for k in range(TOP_K):
        rk = jnp.sum(jnp.where(sels[k], prior, 0.0), axis=-1, keepdims=True)
        ir = jnp.where(lane == k, ams[k].astype(F32), ir)
        ir = jnp.where(lane == TOP_K + k, rk, ir)
        gate_o = jnp.where(lane == k, es[k] / den, gate_o)
    ir_ref[...] = jnp.transpose(ir)[0:2 * TOP_K, :].astype(jnp.int32)
    gate_ref[...] = gate_o
    base_ref[...] += jnp.sum(onehot.astype(F32), axis=0, keepdims=True)
    cnt_ref[...] = base_ref[...]


def _router(h1, w_router, b_router):
    t, d = h1.shape
    n_exp = w_router.shape[1]
    tm = _pick(t, 512)
    w = jnp.pad(w_router.astype(F32), ((0, 0), (0, LANES - n_exp)))
    b = jnp.pad(b_router.astype(F32), (0, LANES - n_exp)).reshape(1, LANES)
    kern = functools.partial(_router_kernel, n_exp=n_exp, tm=tm)
    return pl.pallas_call(
        kern,
        grid=(t // tm,),
        in_specs=[pl.BlockSpec((tm, d), lambda i: (i, 0)),
                  pl.BlockSpec((d, LANES), lambda i: (0, 0)),
                  pl.BlockSpec((1, LANES), lambda i: (0, 0))],
        out_specs=[pl.BlockSpec((2 * TOP_K, tm), lambda i: (0, i)),
                   pl.BlockSpec((tm, LANES), lambda i: (i, 0)),
                   pl.BlockSpec((1, LANES), lambda i: (0, 0))],
        out_shape=[jax.ShapeDtypeStruct((2 * TOP_K, t), jnp.int32), jax.ShapeDtypeStruct((t, LANES), F32),
                   jax.ShapeDtypeStruct((1, LANES), F32)],
        scratch_shapes=[pltpu.VMEM((1, LANES), F32)],
        compiler_params=_cparams(("arbitrary",)),
        name="router",
    )(h1, w, b)


def _dispatch_kernel(dest_ref, pend_ref, nused_ref, x_ref, xs_ref, zero_ref, sem, zsem, *, tt, n_tok, n_exp, nblk):
    i = pl.program_id(0)
    base = i * tt

    @pl.when(i == 0)
    def _():
        zero_ref[...] = jnp.zeros_like(zero_ref)

        def zero_block(start):
            start = pl.multiple_of(start, ROW_BLOCK)
            return pltpu.make_async_copy(zero_ref, xs_ref.at[pl.ds(start, ROW_BLOCK)], zsem)

        def expert_has_rows(e):
            prev = jnp.where(e == 0, 0, pend_ref[jnp.maximum(e - 1, 0)])
            return pend_ref[e] > prev

        def start_expert(e, carry):
            @pl.when(expert_has_rows(e))
            def _():
                zero_block(pend_ref[e] - ROW_BLOCK).start()
            return carry

        def wait_expert(e, carry):
            @pl.when(expert_has_rows(e))
            def _():
                zero_block(0).wait()
            return carry

        def start_tail(b, carry):
            zero_block(b * ROW_BLOCK).start()
            return carry

        def wait_tail(b, carry):
            zero_block(0).wait()
            return carry

        lax.fori_loop(0, n_exp, start_expert, 0)
        lax.fori_loop(nused_ref[0], nblk, start_tail, 0)
        lax.fori_loop(0, n_exp, wait_expert, 0)
        lax.fori_loop(nused_ref[0], nblk, wait_tail, 0)

    def issue(r, carry):
        for k in range(TOP_K):
            row = dest_ref[k * n_tok + base + r]
            pltpu.make_async_copy(x_ref.at[pl.ds(r, 1)], xs_ref.at[pl.ds(row, 1)], sem).start()
        return carry

    lax.fori_loop(0, tt, issue, 0, unroll=4)

    for k in range(TOP_K):
        pltpu.make_async_copy(x_ref, xs_ref.at[pl.ds(0, tt)], sem).wait()


def _dispatch(xp, dest, pad_ends, n_used, n_rows):
    t, dh = xp.shape
    tt = _pick(t, 512)
    kern = functools.partial(_dispatch_kernel, tt=tt, n_tok=t, n_exp=pad_ends.shape[0], nblk=n_rows // ROW_BLOCK)
    return pl.pallas_call(
        kern,
        grid_spec=pltpu.PrefetchScalarGridSpec(
            num_scalar_prefetch=3,
            grid=(t // tt,),
            in_specs=[pl.BlockSpec((tt, dh), lambda i, *_: (i, 0))],
            out_specs=pl.BlockSpec(memory_space=pl.ANY),
            scratch_shapes=[pltpu.VMEM((ROW_BLOCK, dh), jnp.uint32),
                            pltpu.SemaphoreType.DMA(()), pltpu.SemaphoreType.DMA(())]),
        out_shape=jax.ShapeDtypeStruct((n_rows, dh), jnp.uint32),
        compiler_params=_cparams(("arbitrary",)),
        name="dispatch",
    )(dest, pad_ends.astype(jnp.int32), n_used, xp)


def _expert_kernel(be_ref, nb_ref, x_ref, wg_ref, wu_ref, bg_ref, bu_ref, wd_ref, bd_ref, o_ref, *, half):
    i = pl.program_id(0)
    j = pl.program_id(1)

    @pl.when(i < nb_ref[0])
    def _():
        lo, hi = _unpack_bf16_pair(x_ref[...])
        gate = (jnp.dot(lo, wg_ref[0, :half, :], preferred_element_type=F32)
                + jnp.dot(hi, wg_ref[0, half:, :], preferred_element_type=F32) + bg_ref[0])
        up = (jnp.dot(lo, wu_ref[0, :half, :], preferred_element_type=F32)
              + jnp.dot(hi, wu_ref[0, half:, :], preferred_element_type=F32) + bu_ref[0])
        gate = jnp.minimum(gate, SWIGLU_LIMIT)
        up = jnp.clip(up, -SWIGLU_LIMIT, SWIGLU_LIMIT)
        act = gate * _sigmoid(SWIGLU_ALPHA * gate) * (up + 1.0)
        part = jnp.dot(act.astype(BF16), wd_ref[0], preferred_element_type=F32)

        @pl.when(j == 0)
        def _():
            o_ref[...] = part + bd_ref[0]

        @pl.when(j > 0)
        def _():
            o_ref[...] += part

    @pl.when((i >= nb_ref[0]) & (j == 0))
    def _():
        o_ref[...] = jnp.zeros_like(o_ref)


def _experts(xs, block_expert, n_used, w_gu, b_gu, w_down, b_down):
    n_rows, dh = xs.shape
    n_exp, d, ff2 = w_gu.shape
    ff = ff2 // 2
    tf = _pick(ff, 1024)
    nj = ff // tf
    nblk = n_rows // ROW_BLOCK
    b_gu3 = b_gu.astype(F32).reshape(n_exp, 1, ff2)
    b_d3 = b_down.astype(F32).reshape(n_exp, 1, d)

    def blk(i, nb):
        return jnp.maximum(jnp.minimum(i, nb[0] - 1), 0)

    def jj(i, j, nb):
        return jnp.where(i < nb[0], j, nj - 1)

    kern = functools.partial(_expert_kernel, half=d // 2)
    return pl.pallas_call(
        kern,
        grid_spec=pltpu.PrefetchScalarGridSpec(
            num_scalar_prefetch=2,
            grid=(nblk, nj),
            in_specs=[
                pl.BlockSpec((ROW_BLOCK, dh), lambda i, j, be, nb: (blk(i, nb), 0)),
                pl.BlockSpec((1, d, tf), lambda i, j, be, nb: (be[blk(i, nb)], 0, jj(i, j, nb))),
                pl.BlockSpec((1, d, tf), lambda i, j, be, nb: (be[blk(i, nb)], 0, nj + jj(i, j, nb))),
                pl.BlockSpec((1, 1, tf), lambda i, j, be, nb: (be[blk(i, nb)], 0, jj(i, j, nb))),
                pl.BlockSpec((1, 1, tf), lambda i, j, be, nb: (be[blk(i, nb)], 0, nj + jj(i, j, nb))),
                pl.BlockSpec((1, tf, d), lambda i, j, be, nb: (be[blk(i, nb)], jj(i, j, nb), 0)),
                pl.BlockSpec((1, 1, d), lambda i, j, be, nb: (be[blk(i, nb)], 0, 0)),
            ],
            out_specs=pl.BlockSpec((ROW_BLOCK, d), lambda i, j, be, nb: (i, 0))),
        out_shape=jax.ShapeDtypeStruct((n_rows, d), F32),
        compiler_params=_cparams(("arbitrary", "arbitrary")),
        name="experts",
    )(block_expert, n_used, xs, w_gu, w_gu, b_gu3, b_gu3, w_down, b_d3)


def _combine_kernel(dest_ref, y_ref, h_ref, gate_ref, g_ref, b_ref, o_ref, buf_ref, sem, *, tt, n_tok):
    nsteps = n_tok // tt
    i = pl.program_id(0)
    cur = i % 2

    def gather_tile(tile, buf):
        base = tile * tt

        def issue(r, carry):
            for k in range(TOP_K):
                row = dest_ref[k * n_tok + base + r]
                pltpu.make_async_copy(y_ref.at[pl.ds(row, 1)], buf_ref.at[buf, k, pl.ds(r, 1)],
                                      sem.at[buf]).start()
            return carry

        lax.fori_loop(0, tt, issue, 0, unroll=4)

    @pl.when(i == 0)
    def _():
        gather_tile(0, 0)

    @pl.when(i + 1 < nsteps)
    def _():
        gather_tile(i + 1, 1 - cur)

    for k in range(TOP_K):
        pltpu.make_async_copy(y_ref.at[pl.ds(0, tt)], buf_ref.at[cur, k], sem.at[cur]).wait()
    gates = gate_ref[...]
    ffn = gates[:, 0:1] * buf_ref[cur, 0]
    for k in range(1, TOP_K):
        ffn = ffn + gates[:, k:k + 1] * buf_ref[cur, k]
    o_ref[...] = _ln(DEEPNORM_ALPHA * h_ref[...] + ffn, g_ref[...], b_ref[...])


def _combine(y, dest, h1, gates, g, b):
    t, d = h1.shape
    tt = _pick(t, 256)
    kern = functools.partial(_combine_kernel, tt=tt, n_tok=t)
    return pl.pallas_call(
        kern,
        grid_spec=pltpu.PrefetchScalarGridSpec(
            num_scalar_prefetch=1,
            grid=(t // tt,),
            in_specs=[pl.BlockSpec(memory_space=pl.ANY),
                      pl.BlockSpec((tt, d), lambda i, dst: (i, 0)),
                      pl.BlockSpec((tt, LANES), lambda i, dst: (i, 0)),
                      pl.BlockSpec((1, d), lambda i, dst: (0, 0)),
                      pl.BlockSpec((1, d), lambda i, dst: (0, 0))],
            out_specs=pl.BlockSpec((tt, d), lambda i, dst: (i, 0)),
            scratch_shapes=[pltpu.VMEM((2, TOP_K, tt, d), F32), pltpu.SemaphoreType.DMA((2,))]),
        out_shape=jax.ShapeDtypeStruct((t, d), F32),
        compiler_params=_cparams(("arbitrary",)),
        name="combine",
    )(dest, y, h1, gates, g.reshape(1, d), b.reshape(1, d))


def kernel(x, ln_in_g, ln_in_b, w_in, lam_re, lam_im, log_step, ssm_b_re, ssm_b_im, ssm_c_re, ssm_c_im, ssm_d, w_glu, b_glu, conv_w, a_log, dt_bias, dn_norm_w, w_proj_ssm, w_proj_dn, w_out, ln1_g, ln1_b, w_router, b_router, w_gate_up, b_gate_up, w_down, b_down, ln2_g, ln2_b):
    bsz, seq, d = x.shape
    t = bsz * seq
    sw = d // 2
    heads = d // DN_HEAD_DIM
    dn = heads * DN_HEAD_DIM
    groups = sw // SSM_GROUP_DIM
    n_state = lam_re.shape[-1]
    n_exp = w_router.shape[-1]
    assert w_in.shape[0] == DEPTH == 1
    assert seq % (CHUNK * 4) == 0 and sw % LANES == 0 and groups % 2 == 0 and 2 * heads <= LANES

    h, hb = _ln_in(x.reshape(t, d), ln_in_g, ln_in_b)

    c_ab = sw + 4 * dn
    assert c_ab % LANES == 0
    w_in_t = jnp.swapaxes(w_in, 1, 2)
    w_gates_t = w_in_t[0, c_ab + 2 * heads:, :].astype(BF16)
    proj = _proj2(hb, w_in_t, c_ab, w_gates_t, "proj_in")
    ab = _proj_rows(hb, w_in_t, c_ab, "proj_ab")

    ops = _s5_operators(lam_re[0], lam_im[0], log_step[0], ssm_b_re[0], ssm_b_im[0],
                        ssm_c_re[0], ssm_c_im[0], ssm_d[0])
    y_s, _ = _s5_scan(proj, ops, bsz, seq, sw, n_state)
    y_ssm = _ssm_post(y_s, w_glu[0].astype(BF16), b_glu[0].astype(F32))

    ff2 = w_gate_up.shape[-1]
    y_dn, (w_gu_bf, w_dn_bf) = _deltanet(
        proj, ab, conv_w[0], a_log[0], dt_bias[0], dn_norm_w[0], bsz, seq, heads, off_q=sw // LANES,
        cast_weights=(w_gate_up[0].reshape(n_exp * d, ff2), w_down[0].reshape(n_exp * (ff2 // 2), d)))
    w_gu_bf = w_gu_bf.reshape(n_exp, d, ff2)
    w_dn_bf = w_dn_bf.reshape(n_exp, ff2 // 2, d)

    merged = _merge(y_ssm, y_dn, proj, sw + 4 * dn, w_proj_ssm[0].astype(BF16), w_proj_dn[0].astype(BF16))
    h1, h1p = _out_ln(merged, w_out[0].astype(BF16), h, ln1_g[0], ln1_b[0])

    idx_rank, gates, cnt = _router(h1, w_router[0], b_router[0])
    counts = cnt[0, :n_exp].astype(jnp.int32)
    padded = (counts + ROW_BLOCK - 1) // ROW_BLOCK * ROW_BLOCK
    pad_ends = jnp.cumsum(padded)
    pad_starts = pad_ends - padded
    slot_expert = idx_rank[:TOP_K]
    is_e = slot_expert[None] == jnp.arange(n_exp, dtype=jnp.int32)[:, None, None]
    start = jnp.sum(jnp.where(is_e, pad_starts.astype(jnp.int32)[:, None, None], 0), axis=0)
    dest = (start + idx_rank[TOP_K:]).astype(jnp.int32).reshape(TOP_K * t)
    n_rows = t * TOP_K + n_exp * ROW_BLOCK
    nblk = n_rows // ROW_BLOCK
    blk_start = jnp.arange(nblk, dtype=jnp.int32) * ROW_BLOCK
    block_expert = jnp.minimum(jnp.sum((pad_ends[None, :] <= blk_start[:, None]).astype(jnp.int32), axis=1),
                               n_exp - 1).astype(jnp.int32)
    n_used = (pad_ends[-1:] // ROW_BLOCK).astype(jnp.int32)
    xs = _dispatch(h1p, dest, pad_ends, n_used, n_rows)
    y = _experts(xs, block_expert, n_used, w_gu_bf, b_gate_up[0], w_dn_bf, b_down[0])
    out = _combine(y, dest, h1, gates, ln2_g[0], ln2_b[0])
    return out.reshape(bsz, seq, d)
```

```python
import functools
import itertools
import math

import jax
import jax.numpy as jnp
from jax import lax
from jax.experimental import pallas as pl
from jax.experimental.pallas import tpu as pltpu

F32 = jnp.float32
BF16 = jnp.bfloat16
HI = lax.Precision.HIGHEST

LANES = 128
CHUNK = 64
LN_EPS = 1e-5
NORM_EPS = 1e-6
SSM_GROUP_DIM = 16
SSM_TC = 16
DN_HEAD_DIM = 128
CONV_WIDTH = 4
TOP_K = 4
SWIGLU_LIMIT = 7.0
SWIGLU_ALPHA = 1.702
DEPTH = 1
DEEPNORM_ALPHA = (2 * DEPTH) ** 0.25
ROW_BLOCK = 512
VMEM_LIMIT = 56 * 1024 * 1024
VMEM_LIMIT_DN = 62 * 1024 * 1024


def _cparams(sem, vmem_limit=VMEM_LIMIT):
    return pltpu.CompilerParams(dimension_semantics=sem, vmem_limit_bytes=vmem_limit)


def _pick(n, pref):
    t = min(n, pref)
    while n % t:
        t //= 2
    return t


def _ln(x, g, b):
    mu = jnp.mean(x, axis=-1, keepdims=True)
    xc = x - mu
    var = jnp.mean(xc * xc, axis=-1, keepdims=True)
    return xc * lax.rsqrt(var + LN_EPS) * g + b


def _ln_in_kernel(x_ref, g_ref, b_ref, h_ref, hb_ref):
    h = _ln(x_ref[...], g_ref[...], b_ref[...])
    h_ref[...] = h
    hb_ref[...] = h.astype(BF16)


def _ln_in(x, g, b):
    t, d = x.shape
    tm = _pick(t, 512)
    return pl.pallas_call(
        _ln_in_kernel,
        grid=(t // tm,),
        in_specs=[pl.BlockSpec((tm, d), lambda i: (i, 0)),
                  pl.BlockSpec((1, d), lambda i: (0, 0)),
                  pl.BlockSpec((1, d), lambda i: (0, 0))],
        out_specs=[pl.BlockSpec((tm, d), lambda i: (i, 0)),
                   pl.BlockSpec((tm, d), lambda i: (i, 0))],
        out_shape=[jax.ShapeDtypeStruct((t, d), F32), jax.ShapeDtypeStruct((t, d), BF16)],
        compiler_params=_cparams(("arbitrary",)),
        name="ln_in",
    )(x, g.reshape(1, d), b.reshape(1, d))


def _mm_kernel(x_ref, w_ref, o_ref):
    o_ref[...] = jnp.dot(x_ref[...], w_ref[...].astype(BF16), preferred_element_type=F32).astype(o_ref.dtype)


def _matmul(x, w, out_dtype, name, ncols=None):
    m, k = x.shape
    n = w.shape[1] if ncols is None else ncols
    tm = _pick(m, 1024)
    tn = _pick(n, 1024)
    return pl.pallas_call(
        _mm_kernel,
        grid=(n // tn, m // tm),
        in_specs=[pl.BlockSpec((tm, k), lambda j, i: (i, 0)),
                  pl.BlockSpec((k, tn), lambda j, i: (0, j))],
        out_specs=pl.BlockSpec((tm, tn), lambda j, i: (i, j)),
        out_shape=jax.ShapeDtypeStruct((m, n), out_dtype),
        compiler_params=_cparams(("arbitrary", "arbitrary")),
        name=name,
    )(x, w)


def _mm_nt_kernel(x_ref, wt_ref, o_ref):
    o_ref[...] = _dot_nt(x_ref[...], wt_ref[...].astype(BF16)).astype(o_ref.dtype)


def _proj_rows(x, wt3, row0, name):
    m, k = x.shape
    tm = _pick(m, 1024)
    assert row0 % LANES == 0
    return pl.pallas_call(
        _mm_nt_kernel,
        grid=(m // tm,),
        in_specs=[pl.BlockSpec((tm, k), lambda i: (i, 0)),
                  pl.BlockSpec((None, LANES, k), lambda i: (0, row0 // LANES, 0))],
        out_specs=pl.BlockSpec((tm, LANES), lambda i: (i, 0)),
        out_shape=jax.ShapeDtypeStruct((m, LANES), F32),
        compiler_params=_cparams(("arbitrary",)),
        name=name,
    )(x, wt3)


def _proj2_kernel(x_ref, wa_ref, wb_ref, o_ref, wc_ref, *, na):
    j = pl.program_id(0)

    @pl.when((j < na) & (pl.program_id(1) == 0))
    def _():
        wc_ref[...] = wa_ref[...].astype(BF16)

    @pl.when(j < na)
    def _():
        o_ref[...] = _dot_nt(x_ref[...], wc_ref[...])

    @pl.when(j >= na)
    def _():
        o_ref[...] = _dot_nt(x_ref[...], wb_ref[...])


def _proj2(x, wat, nrows_a, wbt, name):
    m, k = x.shape
    nb = wbt.shape[0]
    tm = _pick(m, 1024)
    tn = _pick(math.gcd(nrows_a, nb), 1024)
    na, ng = nrows_a // tn, nb // tn
    return pl.pallas_call(
        functools.partial(_proj2_kernel, na=na),
        grid=(na + ng, m // tm),
        in_specs=[pl.BlockSpec((tm, k), lambda j, i: (i, 0)),
                  pl.BlockSpec((None, tn, k), lambda j, i: (0, jnp.minimum(j, na - 1), 0)),
                  pl.BlockSpec((tn, k), lambda j, i: (jnp.maximum(j - na, 0), 0))],
        out_specs=pl.BlockSpec((tm, tn), lambda j, i: (i, j)),
        out_shape=jax.ShapeDtypeStruct((m, nrows_a + nb), F32),
        scratch_shapes=[pltpu.VMEM((tn, k), BF16)],
        compiler_params=_cparams(("arbitrary", "arbitrary")),
        name=name,
    )(x, wat, wbt)


CAST_SLICES = 16


def _cast_slices(src_refs, dst_refs):
    for sl in range(CAST_SLICES):
        for src, dst in zip(src_refs, dst_refs):
            n = src.shape[0] // CAST_SLICES
            dst[sl * n:(sl + 1) * n, :] = src[sl * n:(sl + 1) * n, :].astype(BF16)
        yield


def _cast_specs(weights, grid):
    g0, g1, g2 = grid
    nsteps = g0 * g1 * g2
    specs = []
    for w in weights:
        rows, cols = w.shape
        assert rows % (nsteps * CAST_SLICES * 16) == 0, (rows, nsteps)
        specs.append(pl.BlockSpec((rows // nsteps, cols), lambda a, b, c: ((a * g1 + b) * g2 + c, 0)))
    return specs


def _s5_operators(lam_re, lam_im, log_step, b_re, b_im, c_re, c_im, d_skip):
    g, p = lam_re.shape
    hd, tc = SSM_GROUP_DIM, SSM_TC
    lr, li = lam_re.astype(F32), lam_im.astype(F32)
    step = jnp.exp(log_step.astype(F32))[:, None]
    tau = jnp.arange(tc + 1, dtype=F32)[:, None, None]
    mag = jnp.exp(lr * step * tau)
    pr, pi = mag * jnp.cos(li * step * tau), mag * jnp.sin(li * step * tau)
    a_re, a_im = pr[1], pi[1]
    den = lr * lr + li * li
    nr, ni = a_re - 1.0, a_im
    f_re = (nr * lr + ni * li) / den
    f_im = (ni * lr - nr * li) / den
    br, bi = b_re.astype(F32), b_im.astype(F32)
    bb_re = f_re[..., None] * br - f_im[..., None] * bi
    bb_im = f_re[..., None] * bi + f_im[..., None] * br
    cr, ci = c_re.astype(F32), c_im.astype(F32)
    ca_re = cr[None] * pr[:, :, None, :] - ci[None] * pi[:, :, None, :]
    ca_im = cr[None] * pi[:, :, None, :] + ci[None] * pr[:, :, None, :]
    ca_g = jnp.concatenate([jnp.transpose(ca_re[:tc], (1, 0, 2, 3)).reshape(g, tc * hd, p),
                            -jnp.transpose(ca_im[:tc], (1, 0, 2, 3)).reshape(g, tc * hd, p)], axis=2)
    bb_g = jnp.concatenate([bb_re, bb_im], axis=1)
    kk = jnp.einsum('gxp,gpi->gxi', ca_g, bb_g, precision=HI)
    kk = jnp.transpose(kk.reshape(g, tc, hd, hd), (1, 0, 2, 3))
    kk = kk.at[0].add(d_skip.astype(F32)[:, :, None] * jnp.eye(hd, dtype=F32)[None])
    gs = LANES // hd
    ns = g // gs
    bm = jnp.transpose(kk.reshape(tc, ns, gs, hd, hd), (1, 0, 2, 4, 3)).reshape(ns, tc, LANES, hd)
    dt = jnp.arange(tc)

    def rows(x_re, x_im):
        x = jnp.stack([x_re, x_im], axis=1).reshape(tc, 2, ns, gs * hd, p)
        return jnp.transpose(x, (2, 0, 1, 3, 4))

    prr, pir = pr[tc - 1 - dt], pi[tc - 1 - dt]
    bc_re = prr[:, :, None, :] * jnp.transpose(bb_re, (0, 2, 1))[None] - pir[:, :, None, :] * jnp.transpose(bb_im, (0, 2, 1))[None]
    bc_im = prr[:, :, None, :] * jnp.transpose(bb_im, (0, 2, 1))[None] + pir[:, :, None, :] * jnp.transpose(bb_re, (0, 2, 1))[None]
    bcc = rows(bc_re, bc_im)
    cct = rows(ca_re[1:], -ca_im[1:])
    a16r = pr[tc].reshape(1, g * p)
    a16i = pi[tc].reshape(1, g * p)
    return (bm.astype(BF16), bcc.astype(BF16), cct.astype(BF16), a16r, a16i)


def _s5_kernel(x_ref, bm_ref, bcc_ref, cct_ref, ar_ref, ai_ref, *rest, ncb, sl, n_state, ncast):
    cast_src, y_ref, cast_dst = rest[:ncast], rest[ncast], rest[ncast + 1:2 * ncast + 1]
    mf_ref, bcf_ref, ccf_ref, st_ref, cr_ref, ci_ref = rest[2 * ncast + 1:]
    tc = SSM_TC
    for _ in _cast_slices(cast_src, cast_dst):
        pass

    @pl.when((pl.program_id(1) == 0) & (pl.program_id(2) == 0))
    def _():
        def block_diag(blk, width):
            n = blk.shape[1]
            rep = (lax.broadcasted_iota(jnp.int32, (n, width), 0)
                   == lax.broadcasted_iota(jnp.int32, (n, width), 1) % n).astype(BF16)
            wide = jnp.dot(blk, rep, preferred_element_type=F32)
            r = lax.broadcasted_iota(jnp.int32, (LANES, width), 0)
            c = lax.broadcasted_iota(jnp.int32, (LANES, width), 1)
            return jnp.where((r // SSM_GROUP_DIM) == (c // n), wide, 0.0).astype(BF16)

        zero = jnp.zeros((LANES, LANES), BF16)
        tiles = [block_diag(bm_ref[0, tau], LANES) for tau in range(tc)]
        for di in range(tc):
            for do in range(tc):
                mf_ref[di * LANES:(di + 1) * LANES, do * LANES:(do + 1) * LANES] = (
                    tiles[do - di] if do >= di else zero)

        def widen(blk):
            return block_diag(blk, sl)

        for dt in range(tc):
            for ri in range(2):
                bcf_ref[dt * LANES:(dt + 1) * LANES, ri * sl:(ri + 1) * sl] = widen(bcc_ref[0, dt, ri])
                ccf_ref[dt * LANES:(dt + 1) * LANES, ri * sl:(ri + 1) * sl] = widen(cct_ref[0, dt, ri])

    @pl.when(pl.program_id(2) == 0)
    def _():
        cr_ref[...] = jnp.zeros_like(cr_ref)
        ci_ref[...] = jnp.zeros_like(ci_ref)

    xs = jnp.concatenate([x_ref[pl.ds(dt, ncb, stride=SSM_TC), :] for dt in range(SSM_TC)],
                         axis=1).astype(BF16)
    st_ref[...] = jnp.dot(xs, bcf_ref[...], preferred_element_type=F32)
    ar = ar_ref[...]
    ai = ai_ref[...]

    def body(c, carry):
        sr, si = carry
        lr = st_ref[pl.ds(c, 1), :sl]
        li = st_ref[pl.ds(c, 1), sl:]
        st_ref[pl.ds(c, 1), :sl] = sr
        st_ref[pl.ds(c, 1), sl:] = si
        return ar * sr - ai * si + lr, ar * si + ai * sr + li

    sr, si = lax.fori_loop(0, ncb, body, (cr_ref[...], ci_ref[...]), unroll=8)
    cr_ref[...] = sr
    ci_ref[...] = si
    y = (jnp.dot(xs, mf_ref[...], preferred_element_type=F32)
         + _dot_nt(st_ref[...].astype(BF16), ccf_ref[...]))
    for dt in range(SSM_TC):
        y_ref[pl.ds(dt, ncb, stride=SSM_TC), :] = y[:, dt * LANES:(dt + 1) * LANES]


def _s5_scan(proj, ops, bsz, seq, sw, n_state, cast_weights=()):
    bm, bcc, cct, a16r, a16i = ops
    ns = sw // LANES
    sl = (LANES // SSM_GROUP_DIM) * n_state
    ncb = _pick(seq // SSM_TC, 512)
    rt = ncb * SSM_TC
    nt = seq // rt
    kw = SSM_TC * LANES
    cast_specs = _cast_specs(cast_weights, (ns, bsz, nt))
    kern = functools.partial(_s5_kernel, ncb=ncb, sl=sl, n_state=n_state, ncast=len(cast_weights))
    outs = pl.pallas_call(
        kern,
        grid=(ns, bsz, nt),
        in_specs=[pl.BlockSpec((rt, LANES), lambda s, b, i: (b * nt + i, s)),
                  pl.BlockSpec((1, SSM_TC, LANES, SSM_GROUP_DIM), lambda s, b, i: (s, 0, 0, 0)),
                  pl.BlockSpec((1, SSM_TC, 2, LANES, n_state), lambda s, b, i: (s, 0, 0, 0, 0)),
                  pl.BlockSpec((1, SSM_TC, 2, LANES, n_state), lambda s, b, i: (s, 0, 0, 0, 0)),
                  pl.BlockSpec((1, sl), lambda s, b, i: (0, s)),
                  pl.BlockSpec((1, sl), lambda s, b, i: (0, s))] + cast_specs,
        out_specs=[pl.BlockSpec((rt, LANES), lambda s, b, i: (b * nt + i, s))] + cast_specs,
        out_shape=([jax.ShapeDtypeStruct((bsz * seq, sw), F32)]
                   + [jax.ShapeDtypeStruct(w.shape, BF16) for w in cast_weights]),
        scratch_shapes=[pltpu.VMEM((kw, kw), BF16), pltpu.VMEM((kw, 2 * sl), BF16), pltpu.VMEM((kw, 2 * sl), BF16),
                        pltpu.VMEM((ncb, 2 * sl), F32), pltpu.VMEM((1, sl), F32), pltpu.VMEM((1, sl), F32)],
        compiler_params=_cparams(("arbitrary", "arbitrary", "arbitrary")),
        name="s5_scan",
    )(proj, bm, bcc, cct, a16r, a16i, *cast_weights)
    return outs[0], outs[1:]


def _ssm_post_kernel(y_ref, w_ref, b_ref, o_ref):
    y = y_ref[...]
    yg = 0.5 * y * (1.0 + lax.erf(y * (1.0 / math.sqrt(2.0))))
    s = jnp.dot(yg.astype(BF16), w_ref[...], preferred_element_type=F32) + b_ref[...]
    o_ref[...] = (yg * jax.nn.sigmoid(s)).astype(o_ref.dtype)


def _ssm_post(y, w_glu, b_glu):
    t, n = y.shape
    tm = _pick(t, 512)
    return pl.pallas_call(
        _ssm_post_kernel,
        grid=(t // tm,),
        in_specs=[pl.BlockSpec((tm, n), lambda i: (i, 0)),
                  pl.BlockSpec((n, n), lambda i: (0, 0)),
                  pl.BlockSpec((1, n), lambda i: (0, 0))],
        out_specs=pl.BlockSpec((tm, n), lambda i: (i, 0)),
        out_shape=jax.ShapeDtypeStruct((t, n), BF16),
        compiler_params=_cparams(("arbitrary",)),
        name="ssm_post",
    )(y, w_glu, b_glu.reshape(1, n))


def _dot_nt(a, b):
    return lax.dot_general(a, b, (((1,), (1,)), ((), ())), preferred_element_type=F32)


def _dot_tn(a, b):
    return lax.dot_general(a, b, (((0,), (0,)), ((), ())), preferred_element_type=F32)


def _dot_hi(a, b):
    return jnp.dot(a, b, preferred_element_type=F32, precision=HI)


def _sigmoid(x):
    return 0.5 * jnp.tanh(0.5 * x) + 0.5


def _silu(x):
    hx = 0.5 * x
    return hx * jnp.tanh(hx) + hx


DN_HALO = 8
DN_HEADS_PER_STEP = 8


def _dn_kernel(q_ref, k_ref, v_ref, z_ref, ab_ref, cq_ref, ck_ref, cv_ref, al_ref, dtb_ref, nw_ref,
               *rest, tt, heads, hp, ncast):
    cast_src, o_ref, cast_dst, scratch = rest[:ncast], rest[ncast], rest[ncast + 1:2 * ncast + 1], rest[2 * ncast + 1:]
    s_refs, xx_refs = scratch[:hp], scratch[hp:]

    @pl.when(pl.program_id(2) == 0)
    def _():
        for s_ref, xx_ref in zip(s_refs, xx_refs):
            s_ref[...] = jnp.zeros_like(s_ref)
            xx_ref[:, 0:DN_HALO, :] = jnp.zeros((3, DN_HALO, DN_HEAD_DIM), F32)

    ab = ab_ref[...]
    xa = ab + dtb_ref[...]
    sp = jnp.maximum(xa, 0.0) + jnp.log(1.0 + jnp.exp(-jnp.abs(xa)))
    gc_all = -jnp.exp(al_ref[...]) * sp
    pos = lax.broadcasted_iota(jnp.int32, gc_all.shape, 0) & (CHUNK - 1)
    sh = 1
    while sh < CHUNK:
        gc_all = gc_all + jnp.where(pos >= sh, pltpu.roll(gc_all, sh, axis=0), 0.0)
        sh *= 2
    gate_vals = (gc_all, jnp.exp(gc_all), _sigmoid(ab))
    gens = [_dn_head(hh, pl.program_id(1) * hp + hh, q_ref, k_ref, v_ref, z_ref, gate_vals, cq_ref, ck_ref, cv_ref,
                     nw_ref, o_ref, s_refs[hh], xx_refs[hh], tt=tt, heads=heads)
            for hh in range(hp)]
    if ncast:
        gens.append(_cast_slices(cast_src, cast_dst))
    for _ in itertools.zip_longest(*gens):
        pass


def _dn_head(hh, h, q_ref, k_ref, v_ref, z_ref, gate_vals, cq_ref, ck_ref, cv_ref, nw_ref,
             o_ref, s_ref, xx_ref, *, tt, heads):
    hal = DN_HALO
    ls = slice(hh * DN_HEAD_DIM, (hh + 1) * DN_HEAD_DIM)

    def conv(idx, x_ref, cw_ref):
        x = x_ref[:, ls]
        xx_ref[idx, hal:, :] = x
        w = cw_ref[:, ls]
        acc = w[CONV_WIDTH - 1:CONV_WIDTH, :] * x
        for j in range(CONV_WIDTH - 1):
            off = hal - (CONV_WIDTH - 1) + j
            acc = acc + w[j:j + 1, :] * xx_ref[idx, off:off + tt, :]
        xx_ref[idx, 0:hal, :] = x[tt - hal:, :]
        return _silu(acc)

    qc = conv(0, q_ref, cq_ref)
    kc = conv(1, k_ref, ck_ref)
    vc = conv(2, v_ref, cv_ref)
    qn = qc * lax.rsqrt(jnp.sum(qc * qc, axis=-1, keepdims=True) + NORM_EPS) * (DN_HEAD_DIM ** -0.5)
    kn = kc * lax.rsqrt(jnp.sum(kc * kc, axis=-1, keepdims=True) + NORM_EPS)

    gc_all, egc_all, beta_all = gate_vals
    lane = lax.broadcasted_iota(jnp.int32, gc_all.shape, 1)

    def pick(vals, ln):
        col = jnp.sum(jnp.where(lane == ln, vals, 0.0), axis=-1, keepdims=True)
        return jnp.broadcast_to(col, (tt, DN_HEAD_DIM))

    beta_col = pick(beta_all, h + heads)
    gc = pick(gc_all, h)
    egc = pick(egc_all, h)
    nch = tt // CHUNK
    gc_row = jnp.transpose(gc)[0:1, :]

    ri = lax.broadcasted_iota(jnp.int32, (tt, tt), 0)
    ci = lax.broadcasted_iota(jnp.int32, (tt, tt), 1)
    same = (ri // CHUNK) == (ci // CHUNK)
    causal = same & (ri >= ci)
    strict = same & (ri > ci)
    gc_wide = jnp.concatenate([gc] * (tt // DN_HEAD_DIM), axis=1)
    decay = jnp.where(causal, jnp.exp(jnp.where(causal, gc_wide - gc_row, 0.0)), 0.0)
    kb = kn * beta_col
    knb = kn.astype(BF16)
    yield
    a_raw = _dot_nt(kb.astype(BF16), knb)
    qk_raw = _dot_nt(qn.astype(BF16), knb)
    yield
    a_bd = jnp.where(strict, a_raw * decay, 0.0)
    qk_bd = jnp.where(causal, qk_raw * decay, 0.0).astype(BF16)

    def fold(m):
        out = m[0:CHUNK]
        for c in range(1, nch):
            out = out + m[c * CHUNK:(c + 1) * CHUNK]
        return out

    def spread(m):
        return jnp.where(same, jnp.concatenate([m] * nch, axis=0), 0.0)

    r64 = lax.broadcasted_iota(jnp.int32, (CHUNK, tt), 0)
    c64 = lax.broadcasted_iota(jnp.int32, (CHUNK, tt), 1)
    eye_cat = (r64 == (c64 & (CHUNK - 1))).astype(F32)
    pw_cat = fold(a_bd)
    inv_cat = eye_cat - pw_cat
    pw_bd = a_bd.astype(BF16)
    for _ in range(5):
        pw_cat = jnp.dot(pw_cat.astype(BF16), pw_bd, preferred_element_type=F32)
        yield
        pw_bd = spread(pw_cat).astype(BF16)
        inv_add = jnp.dot(inv_cat.astype(BF16), pw_bd, preferred_element_type=F32)
        yield
        inv_cat = inv_cat + inv_add
    inv_bd = spread(inv_cat).astype(BF16)
    rhs = jnp.concatenate([vc * beta_col, kb * egc], axis=1).astype(BF16)
    sol = jnp.dot(inv_bd, rhs, preferred_element_type=F32)
    yield
    u_all, w_all = sol[:, :DN_HEAD_DIM], sol[:, DN_HEAD_DIM:]
    qe = qn * egc

    outs = []
    zblk = jnp.zeros((CHUNK, DN_HEAD_DIM), BF16)
    s = s_ref[...]
    for c in range(nch):
        sl = slice(c * CHUNK, (c + 1) * CHUNK)
        wq = jnp.concatenate([w_all[sl], qe[sl]], axis=0).astype(BF16)
        ws = jnp.dot(wq, s.astype(BF16), preferred_element_type=F32)
        yield
        v_new = u_all[sl] - ws[:CHUNK]
        vnb = v_new.astype(BF16)
        v_pad = jnp.concatenate([zblk] * c + [vnb] + [zblk] * (nch - 1 - c), axis=0)
        gc_c = gc[sl]
        g_last = gc_c[CHUNK - 1:CHUNK, :]
        k_dec = (kn[sl] * jnp.exp(g_last - gc_c)).astype(BF16)
        s_add = _dot_tn(k_dec, vnb)
        o_add = jnp.dot(qk_bd[sl], v_pad, preferred_element_type=F32)
        yield
        s = s * jnp.exp(g_last) + s_add
        outs.append(ws[CHUNK:] + o_add)
    s_ref[...] = s
    o = jnp.concatenate(outs, axis=0)
    o = o * lax.rsqrt(jnp.mean(o * o, axis=-1, keepdims=True) + NORM_EPS) * nw_ref[...]
    o_ref[:, ls] = (o * _silu(z_ref[:, ls])).astype(o_ref.dtype)


def _deltanet(proj, ab, conv_w, a_log, dt_bias, norm_w, bsz, seq, heads, off_q, cast_weights=()):
    t = bsz * seq
    tt = _pick(seq, 256)
    nt = seq // tt
    hd = DN_HEAD_DIM
    cw = conv_w.reshape(CONV_WIDTH, 3 * heads * hd).astype(F32)
    pad = LANES - heads
    al = jnp.pad(a_log.astype(F32), (0, pad)).reshape(1, LANES)
    dtb = jnp.pad(dt_bias.astype(F32), (0, pad)).reshape(1, LANES)
    nw = norm_w.astype(F32).reshape(1, hd)

    hp = math.gcd(math.gcd(heads, off_q), DN_HEADS_PER_STEP)
    hw = hp * hd
    oq, nh = off_q // hp, heads // hp

    def act(o):
        return pl.BlockSpec((tt, hw), lambda b, h, i: (b * nt + i, oq + o * nh + h))

    def cws(o):
        return pl.BlockSpec((CONV_WIDTH, hw), lambda b, h, i: (0, o * nh + h))

    row = pl.BlockSpec((1, LANES), lambda b, h, i: (0, 0))
    ncast = len(cast_weights)
    cast_specs = _cast_specs(cast_weights, (bsz, nh, nt))
    kern = functools.partial(_dn_kernel, tt=tt, heads=heads, hp=hp, ncast=ncast)
    outs = pl.pallas_call(
        kern,
        grid=(bsz, nh, nt),
        in_specs=[act(0), act(1), act(2), act(3),
                  pl.BlockSpec((tt, LANES), lambda b, h, i: (b * nt + i, 0)),
                  cws(0), cws(1), cws(2), row, row, row] + cast_specs,
        out_specs=[pl.BlockSpec((tt, hw), lambda b, h, i: (b * nt + i, h))] + cast_specs,
        out_shape=([jax.ShapeDtypeStruct((t, heads * hd), BF16)]
                   + [jax.ShapeDtypeStruct(w.shape, BF16) for w in cast_weights]),
        scratch_shapes=([pltpu.VMEM((hd, hd), F32)] * hp
                        + [pltpu.VMEM((3, tt + DN_HALO, hd), F32)] * hp),
        compiler_params=_cparams(("arbitrary", "arbitrary", "arbitrary"), VMEM_LIMIT_DN),
        name="deltanet",
    )(proj, proj, proj, proj, ab, cw, cw, cw, al, dtb, nw, *cast_weights)
    return outs[0], outs[1:]


def _merge_kernel(ys_ref, yd_ref, gs_ref, gd_ref, ws_ref, wd_ref, o_ref):
    ps = jnp.dot(ys_ref[...], ws_ref[...], preferred_element_type=F32)
    pd = jnp.dot(yd_ref[...], wd_ref[...], preferred_element_type=F32)
    o_ref[...] = (jax.nn.sigmoid(gs_ref[...]) * ps + jax.nn.sigmoid(gd_ref[...]) * pd).astype(o_ref.dtype)


def _merge(ys, yd, gates, gate_col, w_ps, w_pd):
    t, ns = ys.shape
    d = w_ps.shape[1]
    tm = _pick(t, 512)
    tn = _pick(ns, 1024)
    nb = d // tn
    off = gate_col // tn
    assert gate_col % tn == 0
    return pl.pallas_call(
        _merge_kernel,
        grid=(nb, t // tm),
        in_specs=[pl.BlockSpec((tm, ns), lambda j, i: (i, 0)),
                  pl.BlockSpec((tm, d), lambda j, i: (i, 0)),
                  pl.BlockSpec((tm, tn), lambda j, i: (i, off + j)),
                  pl.BlockSpec((tm, tn), lambda j, i: (i, off + nb + j)),
                  pl.BlockSpec((ns, tn), lambda j, i: (0, j)),
                  pl.BlockSpec((d, tn), lambda j, i: (0, j))],
        out_specs=pl.BlockSpec((tm, tn), lambda j, i: (i, j)),
        out_shape=jax.ShapeDtypeStruct((t, d), BF16),
        compiler_params=_cparams(("arbitrary", "arbitrary")),
        name="merge",
    )(ys, yd, gates, gates, w_ps, w_pd)


def _pack_bf16_pair(x):
    half = x.shape[1] // 2
    lo = pltpu.bitcast(x[:, :half].astype(BF16).astype(F32), jnp.uint32)
    hi = pltpu.bitcast(x[:, half:].astype(BF16).astype(F32), jnp.uint32)
    return (lo >> 16) | (hi & jnp.uint32(0xFFFF0000))


def _unpack_bf16_pair(u):
    lo = pltpu.bitcast(u << 16, F32).astype(BF16)
    hi = pltpu.bitcast(u & jnp.uint32(0xFFFF0000), F32).astype(BF16)
    return lo, hi


def _out_ln_kernel(m_ref, w_ref, h_ref, g_ref, b_ref, h1_ref, hp_ref):
    mix = jnp.dot(m_ref[...], w_ref[...], preferred_element_type=F32)
    h1 = _ln(DEEPNORM_ALPHA * h_ref[...] + mix, g_ref[...], b_ref[...])
    h1_ref[...] = h1
    hp_ref[...] = _pack_bf16_pair(h1)


def _out_ln(merged, w_out, h, g, b):
    t, d = h.shape
    tm = _pick(t, 512)
    return pl.pallas_call(
        _out_ln_kernel,
        grid=(t // tm,),
        in_specs=[pl.BlockSpec((tm, d), lambda i: (i, 0)),
                  pl.BlockSpec((d, d), lambda i: (0, 0)),
                  pl.BlockSpec((tm, d), lambda i: (i, 0)),
                  pl.BlockSpec((1, d), lambda i: (0, 0)),
                  pl.BlockSpec((1, d), lambda i: (0, 0))],
        out_specs=[pl.BlockSpec((tm, d), lambda i: (i, 0)),
                   pl.BlockSpec((tm, d // 2), lambda i: (i, 0))],
        out_shape=[jax.ShapeDtypeStruct((t, d), F32), jax.ShapeDtypeStruct((t, d // 2), jnp.uint32)],
        compiler_params=_cparams(("arbitrary",)),
        name="out_ln1",
    )(merged, w_out, h, g.reshape(1, d), b.reshape(1, d))


def _router_kernel(h_ref, w_ref, b_ref, ir_ref, gate_ref, cnt_ref, base_ref, *, n_exp, tm):
    i = pl.program_id(0)

    @pl.when(i == 0)
    def _():
        base_ref[...] = jnp.zeros_like(base_ref)

    h = h_ref[...]
    w = w_ref[...]
    h_hi = h.astype(BF16)
    h_lo = (h - h_hi.astype(F32)).astype(BF16)
    w_hi = w.astype(BF16)
    w_lo = (w - w_hi.astype(F32)).astype(BF16)
    logits = (jnp.dot(h_hi, w_hi, preferred_element_type=F32) + jnp.dot(h_hi, w_lo, preferred_element_type=F32)
              + jnp.dot(h_lo, w_hi, preferred_element_type=F32) + b_ref[...])
    lane = lax.broadcasted_iota(jnp.int32, logits.shape, 1)
    vals = jnp.where(lane < n_exp, logits, -jnp.inf)
    sels, tops, ams = [], [], []
    for _ in range(TOP_K):
        m = jnp.max(vals, axis=-1, keepdims=True)
        am = jnp.min(jnp.where(vals == m, lane, LANES), axis=-1, keepdims=True)
        sel = lane == am
        vals = jnp.where(sel, -jnp.inf, vals)
        sels.append(sel)
        tops.append(m)
        ams.append(am)
    es = [jnp.exp(v - tops[0]) for v in tops]
    den = es[0] + es[1] + es[2] + es[3]
    onehot = sels[0] | sels[1] | sels[2] | sels[3]
    mt = onehot.astype(BF16)
    ri = lax.broadcasted_iota(jnp.int32, (tm, tm), 0)
    ci = lax.broadcasted_iota(jnp.int32, (tm, tm), 1)
    before = (ri > ci).astype(BF16)
    prior = jnp.dot(before, mt, preferred_element_type=F32) + base_ref[...]
    ir = jnp.zeros(logits.shape, F32)
    gate_o = jnp.zeros(logits.shape, F32)
    for k in range(TOP_K):
        rk = jnp.sum(jnp.where(sels[k], prior, 0.0), axis=-1, keepdims=True)
        ir = jnp.where(lane == k, ams[k].astype(F32), ir)
        ir = jnp.where(lane == TOP_K + k, rk, ir)
        gate_o = jnp.where(lane == k, es[k] / den, gate_o)
    ir_ref[...] = jnp.transpose(ir)[0:2 * TOP_K, :].astype(jnp.int32)
    gate_ref[...] = gate_o
    base_ref[...] += jnp.sum(onehot.astype(F32), axis=0, keepdims=True)
    cnt_ref[...] = base_ref[...]


def _router(h1, w_router, b_router):
    t, d = h1.shape
    n_exp = w_router.shape[1]
    tm = _pick(t, 512)
    w = jnp.pad(w_router.astype(F32), ((0, 0), (0, LANES - n_exp)))
    b = jnp.pad(b_router.astype(F32), (0, LANES - n_exp)).reshape(1, LANES)
    kern = functools.partial(_router_kernel, n_exp=n_exp, tm=tm)
    return pl.pallas_call(
        kern,
        grid=(t // tm,),
        in_specs=[pl.BlockSpec((tm, d), lambda i: (i, 0)),
                  pl.BlockSpec((d, LANES), lambda i: (0, 0)),
                  pl.BlockSpec((1, LANES), lambda i: (0, 0))],
        out_specs=[pl.BlockSpec((2 * TOP_K, tm), lambda i: (0, i)),
                   pl.BlockSpec((tm, LANES), lambda i: (i, 0)),
                   pl.BlockSpec((1, LANES), lambda i: (0, 0))],
        out_shape=[jax.ShapeDtypeStruct((2 * TOP_K, t), jnp.int32), jax.ShapeDtypeStruct((t, LANES), F32),
                   jax.ShapeDtypeStruct((1, LANES), F32)],
        scratch_shapes=[pltpu.VMEM((1, LANES), F32)],
        compiler_params=_cparams(("arbitrary",)),
        name="router",
    )(h1, w, b)


def _dispatch_kernel(dest_ref, pend_ref, nused_ref, x_ref, xs_ref, zero_ref, sem, zsem, *, tt, n_tok, n_exp, nblk):
    i = pl.program_id(0)
    base = i * tt

    @pl.when(i == 0)
    def _():
        zero_ref[...] = jnp.zeros_like(zero_ref)

        def zero_block(start):
            start = pl.multiple_of(start, ROW_BLOCK)
            return pltpu.make_async_copy(zero_ref, xs_ref.at[pl.ds(start, ROW_BLOCK)], zsem)

        def expert_has_rows(e):
            prev = jnp.where(e == 0, 0, pend_ref[jnp.maximum(e - 1, 0)])
            return pend_ref[e] > prev

        def start_expert(e, carry):
            @pl.when(expert_has_rows(e))
            def _():
                zero_block(pend_ref[e] - ROW_BLOCK).start()
            return carry

        def wait_expert(e, carry):
            @pl.when(expert_has_rows(e))
            def _():
                zero_block(0).wait()
            return carry

        def start_tail(b, carry):
            zero_block(b * ROW_BLOCK).start()
            return carry

        def wait_tail(b, carry):
            zero_block(0).wait()
            return carry

        lax.fori_loop(0, n_exp, start_expert, 0)
        lax.fori_loop(nused_ref[0], nblk, start_tail, 0)
        lax.fori_loop(0, n_exp, wait_expert, 0)
        lax.fori_loop(nused_ref[0], nblk, wait_tail, 0)

    def issue(r, carry):
        for k in range(TOP_K):
            row = dest_ref[k * n_tok + base + r]
            pltpu.make_async_copy(x_ref.at[pl.ds(r, 1)], xs_ref.at[pl.ds(row, 1)], sem).start()
        return carry

    lax.fori_loop(0, tt, issue, 0, unroll=4)

    for k in range(TOP_K):
        pltpu.make_async_copy(x_ref, xs_ref.at[pl.ds(0, tt)], sem).wait()


def _dispatch(xp, dest, pad_ends, n_used, n_rows):
    t, dh = xp.shape
    tt = _pick(t, 512)
    kern = functools.partial(_dispatch_kernel, tt=tt, n_tok=t, n_exp=pad_ends.shape[0], nblk=n_rows // ROW_BLOCK)
    return pl.pallas_call(
        kern,
        grid_spec=pltpu.PrefetchScalarGridSpec(
            num_scalar_prefetch=3,
            grid=(t // tt,),
            in_specs=[pl.BlockSpec((tt, dh), lambda i, *_: (i, 0))],
            out_specs=pl.BlockSpec(memory_space=pl.ANY),
            scratch_shapes=[pltpu.VMEM((ROW_BLOCK, dh), jnp.uint32),
                            pltpu.SemaphoreType.DMA(()), pltpu.SemaphoreType.DMA(())]),
        out_shape=jax.ShapeDtypeStruct((n_rows, dh), jnp.uint32),
        compiler_params=_cparams(("arbitrary",)),
        name="dispatch",
    )(dest, pad_ends.astype(jnp.int32), n_used, xp)


def _expert_kernel(be_ref, nb_ref, x_ref, wg_ref, wu_ref, bg_ref, bu_ref, wd_ref, bd_ref, o_ref, *, half):
    i = pl.program_id(0)
    j = pl.program_id(1)

    @pl.when(i < nb_ref[0])
    def _():
        lo, hi = _unpack_bf16_pair(x_ref[...])
        gate = (jnp.dot(lo, wg_ref[0, :half, :], preferred_element_type=F32)
                + jnp.dot(hi, wg_ref[0, half:, :], preferred_element_type=F32) + bg_ref[0])
        up = (jnp.dot(lo, wu_ref[0, :half, :], preferred_element_type=F32)
              + jnp.dot(hi, wu_ref[0, half:, :], preferred_element_type=F32) + bu_ref[0])
        gate = jnp.minimum(gate, SWIGLU_LIMIT)
        up = jnp.clip(up, -SWIGLU_LIMIT, SWIGLU_LIMIT)
        act = gate * _sigmoid(SWIGLU_ALPHA * gate) * (up + 1.0)
        part = jnp.dot(act.astype(BF16), wd_ref[0], preferred_element_type=F32)

        @pl.when(j == 0)
        def _():
            o_ref[...] = part + bd_ref[0]

        @pl.when(j > 0)
        def _():
            o_ref[...] += part

    @pl.when((i >= nb_ref[0]) & (j == 0))
    def _():
        o_ref[...] = jnp.zeros_like(o_ref)


def _experts(xs, block_expert, n_used, w_gu, b_gu, w_down, b_down):
    n_rows, dh = xs.shape
    n_exp, d, ff2 = w_gu.shape
    ff = ff2 // 2
    tf = _pick(ff, 1024)
    nj = ff // tf
    nblk = n_rows // ROW_BLOCK
    b_gu3 = b_gu.astype(F32).reshape(n_exp, 1, ff2)
    b_d3 = b_down.astype(F32).reshape(n_exp, 1, d)

    def blk(i, nb):
        return jnp.maximum(jnp.minimum(i, nb[0] - 1), 0)

    def jj(i, j, nb):
        return jnp.where(i < nb[0], j, nj - 1)

    kern = functools.partial(_expert_kernel, half=d // 2)
    return pl.pallas_call(
        kern,
        grid_spec=pltpu.PrefetchScalarGridSpec(
            num_scalar_prefetch=2,
            grid=(nblk, nj),
            in_specs=[
                pl.BlockSpec((ROW_BLOCK, dh), lambda i, j, be, nb: (blk(i, nb), 0)),
                pl.BlockSpec((1, d, tf), lambda i, j, be, nb: (be[blk(i, nb)], 0, jj(i, j, nb))),
                pl.BlockSpec((1, d, tf), lambda i, j, be, nb: (be[blk(i, nb)], 0, nj + jj(i, j, nb))),
                pl.BlockSpec((1, 1, tf), lambda i, j, be, nb: (be[blk(i, nb)], 0, jj(i, j, nb))),
                pl.BlockSpec((1, 1, tf), lambda i, j, be, nb: (be[blk(i, nb)], 0, nj + jj(i, j, nb))),
                pl.BlockSpec((1, tf, d), lambda i, j, be, nb: (be[blk(i, nb)], jj(i, j, nb), 0)),
                pl.BlockSpec((1, 1, d), lambda i, j, be, nb: (be[blk(i, nb)], 0, 0)),
            ],
            out_specs=pl.BlockSpec((ROW_BLOCK, d), lambda i, j, be, nb: (i, 0))),
        out_shape=jax.ShapeDtypeStruct((n_rows, d), F32),
        compiler_params=_cparams(("arbitrary", "arbitrary")),
        name="experts",
    )(block_expert, n_used, xs, w_gu, w_gu, b_gu3, b_gu3, w_down, b_d3)


def _combine_kernel(dest_ref, y_ref, h_ref, gate_ref, g_ref, b_ref, o_ref, buf_ref, sem, *, tt, n_tok):
    nsteps = n_tok // tt
    i = pl.program_id(0)
    cur = i % 2

    def gather_tile(tile, buf):
        base = tile * tt

        def issue(r, carry):
            for k in range(TOP_K):
                row = dest_ref[k * n_tok + base + r]
                pltpu.make_async_copy(y_ref.at[pl.ds(row, 1)], buf_ref.at[buf, k, pl.ds(r, 1)],
                                      sem.at[buf]).start()
            return carry

        lax.fori_loop(0, tt, issue, 0, unroll=4)

    @pl.when(i == 0)
    def _():
        gather_tile(0, 0)

    @pl.when(i + 1 < nsteps)
    def _():
        gather_tile(i + 1, 1 - cur)

    for k in range(TOP_K):
        pltpu.make_async_copy(y_ref.at[pl.ds(0, tt)], buf_ref.at[cur, k], sem.at[cur]).wait()
    gates = gate_ref[...]
    ffn = gates[:, 0:1] * buf_ref[cur, 0]
    for k in range(1, TOP_K):
        ffn = ffn + gates[:, k:k + 1] * buf_ref[cur, k]
    o_ref[...] = _ln(DEEPNORM_ALPHA * h_ref[...] + ffn, g_ref[...], b_ref[...])


def _combine(y, dest, h1, gates, g, b):
    t, d = h1.shape
    tt = _pick(t, 256)
    kern = functools.partial(_combine_kernel, tt=tt, n_tok=t)
    return pl.pallas_call(
        kern,
        grid_spec=pltpu.PrefetchScalarGridSpec(
            num_scalar_prefetch=1,
            grid=(t // tt,),
            in_specs=[pl.BlockSpec(memory_space=pl.ANY),
                      pl.BlockSpec((tt, d), lambda i, dst: (i, 0)),
                      pl.BlockSpec((tt, LANES), lambda i, dst: (i, 0)),
                      pl.BlockSpec((1, d), lambda i, dst: (0, 0)),
                      pl.BlockSpec((1, d), lambda i, dst: (0, 0))],
            out_specs=pl.BlockSpec((tt, d), lambda i, dst: (i, 0)),
            scratch_shapes=[pltpu.VMEM((2, TOP_K, tt, d), F32), pltpu.SemaphoreType.DMA((2,))]),
        out_shape=jax.ShapeDtypeStruct((t, d), F32),
        compiler_params=_cparams(("arbitrary",)),
        name="combine",
    )(dest, y, h1, gates, g.reshape(1, d), b.reshape(1, d))


def kernel(x, ln_in_g, ln_in_b, w_in, lam_re, lam_im, log_step, ssm_b_re, ssm_b_im, ssm_c_re, ssm_c_im, ssm_d, w_glu, b_glu, conv_w, a_log, dt_bias, dn_norm_w, w_proj_ssm, w_proj_dn, w_out, ln1_g, ln1_b, w_router, b_router, w_gate_up, b_gate_up, w_down, b_down, ln2_g, ln2_b):
    bsz, seq, d = x.shape
    t = bsz * seq
    sw = d // 2
    heads = d // DN_HEAD_DIM
    dn = heads * DN_HEAD_DIM
    groups = sw // SSM_GROUP_DIM
    n_state = lam_re.shape[-1]
    n_exp = w_router.shape[-1]
    assert w_in.shape[0] == DEPTH == 1
    assert seq % (CHUNK * 4) == 0 and sw % LANES == 0 and groups % 2 == 0 and 2 * heads <= LANES

    h, hb = _ln_in(x.reshape(t, d), ln_in_g, ln_in_b)

    c_ab = sw + 4 * dn
    assert c_ab % LANES == 0
    w_in_t = jnp.swapaxes(w_in, 1, 2)
    w_gates_t = w_in_t[0, c_ab + 2 * heads:, :].astype(BF16)
    proj = _proj2(hb, w_in_t, c_ab, w_gates_t, "proj_in")
    ab = _proj_rows(hb, w_in_t, c_ab, "proj_ab")

    ops = _s5_operators(lam_re[0], lam_im[0], log_step[0], ssm_b_re[0], ssm_b_im[0],
                        ssm_c_re[0], ssm_c_im[0], ssm_d[0])
    y_s, _ = _s5_scan(proj, ops, bsz, seq, sw, n_state)
    y_ssm = _ssm_post(y_s, w_glu[0].astype(BF16), b_glu[0].astype(F32))

    ff2 = w_gate_up.shape[-1]
    y_dn, (w_gu_bf, w_dn_bf) = _deltanet(
        proj, ab, conv_w[0], a_log[0], dt_bias[0], dn_norm_w[0], bsz, seq, heads, off_q=sw // LANES,
        cast_weights=(w_gate_up[0].reshape(n_exp * d, ff2), w_down[0].reshape(n_exp * (ff2 // 2), d)))
    w_gu_bf = w_gu_bf.reshape(n_exp, d, ff2)
    w_dn_bf = w_dn_bf.reshape(n_exp, ff2 // 2, d)

    merged = _merge(y_ssm, y_dn, proj, sw + 4 * dn, w_proj_ssm[0].astype(BF16), w_proj_dn[0].astype(BF16))
    h1, h1p = _out_ln(merged, w_out[0].astype(BF16), h, ln1_g[0], ln1_b[0])

    idx_rank, gates, cnt = _router(h1, w_router[0], b_router[0])
    counts = cnt[0, :n_exp].astype(jnp.int32)
    padded = (counts + ROW_BLOCK - 1) // ROW_BLOCK * ROW_BLOCK
    pad_ends = jnp.cumsum(padded)
    pad_starts = pad_ends - padded
    slot_expert = idx_rank[:TOP_K]
    is_e = slot_expert[None] == jnp.arange(n_exp, dtype=jnp.int32)[:, None, None]
    start = jnp.sum(jnp.where(is_e, pad_starts.astype(jnp.int32)[:, None, None], 0), axis=0)
    dest = (start + idx_rank[TOP_K:]).astype(jnp.int32).reshape(TOP_K * t)
    n_rows = t * TOP_K + n_exp * ROW_BLOCK
    nblk = n_rows // ROW_BLOCK
    blk_start = jnp.arange(nblk, dtype=jnp.int32) * ROW_BLOCK
    block_expert = jnp.minimum(jnp.sum((pad_ends[None, :] <= blk_start[:, None]).astype(jnp.int32), axis=1),
                               n_exp - 1).astype(jnp.int32)
    n_used = (pad_ends[-1:] // ROW_BLOCK).astype(jnp.int32)
    xs = _dispatch(h1p, dest, pad_ends, n_used, n_rows)
    y = _experts(xs, block_expert, n_used, w_gu_bf, b_gate_up[0], w_dn_bf, b_down[0])
    out = _combine(y, dest, h1, gates, ln2_g[0], ln2_b[0])
    return out.reshape(bsz, seq, d)
```

```python
import functools
import itertools
import math

import jax
import jax.numpy as jnp
from jax import lax
from jax.experimental import pallas as pl
from jax.experimental.pallas import tpu as pltpu

F32 = jnp.float32
BF16 = jnp.bfloat16
HI = lax.Precision.HIGHEST

LANES = 128
CHUNK = 64
LN_EPS = 1e-5
NORM_EPS = 1e-6
SSM_GROUP_DIM = 16
SSM_TC = 16
DN_HEAD_DIM = 128
CONV_WIDTH = 4
TOP_K = 4
SWIGLU_LIMIT = 7.0
SWIGLU_ALPHA = 1.702
DEPTH = 1
DEEPNORM_ALPHA = (2 * DEPTH) ** 0.25
ROW_BLOCK = 512
VMEM_LIMIT = 56 * 1024 * 1024
VMEM_LIMIT_DN = 62 * 1024 * 1024


def _cparams(sem, vmem_limit=VMEM_LIMIT):
    return pltpu.CompilerParams(dimension_semantics=sem, vmem_limit_bytes=vmem_limit)


def _pick(n, pref):
    t = min(n, pref)
    while n % t:
        t //= 2
    return t


def _ln(x, g, b):
    mu = jnp.mean(x, axis=-1, keepdims=True)
    xc = x - mu
    var = jnp.mean(xc * xc, axis=-1, keepdims=True)
    return xc * lax.rsqrt(var + LN_EPS) * g + b


def _ln_in_kernel(x_ref, g_ref, b_ref, h_ref, hb_ref):
    h = _ln(x_ref[...], g_ref[...], b_ref[...])
    h_ref[...] = h
    hb_ref[...] = h.astype(BF16)


def _ln_in(x, g, b):
    t, d = x.shape
    tm = _pick(t, 512)
    return pl.pallas_call(
        _ln_in_kernel,
        grid=(t // tm,),
        in_specs=[pl.BlockSpec((tm, d), lambda i: (i, 0)),
                  pl.BlockSpec((1, d), lambda i: (0, 0)),
                  pl.BlockSpec((1, d), lambda i: (0, 0))],
        out_specs=[pl.BlockSpec((tm, d), lambda i: (i, 0)),
                   pl.BlockSpec((tm, d), lambda i: (i, 0))],
        out_shape=[jax.ShapeDtypeStruct((t, d), F32), jax.ShapeDtypeStruct((t, d), BF16)],
        compiler_params=_cparams(("arbitrary",)),
        name="ln_in",
    )(x, g.reshape(1, d), b.reshape(1, d))


def _mm_kernel(x_ref, w_ref, o_ref):
    o_ref[...] = jnp.dot(x_ref[...], w_ref[...].astype(BF16), preferred_element_type=F32).astype(o_ref.dtype)


def _matmul(x, w, out_dtype, name, ncols=None):
    m, k = x.shape
    n = w.shape[1] if ncols is None else ncols
    tm = _pick(m, 1024)
    tn = _pick(n, 1024)
    return pl.pallas_call(
        _mm_kernel,
        grid=(n // tn, m // tm),
        in_specs=[pl.BlockSpec((tm, k), lambda j, i: (i, 0)),
                  pl.BlockSpec((k, tn), lambda j, i: (0, j))],
        out_specs=pl.BlockSpec((tm, tn), lambda j, i: (i, j)),
        out_shape=jax.ShapeDtypeStruct((m, n), out_dtype),
        compiler_params=_cparams(("arbitrary", "arbitrary")),
        name=name,
    )(x, w)


def _mm_nt_kernel(x_ref, wt_ref, o_ref):
    o_ref[...] = _dot_nt(x_ref[...], wt_ref[...].astype(BF16)).astype(o_ref.dtype)


def _proj_rows(x, wt3, row0, name):
    m, k = x.shape
    tm = _pick(m, 1024)
    assert row0 % LANES == 0
    return pl.pallas_call(
        _mm_nt_kernel,
        grid=(m // tm,),
        in_specs=[pl.BlockSpec((tm, k), lambda i: (i, 0)),
                  pl.BlockSpec((None, LANES, k), lambda i: (0, row0 // LANES, 0))],
        out_specs=pl.BlockSpec((tm, LANES), lambda i: (i, 0)),
        out_shape=jax.ShapeDtypeStruct((m, LANES), F32),
        compiler_params=_cparams(("arbitrary",)),
        name=name,
    )(x, wt3)


def _proj2_kernel(x_ref, wa_ref, wb_ref, o_ref, wc_ref, *, na):
    j = pl.program_id(0)

    @pl.when((j < na) & (pl.program_id(1) == 0))
    def _():
        wc_ref[...] = jnp.transpose(wa_ref[...]).astype(BF16)

    @pl.when(j < na)
    def _():
        o_ref[...] = jnp.dot(x_ref[...], wc_ref[...], preferred_element_type=F32)

    @pl.when(j >= na)
    def _():
        o_ref[...] = _dot_nt(x_ref[...], wb_ref[...])


def _proj2(x, wat, nrows_a, wbt, name):
    m, k = x.shape
    nb = wbt.shape[0]
    tm = _pick(m, 1024)
    tn = _pick(math.gcd(nrows_a, nb), 1024)
    na, ng = nrows_a // tn, nb // tn
    return pl.pallas_call(
        functools.partial(_proj2_kernel, na=na),
        grid=(na + ng, m // tm),
        in_specs=[pl.BlockSpec((tm, k), lambda j, i: (i, 0)),
                  pl.BlockSpec((None, tn, k), lambda j, i: (0, jnp.minimum(j, na - 1), 0)),
                  pl.BlockSpec((tn, k), lambda j, i: (jnp.maximum(j - na, 0), 0))],
        out_specs=pl.BlockSpec((tm, tn), lambda j, i: (i, j)),
        out_shape=jax.ShapeDtypeStruct((m, nrows_a + nb), F32),
        scratch_shapes=[pltpu.VMEM((k, tn), BF16)],
        compiler_params=_cparams(("arbitrary", "arbitrary")),
        name=name,
    )(x, wat, wbt)


CAST_SLICES = 16


def _cast_slices(src_refs, dst_refs):
    for sl in range(CAST_SLICES):
        for src, dst in zip(src_refs, dst_refs):
            n = src.shape[0] // CAST_SLICES
            dst[sl * n:(sl + 1) * n, :] = src[sl * n:(sl + 1) * n, :].astype(BF16)
        yield


def _cast_specs(weights, grid):
    g0, g1, g2 = grid
    nsteps = g0 * g1 * g2
    specs = []
    for w in weights:
        rows, cols = w.shape
        assert rows % (nsteps * CAST_SLICES * 16) == 0, (rows, nsteps)
        specs.append(pl.BlockSpec((rows // nsteps, cols), lambda a, b, c: ((a * g1 + b) * g2 + c, 0)))
    return specs


def _s5_operators(lam_re, lam_im, log_step, b_re, b_im, c_re, c_im, d_skip):
    g, p = lam_re.shape
    hd, tc = SSM_GROUP_DIM, SSM_TC
    lr, li = lam_re.astype(F32), lam_im.astype(F32)
    step = jnp.exp(log_step.astype(F32))[:, None]
    tau = jnp.arange(tc + 1, dtype=F32)[:, None, None]
    mag = jnp.exp(lr * step * tau)
    pr, pi = mag * jnp.cos(li * step * tau), mag * jnp.sin(li * step * tau)
    a_re, a_im = pr[1], pi[1]
    den = lr * lr + li * li
    nr, ni = a_re - 1.0, a_im
    f_re = (nr * lr + ni * li) / den
    f_im = (ni * lr - nr * li) / den
    br, bi = b_re.astype(F32), b_im.astype(F32)
    bb_re = f_re[..., None] * br - f_im[..., None] * bi
    bb_im = f_re[..., None] * bi + f_im[..., None] * br
    cr, ci = c_re.astype(F32), c_im.astype(F32)
    ca_re = cr[None] * pr[:, :, None, :] - ci[None] * pi[:, :, None, :]
    ca_im = cr[None] * pi[:, :, None, :] + ci[None] * pr[:, :, None, :]
    ca_g = jnp.concatenate([jnp.transpose(ca_re[:tc], (1, 0, 2, 3)).reshape(g, tc * hd, p),
                            -jnp.transpose(ca_im[:tc], (1, 0, 2, 3)).reshape(g, tc * hd, p)], axis=2)
    bb_g = jnp.concatenate([bb_re, bb_im], axis=1)
    kk = jnp.einsum('gxp,gpi->gxi', ca_g, bb_g, precision=HI)
    kk = jnp.transpose(kk.reshape(g, tc, hd, hd), (1, 0, 2, 3))
    kk = kk.at[0].add(d_skip.astype(F32)[:, :, None] * jnp.eye(hd, dtype=F32)[None])
    gs = LANES // hd
    ns = g // gs
    bm = jnp.transpose(kk.reshape(tc, ns, gs, hd, hd), (1, 0, 2, 4, 3)).reshape(ns, tc, LANES, hd)
    dt = jnp.arange(tc)

    def rows(x_re, x_im):
        x = jnp.stack([x_re, x_im], axis=1).reshape(tc, 2, ns, gs * hd, p)
        return jnp.transpose(x, (2, 0, 1, 3, 4))

    prr, pir = pr[tc - 1 - dt], pi[tc - 1 - dt]
    bc_re = prr[:, :, None, :] * jnp.transpose(bb_re, (0, 2, 1))[None] - pir[:, :, None, :] * jnp.transpose(bb_im, (0, 2, 1))[None]
    bc_im = prr[:, :, None, :] * jnp.transpose(bb_im, (0, 2, 1))[None] + pir[:, :, None, :] * jnp.transpose(bb_re, (0, 2, 1))[None]
    bcc = rows(bc_re, bc_im)
    cct = rows(ca_re[1:], -ca_im[1:])
    a16r = pr[tc].reshape(1, g * p)
    a16i = pi[tc].reshape(1, g * p)
    return (bm.astype(BF16), bcc.astype(BF16), cct.astype(BF16), a16r, a16i)


def _s5_kernel(x_ref, bm_ref, bcc_ref, cct_ref, ar_ref, ai_ref, *rest, ncb, sl, n_state, ncast):
    cast_src, y_ref, cast_dst = rest[:ncast], rest[ncast], rest[ncast + 1:2 * ncast + 1]
    mf_ref, bcf_ref, ccf_ref, st_ref, cr_ref, ci_ref = rest[2 * ncast + 1:]
    tc = SSM_TC
    for _ in _cast_slices(cast_src, cast_dst):
        pass

    @pl.when((pl.program_id(1) == 0) & (pl.program_id(2) == 0))
    def _():
        def block_diag(blk, width):
            n = blk.shape[1]
            rep = (lax.broadcasted_iota(jnp.int32, (n, width), 0)
                   == lax.broadcasted_iota(jnp.int32, (n, width), 1) % n).astype(BF16)
            wide = jnp.dot(blk, rep, preferred_element_type=F32)
            r = lax.broadcasted_iota(jnp.int32, (LANES, width), 0)
            c = lax.broadcasted_iota(jnp.int32, (LANES, width), 1)
            return jnp.where((r // SSM_GROUP_DIM) == (c // n), wide, 0.0).astype(BF16)

        zero = jnp.zeros((LANES, LANES), BF16)
        tiles = [block_diag(bm_ref[0, tau], LANES) for tau in range(tc)]
        for di in range(tc):
            for do in range(tc):
                mf_ref[di * LANES:(di + 1) * LANES, do * LANES:(do + 1) * LANES] = (
                    tiles[do - di] if do >= di else zero)

        def widen(blk):
            return block_diag(blk, sl)

        for dt in range(tc):
            for ri in range(2):
                bcf_ref[dt * LANES:(dt + 1) * LANES, ri * sl:(ri + 1) * sl] = widen(bcc_ref[0, dt, ri])
                ccf_ref[dt * LANES:(dt + 1) * LANES, ri * sl:(ri + 1) * sl] = widen(cct_ref[0, dt, ri])

    @pl.when(pl.program_id(2) == 0)
    def _():
        cr_ref[...] = jnp.zeros_like(cr_ref)
        ci_ref[...] = jnp.zeros_like(ci_ref)

    xs = jnp.concatenate([x_ref[pl.ds(dt, ncb, stride=SSM_TC), :] for dt in range(SSM_TC)],
                         axis=1).astype(BF16)
    st_ref[...] = jnp.dot(xs, bcf_ref[...], preferred_element_type=F32)
    ar = ar_ref[...]
    ai = ai_ref[...]

    def body(c, carry):
        sr, si = carry
        lr = st_ref[pl.ds(c, 1), :sl]
        li = st_ref[pl.ds(c, 1), sl:]
        st_ref[pl.ds(c, 1), :sl] = sr
        st_ref[pl.ds(c, 1), sl:] = si
        return ar * sr - ai * si + lr, ar * si + ai * sr + li

    sr, si = lax.fori_loop(0, ncb, body, (cr_ref[...], ci_ref[...]), unroll=8)
    cr_ref[...] = sr
    ci_ref[...] = si
    y = (jnp.dot(xs, mf_ref[...], preferred_element_type=F32)
         + _dot_nt(st_ref[...].astype(BF16), ccf_ref[...]))
    for dt in range(SSM_TC):
        y_ref[pl.ds(dt, ncb, stride=SSM_TC), :] = y[:, dt * LANES:(dt + 1) * LANES]


def _s5_scan(proj, ops, bsz, seq, sw, n_state, cast_weights=()):
    bm, bcc, cct, a16r, a16i = ops
    ns = sw // LANES
    sl = (LANES // SSM_GROUP_DIM) * n_state
    ncb = _pick(seq // SSM_TC, 512)
    rt = ncb * SSM_TC
    nt = seq // rt
    kw = SSM_TC * LANES
    cast_specs = _cast_specs(cast_weights, (ns, bsz, nt))
    kern = functools.partial(_s5_kernel, ncb=ncb, sl=sl, n_state=n_state, ncast=len(cast_weights))
    outs = pl.pallas_call(
        kern,
        grid=(ns, bsz, nt),
        in_specs=[pl.BlockSpec((rt, LANES), lambda s, b, i: (b * nt + i, s)),
                  pl.BlockSpec((1, SSM_TC, LANES, SSM_GROUP_DIM), lambda s, b, i: (s, 0, 0, 0)),
                  pl.BlockSpec((1, SSM_TC, 2, LANES, n_state), lambda s, b, i: (s, 0, 0, 0, 0)),
                  pl.BlockSpec((1, SSM_TC, 2, LANES, n_state), lambda s, b, i: (s, 0, 0, 0, 0)),
                  pl.BlockSpec((1, sl), lambda s, b, i: (0, s)),
                  pl.BlockSpec((1, sl), lambda s, b, i: (0, s))] + cast_specs,
        out_specs=[pl.BlockSpec((rt, LANES), lambda s, b, i: (b * nt + i, s))] + cast_specs,
        out_shape=([jax.ShapeDtypeStruct((bsz * seq, sw), F32)]
                   + [jax.ShapeDtypeStruct(w.shape, BF16) for w in cast_weights]),
        scratch_shapes=[pltpu.VMEM((kw, kw), BF16), pltpu.VMEM((kw, 2 * sl), BF16), pltpu.VMEM((kw, 2 * sl), BF16),
                        pltpu.VMEM((ncb, 2 * sl), F32), pltpu.VMEM((1, sl), F32), pltpu.VMEM((1, sl), F32)],
        compiler_params=_cparams(("arbitrary", "arbitrary", "arbitrary")),
        name="s5_scan",
    )(proj, bm, bcc, cct, a16r, a16i, *cast_weights)
    return outs[0], outs[1:]


def _ssm_post_kernel(y_ref, w_ref, b_ref, o_ref):
    y = y_ref[...]
    yg = 0.5 * y * (1.0 + lax.erf(y * (1.0 / math.sqrt(2.0))))
    s = jnp.dot(yg.astype(BF16), w_ref[...], preferred_element_type=F32) + b_ref[...]
    o_ref[...] = (yg * jax.nn.sigmoid(s)).astype(o_ref.dtype)


def _ssm_post(y, w_glu, b_glu):
    t, n = y.shape
    tm = _pick(t, 512)
    return pl.pallas_call(
        _ssm_post_kernel,
        grid=(t // tm,),
        in_specs=[pl.BlockSpec((tm, n), lambda i: (i, 0)),
                  pl.BlockSpec((n, n), lambda i: (0, 0)),
                  pl.BlockSpec((1, n), lambda i: (0, 0))],
        out_specs=pl.BlockSpec((tm, n), lambda i: (i, 0)),
        out_shape=jax.ShapeDtypeStruct((t, n), BF16),
        compiler_params=_cparams(("arbitrary",)),
        name="ssm_post",
    )(y, w_glu, b_glu.reshape(1, n))


def _dot_nt(a, b):
    return lax.dot_general(a, b, (((1,), (1,)), ((), ())), preferred_element_type=F32)


def _dot_tn(a, b):
    return lax.dot_general(a, b, (((0,), (0,)), ((), ())), preferred_element_type=F32)


def _dot_hi(a, b):
    return jnp.dot(a, b, preferred_element_type=F32, precision=HI)


def _sigmoid(x):
    return 0.5 * jnp.tanh(0.5 * x) + 0.5


def _silu(x):
    hx = 0.5 * x
    return hx * jnp.tanh(hx) + hx


DN_HALO = 8
DN_HEADS_PER_STEP = 8


def _dn_kernel(q_ref, k_ref, v_ref, z_ref, ab_ref, cq_ref, ck_ref, cv_ref, al_ref, dtb_ref, nw_ref,
               *rest, tt, heads, hp, ncast):
    cast_src, o_ref, cast_dst, scratch = rest[:ncast], rest[ncast], rest[ncast + 1:2 * ncast + 1], rest[2 * ncast + 1:]
    s_refs, xx_refs = scratch[:hp], scratch[hp:]

    @pl.when(pl.program_id(2) == 0)
    def _():
        for s_ref, xx_ref in zip(s_refs, xx_refs):
            s_ref[...] = jnp.zeros_like(s_ref)
            xx_ref[:, 0:DN_HALO, :] = jnp.zeros((3, DN_HALO, DN_HEAD_DIM), F32)

    ab = ab_ref[...]
    xa = ab + dtb_ref[...]
    sp = jnp.maximum(xa, 0.0) + jnp.log(1.0 + jnp.exp(-jnp.abs(xa)))
    gc_all = -jnp.exp(al_ref[...]) * sp
    pos = lax.broadcasted_iota(jnp.int32, gc_all.shape, 0) & (CHUNK - 1)
    sh = 1
    while sh < CHUNK:
        gc_all = gc_all + jnp.where(pos >= sh, pltpu.roll(gc_all, sh, axis=0), 0.0)
        sh *= 2
    gate_vals = (gc_all, jnp.exp(gc_all), _sigmoid(ab))
    gens = [_dn_head(hh, pl.program_id(1) * hp + hh, q_ref, k_ref, v_ref, z_ref, gate_vals, cq_ref, ck_ref, cv_ref,
                     nw_ref, o_ref, s_refs[hh], xx_refs[hh], tt=tt, heads=heads)
            for hh in range(hp)]
    if ncast:
        gens.append(_cast_slices(cast_src, cast_dst))
    for _ in itertools.zip_longest(*gens):
        pass


def _dn_head(hh, h, q_ref, k_ref, v_ref, z_ref, gate_vals, cq_ref, ck_ref, cv_ref, nw_ref,
             o_ref, s_ref, xx_ref, *, tt, heads):
    hal = DN_HALO
    ls = slice(hh * DN_HEAD_DIM, (hh + 1) * DN_HEAD_DIM)

    def conv(idx, x_ref, cw_ref):
        x = x_ref[:, ls]
        xx_ref[idx, hal:, :] = x
        w = cw_ref[:, ls]
        acc = w[CONV_WIDTH - 1:CONV_WIDTH, :] * x
        for j in range(CONV_WIDTH - 1):
            off = hal - (CONV_WIDTH - 1) + j
            acc = acc + w[j:j + 1, :] * xx_ref[idx, off:off + tt, :]
        xx_ref[idx, 0:hal, :] = x[tt - hal:, :]
        return _silu(acc)

    qc = conv(0, q_ref, cq_ref)
    kc = conv(1, k_ref, ck_ref)
    vc = conv(2, v_ref, cv_ref)
    qn = qc * lax.rsqrt(jnp.sum(qc * qc, axis=-1, keepdims=True) + NORM_EPS) * (DN_HEAD_DIM ** -0.5)
    kn = kc * lax.rsqrt(jnp.sum(kc * kc, axis=-1, keepdims=True) + NORM_EPS)

    gc_all, egc_all, beta_all = gate_vals
    lane = lax.broadcasted_iota(jnp.int32, gc_all.shape, 1)

    def pick(vals, ln):
        col = jnp.sum(jnp.where(lane == ln, vals, 0.0), axis=-1, keepdims=True)
        return jnp.broadcast_to(col, (tt, DN_HEAD_DIM))

    beta_col = pick(beta_all, h + heads)
    gc = pick(gc_all, h)
    egc = pick(egc_all, h)
    nch = tt // CHUNK
    gc_row = jnp.transpose(gc)[0:1, :]

    ri = lax.broadcasted_iota(jnp.int32, (tt, tt), 0)
    ci = lax.broadcasted_iota(jnp.int32, (tt, tt), 1)
    same = (ri // CHUNK) == (ci // CHUNK)
    causal = same & (ri >= ci)
    strict = same & (ri > ci)
    gc_wide = jnp.concatenate([gc] * (tt // DN_HEAD_DIM), axis=1)
    decay = jnp.where(causal, jnp.exp(jnp.where(causal, gc_wide - gc_row, 0.0)), 0.0)
    kb = kn * beta_col
    knb = kn.astype(BF16)
    yield
    a_raw = _dot_nt(kb.astype(BF16), knb)
    qk_raw = _dot_nt(qn.astype(BF16), knb)
    yield
    a_bd = jnp.where(strict, a_raw * decay, 0.0)
    qk_bd = jnp.where(causal, qk_raw * decay, 0.0).astype(BF16)

    def fold(m):
        out = m[0:CHUNK]
        for c in range(1, nch):
            out = out + m[c * CHUNK:(c + 1) * CHUNK]
        return out

    def spread(m):
        return jnp.where(same, jnp.concatenate([m] * nch, axis=0), 0.0)

    r64 = lax.broadcasted_iota(jnp.int32, (CHUNK, tt), 0)
    c64 = lax.broadcasted_iota(jnp.int32, (CHUNK, tt), 1)
    eye_cat = (r64 == (c64 & (CHUNK - 1))).astype(F32)
    pw_cat = fold(a_bd)
    inv_cat = eye_cat - pw_cat
    pw_bd = a_bd.astype(BF16)
    for _ in range(5):
        pw_cat = jnp.dot(pw_cat.astype(BF16), pw_bd, preferred_element_type=F32)
        yield
        pw_bd = spread(pw_cat).astype(BF16)
        inv_add = jnp.dot(inv_cat.astype(BF16), pw_bd, preferred_element_type=F32)
        yield
        inv_cat = inv_cat + inv_add
    inv_bd = spread(inv_cat).astype(BF16)
    rhs = jnp.concatenate([vc * beta_col, kb * egc], axis=1).astype(BF16)
    sol = jnp.dot(inv_bd, rhs, preferred_element_type=F32)
    yield
    u_all, w_all = sol[:, :DN_HEAD_DIM], sol[:, DN_HEAD_DIM:]
    qe = qn * egc

    outs = []
    zblk = jnp.zeros((CHUNK, DN_HEAD_DIM), BF16)
    s = s_ref[...]
    for c in range(nch):
        sl = slice(c * CHUNK, (c + 1) * CHUNK)
        wq = jnp.concatenate([w_all[sl], qe[sl]], axis=0).astype(BF16)
        ws = jnp.dot(wq, s.astype(BF16), preferred_element_type=F32)
        yield
        v_new = u_all[sl] - ws[:CHUNK]
        vnb = v_new.astype(BF16)
        v_pad = jnp.concatenate([zblk] * c + [vnb] + [zblk] * (nch - 1 - c), axis=0)
        gc_c = gc[sl]
        g_last = gc_c[CHUNK - 1:CHUNK, :]
        k_dec = (kn[sl] * jnp.exp(g_last - gc_c)).astype(BF16)
        s_add = _dot_tn(k_dec, vnb)
        o_add = jnp.dot(qk_bd[sl], v_pad, preferred_element_type=F32)
        yield
        s = s * jnp.exp(g_last) + s_add
        outs.append(ws[CHUNK:] + o_add)
    s_ref[...] = s
    o = jnp.concatenate(outs, axis=0)
    o = o * lax.rsqrt(jnp.mean(o * o, axis=-1, keepdims=True) + NORM_EPS) * nw_ref[...]
    o_ref[:, ls] = (o * _silu(z_ref[:, ls])).astype(o_ref.dtype)


def _deltanet(proj, ab, conv_w, a_log, dt_bias, norm_w, bsz, seq, heads, off_q, cast_weights=()):
    t = bsz * seq
    tt = _pick(seq, 256)
    nt = seq // tt
    hd = DN_HEAD_DIM
    cw = conv_w.reshape(CONV_WIDTH, 3 * heads * hd).astype(F32)
    pad = LANES - heads
    al = jnp.pad(a_log.astype(F32), (0, pad)).reshape(1, LANES)
    dtb = jnp.pad(dt_bias.astype(F32), (0, pad)).reshape(1, LANES)
    nw = norm_w.astype(F32).reshape(1, hd)

    hp = math.gcd(math.gcd(heads, off_q), DN_HEADS_PER_STEP)
    hw = hp * hd
    oq, nh = off_q // hp, heads // hp

    def act(o):
        return pl.BlockSpec((tt, hw), lambda b, h, i: (b * nt + i, oq + o * nh + h))

    def cws(o):
        return pl.BlockSpec((CONV_WIDTH, hw), lambda b, h, i: (0, o * nh + h))

    row = pl.BlockSpec((1, LANES), lambda b, h, i: (0, 0))
    ncast = len(cast_weights)
    cast_specs = _cast_specs(cast_weights, (bsz, nh, nt))
    kern = functools.partial(_dn_kernel, tt=tt, heads=heads, hp=hp, ncast=ncast)
    outs = pl.pallas_call(
        kern,
        grid=(bsz, nh, nt),
        in_specs=[act(0), act(1), act(2), act(3),
                  pl.BlockSpec((tt, LANES), lambda b, h, i: (b * nt + i, 0)),
                  cws(0), cws(1), cws(2), row, row, row] + cast_specs,
        out_specs=[pl.BlockSpec((tt, hw), lambda b, h, i: (b * nt + i, h))] + cast_specs,
        out_shape=([jax.ShapeDtypeStruct((t, heads * hd), BF16)]
                   + [jax.ShapeDtypeStruct(w.shape, BF16) for w in cast_weights]),
        scratch_shapes=([pltpu.VMEM((hd, hd), F32)] * hp
                        + [pltpu.VMEM((3, tt + DN_HALO, hd), F32)] * hp),
        compiler_params=_cparams(("arbitrary", "arbitrary", "arbitrary"), VMEM_LIMIT_DN),
        name="deltanet",
    )(proj, proj, proj, proj, ab, cw, cw, cw, al, dtb, nw, *cast_weights)
    return outs[0], outs[1:]


def _merge_kernel(ys_ref, yd_ref, gs_ref, gd_ref, ws_ref, wd_ref, o_ref):
    ps = jnp.dot(ys_ref[...], ws_ref[...], preferred_element_type=F32)
    pd = jnp.dot(yd_ref[...], wd_ref[...], preferred_element_type=F32)
    o_ref[...] = (jax.nn.sigmoid(gs_ref[...]) * ps + jax.nn.sigmoid(gd_ref[...]) * pd).astype(o_ref.dtype)


def _merge(ys, yd, gates, gate_col, w_ps, w_pd):
    t, ns = ys.shape
    d = w_ps.shape[1]
    tm = _pick(t, 512)
    tn = _pick(ns, 1024)
    nb = d // tn
    off = gate_col // tn
    assert gate_col % tn == 0
    return pl.pallas_call(
        _merge_kernel,
        grid=(nb, t // tm),
        in_specs=[pl.BlockSpec((tm, ns), lambda j, i: (i, 0)),
                  pl.BlockSpec((tm, d), lambda j, i: (i, 0)),
                  pl.BlockSpec((tm, tn), lambda j, i: (i, off + j)),
                  pl.BlockSpec((tm, tn), lambda j, i: (i, off + nb + j)),
                  pl.BlockSpec((ns, tn), lambda j, i: (0, j)),
                  pl.BlockSpec((d, tn), lambda j, i: (0, j))],
        out_specs=pl.BlockSpec((tm, tn), lambda j, i: (i, j)),
        out_shape=jax.ShapeDtypeStruct((t, d), BF16),
        compiler_params=_cparams(("arbitrary", "arbitrary")),
        name="merge",
    )(ys, yd, gates, gates, w_ps, w_pd)


def _pack_bf16_pair(x):
    half = x.shape[1] // 2
    lo = pltpu.bitcast(x[:, :half].astype(BF16).astype(F32), jnp.uint32)
    hi = pltpu.bitcast(x[:, half:].astype(BF16).astype(F32), jnp.uint32)
    return (lo >> 16) | (hi & jnp.uint32(0xFFFF0000))


def _unpack_bf16_pair(u):
    lo = pltpu.bitcast(u << 16, F32).astype(BF16)
    hi = pltpu.bitcast(u & jnp.uint32(0xFFFF0000), F32).astype(BF16)
    return lo, hi


def _out_ln_kernel(m_ref, w_ref, h_ref, g_ref, b_ref, h1_ref, hp_ref):
    mix = jnp.dot(m_ref[...], w_ref[...], preferred_element_type=F32)
    h1 = _ln(DEEPNORM_ALPHA * h_ref[...] + mix, g_ref[...], b_ref[...])
    h1_ref[...] = h1
    hp_ref[...] = _pack_bf16_pair(h1)


def _out_ln(merged, w_out, h, g, b):
    t, d = h.shape
    tm = _pick(t, 512)
    return pl.pallas_call(
        _out_ln_kernel,
        grid=(t // tm,),
        in_specs=[pl.BlockSpec((tm, d), lambda i: (i, 0)),
                  pl.BlockSpec((d, d), lambda i: (0, 0)),
                  pl.BlockSpec((tm, d), lambda i: (i, 0)),
                  pl.BlockSpec((1, d), lambda i: (0, 0)),
                  pl.BlockSpec((1, d), lambda i: (0, 0))],
        out_specs=[pl.BlockSpec((tm, d), lambda i: (i, 0)),
                   pl.BlockSpec((tm, d // 2), lambda i: (i, 0))],
        out_shape=[jax.ShapeDtypeStruct((t, d), F32), jax.ShapeDtypeStruct((t, d // 2), jnp.uint32)],
        compiler_params=_cparams(("arbitrary",)),
        name="out_ln1",
    )(merged, w_out, h, g.reshape(1, d), b.reshape(1, d))


def _router_kernel(h_ref, w_ref, b_ref, ir_ref, gate_ref, cnt_ref, base_ref, *, n_exp, tm):
    i = pl.program_id(0)

    @pl.when(i == 0)
    def _():
        base_ref[...] = jnp.zeros_like(base_ref)

    h = h_ref[...]
    w = w_ref[...]
    h_hi = h.astype(BF16)
    h_lo = (h - h_hi.astype(F32)).astype(BF16)
    w_hi = w.astype(BF16)
    w_lo = (w - w_hi.astype(F32)).astype(BF16)
    logits = (jnp.dot(h_hi, w_hi, preferred_element_type=F32) + jnp.dot(h_hi, w_lo, preferred_element_type=F32)
              + jnp.dot(h_lo, w_hi, preferred_element_type=F32) + b_ref[...])
    lane = lax.broadcasted_iota(jnp.int32, logits.shape, 1)
    vals = jnp.where(lane < n_exp, logits, -jnp.inf)
    sels, tops, ams = [], [], []
    for _ in range(TOP_K):
        m = jnp.max(vals, axis=-1, keepdims=True)
        am = jnp.min(jnp.where(vals == m, lane, LANES), axis=-1, keepdims=True)
        sel = lane == am
        vals = jnp.where(sel, -jnp.inf, vals)
        sels.append(sel)
        tops.append(m)
        ams.append(am)
    es = [jnp.exp(v - tops[0]) for v in tops]
    den = es[0] + es[1] + es[2] + es[3]
    onehot = sels[0] | sels[1] | sels[2] | sels[3]
    mt = onehot.astype(BF16)
    ri = lax.broadcasted_iota(jnp.int32, (tm, tm), 0)
    ci = lax.broadcasted_iota(jnp.int32, (tm, tm), 1)
    before = (ri > ci).astype(BF16)
    prior = jnp.dot(before, mt, preferred_element_type=F32) + base_ref[...]
    ir = jnp.zeros(logits.shape, F32)
    gate_o = jnp.zeros(logits.shape, F32)
    for k in range(TOP_K):
        rk = jnp.sum(jnp.where(sels[k], prior, 0.0), axis=-1, keepdims=True)
        ir = jnp.where(lane == k, ams[k].astype(F32), ir)
        ir = jnp.where(lane == TOP_K + k, rk, ir)
        gate_o = jnp.where(lane == k, es[k] / den, gate_o)
    ir_ref[...] = jnp.transpose(ir)[0:2 * TOP_K, :].astype(jnp.int32)
    gate_ref[...] = gate_o
    base_ref[...] += jnp.sum(onehot.astype(F32), axis=0, keepdims=True)
    cnt_ref[...] = base_ref[...]


def _router(h1, w_router, b_router):
    t, d = h1.shape
    n_exp = w_router.shape[1]
    tm = _pick(t, 512)
    w = jnp.pad(w_router.astype(F32), ((0, 0), (0, LANES - n_exp)))
    b = jnp.pad(b_router.astype(F32), (0, LANES - n_exp)).reshape(1, LANES)
    kern = functools.partial(_router_kernel, n_exp=n_exp, tm=tm)
    return pl.pallas_call(
        kern,
        grid=(t // tm,),
        in_specs=[pl.BlockSpec((tm, d), lambda i: (i, 0)),
                  pl.BlockSpec((d, LANES), lambda i: (0, 0)),
                  pl.BlockSpec((1, LANES), lambda i: (0, 0))],
        out_specs=[pl.BlockSpec((2 * TOP_K, tm), lambda i: (0, i)),
                   pl.BlockSpec((tm, LANES), lambda i: (i, 0)),
                   pl.BlockSpec((1, LANES), lambda i: (0, 0))],
        out_shape=[jax.ShapeDtypeStruct((2 * TOP_K, t), jnp.int32), jax.ShapeDtypeStruct((t, LANES), F32),
                   jax.ShapeDtypeStruct((1, LANES), F32)],
        scratch_shapes=[pltpu.VMEM((1, LANES), F32)],
        compiler_params=_cparams(("arbitrary",)),
        name="router",
    )(h1, w, b)


def _dispatch_kernel(dest_ref, pend_ref, nused_ref, x_ref, xs_ref, zero_ref, sem, zsem, *, tt, n_tok, n_exp, nblk):
    i = pl.program_id(0)
    base = i * tt

    @pl.when(i == 0)
    def _():
        zero_ref[...] = jnp.zeros_like(zero_ref)

        def zero_block(start):
            start = pl.multiple_of(start, ROW_BLOCK)
            return pltpu.make_async_copy(zero_ref, xs_ref.at[pl.ds(start, ROW_BLOCK)], zsem)

        def expert_has_rows(e):
            prev = jnp.where(e == 0, 0, pend_ref[jnp.maximum(e - 1, 0)])
            return pend_ref[e] > prev

        def start_expert(e, carry):
            @pl.when(expert_has_rows(e))
            def _():
                zero_block(pend_ref[e] - ROW_BLOCK).start()
            return carry

        def wait_expert(e, carry):
            @pl.when(expert_has_rows(e))
            def _():
                zero_block(0).wait()
            return carry

        def start_tail(b, carry):
            zero_block(b * ROW_BLOCK).start()
            return carry

        def wait_tail(b, carry):
            zero_block(0).wait()
            return carry

        lax.fori_loop(0, n_exp, start_expert, 0)
        lax.fori_loop(nused_ref[0], nblk, start_tail, 0)
        lax.fori_loop(0, n_exp, wait_expert, 0)
        lax.fori_loop(nused_ref[0], nblk, wait_tail, 0)

    def issue(r, carry):
        for k in range(TOP_K):
            row = dest_ref[k * n_tok + base + r]
            pltpu.make_async_copy(x_ref.at[pl.ds(r, 1)], xs_ref.at[pl.ds(row, 1)], sem).start()
        return carry

    lax.fori_loop(0, tt, issue, 0, unroll=4)

    for k in range(TOP_K):
        pltpu.make_async_copy(x_ref, xs_ref.at[pl.ds(0, tt)], sem).wait()


def _dispatch(xp, dest, pad_ends, n_used, n_rows):
    t, dh = xp.shape
    tt = _pick(t, 512)
    kern = functools.partial(_dispatch_kernel, tt=tt, n_tok=t, n_exp=pad_ends.shape[0], nblk=n_rows // ROW_BLOCK)
    return pl.pallas_call(
        kern,
        grid_spec=pltpu.PrefetchScalarGridSpec(
            num_scalar_prefetch=3,
            grid=(t // tt,),
            in_specs=[pl.BlockSpec((tt, dh), lambda i, *_: (i, 0))],
            out_specs=pl.BlockSpec(memory_space=pl.ANY),
            scratch_shapes=[pltpu.VMEM((ROW_BLOCK, dh), jnp.uint32),
                            pltpu.SemaphoreType.DMA(()), pltpu.SemaphoreType.DMA(())]),
        out_shape=jax.ShapeDtypeStruct((n_rows, dh), jnp.uint32),
        compiler_params=_cparams(("arbitrary",)),
        name="dispatch",
    )(dest, pad_ends.astype(jnp.int32), n_used, xp)


def _expert_kernel(be_ref, nb_ref, x_ref, wg_ref, wu_ref, bg_ref, bu_ref, wd_ref, bd_ref, o_ref, *, half):
    i = pl.program_id(0)
    j = pl.program_id(1)

    @pl.when(i < nb_ref[0])
    def _():
        lo, hi = _unpack_bf16_pair(x_ref[...])
        gate = (jnp.dot(lo, wg_ref[0, :half, :], preferred_element_type=F32)
                + jnp.dot(hi, wg_ref[0, half:, :], preferred_element_type=F32) + bg_ref[0])
        up = (jnp.dot(lo, wu_ref[0, :half, :], preferred_element_type=F32)
              + jnp.dot(hi, wu_ref[0, half:, :], preferred_element_type=F32) + bu_ref[0])
        gate = jnp.minimum(gate, SWIGLU_LIMIT)
        up = jnp.clip(up, -SWIGLU_LIMIT, SWIGLU_LIMIT)
        act = gate * _sigmoid(SWIGLU_ALPHA * gate) * (up + 1.0)
        part = jnp.dot(act.astype(BF16), wd_ref[0], preferred_element_type=F32)

        @pl.when(j == 0)
        def _():
            o_ref[...] = part + bd_ref[0]

        @pl.when(j > 0)
        def _():
            o_ref[...] += part

    @pl.when((i >= nb_ref[0]) & (j == 0))
    def _():
        o_ref[...] = jnp.zeros_like(o_ref)


def _experts(xs, block_expert, n_used, w_gu, b_gu, w_down, b_down):
    n_rows, dh = xs.shape
    n_exp, d, ff2 = w_gu.shape
    ff = ff2 // 2
    tf = _pick(ff, 1024)
    nj = ff // tf
    nblk = n_rows // ROW_BLOCK
    b_gu3 = b_gu.astype(F32).reshape(n_exp, 1, ff2)
    b_d3 = b_down.astype(F32).reshape(n_exp, 1, d)

    def blk(i, nb):
        return jnp.maximum(jnp.minimum(i, nb[0] - 1), 0)

    def jj(i, j, nb):
        return jnp.where(i < nb[0], j, nj - 1)

    kern = functools.partial(_expert_kernel, half=d // 2)
    return pl.pallas_call(
        kern,
        grid_spec=pltpu.PrefetchScalarGridSpec(
            num_scalar_prefetch=2,
            grid=(nblk, nj),
            in_specs=[
                pl.BlockSpec((ROW_BLOCK, dh), lambda i, j, be, nb: (blk(i, nb), 0)),
                pl.BlockSpec((1, d, tf), lambda i, j, be, nb: (be[blk(i, nb)], 0, jj(i, j, nb))),
                pl.BlockSpec((1, d, tf), lambda i, j, be, nb: (be[blk(i, nb)], 0, nj + jj(i, j, nb))),
                pl.BlockSpec((1, 1, tf), lambda i, j, be, nb: (be[blk(i, nb)], 0, jj(i, j, nb))),
                pl.BlockSpec((1, 1, tf), lambda i, j, be, nb: (be[blk(i, nb)], 0, nj + jj(i, j, nb))),
                pl.BlockSpec((1, tf, d), lambda i, j, be, nb: (be[blk(i, nb)], jj(i, j, nb), 0)),
                pl.BlockSpec((1, 1, d), lambda i, j, be, nb: (be[blk(i, nb)], 0, 0)),
            ],
            out_specs=pl.BlockSpec((ROW_BLOCK, d), lambda i, j, be, nb: (i, 0))),
        out_shape=jax.ShapeDtypeStruct((n_rows, d), F32),
        compiler_params=_cparams(("arbitrary", "arbitrary")),
        name="experts",
    )(block_expert, n_used, xs, w_gu, w_gu, b_gu3, b_gu3, w_down, b_d3)


def _combine_kernel(dest_ref, y_ref, h_ref, gate_ref, g_ref, b_ref, o_ref, buf_ref, sem, *, tt, n_tok):
    nsteps = n_tok // tt
    i = pl.program_id(0)
    cur = i % 2

    def gather_tile(tile, buf):
        base = tile * tt

        def issue(r, carry):
            for k in range(TOP_K):
                row = dest_ref[k * n_tok + base + r]
                pltpu.make_async_copy(y_ref.at[pl.ds(row, 1)], buf_ref.at[buf, k, pl.ds(r, 1)],
                                      sem.at[buf]).start()
            return carry

        lax.fori_loop(0, tt, issue, 0, unroll=4)

    @pl.when(i == 0)
    def _():
        gather_tile(0, 0)

    @pl.when(i + 1 < nsteps)
    def _():
        gather_tile(i + 1, 1 - cur)

    for k in range(TOP_K):
        pltpu.make_async_copy(y_ref.at[pl.ds(0, tt)], buf_ref.at[cur, k], sem.at[cur]).wait()
    gates = gate_ref[...]
    ffn = gates[:, 0:1] * buf_ref[cur, 0]
    for k in range(1, TOP_K):
        ffn = ffn + gates[:, k:k + 1] * buf_ref[cur, k]
    o_ref[...] = _ln(DEEPNORM_ALPHA * h_ref[...] + ffn, g_ref[...], b_ref[...])


def _combine(y, dest, h1, gates, g, b):
    t, d = h1.shape
    tt = _pick(t, 256)
    kern = functools.partial(_combine_kernel, tt=tt, n_tok=t)
    return pl.pallas_call(
        kern,
        grid_spec=pltpu.PrefetchScalarGridSpec(
            num_scalar_prefetch=1,
            grid=(t // tt,),
            in_specs=[pl.BlockSpec(memory_space=pl.ANY),
                      pl.BlockSpec((tt, d), lambda i, dst: (i, 0)),
                      pl.BlockSpec((tt, LANES), lambda i, dst: (i, 0)),
                      pl.BlockSpec((1, d), lambda i, dst: (0, 0)),
                      pl.BlockSpec((1, d), lambda i, dst: (0, 0))],
            out_specs=pl.BlockSpec((tt, d), lambda i, dst: (i, 0)),
            scratch_shapes=[pltpu.VMEM((2, TOP_K, tt, d), F32), pltpu.SemaphoreType.DMA((2,))]),
        out_shape=jax.ShapeDtypeStruct((t, d), F32),
        compiler_params=_cparams(("arbitrary",)),
        name="combine",
    )(dest, y, h1, gates, g.reshape(1, d), b.reshape(1, d))


def kernel(x, ln_in_g, ln_in_b, w_in, lam_re, lam_im, log_step, ssm_b_re, ssm_b_im, ssm_c_re, ssm_c_im, ssm_d, w_glu, b_glu, conv_w, a_log, dt_bias, dn_norm_w, w_proj_ssm, w_proj_dn, w_out, ln1_g, ln1_b, w_router, b_router, w_gate_up, b_gate_up, w_down, b_down, ln2_g, ln2_b):
    bsz, seq, d = x.shape
    t = bsz * seq
    sw = d // 2
    heads = d // DN_HEAD_DIM
    dn = heads * DN_HEAD_DIM
    groups = sw // SSM_GROUP_DIM
    n_state = lam_re.shape[-1]
    n_exp = w_router.shape[-1]
    assert w_in.shape[0] == DEPTH == 1
    assert seq % (CHUNK * 4) == 0 and sw % LANES == 0 and groups % 2 == 0 and 2 * heads <= LANES

    h, hb = _ln_in(x.reshape(t, d), ln_in_g, ln_in_b)

    c_ab = sw + 4 * dn
    assert c_ab % LANES == 0
    w_in_t = jnp.swapaxes(w_in, 1, 2)
    w_gates_t = w_in_t[0, c_ab + 2 * heads:, :].astype(BF16)
    proj = _proj2(hb, w_in_t, c_ab, w_gates_t, "proj_in")
    ab = _proj_rows(hb, w_in_t, c_ab, "proj_ab")

    ops = _s5_operators(lam_re[0], lam_im[0], log_step[0], ssm_b_re[0], ssm_b_im[0],
                        ssm_c_re[0], ssm_c_im[0], ssm_d[0])
    y_s, _ = _s5_scan(proj, ops, bsz, seq, sw, n_state)
    y_ssm = _ssm_post(y_s, w_glu[0].astype(BF16), b_glu[0].astype(F32))

    ff2 = w_gate_up.shape[-1]
    y_dn, (w_gu_bf, w_dn_bf) = _deltanet(
        proj, ab, conv_w[0], a_log[0], dt_bias[0], dn_norm_w[0], bsz, seq, heads, off_q=sw // LANES,
        cast_weights=(w_gate_up[0].reshape(n_exp * d, ff2), w_down[0].reshape(n_exp * (ff2 // 2), d)))
    w_gu_bf = w_gu_bf.reshape(n_exp, d, ff2)
    w_dn_bf = w_dn_bf.reshape(n_exp, ff2 // 2, d)

    merged = _merge(y_ssm, y_dn, proj, sw + 4 * dn, w_proj_ssm[0].astype(BF16), w_proj_dn[0].astype(BF16))
    h1, h1p = _out_ln(merged, w_out[0].astype(BF16), h, ln1_g[0], ln1_b[0])

    idx_rank, gates, cnt = _router(h1, w_router[0], b_router[0])
    counts = cnt[0, :n_exp].astype(jnp.int32)
    padded = (counts + ROW_BLOCK - 1) // ROW_BLOCK * ROW_BLOCK
    pad_ends = jnp.cumsum(padded)
    pad_starts = pad_ends - padded
    slot_expert = idx_rank[:TOP_K]
    is_e = slot_expert[None] == jnp.arange(n_exp, dtype=jnp.int32)[:, None, None]
    start = jnp.sum(jnp.where(is_e, pad_starts.astype(jnp.int32)[:, None, None], 0), axis=0)
    dest = (start + idx_rank[TOP_K:]).astype(jnp.int32).reshape(TOP_K * t)
    n_rows = t * TOP_K + n_exp * ROW_BLOCK
    nblk = n_rows // ROW_BLOCK
    blk_start = jnp.arange(nblk, dtype=jnp.int32) * ROW_BLOCK
    block_expert = jnp.minimum(jnp.sum((pad_ends[None, :] <= blk_start[:, None]).astype(jnp.int32), axis=1),
                               n_exp - 1).astype(jnp.int32)
    n_used = (pad_ends[-1:] // ROW_BLOCK).astype(jnp.int32)
    xs = _dispatch(h1p, dest, pad_ends, n_used, n_rows)
    y = _experts(xs, block_expert, n_used, w_gu_bf, b_gate_up[0], w_dn_bf, b_down[0])
    out = _combine(y, dest, h1, gates, ln2_g[0], ln2_b[0])
    return out.reshape(bsz, seq, d)
```

```python
import functools
import itertools
import math

import jax
import jax.numpy as jnp
from jax import lax
from jax.experimental import pallas as pl
from jax.experimental.pallas import tpu as pltpu

F32 = jnp.float32
BF16 = jnp.bfloat16
HI = lax.Precision.HIGHEST

LANES = 128
CHUNK = 64
LN_EPS = 1e-5
NORM_EPS = 1e-6
SSM_GROUP_DIM = 16
SSM_TC = 16
DN_HEAD_DIM = 128
CONV_WIDTH = 4
TOP_K = 4
SWIGLU_LIMIT = 7.0
SWIGLU_ALPHA = 1.702
DEPTH = 1
DEEPNORM_ALPHA = (2 * DEPTH) ** 0.25
ROW_BLOCK = 512
VMEM_LIMIT = 56 * 1024 * 1024
VMEM_LIMIT_DN = 62 * 1024 * 1024


def _cparams(sem, vmem_limit=VMEM_LIMIT):
    return pltpu.CompilerParams(dimension_semantics=sem, vmem_limit_bytes=vmem_limit)


def _pick(n, pref):
    t = min(n, pref)
    while n % t:
        t //= 2
    return t


def _ln(x, g, b):
    mu = jnp.mean(x, axis=-1, keepdims=True)
    xc = x - mu
    var = jnp.mean(xc * xc, axis=-1, keepdims=True)
    return xc * lax.rsqrt(var + LN_EPS) * g + b


def _ln_in_kernel(x_ref, g_ref, b_ref, h_ref, hb_ref):
    h = _ln(x_ref[...], g_ref[...], b_ref[...])
    h_ref[...] = h
    hb_ref[...] = h.astype(BF16)


def _ln_in(x, g, b):
    t, d = x.shape
    tm = _pick(t, 512)
    return pl.pallas_call(
        _ln_in_kernel,
        grid=(t // tm,),
        in_specs=[pl.BlockSpec((tm, d), lambda i: (i, 0)),
                  pl.BlockSpec((1, d), lambda i: (0, 0)),
                  pl.BlockSpec((1, d), lambda i: (0, 0))],
        out_specs=[pl.BlockSpec((tm, d), lambda i: (i, 0)),
                   pl.BlockSpec((tm, d), lambda i: (i, 0))],
        out_shape=[jax.ShapeDtypeStruct((t, d), F32), jax.ShapeDtypeStruct((t, d), BF16)],
        compiler_params=_cparams(("arbitrary",)),
        name="ln_in",
    )(x, g.reshape(1, d), b.reshape(1, d))


def _mm_kernel(x_ref, w_ref, o_ref):
    o_ref[...] = jnp.dot(x_ref[...], w_ref[...].astype(BF16), preferred_element_type=F32).astype(o_ref.dtype)


def _matmul(x, w, out_dtype, name, ncols=None):
    m, k = x.shape
    n = w.shape[1] if ncols is None else ncols
    tm = _pick(m, 1024)
    tn = _pick(n, 1024)
    return pl.pallas_call(
        _mm_kernel,
        grid=(n // tn, m // tm),
        in_specs=[pl.BlockSpec((tm, k), lambda j, i: (i, 0)),
                  pl.BlockSpec((k, tn), lambda j, i: (0, j))],
        out_specs=pl.BlockSpec((tm, tn), lambda j, i: (i, j)),
        out_shape=jax.ShapeDtypeStruct((m, n), out_dtype),
        compiler_params=_cparams(("arbitrary", "arbitrary")),
        name=name,
    )(x, w)


def _mm_nt_kernel(x_ref, wt_ref, o_ref):
    o_ref[...] = _dot_nt(x_ref[...], wt_ref[...].astype(BF16)).astype(o_ref.dtype)


def _proj_rows(x, wt3, row0, name):
    m, k = x.shape
    tm = _pick(m, 1024)
    assert row0 % LANES == 0
    return pl.pallas_call(
        _mm_nt_kernel,
        grid=(m // tm,),
        in_specs=[pl.BlockSpec((tm, k), lambda i: (i, 0)),
                  pl.BlockSpec((None, LANES, k), lambda i: (0, row0 // LANES, 0))],
        out_specs=pl.BlockSpec((tm, LANES), lambda i: (i, 0)),
        out_shape=jax.ShapeDtypeStruct((m, LANES), F32),
        compiler_params=_cparams(("arbitrary",)),
        name=name,
    )(x, wt3)


def _proj2_kernel(x_ref, wa_ref, wb_ref, o_ref, *, na):
    j = pl.program_id(0)

    @pl.when(j < na)
    def _():
        o_ref[...] = _dot_nt(x_ref[...], wa_ref[...].astype(BF16))

    @pl.when(j >= na)
    def _():
        o_ref[...] = _dot_nt(x_ref[...], wb_ref[...])


def _proj2(x, wat, nrows_a, wbt, name):
    m, k = x.shape
    nb = wbt.shape[0]
    tm = _pick(m, 1024)
    tn = _pick(math.gcd(nrows_a, nb), 1024)
    na, ng = nrows_a // tn, nb // tn
    return pl.pallas_call(
        functools.partial(_proj2_kernel, na=na),
        grid=(na + ng, m // tm),
        in_specs=[pl.BlockSpec((tm, k), lambda j, i: (i, 0)),
                  pl.BlockSpec((None, tn, k), lambda j, i: (0, jnp.minimum(j, na - 1), 0)),
                  pl.BlockSpec((tn, k), lambda j, i: (jnp.maximum(j - na, 0), 0))],
        out_specs=pl.BlockSpec((tm, tn), lambda j, i: (i, j)),
        out_shape=jax.ShapeDtypeStruct((m, nrows_a + nb), F32),
        compiler_params=_cparams(("arbitrary", "arbitrary")),
        name=name,
    )(x, wat, wbt)


CAST_SLICES = 16


def _cast_slices(src_refs, dst_refs):
    for sl in range(CAST_SLICES):
        for src, dst in zip(src_refs, dst_refs):
            n = src.shape[0] // CAST_SLICES
            dst[sl * n:(sl + 1) * n, :] = src[sl * n:(sl + 1) * n, :].astype(BF16)
        yield


def _cast_specs(weights, grid):
    g0, g1, g2 = grid
    nsteps = g0 * g1 * g2
    specs = []
    for w in weights:
        rows, cols = w.shape
        assert rows % (nsteps * CAST_SLICES * 16) == 0, (rows, nsteps)
        specs.append(pl.BlockSpec((rows // nsteps, cols), lambda a, b, c: ((a * g1 + b) * g2 + c, 0)))
    return specs


def _s5_operators(lam_re, lam_im, log_step, b_re, b_im, c_re, c_im, d_skip):
    g, p = lam_re.shape
    hd, tc = SSM_GROUP_DIM, SSM_TC
    lr, li = lam_re.astype(F32), lam_im.astype(F32)
    step = jnp.exp(log_step.astype(F32))[:, None]
    tau = jnp.arange(tc + 1, dtype=F32)[:, None, None]
    mag = jnp.exp(lr * step * tau)
    pr, pi = mag * jnp.cos(li * step * tau), mag * jnp.sin(li * step * tau)
    a_re, a_im = pr[1], pi[1]
    den = lr * lr + li * li
    nr, ni = a_re - 1.0, a_im
    f_re = (nr * lr + ni * li) / den
    f_im = (ni * lr - nr * li) / den
    br, bi = b_re.astype(F32), b_im.astype(F32)
    bb_re = f_re[..., None] * br - f_im[..., None] * bi
    bb_im = f_re[..., None] * bi + f_im[..., None] * br
    cr, ci = c_re.astype(F32), c_im.astype(F32)
    ca_re = cr[None] * pr[:, :, None, :] - ci[None] * pi[:, :, None, :]
    ca_im = cr[None] * pi[:, :, None, :] + ci[None] * pr[:, :, None, :]
    ca_g = jnp.concatenate([jnp.transpose(ca_re[:tc], (1, 0, 2, 3)).reshape(g, tc * hd, p),
                            -jnp.transpose(ca_im[:tc], (1, 0, 2, 3)).reshape(g, tc * hd, p)], axis=2)
    bb_g = jnp.concatenate([bb_re, bb_im], axis=1)
    kk = jnp.einsum('gxp,gpi->gxi', ca_g, bb_g, precision=HI)
    kk = jnp.transpose(kk.reshape(g, tc, hd, hd), (1, 0, 2, 3))
    kk = kk.at[0].add(d_skip.astype(F32)[:, :, None] * jnp.eye(hd, dtype=F32)[None])
    gs = LANES // hd
    ns = g // gs
    bm = jnp.transpose(kk.reshape(tc, ns, gs, hd, hd), (1, 0, 2, 4, 3)).reshape(ns, tc, LANES, hd)
    dt = jnp.arange(tc)

    def rows(x_re, x_im):
        x = jnp.stack([x_re, x_im], axis=1).reshape(tc, 2, ns, gs * hd, p)
        return jnp.transpose(x, (2, 0, 1, 3, 4))

    prr, pir = pr[tc - 1 - dt], pi[tc - 1 - dt]
    bc_re = prr[:, :, None, :] * jnp.transpose(bb_re, (0, 2, 1))[None] - pir[:, :, None, :] * jnp.transpose(bb_im, (0, 2, 1))[None]
    bc_im = prr[:, :, None, :] * jnp.transpose(bb_im, (0, 2, 1))[None] + pir[:, :, None, :] * jnp.transpose(bb_re, (0, 2, 1))[None]
    bcc = rows(bc_re, bc_im)
    cct = rows(ca_re[1:], -ca_im[1:])
    a16r = pr[tc].reshape(1, g * p)
    a16i = pi[tc].reshape(1, g * p)
    return (bm.astype(BF16), bcc.astype(BF16), cct.astype(BF16), a16r, a16i)


def _s5_kernel(x_ref, bm_ref, bcc_ref, cct_ref, ar_ref, ai_ref, *rest, ncb, sl, n_state, ncast):
    cast_src, y_ref, cast_dst = rest[:ncast], rest[ncast], rest[ncast + 1:2 * ncast + 1]
    mf_ref, bcf_ref, ccf_ref, st_ref, cr_ref, ci_ref = rest[2 * ncast + 1:]
    tc = SSM_TC
    for _ in _cast_slices(cast_src, cast_dst):
        pass

    @pl.when((pl.program_id(1) == 0) & (pl.program_id(2) == 0))
    def _():
        def block_diag(blk, width):
            n = blk.shape[1]
            rep = (lax.broadcasted_iota(jnp.int32, (n, width), 0)
                   == lax.broadcasted_iota(jnp.int32, (n, width), 1) % n).astype(BF16)
            wide = jnp.dot(blk, rep, preferred_element_type=F32)
            r = lax.broadcasted_iota(jnp.int32, (LANES, width), 0)
            c = lax.broadcasted_iota(jnp.int32, (LANES, width), 1)
            return jnp.where((r // SSM_GROUP_DIM) == (c // n), wide, 0.0).astype(BF16)

        zero = jnp.zeros((LANES, LANES), BF16)
        tiles = [block_diag(bm_ref[0, tau], LANES) for tau in range(tc)]
        for di in range(tc):
            for do in range(tc):
                mf_ref[di * LANES:(di + 1) * LANES, do * LANES:(do + 1) * LANES] = (
                    tiles[do - di] if do >= di else zero)

        def widen(blk):
            return block_diag(blk, sl)

        for dt in range(tc):
            for ri in range(2):
                bcf_ref[dt * LANES:(dt + 1) * LANES, ri * sl:(ri + 1) * sl] = widen(bcc_ref[0, dt, ri])
                ccf_ref[dt * LANES:(dt + 1) * LANES, ri * sl:(ri + 1) * sl] = widen(cct_ref[0, dt, ri])

    @pl.when(pl.program_id(2) == 0)
    def _():
        cr_ref[...] = jnp.zeros_like(cr_ref)
        ci_ref[...] = jnp.zeros_like(ci_ref)

    xs = jnp.concatenate([x_ref[pl.ds(dt, ncb, stride=SSM_TC), :] for dt in range(SSM_TC)],
                         axis=1).astype(BF16)
    st_ref[...] = jnp.dot(xs, bcf_ref[...], preferred_element_type=F32)
    ar = ar_ref[...]
    ai = ai_ref[...]

    def body(c, carry):
        sr, si = carry
        lr = st_ref[pl.ds(c, 1), :sl]
        li = st_ref[pl.ds(c, 1), sl:]
        st_ref[pl.ds(c, 1), :sl] = sr
        st_ref[pl.ds(c, 1), sl:] = si
        return ar * sr - ai * si + lr, ar * si + ai * sr + li

    sr, si = lax.fori_loop(0, ncb, body, (cr_ref[...], ci_ref[...]), unroll=8)
    cr_ref[...] = sr
    ci_ref[...] = si
    y = (jnp.dot(xs, mf_ref[...], preferred_element_type=F32)
         + _dot_nt(st_ref[...].astype(BF16), ccf_ref[...]))
    for dt in range(SSM_TC):
        y_ref[pl.ds(dt, ncb, stride=SSM_TC), :] = y[:, dt * LANES:(dt + 1) * LANES]


def _s5_scan(proj, ops, bsz, seq, sw, n_state, cast_weights=()):
    bm, bcc, cct, a16r, a16i = ops
    ns = sw // LANES
    sl = (LANES // SSM_GROUP_DIM) * n_state
    ncb = _pick(seq // SSM_TC, 512)
    rt = ncb * SSM_TC
    nt = seq // rt
    kw = SSM_TC * LANES
    cast_specs = _cast_specs(cast_weights, (ns, bsz, nt))
    kern = functools.partial(_s5_kernel, ncb=ncb, sl=sl, n_state=n_state, ncast=len(cast_weights))
    outs = pl.pallas_call(
        kern,
        grid=(ns, bsz, nt),
        in_specs=[pl.BlockSpec((rt, LANES), lambda s, b, i: (b * nt + i, s)),
                  pl.BlockSpec((1, SSM_TC, LANES, SSM_GROUP_DIM), lambda s, b, i: (s, 0, 0, 0)),
                  pl.BlockSpec((1, SSM_TC, 2, LANES, n_state), lambda s, b, i: (s, 0, 0, 0, 0)),
                  pl.BlockSpec((1, SSM_TC, 2, LANES, n_state), lambda s, b, i: (s, 0, 0, 0, 0)),
                  pl.BlockSpec((1, sl), lambda s, b, i: (0, s)),
                  pl.BlockSpec((1, sl), lambda s, b, i: (0, s))] + cast_specs,
        out_specs=[pl.BlockSpec((rt, LANES), lambda s, b, i: (b * nt + i, s))] + cast_specs,
        out_shape=([jax.ShapeDtypeStruct((bsz * seq, sw), F32)]
                   + [jax.ShapeDtypeStruct(w.shape, BF16) for w in cast_weights]),
        scratch_shapes=[pltpu.VMEM((kw, kw), BF16), pltpu.VMEM((kw, 2 * sl), BF16), pltpu.VMEM((kw, 2 * sl), BF16),
                        pltpu.VMEM((ncb, 2 * sl), F32), pltpu.VMEM((1, sl), F32), pltpu.VMEM((1, sl), F32)],
        compiler_params=_cparams(("arbitrary", "arbitrary", "arbitrary")),
        name="s5_scan",
    )(proj, bm, bcc, cct, a16r, a16i, *cast_weights)
    return outs[0], outs[1:]


def _ssm_post_kernel(y_ref, w_ref, b_ref, o_ref):
    y = y_ref[...]
    yg = 0.5 * y * (1.0 + lax.erf(y * (1.0 / math.sqrt(2.0))))
    s = jnp.dot(yg.astype(BF16), w_ref[...], preferred_element_type=F32) + b_ref[...]
    o_ref[...] = (yg * jax.nn.sigmoid(s)).astype(o_ref.dtype)


def _ssm_post(y, w_glu, b_glu):
    t, n = y.shape
    tm = _pick(t, 512)
    return pl.pallas_call(
        _ssm_post_kernel,
        grid=(t // tm,),
        in_specs=[pl.BlockSpec((tm, n), lambda i: (i, 0)),
                  pl.BlockSpec((n, n), lambda i: (0, 0)),
                  pl.BlockSpec((1, n), lambda i: (0, 0))],
        out_specs=pl.BlockSpec((tm, n), lambda i: (i, 0)),
        out_shape=jax.ShapeDtypeStruct((t, n), BF16),
        compiler_params=_cparams(("arbitrary",)),
        name="ssm_post",
    )(y, w_glu, b_glu.reshape(1, n))


def _dot_nt(a, b):
    return lax.dot_general(a, b, (((1,), (1,)), ((), ())), preferred_element_type=F32)


def _dot_tn(a, b):
    return lax.dot_general(a, b, (((0,), (0,)), ((), ())), preferred_element_type=F32)


def _dot_hi(a, b):
    return jnp.dot(a, b, preferred_element_type=F32, precision=HI)


def _sigmoid(x):
    return 0.5 * jnp.tanh(0.5 * x) + 0.5


def _silu(x):
    hx = 0.5 * x
    return hx * jnp.tanh(hx) + hx


DN_HALO = 8
DN_HEADS_PER_STEP = 8


def _dn_kernel(q_ref, k_ref, v_ref, z_ref, ab_ref, cq_ref, ck_ref, cv_ref, al_ref, dtb_ref, nw_ref,
               *rest, tt, heads, hp, ncast):
    cast_src, o_ref, cast_dst, scratch = rest[:ncast], rest[ncast], rest[ncast + 1:2 * ncast + 1], rest[2 * ncast + 1:]
    s_refs, xx_refs = scratch[:hp], scratch[hp:]

    @pl.when(pl.program_id(2) == 0)
    def _():
        for s_ref, xx_ref in zip(s_refs, xx_refs):
            s_ref[...] = jnp.zeros_like(s_ref)
            xx_ref[:, 0:DN_HALO, :] = jnp.zeros((3, DN_HALO, DN_HEAD_DIM), F32)

    ab = ab_ref[...]
    xa = ab + dtb_ref[...]
    sp = jnp.maximum(xa, 0.0) + jnp.log(1.0 + jnp.exp(-jnp.abs(xa)))
    gc_all = -jnp.exp(al_ref[...]) * sp
    pos = lax.broadcasted_iota(jnp.int32, gc_all.shape, 0) & (CHUNK - 1)
    sh = 1
    while sh < CHUNK:
        gc_all = gc_all + jnp.where(pos >= sh, pltpu.roll(gc_all, sh, axis=0), 0.0)
        sh *= 2
    gate_vals = (gc_all, jnp.exp(gc_all), _sigmoid(ab))
    gens = [_dn_head(hh, pl.program_id(1) * hp + hh, q_ref, k_ref, v_ref, z_ref, gate_vals, cq_ref, ck_ref, cv_ref,
                     nw_ref, o_ref, s_refs[hh], xx_refs[hh], tt=tt, heads=heads)
            for hh in range(hp)]
    if ncast:
        gens.append(_cast_slices(cast_src, cast_dst))
    for _ in itertools.zip_longest(*gens):
        pass


def _dn_head(hh, h, q_ref, k_ref, v_ref, z_ref, gate_vals, cq_ref, ck_ref, cv_ref, nw_ref,
             o_ref, s_ref, xx_ref, *, tt, heads):
    hal = DN_HALO
    ls = slice(hh * DN_HEAD_DIM, (hh + 1) * DN_HEAD_DIM)

    def conv(idx, x_ref, cw_ref):
        x = x_ref[:, ls]
        xx_ref[idx, hal:, :] = x
        w = cw_ref[:, ls]
        acc = w[CONV_WIDTH - 1:CONV_WIDTH, :] * x
        for j in range(CONV_WIDTH - 1):
            off = hal - (CONV_WIDTH - 1) + j
            acc = acc + w[j:j + 1, :] * xx_ref[idx, off:off + tt, :]
        xx_ref[idx, 0:hal, :] = x[tt - hal:, :]
        return _silu(acc)

    qc = conv(0, q_ref, cq_ref)
    kc = conv(1, k_ref, ck_ref)
    vc = conv(2, v_ref, cv_ref)
    qn = qc * lax.rsqrt(jnp.sum(qc * qc, axis=-1, keepdims=True) + NORM_EPS) * (DN_HEAD_DIM ** -0.5)
    kn = kc * lax.rsqrt(jnp.sum(kc * kc, axis=-1, keepdims=True) + NORM_EPS)

    gc_all, egc_all, beta_all = gate_vals
    lane = lax.broadcasted_iota(jnp.int32, gc_all.shape, 1)

    def pick(vals, ln):
        col = jnp.sum(jnp.where(lane == ln, vals, 0.0), axis=-1, keepdims=True)
        return jnp.broadcast_to(col, (tt, DN_HEAD_DIM))

    beta_col = pick(beta_all, h + heads)
    gc = pick(gc_all, h)
    egc = pick(egc_all, h)
    nch = tt // CHUNK
    gc_row = jnp.transpose(gc)[0:1, :]

    ri = lax.broadcasted_iota(jnp.int32, (tt, tt), 0)
    ci = lax.broadcasted_iota(jnp.int32, (tt, tt), 1)
    same = (ri // CHUNK) == (ci // CHUNK)
    causal = same & (ri >= ci)
    strict = same & (ri > ci)
    gc_wide = jnp.concatenate([gc] * (tt // DN_HEAD_DIM), axis=1)
    decay = jnp.where(causal, jnp.exp(jnp.where(causal, gc_wide - gc_row, 0.0)), 0.0)
    kb = kn * beta_col
    knb = kn.astype(BF16)
    yield
    a_raw = _dot_nt(kb.astype(BF16), knb)
    qk_raw = _dot_nt(qn.astype(BF16), knb)
    yield
    a_bd = jnp.where(strict, a_raw * decay, 0.0)
    qk_bd = jnp.where(causal, qk_raw * decay, 0.0).astype(BF16)

    def fold(m):
        out = m[0:CHUNK]
        for c in range(1, nch):
            out = out + m[c * CHUNK:(c + 1) * CHUNK]
        return out

    def spread(m):
        return jnp.where(same, jnp.concatenate([m] * nch, axis=0), 0.0)

    r64 = lax.broadcasted_iota(jnp.int32, (CHUNK, tt), 0)
    c64 = lax.broadcasted_iota(jnp.int32, (CHUNK, tt), 1)
    eye_cat = (r64 == (c64 & (CHUNK - 1))).astype(F32)
    pw_cat = fold(a_bd)
    inv_cat = eye_cat - pw_cat
    pw_bd = a_bd.astype(BF16)
    for _ in range(5):
        pw_cat = jnp.dot(pw_cat.astype(BF16), pw_bd, preferred_element_type=F32)
        yield
        pw_bd = spread(pw_cat).astype(BF16)
        inv_add = jnp.dot(inv_cat.astype(BF16), pw_bd, preferred_element_type=F32)
        yield
        inv_cat = inv_cat + inv_add
    inv_bd = spread(inv_cat).astype(BF16)
    rhs = jnp.concatenate([vc * beta_col, kb * egc], axis=1).astype(BF16)
    sol = jnp.dot(inv_bd, rhs, preferred_element_type=F32)
    yield
    u_all, w_all = sol[:, :DN_HEAD_DIM], sol[:, DN_HEAD_DIM:]
    qe = qn * egc

    outs = []
    zblk = jnp.zeros((CHUNK, DN_HEAD_DIM), BF16)
    s = s_ref[...]
    for c in range(nch):
        sl = slice(c * CHUNK, (c + 1) * CHUNK)
        wq = jnp.concatenate([w_all[sl], qe[sl]], axis=0).astype(BF16)
        ws = jnp.dot(wq, s.astype(BF16), preferred_element_type=F32)
        yield
        v_new = u_all[sl] - ws[:CHUNK]
        vnb = v_new.astype(BF16)
        v_pad = jnp.concatenate([zblk] * c + [vnb] + [zblk] * (nch - 1 - c), axis=0)
        gc_c = gc[sl]
        g_last = gc_c[CHUNK - 1:CHUNK, :]
        k_dec = (kn[sl] * jnp.exp(g_last - gc_c)).astype(BF16)
        s_add = _dot_tn(k_dec, vnb)
        o_add = jnp.dot(qk_bd[sl], v_pad, preferred_element_type=F32)
        yield
        s = s * jnp.exp(g_last) + s_add
        outs.append(ws[CHUNK:] + o_add)
    s_ref[...] = s
    o = jnp.concatenate(outs, axis=0)
    o = o * lax.rsqrt(jnp.mean(o * o, axis=-1, keepdims=True) + NORM_EPS) * nw_ref[...]
    o_ref[:, ls] = (o * _silu(z_ref[:, ls])).astype(o_ref.dtype)


def _deltanet(proj, ab, conv_w, a_log, dt_bias, norm_w, bsz, seq, heads, off_q, cast_weights=()):
    t = bsz * seq
    tt = _pick(seq, 256)
    nt = seq // tt
    hd = DN_HEAD_DIM
    cw = conv_w.reshape(CONV_WIDTH, 3 * heads * hd).astype(F32)
    pad = LANES - heads
    al = jnp.pad(a_log.astype(F32), (0, pad)).reshape(1, LANES)
    dtb = jnp.pad(dt_bias.astype(F32), (0, pad)).reshape(1, LANES)
    nw = norm_w.astype(F32).reshape(1, hd)

    hp = math.gcd(math.gcd(heads, off_q), DN_HEADS_PER_STEP)
    hw = hp * hd
    oq, nh = off_q // hp, heads // hp

    def act(o):
        return pl.BlockSpec((tt, hw), lambda b, h, i: (b * nt + i, oq + o * nh + h))

    def cws(o):
        return pl.BlockSpec((CONV_WIDTH, hw), lambda b, h, i: (0, o * nh + h))

    row = pl.BlockSpec((1, LANES), lambda b, h, i: (0, 0))
    ncast = len(cast_weights)
    cast_specs = _cast_specs(cast_weights, (bsz, nh, nt))
    kern = functools.partial(_dn_kernel, tt=tt, heads=heads, hp=hp, ncast=ncast)
    outs = pl.pallas_call(
        kern,
        grid=(bsz, nh, nt),
        in_specs=[act(0), act(1), act(2), act(3),
                  pl.BlockSpec((tt, LANES), lambda b, h, i: (b * nt + i, 0)),
                  cws(0), cws(1), cws(2), row, row, row] + cast_specs,
        out_specs=[pl.BlockSpec((tt, hw), lambda b, h, i: (b * nt + i, h))] + cast_specs,
        out_shape=([jax.ShapeDtypeStruct((t, heads * hd), BF16)]
                   + [jax.ShapeDtypeStruct(w.shape, BF16) for w in cast_weights]),
        scratch_shapes=([pltpu.VMEM((hd, hd), F32)] * hp
                        + [pltpu.VMEM((3, tt + DN_HALO, hd), F32)] * hp),
        compiler_params=_cparams(("arbitrary", "arbitrary", "arbitrary"), VMEM_LIMIT_DN),
        name="deltanet",
    )(proj, proj, proj, proj, ab, cw, cw, cw, al, dtb, nw, *cast_weights)
    return outs[0], outs[1:]


def _merge_kernel(ys_ref, yd_ref, gs_ref, gd_ref, ws_ref, wd_ref, o_ref):
    ps = jnp.dot(ys_ref[...], ws_ref[...], preferred_element_type=F32)
    pd = jnp.dot(yd_ref[...], wd_ref[...], preferred_element_type=F32)
    o_ref[...] = (jax.nn.sigmoid(gs_ref[...]) * ps + jax.nn.sigmoid(gd_ref[...]) * pd).astype(o_ref.dtype)


def _merge(ys, yd, gates, gate_col, w_ps, w_pd):
    t, ns = ys.shape
    d = w_ps.shape[1]
    tm = _pick(t, 512)
    tn = _pick(ns, 1024)
    nb = d // tn
    off = gate_col // tn
    assert gate_col % tn == 0
    return pl.pallas_call(
        _merge_kernel,
        grid=(nb, t // tm),
        in_specs=[pl.BlockSpec((tm, ns), lambda j, i: (i, 0)),
                  pl.BlockSpec((tm, d), lambda j, i: (i, 0)),
                  pl.BlockSpec((tm, tn), lambda j, i: (i, off + j)),
                  pl.BlockSpec((tm, tn), lambda j, i: (i, off + nb + j)),
                  pl.BlockSpec((ns, tn), lambda j, i: (0, j)),
                  pl.BlockSpec((d, tn), lambda j, i: (0, j))],
        out_specs=pl.BlockSpec((tm, tn), lambda j, i: (i, j)),
        out_shape=jax.ShapeDtypeStruct((t, d), BF16),
        compiler_params=_cparams(("arbitrary", "arbitrary")),
        name="merge",
    )(ys, yd, gates, gates, w_ps, w_pd)


def _pack_bf16_pair(x):
    half = x.shape[1] // 2
    lo = pltpu.bitcast(x[:, :half].astype(BF16).astype(F32), jnp.uint32)
    hi = pltpu.bitcast(x[:, half:].astype(BF16).astype(F32), jnp.uint32)
    return (lo >> 16) | (hi & jnp.uint32(0xFFFF0000))


def _unpack_bf16_pair(u):
    lo = pltpu.bitcast(u << 16, F32).astype(BF16)
    hi = pltpu.bitcast(u & jnp.uint32(0xFFFF0000), F32).astype(BF16)
    return lo, hi


def _out_ln_kernel(m_ref, w_ref, h_ref, g_ref, b_ref, h1_ref, hp_ref):
    mix = jnp.dot(m_ref[...], w_ref[...], preferred_element_type=F32)
    h1 = _ln(DEEPNORM_ALPHA * h_ref[...] + mix, g_ref[...], b_ref[...])
    h1_ref[...] = h1
    hp_ref[...] = _pack_bf16_pair(h1)


def _out_ln(merged, w_out, h, g, b):
    t, d = h.shape
    tm = _pick(t, 512)
    return pl.pallas_call(
        _out_ln_kernel,
        grid=(t // tm,),
        in_specs=[pl.BlockSpec((tm, d), lambda i: (i, 0)),
                  pl.BlockSpec((d, d), lambda i: (0, 0)),
                  pl.BlockSpec((tm, d), lambda i: (i, 0)),
                  pl.BlockSpec((1, d), lambda i: (0, 0)),
                  pl.BlockSpec((1, d), lambda i: (0, 0))],
        out_specs=[pl.BlockSpec((tm, d), lambda i: (i, 0)),
                   pl.BlockSpec((tm, d // 2), lambda i: (i, 0))],
        out_shape=[jax.ShapeDtypeStruct((t, d), F32), jax.ShapeDtypeStruct((t, d // 2), jnp.uint32)],
        compiler_params=_cparams(("arbitrary",)),
        name="out_ln1",
    )(merged, w_out, h, g.reshape(1, d), b.reshape(1, d))


def _router_kernel(h_ref, w_ref, b_ref, ir_ref, gate_ref, cnt_ref, base_ref, *, n_exp, tm):
    i = pl.program_id(0)

    @pl.when(i == 0)
    def _():
        base_ref[...] = jnp.zeros_like(base_ref)

    h = h_ref[...]
    w = w_ref[...]
    h_hi = h.astype(BF16)
    h_lo = (h - h_hi.astype(F32)).astype(BF16)
    w_hi = w.astype(BF16)
    w_lo = (w - w_hi.astype(F32)).astype(BF16)
    logits = (jnp.dot(h_hi, w_hi, preferred_element_type=F32) + jnp.dot(h_hi, w_lo, preferred_element_type=F32)
              + jnp.dot(h_lo, w_hi, preferred_element_type=F32) + b_ref[...])
    lane = lax.broadcasted_iota(jnp.int32, logits.shape, 1)
    vals = jnp.where(lane < n_exp, logits, -jnp.inf)
    sels, tops, ams = [], [], []
    for _ in range(TOP_K):
        m = jnp.max(vals, axis=-1, keepdims=True)
        am = jnp.min(jnp.where(vals == m, lane, LANES), axis=-1, keepdims=True)
        sel = lane == am
        vals = jnp.where(sel, -jnp.inf, vals)
        sels.append(sel)
        tops.append(m)
        ams.append(am)
    es = [jnp.exp(v - tops[0]) for v in tops]
    den = es[0] + es[1] + es[2] + es[3]
    onehot = sels[0] | sels[1] | sels[2] | sels[3]
    mt = onehot.astype(BF16)
    ri = lax.broadcasted_iota(jnp.int32, (tm, tm), 0)
    ci = lax.broadcasted_iota(jnp.int32, (tm, tm), 1)
    before = (ri > ci).astype(BF16)
    prior = jnp.dot(before, mt, preferred_element_type=F32) + base_ref[...]
    ir = jnp.zeros(logits.shape, F32)
    gate_o = jnp.zeros(logits.shape, F32)
    for k in range(TOP_K):
        rk = jnp.sum(jnp.where(sels[k], prior, 0.0), axis=-1, keepdims=True)
        ir = jnp.where(lane == k, ams[k].astype(F32), ir)
        ir = jnp.where(lane == TOP_K + k, rk, ir)
        gate_o = jnp.where(lane == k, es[k] / den, gate_o)
    ir_ref[...] = jnp.transpose(ir)[0:2 * TOP_K, :].astype(jnp.int32)
    gate_ref[...] = gate_o
    base_ref[...] += jnp.sum(onehot.astype(F32), axis=0, keepdims=True)
    cnt_ref[...] = base_ref[...]


def _router(h1, w_router, b_router):
    t, d = h1.shape
    n_exp = w_router.shape[1]
    tm = _pick(t, 512)
    w = jnp.pad(w_router.astype(F32), ((0, 0), (0, LANES - n_exp)))
    b = jnp.pad(b_router.astype(F32), (0, LANES - n_exp)).reshape(1, LANES)
    kern = functools.partial(_router_kernel, n_exp=n_exp, tm=tm)
    return pl.pallas_call(
        kern,
        grid=(t // tm,),
        in_specs=[pl.BlockSpec((tm, d), lambda i: (i, 0)),
                  pl.BlockSpec((d, LANES), lambda i: (0, 0)),
                  pl.BlockSpec((1, LANES), lambda i: (0, 0))],
        out_specs=[pl.BlockSpec((2 * TOP_K, tm), lambda i: (0, i)),
                   pl.BlockSpec((tm, LANES), lambda i: (i, 0)),
                   pl.BlockSpec((1, LANES), lambda i: (0, 0))],
        out_shape=[jax.ShapeDtypeStruct((2 * TOP_K, t), jnp.int32), jax.ShapeDtypeStruct((t, LANES), F32),
                   jax.ShapeDtypeStruct((1, LANES), F32)],
        scratch_shapes=[pltpu.VMEM((1, LANES), F32)],
        compiler_params=_cparams(("arbitrary",)),
        name="router",
    )(h1, w, b)


def _dispatch_kernel(dest_ref, pend_ref, nused_ref, x_ref, xs_ref, zero_ref, sem, zsem, *, tt, n_tok, n_exp, nblk):
    i = pl.program_id(0)
    base = i * tt

    @pl.when(i == 0)
    def _():
        zero_ref[...] = jnp.zeros_like(zero_ref)

        def zero_block(start):
            start = pl.multiple_of(start, ROW_BLOCK)
            return pltpu.make_async_copy(zero_ref, xs_ref.at[pl.ds(start, ROW_BLOCK)], zsem)

        def expert_has_rows(e):
            prev = jnp.where(e == 0, 0, pend_ref[jnp.maximum(e - 1, 0)])
            return pend_ref[e] > prev

        def start_expert(e, carry):
            @pl.when(expert_has_rows(e))
            def _():
                zero_block(pend_ref[e] - ROW_BLOCK).start()
            return carry

        def wait_expert(e, carry):
            @pl.when(expert_has_rows(e))
            def _():
                zero_block(0).wait()
            return carry

        def start_tail(b, carry):
            zero_block(b * ROW_BLOCK).start()
            return carry

        def wait_tail(b, carry):
            zero_block(0).wait()
            return carry

        lax.fori_loop(0, n_exp, start_expert, 0)
        lax.fori_loop(nused_ref[0], nblk, start_tail, 0)
        lax.fori_loop(0, n_exp, wait_expert, 0)
        lax.fori_loop(nused_ref[0], nblk, wait_tail, 0)

    def issue(r, carry):
        for k in range(TOP_K):
            row = dest_ref[k * n_tok + base + r]
            pltpu.make_async_copy(x_ref.at[pl.ds(r, 1)], xs_ref.at[pl.ds(row, 1)], sem).start(priority=k % 2)
        return carry

    lax.fori_loop(0, tt, issue, 0, unroll=4)

    for k in range(TOP_K):
        pltpu.make_async_copy(x_ref, xs_ref.at[pl.ds(0, tt)], sem).wait()


def _dispatch(xp, dest, pad_ends, n_used, n_rows):
    t, dh = xp.shape
    tt = _pick(t, 512)
    kern = functools.partial(_dispatch_kernel, tt=tt, n_tok=t, n_exp=pad_ends.shape[0], nblk=n_rows // ROW_BLOCK)
    return pl.pallas_call(
        kern,
        grid_spec=pltpu.PrefetchScalarGridSpec(
            num_scalar_prefetch=3,
            grid=(t // tt,),
            in_specs=[pl.BlockSpec((tt, dh), lambda i, *_: (i, 0))],
            out_specs=pl.BlockSpec(memory_space=pl.ANY),
            scratch_shapes=[pltpu.VMEM((ROW_BLOCK, dh), jnp.uint32),
                            pltpu.SemaphoreType.DMA(()), pltpu.SemaphoreType.DMA(())]),
        out_shape=jax.ShapeDtypeStruct((n_rows, dh), jnp.uint32),
        compiler_params=_cparams(("arbitrary",)),
        name="dispatch",
    )(dest, pad_ends.astype(jnp.int32), n_used, xp)


def _expert_kernel(be_ref, nb_ref, x_ref, wg_ref, wu_ref, bg_ref, bu_ref, wd_ref, bd_ref, o_ref, *, half):
    i = pl.program_id(0)
    j = pl.program_id(1)

    @pl.when(i < nb_ref[0])
    def _():
        lo, hi = _unpack_bf16_pair(x_ref[...])
        gate = (jnp.dot(lo, wg_ref[0, :half, :], preferred_element_type=F32)
                + jnp.dot(hi, wg_ref[0, half:, :], preferred_element_type=F32) + bg_ref[0])
        up = (jnp.dot(lo, wu_ref[0, :half, :], preferred_element_type=F32)
              + jnp.dot(hi, wu_ref[0, half:, :], preferred_element_type=F32) + bu_ref[0])
        gate = jnp.minimum(gate, SWIGLU_LIMIT)
        up = jnp.clip(up, -SWIGLU_LIMIT, SWIGLU_LIMIT)
        act = gate * _sigmoid(SWIGLU_ALPHA * gate) * (up + 1.0)
        part = jnp.dot(act.astype(BF16), wd_ref[0], preferred_element_type=F32)

        @pl.when(j == 0)
        def _():
            o_ref[...] = part + bd_ref[0]

        @pl.when(j > 0)
        def _():
            o_ref[...] += part

    @pl.when((i >= nb_ref[0]) & (j == 0))
    def _():
        o_ref[...] = jnp.zeros_like(o_ref)


def _experts(xs, block_expert, n_used, w_gu, b_gu, w_down, b_down):
    n_rows, dh = xs.shape
    n_exp, d, ff2 = w_gu.shape
    ff = ff2 // 2
    tf = _pick(ff, 1024)
    nj = ff // tf
    nblk = n_rows // ROW_BLOCK
    b_gu3 = b_gu.astype(F32).reshape(n_exp, 1, ff2)
    b_d3 = b_down.astype(F32).reshape(n_exp, 1, d)

    def blk(i, nb):
        return jnp.maximum(jnp.minimum(i, nb[0] - 1), 0)

    def jj(i, j, nb):
        return jnp.where(i < nb[0], j, nj - 1)

    kern = functools.partial(_expert_kernel, half=d // 2)
    return pl.pallas_call(
        kern,
        grid_spec=pltpu.PrefetchScalarGridSpec(
            num_scalar_prefetch=2,
            grid=(nblk, nj),
            in_specs=[
                pl.BlockSpec((ROW_BLOCK, dh), lambda i, j, be, nb: (blk(i, nb), 0)),
                pl.BlockSpec((1, d, tf), lambda i, j, be, nb: (be[blk(i, nb)], 0, jj(i, j, nb))),
                pl.BlockSpec((1, d, tf), lambda i, j, be, nb: (be[blk(i, nb)], 0, nj + jj(i, j, nb))),
                pl.BlockSpec((1, 1, tf), lambda i, j, be, nb: (be[blk(i, nb)], 0, jj(i, j, nb))),
                pl.BlockSpec((1, 1, tf), lambda i, j, be, nb: (be[blk(i, nb)], 0, nj + jj(i, j, nb))),
                pl.BlockSpec((1, tf, d), lambda i, j, be, nb: (be[blk(i, nb)], jj(i, j, nb), 0)),
                pl.BlockSpec((1, 1, d), lambda i, j, be, nb: (be[blk(i, nb)], 0, 0)),
            ],
            out_specs=pl.BlockSpec((ROW_BLOCK, d), lambda i, j, be, nb: (i, 0))),
        out_shape=jax.ShapeDtypeStruct((n_rows, d), F32),
        compiler_params=_cparams(("arbitrary", "arbitrary")),
        name="experts",
    )(block_expert, n_used, xs, w_gu, w_gu, b_gu3, b_gu3, w_down, b_d3)


def _combine_kernel(dest_ref, y_ref, h_ref, gate_ref, g_ref, b_ref, o_ref, buf_ref, sem, *, tt, n_tok):
    nsteps = n_tok // tt
    i = pl.program_id(0)
    cur = i % 2

    def gather_tile(tile, buf):
        base = tile * tt

        def issue(r, carry):
            for k in range(TOP_K):
                row = dest_ref[k * n_tok + base + r]
                pltpu.make_async_copy(y_ref.at[pl.ds(row, 1)], buf_ref.at[buf, k, pl.ds(r, 1)],
                                      sem.at[buf]).start(priority=k % 2)
            return carry

        lax.fori_loop(0, tt, issue, 0, unroll=4)

    @pl.when(i == 0)
    def _():
        gather_tile(0, 0)

    @pl.when(i + 1 < nsteps)
    def _():
        gather_tile(i + 1, 1 - cur)

    for k in range(TOP_K):
        pltpu.make_async_copy(y_ref.at[pl.ds(0, tt)], buf_ref.at[cur, k], sem.at[cur]).wait()
    gates = gate_ref[...]
    ffn = gates[:, 0:1] * buf_ref[cur, 0]
    for k in range(1, TOP_K):
        ffn = ffn + gates[:, k:k + 1] * buf_ref[cur, k]
    o_ref[...] = _ln(DEEPNORM_ALPHA * h_ref[...] + ffn, g_ref[...], b_ref[...])


def _combine(y, dest, h1, gates, g, b):
    t, d = h1.shape
    tt = _pick(t, 256)
    kern = functools.partial(_combine_kernel, tt=tt, n_tok=t)
    return pl.pallas_call(
        kern,
        grid_spec=pltpu.PrefetchScalarGridSpec(
            num_scalar_prefetch=1,
            grid=(t // tt,),
            in_specs=[pl.BlockSpec(memory_space=pl.ANY),
                      pl.BlockSpec((tt, d), lambda i, dst: (i, 0)),
                      pl.BlockSpec((tt, LANES), lambda i, dst: (i, 0)),
                      pl.BlockSpec((1, d), lambda i, dst: (0, 0)),
                      pl.BlockSpec((1, d), lambda i, dst: (0, 0))],
            out_specs=pl.BlockSpec((tt, d), lambda i, dst: (i, 0)),
            scratch_shapes=[pltpu.VMEM((2, TOP_K, tt, d), F32), pltpu.SemaphoreType.DMA((2,))]),
        out_shape=jax.ShapeDtypeStruct((t, d), F32),
        compiler_params=_cparams(("arbitrary",)),
        name="combine",
    )(dest, y, h1, gates, g.reshape(1, d), b.reshape(1, d))


def kernel(x, ln_in_g, ln_in_b, w_in, lam_re, lam_im, log_step, ssm_b_re, ssm_b_im, ssm_c_re, ssm_c_im, ssm_d, w_glu, b_glu, conv_w, a_log, dt_bias, dn_norm_w, w_proj_ssm, w_proj_dn, w_out, ln1_g, ln1_b, w_router, b_router, w_gate_up, b_gate_up, w_down, b_down, ln2_g, ln2_b):
    bsz, seq, d = x.shape
    t = bsz * seq
    sw = d // 2
    heads = d // DN_HEAD_DIM
    dn = heads * DN_HEAD_DIM
    groups = sw // SSM_GROUP_DIM
    n_state = lam_re.shape[-1]
    n_exp = w_router.shape[-1]
    assert w_in.shape[0] == DEPTH == 1
    assert seq % (CHUNK * 4) == 0 and sw % LANES == 0 and groups % 2 == 0 and 2 * heads <= LANES

    h, hb = _ln_in(x.reshape(t, d), ln_in_g, ln_in_b)

    c_ab = sw + 4 * dn
    assert c_ab % LANES == 0
    w_in_t = jnp.swapaxes(w_in, 1, 2)
    w_gates_t = w_in_t[0, c_ab + 2 * heads:, :].astype(BF16)
    proj = _proj2(hb, w_in_t, c_ab, w_gates_t, "proj_in")
    ab = _proj_rows(hb, w_in_t, c_ab, "proj_ab")

    ops = _s5_operators(lam_re[0], lam_im[0], log_step[0], ssm_b_re[0], ssm_b_im[0],
                        ssm_c_re[0], ssm_c_im[0], ssm_d[0])
    y_s, _ = _s5_scan(proj, ops, bsz, seq, sw, n_state)
    y_ssm = _ssm_post(y_s, w_glu[0].astype(BF16), b_glu[0].astype(F32))

    ff2 = w_gate_up.shape[-1]
    y_dn, (w_gu_bf, w_dn_bf) = _deltanet(
        proj, ab, conv_w[0], a_log[0], dt_bias[0], dn_norm_w[0], bsz, seq, heads, off_q=sw // LANES,
        cast_weights=(w_gate_up[0].reshape(n_exp * d, ff2), w_down[0].reshape(n_exp * (ff2 // 2), d)))
    w_gu_bf = w_gu_bf.reshape(n_exp, d, ff2)
    w_dn_bf = w_dn_bf.reshape(n_exp, ff2 // 2, d)

    merged = _merge(y_ssm, y_dn, proj, sw + 4 * dn, w_proj_ssm[0].astype(BF16), w_proj_dn[0].astype(BF16))
    h1, h1p = _out_ln(merged, w_out[0].astype(BF16), h, ln1_g[0], ln1_b[0])

    idx_rank, gates, cnt = _router(h1, w_router[0], b_router[0])
    counts = cnt[0, :n_exp].astype(jnp.int32)
    padded = (counts + ROW_BLOCK - 1) // ROW_BLOCK * ROW_BLOCK
    pad_ends = jnp.cumsum(padded)
    pad_starts = pad_ends - padded
    slot_expert = idx_rank[:TOP_K]
    is_e = slot_expert[None] == jnp.arange(n_exp, dtype=jnp.int32)[:, None, None]
    start = jnp.sum(jnp.where(is_e, pad_starts.astype(jnp.int32)[:, None, None], 0), axis=0)
    dest = (start + idx_rank[TOP_K:]).astype(jnp.int32).reshape(TOP_K * t)
    n_rows = t * TOP_K + n_exp * ROW_BLOCK
    nblk = n_rows // ROW_BLOCK
    blk_start = jnp.arange(nblk, dtype=jnp.int32) * ROW_BLOCK
    block_expert = jnp.minimum(jnp.sum((pad_ends[None, :] <= blk_start[:, None]).astype(jnp.int32), axis=1),
                               n_exp - 1).astype(jnp.int32)
    n_used = (pad_ends[-1:] // ROW_BLOCK).astype(jnp.int32)
    xs = _dispatch(h1p, dest, pad_ends, n_used, n_rows)
    y = _experts(xs, block_expert, n_used, w_gu_bf, b_gate_up[0], w_dn_bf, b_down[0])
    out = _combine(y, dest, h1, gates, ln2_g[0], ln2_b[0])
    return out.reshape(bsz, seq, d)
```
